```python
import math
import numpy as np
import jax
import jax.numpy as jnp
from jax import lax

D_MODEL = 2048
BATCH = 2
SEQ = 16384
DEPTH = 1

CTX_LEN = 256
GRID_W = 64

MLA_HEADS = 8
MLA_Q_RANK = 512
MLA_KV_RANK = 512
MLA_NOPE = 128
MLA_ROPE = 64
MLA_V = 128
MLA_QK = MLA_NOPE + MLA_ROPE
ROPE_BASE = 10000.0
Q_BLOCK = 128

DN_HEADS = 8
DN_DK = 128
DN_DV = 128
DN_CONV = 3
DN_CHUNK = 64

FFN_HIDDEN = 5504
FFN_CONV = 3

EPS = 1e-6

IN_SPLITS = (MLA_Q_RANK, MLA_KV_RANK, MLA_ROPE,
             DN_HEADS * DN_DK, DN_HEADS * DN_DK, DN_HEADS * DN_DV, DN_HEADS * DN_DV,
             2 * DN_HEADS, 2 * DN_HEADS, D_MODEL, D_MODEL)
IN_WIDTH = sum(IN_SPLITS)

kernel_name = 'hybrid_mla_gdn_convglu_dit'


def rmsnorm(x, w):
    xf = x.astype(jnp.float32)
    y = xf * lax.rsqrt(jnp.mean(xf * xf, axis=-1, keepdims=True) + EPS)
    return (y * w.astype(jnp.float32)).astype(x.dtype)


def l2norm(x):
    xf = x.astype(jnp.float32)
    return xf * lax.rsqrt(jnp.sum(xf * xf, axis=-1, keepdims=True) + EPS)


def modulate(x, shift, scale):
    return x * (1 + jnp.expand_dims(scale, -2)) + jnp.expand_dims(shift, -2)


def dwconv_centred(x, w):
    k = w.shape[0]
    p = k // 2
    t = x.shape[1]
    xp = jnp.pad(x, ((0, 0), (p, p), (0, 0)))
    return sum(xp[:, j:j + t] * w[j] for j in range(k))


def split_columns(p):
    offs = np.cumsum(np.array(IN_SPLITS))[:-1].tolist()
    return jnp.split(p, offs, axis=-1)


def axial_rope_tables(rows, dtype):
    n = MLA_ROPE // 4
    row = jnp.repeat(jnp.arange(rows, dtype=jnp.float32), GRID_W)
    col = jnp.tile(jnp.arange(GRID_W, dtype=jnp.float32), rows)
    inv = ROPE_BASE ** (-jnp.arange(n, dtype=jnp.float32) / n)
    ang_r = row[:, None, None] * inv
    ang_c = col[:, None, None] * inv
    return tuple(a.astype(dtype) for a in (jnp.cos(ang_r), jnp.sin(ang_r), jnp.cos(ang_c), jnp.sin(ang_c)))


def rotate_pairs(x, cos, sin):
    a, b = jnp.split(x, 2, axis=-1)
    return jnp.concatenate([a * cos - b * sin, b * cos + a * sin], axis=-1)


def apply_axial_rope(t, tabs):
    cos_r, sin_r, cos_c, sin_c = tabs
    nope, rr, rc = jnp.split(t, [MLA_NOPE, MLA_NOPE + MLA_ROPE // 2], axis=-1)
    return jnp.concatenate([nope, rotate_pairs(rr, cos_r, sin_r), rotate_pairs(rc, cos_c, sin_c)], axis=-1)


def mla_q(q_a, q_a_norm, w_q_b, q_norm, tabs):
    b, t, _ = q_a.shape
    q = (rmsnorm(q_a, q_a_norm) @ w_q_b).reshape(b, t, MLA_HEADS, MLA_QK)
    q = rmsnorm(q, q_norm)
    return q if tabs is None else apply_axial_rope(q, tabs)


def mla_kv(kv_a, k_rope, kv_a_norm, w_kv_b, k_norm, tabs):
    b, t, _ = kv_a.shape
    kv = (rmsnorm(kv_a, kv_a_norm) @ w_kv_b).reshape(b, t, MLA_HEADS, MLA_NOPE + MLA_V)
    k_nope, v = jnp.split(kv, [MLA_NOPE], axis=-1)
    k_r = jnp.broadcast_to(k_rope[:, :, None, :], (b, t, MLA_HEADS, MLA_ROPE))
    k = rmsnorm(jnp.concatenate([k_nope, k_r], axis=-1), k_norm)
    k = k if tabs is None else apply_axial_rope(k, tabs)
    return k, v


def attend_dense(q, k, v):
    b, n, h, _ = q.shape
    s = jnp.einsum('bqhd,bkhd->bhqk', q, k).astype(jnp.float32) * (MLA_QK ** -0.5)
    p = jax.nn.softmax(s, axis=-1).astype(v.dtype)
    return jnp.einsum('bhqk,bkhd->bqhd', p, v).reshape(b, n, h * MLA_V)


def attend_latent(q, k_lat, v_lat, k_ctx, v_ctx):
    b, t, h, dq = q.shape
    nb = t // Q_BLOCK
    qb = q.reshape(b, nb, Q_BLOCK, h, dq).transpose(1, 0, 2, 3, 4)

    def block(qi):
        s = jnp.concatenate([jnp.einsum('bqhd,bkhd->bhqk', qi, k_lat),
                             jnp.einsum('bqhd,bkhd->bhqk', qi, k_ctx)], axis=-1)
        p = jax.nn.softmax(s.astype(jnp.float32) * (MLA_QK ** -0.5), axis=-1).astype(v_lat.dtype)
        return (jnp.einsum('bhqk,bkhd->bqhd', p[..., :t], v_lat)
                + jnp.einsum('bhqk,bkhd->bqhd', p[..., t:], v_ctx))

    o = lax.map(block, qb)
    return o.transpose(1, 0, 2, 3, 4).reshape(b, t, h * MLA_V)


def dn_prepare(dq, dk, dv, da, db, conv_w, a_log, dt_bias, need_q):
    b, t, _ = dk.shape
    wq, wk, wv = jnp.split(conv_w, [DN_HEADS * DN_DK, 2 * DN_HEADS * DN_DK], axis=-1)

    def conv_heads(a, w, d):
        return jax.nn.silu(dwconv_centred(a, w)).reshape(b, t, DN_HEADS, d).transpose(0, 2, 1, 3)

    k = l2norm(conv_heads(dk, wk, DN_DK))
    v = conv_heads(dv, wv, DN_DV).astype(jnp.float32)
    q = l2norm(conv_heads(dq, wq, DN_DK)) * (DN_DK ** -0.5) if need_q else None
    g = -jnp.exp(a_log.astype(jnp.float32)) * jax.nn.softplus(
        da.reshape(b, t, 2, DN_HEADS).astype(jnp.float32) + dt_bias.astype(jnp.float32))
    beta = jax.nn.sigmoid(db.reshape(b, t, 2, DN_HEADS).astype(jnp.float32))
    return q, k, v, g.transpose(2, 0, 3, 1), beta.transpose(2, 0, 3, 1)


def gated_delta_chunked(q, k, v, g, beta, s0, with_out):
    b, h, t, dk = k.shape
    dv = v.shape[-1]
    c = DN_CHUNK
    nc = t // c
    k = k.reshape(b, h, nc, c, dk)
    v = v.reshape(b, h, nc, c, dv)
    gc = jnp.cumsum(g.reshape(b, h, nc, c), axis=-1)
    beta = beta.reshape(b, h, nc, c, 1)
    incl = jnp.tril(jnp.ones((c, c), dtype=bool))
    strict = jnp.tril(jnp.ones((c, c), dtype=bool), -1)
    decay = jnp.exp(jnp.where(incl, gc[..., :, None] - gc[..., None, :], -jnp.inf))
    kb = k * beta
    m = jnp.where(strict, jnp.einsum('bhnid,bhnjd->bhnij', kb, k) * decay, 0.0)
    rhs = jnp.concatenate([v * beta, kb * jnp.exp(gc)[..., None]], axis=-1)
    sol = lax.linalg.triangular_solve(m + jnp.eye(c, dtype=m.dtype), rhs,
                                      left_side=True, lower=True, unit_diagonal=True)
    u, w = sol[..., :dv], sol[..., dv:]
    k_end = k * jnp.exp(gc[..., -1:] - gc)[..., None]
    g_end = jnp.exp(gc[..., -1])[..., None, None]
    xs = [u, w, k_end, g_end]
    if with_out:
        q = q.reshape(b, h, nc, c, dk)
        xs += [q * jnp.exp(gc)[..., None], jnp.einsum('bhnid,bhnjd->bhnij', q, k) * decay]
    xs = [jnp.moveaxis(a, 2, 0) for a in xs]

    def step(s, inp):
        u_c, w_c, k_c, g_c = inp[:4]
        v_new = u_c - jnp.einsum('bhcd,bhde->bhce', w_c, s)
        s_next = s * g_c + jnp.einsum('bhcd,bhce->bhde', k_c, v_new)
        if not with_out:
            return s_next, None
        q_c, a_c = inp[4:]
        o = jnp.einsum('bhcd,bhde->bhce', q_c, s) + jnp.einsum('bhcj,bhje->bhce', a_c, v_new)
        return s_next, o

    s_final, o = lax.scan(step, s0, xs)
    if not with_out:
        return None, s_final
    return jnp.moveaxis(o, 0, 2).reshape(b, h, t, dv), s_final


def dn_bidirectional(q, k, v, g, beta, s0, with_out):
    def flip(a):
        return None if a is None else jnp.flip(a, axis=2)

    o_f, s_f = gated_delta_chunked(q, k, v, g[0], beta[0], s0[0], with_out)
    o_b, s_b = gated_delta_chunked(flip(q), flip(k), flip(v), flip(g[1]), flip(beta[1]), s0[1], with_out)
    states = jnp.stack([s_f, s_b])
    if not with_out:
        return None, states
    return o_f + flip(o_b), states


def dn_output(o, z, o_norm, dtype):
    b, h, t, dv = o.shape
    y = rmsnorm(o.transpose(0, 2, 1, 3), o_norm) * jax.nn.silu(z.reshape(b, t, h, dv).astype(jnp.float32))
    return y.reshape(b, t, h * dv).astype(dtype)


def merge_branches(y_mla, y_dn, gate_mla, gate_dn, w_o_mla, w_o_dn, w_out):
    merged = jax.nn.sigmoid(gate_mla) * (y_mla @ w_o_mla) + jax.nn.sigmoid(gate_dn) * (y_dn @ w_o_dn)
    return merged @ w_out


def conv_glu(h, w_up, conv_w, w_down):
    gate, val = jnp.split(h @ w_up, 2, axis=-1)
    return (jax.nn.silu(dwconv_centred(gate, conv_w)) * val) @ w_down


def setup_inputs(seed: int = 0) -> dict:
    key = jax.random.key(seed)
    ks = jax.random.split(key, 32)
    L, D, H = DEPTH, D_MODEL, DN_HEADS
    f32 = jnp.float32

    def nrm(i, shape, scale):
        return jax.random.normal(ks[i], shape, f32) * scale

    def gain(i, n):
        return 1.0 + 0.01 * jax.random.normal(ks[i], (L, n), f32)

    dt = jnp.exp(jax.random.uniform(ks[20], (L, 2, H), f32, math.log(1e-3), math.log(1e-1)))
    dt_bias = dt + jnp.log(-jnp.expm1(-dt))
    a_log = jnp.log(jax.random.uniform(ks[21], (L, 2, H), f32, 1.0, 16.0))
    return {
        'x': nrm(0, (BATCH, SEQ, D), 1.0),
        'c': nrm(1, (BATCH, D), 1.0),
        'ctx': nrm(2, (BATCH, CTX_LEN, D), 1.0),
        'c_ctx': nrm(3, (D,), 1.0),
        'w_ada': nrm(4, (L, D, 6 * D), 0.5 * D ** -0.5),
        'b_ada': nrm(5, (L, 6 * D), 0.01),
        'norm_mix': gain(6, D),
        'w_in': nrm(7, (L, D, IN_WIDTH), D ** -0.5),
        'q_a_norm': gain(8, MLA_Q_RANK),
        'w_q_b': nrm(9, (L, MLA_Q_RANK, MLA_HEADS * MLA_QK), MLA_Q_RANK ** -0.5),
        'kv_a_norm': gain(10, MLA_KV_RANK),
        'w_kv_b': nrm(11, (L, MLA_KV_RANK, MLA_HEADS * (MLA_NOPE + MLA_V)), MLA_KV_RANK ** -0.5),
        'q_norm': gain(12, MLA_QK),
        'k_norm': gain(13, MLA_QK),
        'w_o_mla': nrm(14, (L, MLA_HEADS * MLA_V, D), (MLA_HEADS * MLA_V) ** -0.5),
        'dn_conv': nrm(15, (L, DN_CONV, DN_HEADS * (2 * DN_DK + DN_DV)), DN_CONV ** -0.5),
        'dn_a_log': a_log,
        'dn_dt_bias': dt_bias,
        'dn_o_norm': gain(16, DN_DV),
        'w_o_dn': nrm(17, (L, DN_HEADS * DN_DV, D), (DN_HEADS * DN_DV) ** -0.5),
        'w_out': nrm(18, (L, D, D), D ** -0.5),
        'norm_ffn': gain(19, D),
        'w_ffn_up': nrm(22, (L, D, 2 * FFN_HIDDEN), D ** -0.5),
        'ffn_conv': nrm(23, (L, FFN_CONV, FFN_HIDDEN), FFN_CONV ** -0.5),
        'w_ffn_down': nrm(24, (L, FFN_HIDDEN, D), FFN_HIDDEN ** -0.5),
    }


def reference(x, c, ctx, c_ctx, w_ada, b_ada, norm_mix, w_in, q_a_norm, w_q_b, kv_a_norm, w_kv_b,
              q_norm, k_norm, w_o_mla, dn_conv, dn_a_log, dn_dt_bias, dn_o_norm, w_o_dn, w_out,
              norm_ffn, w_ffn_up, ffn_conv, w_ffn_down):
    b, t, _ = x.shape
    rows = t // GRID_W
    tabs = axial_rope_tables(rows, x.dtype)
    s_zero = jnp.zeros((2, b, DN_HEADS, DN_DK, DN_DV), jnp.float32)
    for l in range(DEPTH):
        last = l == DEPTH - 1
        sh1, sc1, g1, sh2, sc2, g2 = jnp.split(jax.nn.silu(c) @ w_ada[l] + b_ada[l], 6, axis=-1)
        csh1, csc1, cg1, csh2, csc2, cg2 = jnp.split(jax.nn.silu(c_ctx) @ w_ada[l] + b_ada[l], 6, axis=-1)

        (lq_a, lkv_a, lk_r, ldq, ldk, ldv, ldz, lda, ldb, lga, lgb) = split_columns(
            modulate(rmsnorm(x, norm_mix[l]), sh1, sc1) @ w_in[l])
        (cq_a, ckv_a, ck_r, cdq, cdk, cdv, cdz, cda, cdb, cga, cgb) = split_columns(
            modulate(rmsnorm(ctx, norm_mix[l]), csh1, csc1) @ w_in[l])

        k_c, v_c = mla_kv(ckv_a, ck_r, kv_a_norm[l], w_kv_b[l], k_norm[l], None)
        k_l, v_l = mla_kv(lkv_a, lk_r, kv_a_norm[l], w_kv_b[l], k_norm[l], tabs)
        q_l = mla_q(lq_a, q_a_norm[l], w_q_b[l], q_norm[l], tabs)
        y_mla_l = attend_latent(q_l, k_l, v_l, k_c, v_c)

        cq, ck, cv, cgd, cbeta = dn_prepare(cdq, cdk, cdv, cda, cdb, dn_conv[l], dn_a_log[l], dn_dt_bias[l], not last)
        o_dn_c, s_ctx = dn_bidirectional(cq, ck, cv, cgd, cbeta, s_zero, not last)
        lq, lk, lv, lgd, lbeta = dn_prepare(ldq, ldk, ldv, lda, ldb, dn_conv[l], dn_a_log[l], dn_dt_bias[l], True)
        o_dn_l, _ = dn_bidirectional(lq, lk, lv, lgd, lbeta, s_ctx, True)

        y_l = merge_branches(y_mla_l, dn_output(o_dn_l, ldz, dn_o_norm[l], x.dtype), lga, lgb,
                             w_o_mla[l], w_o_dn[l], w_out[l])

        if not last:
            q_c = mla_q(cq_a, q_a_norm[l], w_q_b[l], q_norm[l], None)
            y_c = merge_branches(attend_dense(q_c, k_c, v_c), dn_output(o_dn_c, cdz, dn_o_norm[l], ctx.dtype),
                                 cga, cgb, w_o_mla[l], w_o_dn[l], w_out[l])
            ctx = ctx + jnp.expand_dims(cg1, -2) * y_c
            ctx = ctx + jnp.expand_dims(cg2, -2) * conv_glu(
                modulate(rmsnorm(ctx, norm_ffn[l]), csh2, csc2), w_ffn_up[l], ffn_conv[l], w_ffn_down[l])

        x = x + jnp.expand_dims(g1, -2) * y_l

        x = x + jnp.expand_dims(g2, -2) * conv_glu(
            modulate(rmsnorm(x, norm_ffn[l]), sh2, sc2), w_ffn_up[l], ffn_conv[l], w_ffn_down[l])
    return x
```

```python
import functools
import math

import jax
import jax.numpy as jnp
import numpy as np
from jax import lax
from jax.experimental import pallas as pl
from jax.experimental.pallas import tpu as pltpu

F32 = jnp.float32
BF16 = jnp.bfloat16
HIGHEST = lax.Precision.HIGHEST

EPS = 1e-6
GRID_W = 64
ROPE_BASE = 10000.0

MLA_HEADS = 8
MLA_Q_RANK = 512
MLA_KV_RANK = 512
MLA_NOPE = 128
MLA_ROPE = 64
MLA_V = 128
MLA_QK = MLA_NOPE + MLA_ROPE
MLA_QK_PAD = 256

DN_HEADS = 8
DN_DK = 128
DN_DV = 128
DN_CHUNK = 64

LANE = 128
ROPE_PAD = 128
VMEM_LIMIT = 56 * 1024 * 1024

A_QA, A_KVA, A_KR, A_DA, A_DB, A_WIDTH = 0, 512, 1024, 1152, 1168, 1280
B_DQ, B_DK, B_DV, B_DZ, B_GA, B_GB, B_WIDTH = 0, 1024, 2048, 3072, 4096, 6144, 8192

FFN_TILE = 512
HALO16 = 16


def _params(sem, vmem=VMEM_LIMIT):
    return pltpu.CompilerParams(dimension_semantics=sem, vmem_limit_bytes=vmem)


def _nt_dot(a, b):
    return lax.dot_general(a, b, (((1,), (1,)), ((), ())), preferred_element_type=F32)


def _silu(x):
    return x * jax.nn.sigmoid(x)


def _ada_kernel(c_ref, w_ref, b_ref, o_ref):
    a = _silu(c_ref[...]).astype(BF16)
    o_ref[...] = jnp.dot(a, w_ref[...].astype(BF16), preferred_element_type=F32) + b_ref[...]


def _ada(cs, w, b):
    m, d = cs.shape
    n = w.shape[1]
    tn = 512
    return pl.pallas_call(
        _ada_kernel,
        grid=(n // tn,),
        in_specs=[pl.BlockSpec((m, d), lambda j: (0, 0)),
                  pl.BlockSpec((d, tn), lambda j: (0, j)),
                  pl.BlockSpec((1, tn), lambda j: (0, j))],
        out_specs=pl.BlockSpec((m, tn), lambda j: (0, j)),
        out_shape=jax.ShapeDtypeStruct((m, n), F32),
        compiler_params=_params(("parallel",)),
        name="ada",
    )(cs, w, b)


def _norm_mod(x, nw, shift, scale):
    ms = jnp.mean(x * x, axis=-1, keepdims=True)
    y = x * lax.rsqrt(ms + EPS) * nw
    return y * (1.0 + scale) + shift


def _nmm_kernel(x_ref, nw_ref, mod_ref, w_ref, o_ref, xn_ref, *, k_shift, k_scale):
    @pl.when(pl.program_id(1) == 0)
    def _():
        y = _norm_mod(x_ref[...], nw_ref[...], mod_ref[0, k_shift:k_shift + 1, :],
                      mod_ref[0, k_scale:k_scale + 1, :])
        xn_ref[...] = y.astype(BF16)

    o_ref[...] = jnp.dot(xn_ref[...], w_ref[...], preferred_element_type=F32).astype(o_ref.dtype)


def _norm_mod_matmul(x, nw, mod, row_of_tile, w, out_dtype, tm, tn, k_shift, k_scale, name):
    m, d = x.shape
    n = w.shape[1]
    kern = functools.partial(_nmm_kernel, k_shift=k_shift, k_scale=k_scale)
    return pl.pallas_call(
        kern,
        grid=(m // tm, n // tn),
        in_specs=[pl.BlockSpec((tm, d), lambda i, j: (i, 0)),
                  pl.BlockSpec((1, d), lambda i, j: (0, 0)),
                  pl.BlockSpec((1, 6, d), lambda i, j: (row_of_tile(i), 0, 0)),
                  pl.BlockSpec((d, tn), lambda i, j: (0, j))],
        out_specs=pl.BlockSpec((tm, tn), lambda i, j: (i, j)),
        out_shape=jax.ShapeDtypeStruct((m, n), out_dtype),
        scratch_shapes=[pltpu.VMEM((tm, d), BF16)],
        compiler_params=_params(("parallel", "arbitrary")),
        name=name,
    )(x, nw, mod, w)


def _mla_proj_kernel(a_ref, cos_ref, sin_ref, qan_ref, kvan_ref, qn_ref, kn_ref, wq_ref, wkv_ref,
                     q_ref, k_ref, v_ref):
    cos = cos_ref[...]
    sin = sin_ref[...]

    def rope(r):
        return r * cos + pltpu.roll(r, 64, axis=1) * sin

    def rms_rows(t, w):
        return t * lax.rsqrt(jnp.mean(t * t, axis=-1, keepdims=True) + EPS) * w

    qa = rms_rows(a_ref[:, A_QA:A_QA + MLA_Q_RANK], qan_ref[...]).astype(BF16)
    q = jnp.dot(qa, wq_ref[...], preferred_element_type=F32)
    kva = rms_rows(a_ref[:, A_KVA:A_KVA + MLA_KV_RANK], kvan_ref[...]).astype(BF16)
    kv = jnp.dot(kva, wkv_ref[...], preferred_element_type=F32)
    kr = a_ref[:, A_KR:A_KR + ROPE_PAD]
    kr_ss = jnp.sum(kr * kr, axis=-1, keepdims=True)
    qn = qn_ref[...]
    kn = kn_ref[...]
    scale = MLA_QK ** -0.5
    for h in range(MLA_HEADS):
        qh = q[:, h * MLA_QK_PAD:(h + 1) * MLA_QK_PAD]
        inv = lax.rsqrt(jnp.sum(qh * qh, axis=-1, keepdims=True) * (1.0 / MLA_QK) + EPS) * scale
        qh = qh * inv * qn
        q_ref[:, h * MLA_QK_PAD:h * MLA_QK_PAD + MLA_NOPE] = qh[:, :MLA_NOPE].astype(BF16)
        q_ref[:, h * MLA_QK_PAD + MLA_NOPE:(h + 1) * MLA_QK_PAD] = rope(qh[:, MLA_NOPE:]).astype(BF16)

        kh = kv[:, h * MLA_NOPE:(h + 1) * MLA_NOPE]
        inv = lax.rsqrt((jnp.sum(kh * kh, axis=-1, keepdims=True) + kr_ss) * (1.0 / MLA_QK) + EPS)
        k_ref[:, h * MLA_QK_PAD:h * MLA_QK_PAD + MLA_NOPE] = (kh * inv * kn[:, :MLA_NOPE]).astype(BF16)
        k_ref[:, h * MLA_QK_PAD + MLA_NOPE:(h + 1) * MLA_QK_PAD] = rope(
            kr * inv * kn[:, MLA_NOPE:]).astype(BF16)
    v_ref[...] = kv[:, MLA_HEADS * MLA_NOPE:].astype(BF16)


def _mla_proj(a, cos, sin, qan, kvan, qn, kn, wq, wkv, tm, seq):
    m = a.shape[0]
    nseq = seq // tm
    full = lambda arr: pl.BlockSpec(arr.shape, lambda i: (0, 0))
    return pl.pallas_call(
        _mla_proj_kernel,
        grid=(m // tm,),
        in_specs=[pl.BlockSpec((tm, A_WIDTH), lambda i: (i, 0)),
                  pl.BlockSpec((tm, ROPE_PAD), lambda i: (i % nseq, 0)),
                  pl.BlockSpec((tm, ROPE_PAD), lambda i: (i % nseq, 0)),
                  full(qan), full(kvan), full(qn), full(kn), full(wq), full(wkv)],
        out_specs=[pl.BlockSpec((tm, MLA_HEADS * MLA_QK_PAD), lambda i: (i, 0)),
                   pl.BlockSpec((tm, MLA_HEADS * MLA_QK_PAD), lambda i: (i, 0)),
                   pl.BlockSpec((tm, MLA_HEADS * MLA_V), lambda i: (i, 0))],
        out_shape=[jax.ShapeDtypeStruct((m, MLA_HEADS * MLA_QK_PAD), BF16),
                   jax.ShapeDtypeStruct((m, MLA_HEADS * MLA_QK_PAD), BF16),
                   jax.ShapeDtypeStruct((m, MLA_HEADS * MLA_V), BF16)],
        compiler_params=_params(("parallel",)),
        name="mla_proj",
    )(a, cos, sin, qan, kvan, qn, kn, wq, wkv)


def _attn_kernel(q_ref, kc_ref, vc_ref, k_ref, v_ref, o_ref, *, tk, nk):
    q = q_ref[0]
    tq = q.shape[0]

    def chunk(k, v, carry):
        m, l, acc = carry
        s = _nt_dot(q, k)
        m_new = jnp.maximum(m, jnp.max(s, axis=-1, keepdims=True))
        alpha = jnp.exp(m - m_new)
        p = jnp.exp(s - m_new)
        l = alpha * l + jnp.sum(p, axis=-1, keepdims=True)
        acc = alpha * acc + jnp.dot(p.astype(BF16), v, preferred_element_type=F32)
        return m_new, l, acc

    init = (jnp.full((tq, 1), -jnp.inf, F32), jnp.zeros((tq, 1), F32), jnp.zeros((tq, MLA_V), F32))
    carry = chunk(kc_ref[0], vc_ref[0], init)

    def body(j, c):
        off = pl.multiple_of(j * tk, tk)
        return chunk(k_ref[0, pl.ds(off, tk), :], v_ref[0, pl.ds(off, tk), :], c)

    _, l, acc = lax.fori_loop(0, nk, body, carry)
    o_ref[0] = (acc / l).astype(o_ref.dtype)


def _attention(q, k, v, kc, vc, tq, tk):
    b, t, _ = q.shape
    lc = kc.shape[1]
    kern = functools.partial(_attn_kernel, tk=tk, nk=t // tk)
    return pl.pallas_call(
        kern,
        grid=(b, MLA_HEADS, t // tq),
        in_specs=[pl.BlockSpec((1, tq, MLA_QK_PAD), lambda bi, h, i: (bi, i, h)),
                  pl.BlockSpec((1, lc, MLA_QK_PAD), lambda bi, h, i: (bi, 0, h)),
                  pl.BlockSpec((1, lc, MLA_V), lambda bi, h, i: (bi, 0, h)),
                  pl.BlockSpec((1, t, MLA_QK_PAD), lambda bi, h, i: (bi, 0, h)),
                  pl.BlockSpec((1, t, MLA_V), lambda bi, h, i: (bi, 0, h))],
        out_specs=pl.BlockSpec((1, tq, MLA_V), lambda bi, h, i: (bi, i, h)),
        out_shape=jax.ShapeDtypeStruct((b, t, MLA_HEADS * MLA_V), BF16),
        compiler_params=_params(("parallel", "parallel", "arbitrary")),
        name="attention",
    )(q, kc, vc, k, v)


def _tri_inverse(m):
    c = m.shape[0]
    rows = lax.broadcasted_iota(jnp.int32, (c, c), 0)
    cols = lax.broadcasted_iota(jnp.int32, (c, c), 1)
    n = -m
    acc = jnp.where(rows == cols, 1.0, 0.0) + n
    steps = int(math.log2(c)) - 1
    for _ in range(steps):
        n = jnp.dot(n, n, precision=HIGHEST, preferred_element_type=F32)
        acc = acc + jnp.dot(acc, n, precision=HIGHEST, preferred_element_type=F32)
    return acc


def _dn_prep_kernel(qm_ref, qp_ref, qx_ref, km_ref, kp_ref, kx_ref, vm_ref, vp_ref, vx_ref,
                    cwq_ref, cwk_ref, cwv_ref, ab_ref, hp_ref,
                    u_ref, w_ref, ke_ref, qg_ref, a_ref, ge_ref, ext_ref, *, tm, tiles_per_seq):
    i = pl.program_id(0)
    first = (i % tiles_per_seq) == 0
    last = (i % tiles_per_seq) == tiles_per_seq - 1
    c = DN_CHUNK

    def conv_silu(main_ref, prev_ref, next_ref, cw_ref):
        ext_ref[0:HALO16, :] = jnp.where(first, 0.0, prev_ref[...].astype(F32))
        ext_ref[HALO16:HALO16 + tm, :] = main_ref[...].astype(F32)
        ext_ref[HALO16 + tm:, :] = jnp.where(last, 0.0, next_ref[...].astype(F32))
        y = (cw_ref[0:1, :] * ext_ref[HALO16 - 1:HALO16 - 1 + tm, :]
             + cw_ref[1:2, :] * ext_ref[HALO16:HALO16 + tm, :]
             + cw_ref[2:3, :] * ext_ref[HALO16 + 1:HALO16 + 1 + tm, :])
        return _silu(y)

    def l2n(t):
        return t * lax.rsqrt(jnp.sum(t * t, axis=-1, keepdims=True) + EPS)

    q = l2n(conv_silu(qm_ref, qp_ref, qx_ref, cwq_ref)) * (DN_DK ** -0.5)
    k = l2n(conv_silu(km_ref, kp_ref, kx_ref, cwk_ref))
    v = conv_silu(vm_ref, vp_ref, vx_ref, cwv_ref)

    hp = hp_ref[0]
    ab = ab_ref[0]
    z = ab[:, 0:2] + hp[:, 2:4]
    softplus = jnp.maximum(z, 0.0) + jnp.log(1.0 + jnp.exp(-jnp.abs(z)))
    g_all = -jnp.exp(hp[:, 0:2]) * softplus
    beta_all = jax.nn.sigmoid(ab[:, 2:4])

    rows = lax.broadcasted_iota(jnp.int32, (c, c), 0)
    cols = lax.broadcasted_iota(jnp.int32, (c, c), 1)
    eye = rows == cols
    for d in range(2):
        incl = (cols <= rows) if d == 0 else (cols >= rows)
        strict = (cols < rows) if d == 0 else (cols > rows)
        tri = jnp.where(incl, 1.0, 0.0)
        for ci in range(tm // c):
            r0 = ci * c
            kc = k[r0:r0 + c]
            qc = q[r0:r0 + c]
            vc = v[r0:r0 + c]
            g = jnp.broadcast_to(g_all[r0:r0 + c, d:d + 1], (c, LANE))
            beta = beta_all[r0:r0 + c, d:d + 1]
            gc = jnp.dot(tri, g, precision=HIGHEST, preferred_element_type=F32)
            gc_sq = gc[:, :c]
            gc_row = jnp.sum(jnp.where(eye, gc_sq, 0.0), axis=0, keepdims=True)
            diff = jnp.where(incl, gc_sq - gc_row, 0.0)
            decay = jnp.where(incl, jnp.exp(diff), 0.0)
            gtot = gc[c - 1:c, :] if d == 0 else gc[0:1, :]
            eg = jnp.exp(gc)
            kb = kc * beta
            kcb = kc.astype(BF16)
            m = jnp.where(strict, _nt_dot(kb.astype(BF16), kcb) * decay, 0.0)
            tinv = _tri_inverse(m)
            rhs = jnp.concatenate([vc * beta, kb * eg], axis=1)
            sol = jnp.dot(tinv, rhs, precision=HIGHEST, preferred_element_type=F32)
            u_ref[d, 0, 0, r0:r0 + c, :] = sol[:, :DN_DV]
            w_ref[d, 0, 0, r0:r0 + c, :] = sol[:, DN_DV:].astype(BF16)
            ke_ref[d, 0, 0, r0:r0 + c, :] = (kc * jnp.exp(gtot - gc)).astype(BF16)
            qg_ref[d, 0, 0, r0:r0 + c, :] = (qc * eg).astype(BF16)
            a_ref[d, 0, 0, r0:r0 + c, :] = (_nt_dot(qc.astype(BF16), kcb) * decay).astype(BF16)
            ge_ref[d, 0, 0, ci, :, :] = jnp.exp(gtot)


def _dn_prep(p, ab, hp, conv_w, batch, seq, tm):
    h = DN_HEADS
    nt = seq // tm
    nb16 = tm // HALO16
    total16 = batch * seq // HALO16
    kern = functools.partial(_dn_prep_kernel, tm=tm, tiles_per_seq=nt)

    def triple(col0):
        return [pl.BlockSpec((tm, LANE), lambda i, hh: (i, col0 + hh)),
                pl.BlockSpec((HALO16, LANE), lambda i, hh: (jnp.maximum(i * nb16 - 1, 0), col0 + hh)),
                pl.BlockSpec((HALO16, LANE), lambda i, hh: (jnp.minimum((i + 1) * nb16, total16 - 1), col0 + hh))]

    cq, ck, cv = B_DQ // LANE, B_DK // LANE, B_DV // LANE
    row = lambda shape: pl.BlockSpec(shape, lambda i, hh: (0, i // nt, hh, i % nt, 0))
    seq_shape = lambda width, dt: jax.ShapeDtypeStruct((2, batch, h, seq, width), dt)
    return pl.pallas_call(
        kern,
        grid=(batch * nt, h),
        in_specs=triple(cq) + triple(ck) + triple(cv) + [
            pl.BlockSpec((3, LANE), lambda i, hh: (0, hh)),
            pl.BlockSpec((3, LANE), lambda i, hh: (0, h + hh)),
            pl.BlockSpec((3, LANE), lambda i, hh: (0, 2 * h + hh)),
            pl.BlockSpec((1, tm, 4), lambda i, hh: (hh, i, 0)),
            pl.BlockSpec((1, 1, 4), lambda i, hh: (hh, 0, 0))],
        out_specs=[row((2, 1, 1, tm, DN_DV)), row((2, 1, 1, tm, DN_DK)), row((2, 1, 1, tm, DN_DK)),
                   row((2, 1, 1, tm, DN_DK)), row((2, 1, 1, tm, DN_CHUNK)),
                   pl.BlockSpec((2, 1, 1, tm // DN_CHUNK, 1, LANE), lambda i, hh: (0, i // nt, hh, i % nt, 0, 0))],
        out_shape=[seq_shape(DN_DV, F32), seq_shape(DN_DK, BF16), seq_shape(DN_DK, BF16),
                   seq_shape(DN_DK, BF16), seq_shape(DN_CHUNK, BF16),
                   jax.ShapeDtypeStruct((2, batch, h, seq // DN_CHUNK, 1, LANE), F32)],
        scratch_shapes=[pltpu.VMEM((tm + 2 * HALO16, LANE), F32)],
        compiler_params=_params(("parallel", "parallel")),
        name="dn_prep",
    )(p, p, p, p, p, p, p, p, p, conv_w, conv_w, conv_w, ab, hp)


def _dn_scan_kernel(s0_ref, uf_ref, wf_ref, kf_ref, qf_ref, af_ref, gf_ref,
                    ub_ref, wb_ref, kb_ref, qb_ref, ab_ref, gb_ref,
                    of_ref, ob_ref, sfin_ref, sf_ref, sb_ref, *, cs):
    n = pl.program_id(1)
    c = DN_CHUNK

    @pl.when(n == 0)
    def _():
        sf_ref[...] = s0_ref[0, 0]
        sb_ref[...] = s0_ref[1, 0]

    def step(s_ref, u_ref, w_ref, k_ref, q_ref, a_ref, g_ref, o_ref, ci):
        r0 = ci * c
        s = s_ref[...]
        sb16 = s.astype(BF16)
        v_new = u_ref[0, 0, 0, r0:r0 + c, :] - jnp.dot(w_ref[0, 0, 0, r0:r0 + c, :], sb16,
                                                        preferred_element_type=F32)
        vb16 = v_new.astype(BF16)
        o = (jnp.dot(q_ref[0, 0, 0, r0:r0 + c, :], sb16, preferred_element_type=F32)
             + jnp.dot(a_ref[0, 0, 0, r0:r0 + c, :], vb16, preferred_element_type=F32))
        o_ref[0, 0, r0:r0 + c, :] = o
        s_ref[...] = s * g_ref[0, 0, 0, ci] + lax.dot_general(
            k_ref[0, 0, 0, r0:r0 + c, :], vb16, (((0,), (0,)), ((), ())), preferred_element_type=F32)

    for ci in range(cs):
        step(sf_ref, uf_ref, wf_ref, kf_ref, qf_ref, af_ref, gf_ref, of_ref, ci)
        step(sb_ref, ub_ref, wb_ref, kb_ref, qb_ref, ab_ref, gb_ref, ob_ref, cs - 1 - ci)

    @pl.when(n == pl.num_programs(1) - 1)
    def _():
        sfin_ref[0, 0] = sf_ref[...]
        sfin_ref[1, 0] = sb_ref[...]


def _dn_scan(s0, u, w, ke, qg, a, ge, cs):
    _, batch, h, seq, _ = u.shape
    ts = cs * DN_CHUNK
    ns = seq // ts
    kern = functools.partial(_dn_scan_kernel, cs=cs)

    def specs(d, blk):
        idx = (lambda bh, n: n) if d == 0 else (lambda bh, n: ns - 1 - n)
        seqs = [pl.BlockSpec((1, 1, 1, ts, width), lambda bh, n: (d, bh // h, bh % h, idx(bh, n), 0))
                for width in (DN_DV, DN_DK, DN_DK, DN_DK, DN_CHUNK)]
        return seqs + [pl.BlockSpec((1, 1, 1, cs, 1, LANE), lambda bh, n: (d, bh // h, bh % h, idx(bh, n), 0, 0))]

    o_spec = lambda d: pl.BlockSpec(
        (1, 1, ts, DN_DV), lambda bh, n: (bh // h, bh % h, n if d == 0 else ns - 1 - n, 0))
    return pl.pallas_call(
        kern,
        grid=(batch * h, ns),
        in_specs=[pl.BlockSpec((2, 1, DN_DK, DN_DV), lambda bh, n: (0, bh, 0, 0))]
        + specs(0, None) + specs(1, None),
        out_specs=[o_spec(0), o_spec(1),
                   pl.BlockSpec((2, 1, DN_DK, DN_DV), lambda bh, n: (0, bh, 0, 0))],
        out_shape=[jax.ShapeDtypeStruct((batch, h, seq, DN_DV), F32),
                   jax.ShapeDtypeStruct((batch, h, seq, DN_DV), F32),
                   jax.ShapeDtypeStruct((2, batch * h, DN_DK, DN_DV), F32)],
        scratch_shapes=[pltpu.VMEM((DN_DK, DN_DV), F32), pltpu.VMEM((DN_DK, DN_DV), F32)],
        compiler_params=_params(("parallel", "arbitrary")),
        name="dn_scan",
    )(s0, u, w, ke, qg, a, ge, u, w, ke, qg, a, ge)


def _dn_out_kernel(of_ref, ob_ref, z_ref, nw_ref, y_ref):
    o = of_ref[0, 0] + ob_ref[0, 0]
    y = o * lax.rsqrt(jnp.mean(o * o, axis=-1, keepdims=True) + EPS) * nw_ref[...]
    y_ref[...] = (y * _silu(z_ref[...].astype(F32))).astype(y_ref.dtype)


def _dn_out(o_f, o_b, p, nw, tm):
    batch, h, seq, _ = o_f.shape
    nt = seq // tm
    zc = B_DZ // LANE
    return pl.pallas_call(
        _dn_out_kernel,
        grid=(batch, h, nt),
        in_specs=[pl.BlockSpec((1, 1, tm, DN_DV), lambda b, hh, i: (b, hh, i, 0)),
                  pl.BlockSpec((1, 1, tm, DN_DV), lambda b, hh, i: (b, hh, i, 0)),
                  pl.BlockSpec((tm, LANE), lambda b, hh, i: (b * nt + i, zc + hh)),
                  pl.BlockSpec((1, DN_DV), lambda b, hh, i: (0, 0))],
        out_specs=pl.BlockSpec((tm, LANE), lambda b, hh, i: (b * nt + i, hh)),
        out_shape=jax.ShapeDtypeStruct((batch * seq, h * DN_DV), BF16),
        compiler_params=_params(("parallel", "parallel", "parallel")),
        name="dn_out",
    )(o_f, o_b, p, nw)


def _merge_kernel(x_ref, mod_ref, ym_ref, yd_ref, ga_ref, gb_ref, wm_ref, wd_ref, wo_ref, o_ref, *, k_gate):
    pm = jnp.dot(ym_ref[...], wm_ref[...], preferred_element_type=F32)
    pd = jnp.dot(yd_ref[...], wd_ref[...], preferred_element_type=F32)
    merged = (jax.nn.sigmoid(ga_ref[...].astype(F32)) * pm
              + jax.nn.sigmoid(gb_ref[...].astype(F32)) * pd).astype(BF16)
    y = jnp.dot(merged, wo_ref[...], preferred_element_type=F32)
    o_ref[...] = x_ref[...] + mod_ref[0, k_gate:k_gate + 1, :] * y


def _merge(x, mod, row_of_tile, ym, yd, p, wm, wd, wo, tm, k_gate):
    m, d = x.shape
    kern = functools.partial(_merge_kernel, k_gate=k_gate)
    const = lambda arr: pl.BlockSpec(arr.shape, lambda i: (0, 0), pipeline_mode=pl.Buffered(1))
    return pl.pallas_call(
        kern,
        grid=(m // tm,),
        in_specs=[pl.BlockSpec((tm, d), lambda i: (i, 0)),
                  pl.BlockSpec((1, 6, d), lambda i: (row_of_tile(i), 0, 0)),
                  pl.BlockSpec((tm, ym.shape[1]), lambda i: (i, 0)),
                  pl.BlockSpec((tm, yd.shape[1]), lambda i: (i, 0)),
                  pl.BlockSpec((tm, d), lambda i: (i, B_GA // d)),
                  pl.BlockSpec((tm, d), lambda i: (i, B_GB // d)),
                  const(wm), const(wd), const(wo)],
        out_specs=pl.BlockSpec((tm, d), lambda i: (i, 0)),
        out_shape=jax.ShapeDtypeStruct((m, d), F32),
        compiler_params=_params(("parallel",)),
        name="merge",
    )(x, mod, ym, yd, p, p, wm, wd, wo)


def _ffn_kernel(x_ref, xp_ref, xx_ref, nw_ref, mod_ref, wup_ref, cw_ref, wdn_ref, o_ref, xn_ref, ge_ref,
                *, tm, tiles_per_seq, k_shift, k_scale, k_gate):
    i = pl.program_id(0)
    j = pl.program_id(1)
    th = FFN_TILE

    @pl.when(j == 0)
    def _():
        first = (i % tiles_per_seq) == 0
        last = (i % tiles_per_seq) == tiles_per_seq - 1
        nw = nw_ref[...]
        sh = mod_ref[0, k_shift:k_shift + 1, :]
        sc = mod_ref[0, k_scale:k_scale + 1, :]
        xn_ref[0:HALO16, :] = jnp.where(first, 0.0, _norm_mod(xp_ref[...], nw, sh, sc)).astype(BF16)
        xn_ref[HALO16:HALO16 + tm, :] = _norm_mod(x_ref[...], nw, sh, sc).astype(BF16)
        xn_ref[HALO16 + tm:, :] = jnp.where(last, 0.0, _norm_mod(xx_ref[...], nw, sh, sc)).astype(BF16)

    up = jnp.dot(xn_ref[...], wup_ref[...], preferred_element_type=F32)
    ge_ref[...] = up[:, :th]
    conv = (cw_ref[0:1, :] * ge_ref[HALO16 - 1:HALO16 - 1 + tm, :]
            + cw_ref[1:2, :] * ge_ref[HALO16:HALO16 + tm, :]
            + cw_ref[2:3, :] * ge_ref[HALO16 + 1:HALO16 + 1 + tm, :])
    hid = (_silu(conv) * up[HALO16:HALO16 + tm, th:]).astype(BF16)
    part = jnp.dot(hid, wdn_ref[...], preferred_element_type=F32)

    @pl.when(j == 0)
    def _():
        o_ref[...] = part

    @pl.when(j > 0)
    def _():
        o_ref[...] += part

    @pl.when(j == pl.num_programs(1) - 1)
    def _():
        o_ref[...] = x_ref[...] + mod_ref[0, k_gate:k_gate + 1, :] * o_ref[...]


def _ffn(x, nw, mod, row_of_tile, wup, cw, wdn, seq, tm, k_shift, k_scale, k_gate):
    m, d = x.shape
    th = FFN_TILE
    nj = wdn.shape[0] // th
    nt = seq // tm
    nb16 = tm // HALO16
    total16 = m // HALO16
    kern = functools.partial(_ffn_kernel, tm=tm, tiles_per_seq=nt, k_shift=k_shift, k_scale=k_scale,
                             k_gate=k_gate)
    return pl.pallas_call(
        kern,
        grid=(m // tm, nj),
        in_specs=[pl.BlockSpec((tm, d), lambda i, j: (i, 0)),
                  pl.BlockSpec((HALO16, d), lambda i, j: (jnp.maximum(i * nb16 - 1, 0), 0)),
                  pl.BlockSpec((HALO16, d), lambda i, j: (jnp.minimum((i + 1) * nb16, total16 - 1), 0)),
                  pl.BlockSpec((1, d), lambda i, j: (0, 0)),
                  pl.BlockSpec((1, 6, d), lambda i, j: (row_of_tile(i), 0, 0)),
                  pl.BlockSpec((d, 2 * th), lambda i, j: (0, j)),
                  pl.BlockSpec((3, th), lambda i, j: (0, j)),
                  pl.BlockSpec((th, d), lambda i, j: (j, 0))],
        out_specs=pl.BlockSpec((tm, d), lambda i, j: (i, 0)),
        out_shape=jax.ShapeDtypeStruct((m, d), F32),
        scratch_shapes=[pltpu.VMEM((tm + 2 * HALO16, d), BF16), pltpu.VMEM((tm + 2 * HALO16, th), F32)],
        compiler_params=_params(("parallel", "arbitrary")),
        name="ffn",
    )(x, x, x, nw, mod, wup, cw, wdn)


def _rope_pad_cols(r):
    n = MLA_ROPE // 4
    z = jnp.zeros(r.shape[:-1] + (2 * n,), r.dtype)
    return jnp.concatenate([r[..., 0:n], r[..., 2 * n:3 * n], z, r[..., n:2 * n], r[..., 3 * n:4 * n], z], axis=-1)


def _qk_pad_cols(w):
    lead = w.shape[:-1]
    w = w.reshape(lead + (MLA_HEADS, MLA_QK))
    out = jnp.concatenate([w[..., :MLA_NOPE], _rope_pad_cols(w[..., MLA_NOPE:])], axis=-1)
    return out.reshape(lead + (MLA_HEADS * MLA_QK_PAD,))


def _rope_tables(rows):
    n = MLA_ROPE // 4
    row = jnp.repeat(jnp.arange(rows, dtype=F32), GRID_W)
    col = jnp.tile(jnp.arange(GRID_W, dtype=F32), rows)
    inv = ROPE_BASE ** (-jnp.arange(n, dtype=F32) / n)
    ang_r = row[:, None] * inv
    ang_c = col[:, None] * inv
    t = row.shape[0]
    one = jnp.ones((t, 2 * n), F32)
    zero = jnp.zeros((t, 2 * n), F32)
    cos = jnp.concatenate([jnp.cos(ang_r), jnp.cos(ang_c), one, jnp.cos(ang_r), jnp.cos(ang_c), one], axis=-1)
    sin = jnp.concatenate([-jnp.sin(ang_r), -jnp.sin(ang_c), zero, jnp.sin(ang_r), jnp.sin(ang_c), zero], axis=-1)
    return cos, sin


def _pick_tile(n, pref):
    t = min(pref, n)
    while n % t:
        t //= 2
    return t


def kernel(x, c, ctx, c_ctx, w_ada, b_ada, norm_mix, w_in, q_a_norm, w_q_b, kv_a_norm, w_kv_b, q_norm, k_norm,
           w_o_mla, dn_conv, dn_a_log, dn_dt_bias, dn_o_norm, w_o_dn, w_out, norm_ffn, w_ffn_up, ffn_conv,
           w_ffn_down):
    batch, seq, d = x.shape
    lc = ctx.shape[1]
    assert w_ada.shape[0] == 1, "single-layer stack"
    assert seq % GRID_W == 0 and seq % DN_CHUNK == 0 and lc % DN_CHUNK == 0
    h = DN_HEADS
    l = 0

    wi = w_in[l]
    o_qa, o_kva, o_kr = 0, MLA_Q_RANK, MLA_Q_RANK + MLA_KV_RANK
    o_dq = o_kr + MLA_ROPE
    o_da = o_dq + 4 * h * DN_DK
    o_ga = o_da + 4 * h
    w_a = jnp.concatenate([wi[:, o_qa:o_kr], _rope_pad_cols(wi[:, o_kr:o_dq]), wi[:, o_da:o_ga],
                           jnp.zeros((d, A_WIDTH - A_DB - 2 * h), F32)], axis=1).astype(BF16)
    w_b = jnp.concatenate([wi[:, o_dq:o_da], wi[:, o_ga:]], axis=1).astype(BF16)
    wq = _qk_pad_cols(w_q_b[l]).astype(BF16)
    wkv = w_kv_b[l].reshape(MLA_KV_RANK, MLA_HEADS, MLA_NOPE + MLA_V)
    wkv = jnp.concatenate([wkv[:, :, :MLA_NOPE].reshape(MLA_KV_RANK, -1),
                           wkv[:, :, MLA_NOPE:].reshape(MLA_KV_RANK, -1)], axis=1).astype(BF16)
    qn = jnp.concatenate([q_norm[l, :MLA_NOPE], _rope_pad_cols(q_norm[l, MLA_NOPE:])])[None, :]
    kn = jnp.concatenate([k_norm[l, :MLA_NOPE], _rope_pad_cols(k_norm[l, MLA_NOPE:])])[None, :]
    hidden = w_ffn_down.shape[1]
    nj = -(-hidden // FFN_TILE)
    hpad = nj * FFN_TILE - hidden
    wg = jnp.pad(w_ffn_up[l][:, :hidden], ((0, 0), (0, hpad))).reshape(d, nj, FFN_TILE)
    wv = jnp.pad(w_ffn_up[l][:, hidden:], ((0, 0), (0, hpad))).reshape(d, nj, FFN_TILE)
    wup = jnp.concatenate([wg, wv], axis=2).reshape(d, nj * 2 * FFN_TILE).astype(BF16)
    fcw = jnp.pad(ffn_conv[l], ((0, 0), (0, hpad)))
    wdn = jnp.pad(w_ffn_down[l], ((0, hpad), (0, 0))).astype(BF16)
    hp = jnp.concatenate([dn_a_log[l].T, dn_dt_bias[l].T], axis=1)[:, None, :]
    cos_l, sin_l = _rope_tables(seq // GRID_W)
    cos_c, sin_c = jnp.ones((lc, ROPE_PAD), F32), jnp.zeros((lc, ROPE_PAD), F32)

    rows = jnp.concatenate([c, c_ctx[None, :], jnp.zeros((8 - batch - 1, d), F32)], axis=0)
    mod = _ada(rows, w_ada[l], b_ada[l][None, :]).reshape(8, 6, d)

    xf = x.reshape(batch * seq, d)
    cf = ctx.reshape(batch * lc, d)

    def stream(tokens, n, tm_a, tm_b, row_of):
        nm = norm_mix[l][None, :]
        a = _norm_mod_matmul(tokens, nm, mod, row_of(tm_a), w_a, F32, tm_a, A_WIDTH, 0, 1, "in_proj_small")
        p = _norm_mod_matmul(tokens, nm, mod, row_of(tm_b), w_b, BF16, tm_b, 1024, 0, 1, "in_proj_wide")
        ab = a[:, A_DA:A_DA + 4 * h].reshape(batch * n, 4, h).transpose(2, 0, 1)
        return a, p, ab

    tm_a = _pick_tile(seq, 512)
    tm_b = _pick_tile(seq, 1024)
    lat_row = lambda tm: (lambda i: i // (seq // tm))
    a_l, p_l, ab_l = stream(xf, seq, tm_a, tm_b, lat_row)
    tc = _pick_tile(lc, 256)
    ctx_row = lambda tm: (lambda i: batch)
    a_c, p_c, ab_c = stream(cf, lc, tc, tc, ctx_row)

    qan, kvan = q_a_norm[l][None, :], kv_a_norm[l][None, :]
    tmp = _pick_tile(seq, 512)
    q_l, k_l, v_l = _mla_proj(a_l, cos_l, sin_l, qan, kvan, qn, kn, wq, wkv, tmp, seq)
    _, k_c, v_c = _mla_proj(a_c, cos_c, sin_c, qan, kvan, qn, kn, wq, wkv, tc, lc)
    r3 = lambda t, n: t.reshape(batch, n, t.shape[-1])
    y_mla = _attention(r3(q_l, seq), r3(k_l, seq), r3(v_l, seq), r3(k_c, lc), r3(v_c, lc),
                       _pick_tile(seq, 512), _pick_tile(seq, 512))
    y_mla = y_mla.reshape(batch * seq, MLA_HEADS * MLA_V)

    conv_w = dn_conv[l]
    prep_c = _dn_prep(p_c, ab_c, hp, conv_w, batch, lc, _pick_tile(lc, 256))
    s_zero = jnp.zeros((2, batch * h, DN_DK, DN_DV), F32)
    _, _, s_ctx = _dn_scan(s_zero, *prep_c, _pick_tile(lc, 256) // DN_CHUNK)
    prep_l = _dn_prep(p_l, ab_l, hp, conv_w, batch, seq, _pick_tile(seq, 256))
    o_f, o_b, _ = _dn_scan(s_ctx, *prep_l, _pick_tile(seq, 256) // DN_CHUNK)
    y_dn = _dn_out(o_f, o_b, p_l, dn_o_norm[l][None, :], _pick_tile(seq, 1024))

    tmm = _pick_tile(seq, 256)
    x1 = _merge(xf, mod, lat_row(tmm), y_mla, y_dn, p_l, w_o_mla[l].astype(BF16), w_o_dn[l].astype(BF16),
                w_out[l].astype(BF16), tmm, 2)

    tmf = _pick_tile(seq, 512)
    out = _ffn(x1, norm_ffn[l][None, :], mod, lat_row(tmf), wup, fcw, wdn, seq, tmf, 3, 4, 5)
    return out.reshape(batch, seq, d)
```

```python
import functools
import math

import jax
import jax.numpy as jnp
import numpy as np
from jax import lax
from jax.experimental import pallas as pl
from jax.experimental.pallas import tpu as pltpu

F32 = jnp.float32
BF16 = jnp.bfloat16
HIGHEST = lax.Precision.HIGHEST

EPS = 1e-6
GRID_W = 64
ROPE_BASE = 10000.0

MLA_HEADS = 8
MLA_Q_RANK = 512
MLA_KV_RANK = 512
MLA_NOPE = 128
MLA_ROPE = 64
MLA_V = 128
MLA_QK = MLA_NOPE + MLA_ROPE
MLA_QK_PAD = 256

DN_HEADS = 8
DN_DK = 128
DN_DV = 128
DN_CHUNK = 64

LANE = 128
ROPE_PAD = 128
VMEM_LIMIT = 56 * 1024 * 1024

A_QA, A_KVA, A_KR, A_DA, A_DB, A_WIDTH = 0, 512, 1024, 1152, 1168, 1280
B_DQ, B_DK, B_DV, B_DZ, B_GA, B_GB, B_WIDTH = 0, 1024, 2048, 3072, 4096, 6144, 8192

FFN_TILE = 512
HALO16 = 16


def _params(sem, vmem=VMEM_LIMIT):
    return pltpu.CompilerParams(dimension_semantics=sem, vmem_limit_bytes=vmem)


def _nt_dot(a, b):
    return lax.dot_general(a, b, (((1,), (1,)), ((), ())), preferred_element_type=F32)


def _silu(x):
    return x * jax.nn.sigmoid(x)


def _ada_kernel(c_ref, w_ref, b_ref, o_ref):
    a = _silu(c_ref[...]).astype(BF16)
    o_ref[...] = jnp.dot(a, w_ref[...].astype(BF16), preferred_element_type=F32) + b_ref[...]


def _ada(cs, w, b):
    m, d = cs.shape
    n = w.shape[1]
    tn = 512
    return pl.pallas_call(
        _ada_kernel,
        grid=(n // tn,),
        in_specs=[pl.BlockSpec((m, d), lambda j: (0, 0)),
                  pl.BlockSpec((d, tn), lambda j: (0, j)),
                  pl.BlockSpec((1, tn), lambda j: (0, j))],
        out_specs=pl.BlockSpec((m, tn), lambda j: (0, j)),
        out_shape=jax.ShapeDtypeStruct((m, n), F32),
        compiler_params=_params(("parallel",)),
        name="ada",
    )(cs, w, b)


def _norm_mod(x, nw, shift, scale):
    ms = jnp.mean(x * x, axis=-1, keepdims=True)
    y = x * lax.rsqrt(ms + EPS) * nw
    return y * (1.0 + scale) + shift


def _nmm_kernel(x_ref, nw_ref, mod_ref, w_ref, o_ref, xn_ref, *, k_shift, k_scale):
    @pl.when(pl.program_id(1) == 0)
    def _():
        y = _norm_mod(x_ref[...], nw_ref[...], mod_ref[0, k_shift:k_shift + 1, :],
                      mod_ref[0, k_scale:k_scale + 1, :])
        xn_ref[...] = y.astype(BF16)

    o_ref[...] = jnp.dot(xn_ref[...], w_ref[...], preferred_element_type=F32).astype(o_ref.dtype)


def _norm_mod_matmul(x, nw, mod, row_of_tile, w, out_dtype, tm, tn, k_shift, k_scale, name):
    m, d = x.shape
    n = w.shape[1]
    kern = functools.partial(_nmm_kernel, k_shift=k_shift, k_scale=k_scale)
    return pl.pallas_call(
        kern,
        grid=(m // tm, n // tn),
        in_specs=[pl.BlockSpec((tm, d), lambda i, j: (i, 0)),
                  pl.BlockSpec((1, d), lambda i, j: (0, 0)),
                  pl.BlockSpec((1, 6, d), lambda i, j: (row_of_tile(i), 0, 0)),
                  pl.BlockSpec((d, tn), lambda i, j: (0, j))],
        out_specs=pl.BlockSpec((tm, tn), lambda i, j: (i, j)),
        out_shape=jax.ShapeDtypeStruct((m, n), out_dtype),
        scratch_shapes=[pltpu.VMEM((tm, d), BF16)],
        compiler_params=_params(("parallel", "arbitrary")),
        name=name,
    )(x, nw, mod, w)


def _mla_proj_kernel(a_ref, cos_ref, sin_ref, qan_ref, kvan_ref, qn_ref, kn_ref, wq_ref, wkv_ref,
                     q_ref, k_ref, v_ref):
    cos = cos_ref[...]
    sin = sin_ref[...]

    def rope(r):
        return r * cos + pltpu.roll(r, 64, axis=1) * sin

    def rms_rows(t, w):
        return t * lax.rsqrt(jnp.mean(t * t, axis=-1, keepdims=True) + EPS) * w

    qa = rms_rows(a_ref[:, A_QA:A_QA + MLA_Q_RANK], qan_ref[...]).astype(BF16)
    q = jnp.dot(qa, wq_ref[...], preferred_element_type=F32)
    kva = rms_rows(a_ref[:, A_KVA:A_KVA + MLA_KV_RANK], kvan_ref[...]).astype(BF16)
    kv = jnp.dot(kva, wkv_ref[...], preferred_element_type=F32)
    kr = a_ref[:, A_KR:A_KR + ROPE_PAD]
    kr_ss = jnp.sum(kr * kr, axis=-1, keepdims=True)
    qn = qn_ref[...]
    kn = kn_ref[...]
    scale = MLA_QK ** -0.5 * math.log2(math.e)
    for h in range(MLA_HEADS):
        qh = q[:, h * MLA_QK_PAD:(h + 1) * MLA_QK_PAD]
        inv = lax.rsqrt(jnp.sum(qh * qh, axis=-1, keepdims=True) * (1.0 / MLA_QK) + EPS) * scale
        qh = qh * inv * qn
        q_ref[:, h * MLA_QK_PAD:h * MLA_QK_PAD + MLA_NOPE] = qh[:, :MLA_NOPE].astype(BF16)
        q_ref[:, h * MLA_QK_PAD + MLA_NOPE:(h + 1) * MLA_QK_PAD] = rope(qh[:, MLA_NOPE:]).astype(BF16)

        kh = kv[:, h * MLA_NOPE:(h + 1) * MLA_NOPE]
        inv = lax.rsqrt((jnp.sum(kh * kh, axis=-1, keepdims=True) + kr_ss) * (1.0 / MLA_QK) + EPS)
        k_ref[:, h * MLA_QK_PAD:h * MLA_QK_PAD + MLA_NOPE] = (kh * inv * kn[:, :MLA_NOPE]).astype(BF16)
        k_ref[:, h * MLA_QK_PAD + MLA_NOPE:(h + 1) * MLA_QK_PAD] = rope(
            kr * inv * kn[:, MLA_NOPE:]).astype(BF16)
    v_ref[...] = kv[:, MLA_HEADS * MLA_NOPE:].astype(BF16)


def _mla_proj(a, cos, sin, qan, kvan, qn, kn, wq, wkv, tm, seq):
    m = a.shape[0]
    nseq = seq // tm
    full = lambda arr: pl.BlockSpec(arr.shape, lambda i: (0, 0))
    return pl.pallas_call(
        _mla_proj_kernel,
        grid=(m // tm,),
        in_specs=[pl.BlockSpec((tm, A_WIDTH), lambda i: (i, 0)),
                  pl.BlockSpec((tm, ROPE_PAD), lambda i: (i % nseq, 0)),
                  pl.BlockSpec((tm, ROPE_PAD), lambda i: (i % nseq, 0)),
                  full(qan), full(kvan), full(qn), full(kn), full(wq), full(wkv)],
        out_specs=[pl.BlockSpec((tm, MLA_HEADS * MLA_QK_PAD), lambda i: (i, 0)),
                   pl.BlockSpec((tm, MLA_HEADS * MLA_QK_PAD), lambda i: (i, 0)),
                   pl.BlockSpec((tm, MLA_HEADS * MLA_V), lambda i: (i, 0))],
        out_shape=[jax.ShapeDtypeStruct((m, MLA_HEADS * MLA_QK_PAD), BF16),
                   jax.ShapeDtypeStruct((m, MLA_HEADS * MLA_QK_PAD), BF16),
                   jax.ShapeDtypeStruct((m, MLA_HEADS * MLA_V), BF16)],
        compiler_params=_params(("parallel",)),
        name="mla_proj",
    )(a, cos, sin, qan, kvan, qn, kn, wq, wkv)


ATTN_UNROLL = 4


def _attn_kernel(qt_ref, kc_ref, vct_ref, k_ref, vt_ref, o_ref, s_ref, p_ref, *, nk, unroll):
    qt = qt_ref[0, 0]
    tq = qt.shape[1]
    tk = s_ref.shape[1]

    def softmax_step(s, m, l):
        m_new = jnp.maximum(m, jnp.max(s, axis=0, keepdims=True))
        alpha = jnp.exp2(m - m_new)
        p = jnp.exp2(s - m_new)
        return m_new, alpha * l + jnp.sum(p, axis=0, keepdims=True), alpha, p.astype(BF16)

    def scores(j):
        return jnp.dot(k_ref[0, pl.ds(pl.multiple_of(j * tk, tk), tk), :], qt, preferred_element_type=F32)

    m, l, _, p = softmax_step(jnp.dot(kc_ref[0], qt, preferred_element_type=F32),
                              jnp.full((1, tq), -jnp.inf, F32), jnp.zeros((1, tq), F32))
    acc = jnp.dot(vct_ref[0, 0], p, preferred_element_type=F32)

    s_ref[0] = scores(0)
    p_ref[1] = jnp.zeros((tk, tq), BF16)

    def tile(j, slot, carry):
        m, l, alpha_prev, acc = carry
        s_ref[1 - slot] = scores(jnp.minimum(j + 1, nk - 1))
        m, l, alpha, p = softmax_step(s_ref[slot], m, l)
        acc = alpha_prev * acc + jnp.dot(vt_ref[0, 0, jnp.maximum(j - 1, 0)], p_ref[1 - slot],
                                         preferred_element_type=F32)
        p_ref[slot] = p
        return m, l, alpha, acc

    def body(i, carry):
        for u in range(unroll):
            carry = tile(unroll * i + u, u % 2, carry)
        return carry

    m, l, alpha, acc = lax.fori_loop(0, nk // unroll, body, (m, l, jnp.ones((1, tq), F32), acc))
    acc = alpha * acc + jnp.dot(vt_ref[0, 0, nk - 1], p_ref[1], preferred_element_type=F32)
    o_ref[0] = (acc / l).T.astype(o_ref.dtype)


def _attention(qt, k, vt, kc, vct, tq):
    b, t, _ = k.shape
    lc = kc.shape[1]
    nk, tk = vt.shape[2], vt.shape[4]
    unroll = ATTN_UNROLL if nk % ATTN_UNROLL == 0 else 2
    assert nk % unroll == 0
    kern = functools.partial(_attn_kernel, nk=nk, unroll=unroll)
    return pl.pallas_call(
        kern,
        grid=(b, MLA_HEADS, t // tq),
        in_specs=[pl.BlockSpec((1, 1, MLA_QK_PAD, tq), lambda bi, h, i: (bi, h, 0, i)),
                  pl.BlockSpec((1, lc, MLA_QK_PAD), lambda bi, h, i: (bi, 0, h)),
                  pl.BlockSpec((1, 1, MLA_V, lc), lambda bi, h, i: (bi, h, 0, 0)),
                  pl.BlockSpec((1, t, MLA_QK_PAD), lambda bi, h, i: (bi, 0, h)),
                  pl.BlockSpec((1, 1, nk, MLA_V, tk), lambda bi, h, i: (bi, h, 0, 0, 0))],
        out_specs=pl.BlockSpec((1, tq, MLA_V), lambda bi, h, i: (bi, i, h)),
        out_shape=jax.ShapeDtypeStruct((b, t, MLA_HEADS * MLA_V), BF16),
        scratch_shapes=[pltpu.VMEM((2, tk, tq), F32), pltpu.VMEM((2, tk, tq), BF16)],
        compiler_params=_params(("parallel", "parallel", "arbitrary")),
        name="attention",
    )(qt, kc, vct, k, vt)


def _split2(x):
    hi = x.astype(BF16)
    return hi, (x - hi.astype(F32)).astype(BF16)


def _bmm(a, b):
    return lax.dot_general(a, b, (((2,), (1,)), ((0,), (0,))), preferred_element_type=F32)


def _bmm_nt(a, b):
    return lax.dot_general(a, b, (((2,), (2,)), ((0,), (0,))), preferred_element_type=F32)


def _bmm_hp(a, b):
    ah, al = _split2(a)
    bh, bl = _split2(b)
    return _bmm(ah, bh) + _bmm(al, bh) + _bmm(ah, bl)


def _tri_inverse(m, eye):
    c = m.shape[-1]
    n = -m
    x = eye + n
    for _ in range(int(math.log2(c)) - 1):
        nb = n.astype(BF16)
        n = _bmm(nb, nb)
        x = x + _bmm(x.astype(BF16), n.astype(BF16))
    for _ in range(2):
        r = eye - x - _bmm_hp(m, x)
        x = x + _bmm(x.astype(BF16), r.astype(BF16))
    return x


def _dn_prep_kernel(qm_ref, qp_ref, qx_ref, km_ref, kp_ref, kx_ref, vm_ref, vp_ref, vx_ref,
                    cwq_ref, cwk_ref, cwv_ref, ab_ref, hp_ref,
                    u_ref, w_ref, ke_ref, qg_ref, a_ref, ge_ref, ext_ref, *, tm, tiles_per_seq):
    i = pl.program_id(0)
    first = (i % tiles_per_seq) == 0
    last = (i % tiles_per_seq) == tiles_per_seq - 1
    c = DN_CHUNK

    def conv_silu(main_ref, prev_ref, next_ref, cw_ref):
        ext_ref[0:HALO16, :] = jnp.where(first, 0.0, prev_ref[...].astype(F32))
        ext_ref[HALO16:HALO16 + tm, :] = main_ref[...].astype(F32)
        ext_ref[HALO16 + tm:, :] = jnp.where(last, 0.0, next_ref[...].astype(F32))
        y = (cw_ref[0:1, :] * ext_ref[HALO16 - 1:HALO16 - 1 + tm, :]
             + cw_ref[1:2, :] * ext_ref[HALO16:HALO16 + tm, :]
             + cw_ref[2:3, :] * ext_ref[HALO16 + 1:HALO16 + 1 + tm, :])
        return _silu(y)

    def l2n(t):
        return t * lax.rsqrt(jnp.sum(t * t, axis=-1, keepdims=True) + EPS)

    q = l2n(conv_silu(qm_ref, qp_ref, qx_ref, cwq_ref)) * (DN_DK ** -0.5)
    k = l2n(conv_silu(km_ref, kp_ref, kx_ref, cwk_ref))
    v = conv_silu(vm_ref, vp_ref, vx_ref, cwv_ref)

    hp = hp_ref[0]
    ab = ab_ref[0]
    z = ab[:, 0:2] + hp[:, 2:4]
    softplus = jnp.maximum(z, 0.0) + jnp.log(1.0 + jnp.exp(-jnp.abs(z)))
    g_all = -jnp.exp(hp[:, 0:2]) * softplus
    beta_all = jax.nn.sigmoid(ab[:, 2:4])

    nc = tm // c

    def per_chunk(t):
        return t.reshape(nc, c, t.shape[-1])

    def both(t):
        return jnp.concatenate([t, t], axis=0)

    def by_dir(t2):
        return jnp.concatenate([per_chunk(jnp.broadcast_to(t2[:, d:d + 1], (tm, LANE))) for d in range(2)],
                               axis=0)

    shape = (2 * nc, c, c)
    fwd = lax.broadcasted_iota(jnp.int32, shape, 0) < nc
    rows = lax.broadcasted_iota(jnp.int32, shape, 1)
    cols = lax.broadcasted_iota(jnp.int32, shape, 2)
    eye_mask = rows == cols
    ahead = jnp.where(fwd, cols - rows, rows - cols)
    incl = ahead <= 0
    strict = ahead < 0
    eye = jnp.where(eye_mask, 1.0, 0.0)
    tri = jnp.where(incl, 1.0, 0.0).astype(BF16)

    g = by_dir(g_all)
    g_hi = g.astype(BF16)
    g_r = g - g_hi.astype(F32)
    g_mid = g_r.astype(BF16)
    g_lo = (g_r - g_mid.astype(F32)).astype(BF16)
    gc = _bmm(tri, g_hi) + _bmm(tri, g_mid) + _bmm(tri, g_lo)
    gc_sq = gc[:, :, :c]
    gc_row = jnp.sum(jnp.where(eye_mask, gc_sq, 0.0), axis=1, keepdims=True)
    decay = jnp.where(incl, jnp.exp(jnp.where(incl, gc_sq - gc_row, 0.0)), 0.0)
    gtot = jnp.concatenate([gc[:nc, c - 1:c, :], gc[nc:, 0:1, :]], axis=0)
    eg = jnp.exp(gc)

    beta = by_dir(beta_all)
    k3 = per_chunk(k)
    q3 = per_chunk(q)
    k3b = k3.astype(BF16)
    kk = both(_bmm_nt(k3b, k3b))
    qk = both(_bmm_nt(q3.astype(BF16), k3b))
    k2 = both(k3)
    kb = k2 * beta
    m = jnp.where(strict, beta[:, :, :c] * kk * decay, 0.0)
    tinv = _tri_inverse(m, eye)
    sol = _bmm_hp(tinv, jnp.concatenate([both(per_chunk(v)) * beta, kb * eg], axis=2))
    ke = k2 * jnp.exp(gtot - gc)
    qg = both(q3) * eg
    att = qk * decay
    gend = jnp.exp(gtot)
    for d in range(2):
        sl = slice(d * nc, (d + 1) * nc)
        u_ref[d, 0, 0] = sol[sl, :, :DN_DV].reshape(tm, DN_DV)
        w_ref[d, 0, 0] = sol[sl, :, DN_DV:].reshape(tm, DN_DK).astype(BF16)
        ke_ref[d, 0, 0] = ke[sl].reshape(tm, DN_DK).astype(BF16)
        qg_ref[d, 0, 0] = qg[sl].reshape(tm, DN_DK).astype(BF16)
        a_ref[d, 0, 0] = att[sl].reshape(tm, c).astype(BF16)
        ge_ref[d, 0, 0] = gend[sl]


def _dn_prep(p, ab, hp, conv_w, batch, seq, tm):
    h = DN_HEADS
    nt = seq // tm
    nb16 = tm // HALO16
    total16 = batch * seq // HALO16
    kern = functools.partial(_dn_prep_kernel, tm=tm, tiles_per_seq=nt)

    def triple(col0):
        return [pl.BlockSpec((tm, LANE), lambda i, hh: (i, col0 + hh)),
                pl.BlockSpec((HALO16, LANE), lambda i, hh: (jnp.maximum(i * nb16 - 1, 0), col0 + hh)),
                pl.BlockSpec((HALO16, LANE), lambda i, hh: (jnp.minimum((i + 1) * nb16, total16 - 1), col0 + hh))]

    cq, ck, cv = B_DQ // LANE, B_DK // LANE, B_DV // LANE
    row = lambda shape: pl.BlockSpec(shape, lambda i, hh: (0, i // nt, hh, i % nt, 0))
    seq_shape = lambda width, dt: jax.ShapeDtypeStruct((2, batch, h, seq, width), dt)
    return pl.pallas_call(
        kern,
        grid=(batch * nt, h),
        in_specs=triple(cq) + triple(ck) + triple(cv) + [
            pl.BlockSpec((3, LANE), lambda i, hh: (0, hh)),
            pl.BlockSpec((3, LANE), lambda i, hh: (0, h + hh)),
            pl.BlockSpec((3, LANE), lambda i, hh: (0, 2 * h + hh)),
            pl.BlockSpec((1, tm, 4), lambda i, hh: (hh, i, 0)),
            pl.BlockSpec((1, 1, 4), lambda i, hh: (hh, 0, 0))],
        out_specs=[row((2, 1, 1, tm, DN_DV)), row((2, 1, 1, tm, DN_DK)), row((2, 1, 1, tm, DN_DK)),
                   row((2, 1, 1, tm, DN_DK)), row((2, 1, 1, tm, DN_CHUNK)),
                   pl.BlockSpec((2, 1, 1, tm // DN_CHUNK, 1, LANE), lambda i, hh: (0, i // nt, hh, i % nt, 0, 0))],
        out_shape=[seq_shape(DN_DV, F32), seq_shape(DN_DK, BF16), seq_shape(DN_DK, BF16),
                   seq_shape(DN_DK, BF16), seq_shape(DN_CHUNK, BF16),
                   jax.ShapeDtypeStruct((2, batch, h, seq // DN_CHUNK, 1, LANE), F32)],
        scratch_shapes=[pltpu.VMEM((tm + 2 * HALO16, LANE), F32)],
        compiler_params=_params(("parallel", "parallel")),
        name="dn_prep",
    )(p, p, p, p, p, p, p, p, p, conv_w, conv_w, conv_w, ab, hp)


def _dn_scan_kernel(s0_ref, uf_ref, wf_ref, kf_ref, qf_ref, af_ref, gf_ref,
                    ub_ref, wb_ref, kb_ref, qb_ref, ab_ref, gb_ref,
                    of_ref, ob_ref, sfin_ref, sf_ref, sb_ref, *, cs):
    n = pl.program_id(1)
    c = DN_CHUNK

    @pl.when(n == 0)
    def _():
        sf_ref[...] = s0_ref[0, 0]
        sb_ref[...] = s0_ref[1, 0]

    def step(s_ref, u_ref, w_ref, k_ref, q_ref, a_ref, g_ref, o_ref, ci):
        r0 = ci * c
        s = s_ref[...]
        sb16 = s.astype(BF16)
        v_new = u_ref[0, 0, 0, r0:r0 + c, :] - jnp.dot(w_ref[0, 0, 0, r0:r0 + c, :], sb16,
                                                        preferred_element_type=F32)
        vb16 = v_new.astype(BF16)
        o = (jnp.dot(q_ref[0, 0, 0, r0:r0 + c, :], sb16, preferred_element_type=F32)
             + jnp.dot(a_ref[0, 0, 0, r0:r0 + c, :], vb16, preferred_element_type=F32))
        o_ref[0, 0, r0:r0 + c, :] = o
        s_ref[...] = s * g_ref[0, 0, 0, ci] + lax.dot_general(
            k_ref[0, 0, 0, r0:r0 + c, :], vb16, (((0,), (0,)), ((), ())), preferred_element_type=F32)

    for ci in range(cs):
        step(sf_ref, uf_ref, wf_ref, kf_ref, qf_ref, af_ref, gf_ref, of_ref, ci)
        step(sb_ref, ub_ref, wb_ref, kb_ref, qb_ref, ab_ref, gb_ref, ob_ref, cs - 1 - ci)

    @pl.when(n == pl.num_programs(1) - 1)
    def _():
        sfin_ref[0, 0] = sf_ref[...]
        sfin_ref[1, 0] = sb_ref[...]


def _dn_scan(s0, u, w, ke, qg, a, ge, cs):
    _, batch, h, seq, _ = u.shape
    ts = cs * DN_CHUNK
    ns = seq // ts
    kern = functools.partial(_dn_scan_kernel, cs=cs)

    def specs(d, blk):
        idx = (lambda bh, n: n) if d == 0 else (lambda bh, n: ns - 1 - n)
        seqs = [pl.BlockSpec((1, 1, 1, ts, width), lambda bh, n: (d, bh // h, bh % h, idx(bh, n), 0))
                for width in (DN_DV, DN_DK, DN_DK, DN_DK, DN_CHUNK)]
        return seqs + [pl.BlockSpec((1, 1, 1, cs, 1, LANE), lambda bh, n: (d, bh // h, bh % h, idx(bh, n), 0, 0))]

    o_spec = lambda d: pl.BlockSpec(
        (1, 1, ts, DN_DV), lambda bh, n: (bh // h, bh % h, n if d == 0 else ns - 1 - n, 0))
    return pl.pallas_call(
        kern,
        grid=(batch * h, ns),
        in_specs=[pl.BlockSpec((2, 1, DN_DK, DN_DV), lambda bh, n: (0, bh, 0, 0))]
        + specs(0, None) + specs(1, None),
        out_specs=[o_spec(0), o_spec(1),
                   pl.BlockSpec((2, 1, DN_DK, DN_DV), lambda bh, n: (0, bh, 0, 0))],
        out_shape=[jax.ShapeDtypeStruct((batch, h, seq, DN_DV), F32),
                   jax.ShapeDtypeStruct((batch, h, seq, DN_DV), F32),
                   jax.ShapeDtypeStruct((2, batch * h, DN_DK, DN_DV), F32)],
        scratch_shapes=[pltpu.VMEM((DN_DK, DN_DV), F32), pltpu.VMEM((DN_DK, DN_DV), F32)],
        compiler_params=_params(("parallel", "arbitrary")),
        name="dn_scan",
    )(s0, u, w, ke, qg, a, ge, u, w, ke, qg, a, ge)


def _dn_out_kernel(of_ref, ob_ref, z_ref, nw_ref, y_ref):
    o = of_ref[0, 0] + ob_ref[0, 0]
    y = o * lax.rsqrt(jnp.mean(o * o, axis=-1, keepdims=True) + EPS) * nw_ref[...]
    y_ref[...] = (y * _silu(z_ref[...].astype(F32))).astype(y_ref.dtype)


def _dn_out(o_f, o_b, p, nw, tm):
    batch, h, seq, _ = o_f.shape
    nt = seq // tm
    zc = B_DZ // LANE
    return pl.pallas_call(
        _dn_out_kernel,
        grid=(batch, h, nt),
        in_specs=[pl.BlockSpec((1, 1, tm, DN_DV), lambda b, hh, i: (b, hh, i, 0)),
                  pl.BlockSpec((1, 1, tm, DN_DV), lambda b, hh, i: (b, hh, i, 0)),
                  pl.BlockSpec((tm, LANE), lambda b, hh, i: (b * nt + i, zc + hh)),
                  pl.BlockSpec((1, DN_DV), lambda b, hh, i: (0, 0))],
        out_specs=pl.BlockSpec((tm, LANE), lambda b, hh, i: (b * nt + i, hh)),
        out_shape=jax.ShapeDtypeStruct((batch * seq, h * DN_DV), BF16),
        compiler_params=_params(("parallel", "parallel", "parallel")),
        name="dn_out",
    )(o_f, o_b, p, nw)


def _merge_kernel(x_ref, mod_ref, ym_ref, yd_ref, ga_ref, gb_ref, wm_ref, wd_ref, wo_ref, o_ref, *, k_gate):
    pm = jnp.dot(ym_ref[...], wm_ref[...], preferred_element_type=F32)
    pd = jnp.dot(yd_ref[...], wd_ref[...], preferred_element_type=F32)
    merged = (jax.nn.sigmoid(ga_ref[...].astype(F32)) * pm
              + jax.nn.sigmoid(gb_ref[...].astype(F32)) * pd).astype(BF16)
    y = jnp.dot(merged, wo_ref[...], preferred_element_type=F32)
    o_ref[...] = x_ref[...] + mod_ref[0, k_gate:k_gate + 1, :] * y


def _merge(x, mod, row_of_tile, ym, yd, p, wm, wd, wo, tm, k_gate):
    m, d = x.shape
    kern = functools.partial(_merge_kernel, k_gate=k_gate)
    const = lambda arr: pl.BlockSpec(arr.shape, lambda i: (0, 0), pipeline_mode=pl.Buffered(1))
    return pl.pallas_call(
        kern,
        grid=(m // tm,),
        in_specs=[pl.BlockSpec((tm, d), lambda i: (i, 0)),
                  pl.BlockSpec((1, 6, d), lambda i: (row_of_tile(i), 0, 0)),
                  pl.BlockSpec((tm, ym.shape[1]), lambda i: (i, 0)),
                  pl.BlockSpec((tm, yd.shape[1]), lambda i: (i, 0)),
                  pl.BlockSpec((tm, d), lambda i: (i, B_GA // d)),
                  pl.BlockSpec((tm, d), lambda i: (i, B_GB // d)),
                  const(wm), const(wd), const(wo)],
        out_specs=pl.BlockSpec((tm, d), lambda i: (i, 0)),
        out_shape=jax.ShapeDtypeStruct((m, d), F32),
        compiler_params=_params(("parallel",)),
        name="merge",
    )(x, mod, ym, yd, p, p, wm, wd, wo)


def _ffn_kernel(x_ref, xp_ref, xx_ref, nw_ref, mod_ref, wup_ref, cw_ref, wdn_ref, o_ref, xn_ref, ge_ref,
                *, tm, tiles_per_seq, k_shift, k_scale, k_gate):
    i = pl.program_id(0)
    j = pl.program_id(1)
    th = FFN_TILE

    @pl.when(j == 0)
    def _():
        first = (i % tiles_per_seq) == 0
        last = (i % tiles_per_seq) == tiles_per_seq - 1
        nw = nw_ref[...]
        sh = mod_ref[0, k_shift:k_shift + 1, :]
        sc = mod_ref[0, k_scale:k_scale + 1, :]
        xn_ref[0:HALO16, :] = jnp.where(first, 0.0, _norm_mod(xp_ref[...], nw, sh, sc)).astype(BF16)
        xn_ref[HALO16:HALO16 + tm, :] = _norm_mod(x_ref[...], nw, sh, sc).astype(BF16)
        xn_ref[HALO16 + tm:, :] = jnp.where(last, 0.0, _norm_mod(xx_ref[...], nw, sh, sc)).astype(BF16)

    up = jnp.dot(xn_ref[...], wup_ref[...], preferred_element_type=F32)
    ge_ref[...] = up[:, :th]
    conv = (cw_ref[0:1, :] * ge_ref[HALO16 - 1:HALO16 - 1 + tm, :]
            + cw_ref[1:2, :] * ge_ref[HALO16:HALO16 + tm, :]
            + cw_ref[2:3, :] * ge_ref[HALO16 + 1:HALO16 + 1 + tm, :])
    hid = (_silu(conv) * up[HALO16:HALO16 + tm, th:]).astype(BF16)
    part = jnp.dot(hid, wdn_ref[...], preferred_element_type=F32)

    @pl.when(j == 0)
    def _():
        o_ref[...] = part

    @pl.when(j > 0)
    def _():
        o_ref[...] += part

    @pl.when(j == pl.num_programs(1) - 1)
    def _():
        o_ref[...] = x_ref[...] + mod_ref[0, k_gate:k_gate + 1, :] * o_ref[...]


def _ffn(x, nw, mod, row_of_tile, wup, cw, wdn, seq, tm, k_shift, k_scale, k_gate):
    m, d = x.shape
    th = FFN_TILE
    nj = wdn.shape[0] // th
    nt = seq // tm
    nb16 = tm // HALO16
    total16 = m // HALO16
    kern = functools.partial(_ffn_kernel, tm=tm, tiles_per_seq=nt, k_shift=k_shift, k_scale=k_scale,
                             k_gate=k_gate)
    return pl.pallas_call(
        kern,
        grid=(m // tm, nj),
        in_specs=[pl.BlockSpec((tm, d), lambda i, j: (i, 0)),
                  pl.BlockSpec((HALO16, d), lambda i, j: (jnp.maximum(i * nb16 - 1, 0), 0)),
                  pl.BlockSpec((HALO16, d), lambda i, j: (jnp.minimum((i + 1) * nb16, total16 - 1), 0)),
                  pl.BlockSpec((1, d), lambda i, j: (0, 0)),
                  pl.BlockSpec((1, 6, d), lambda i, j: (row_of_tile(i), 0, 0)),
                  pl.BlockSpec((d, 2 * th), lambda i, j: (0, j)),
                  pl.BlockSpec((3, th), lambda i, j: (0, j)),
                  pl.BlockSpec((th, d), lambda i, j: (j, 0))],
        out_specs=pl.BlockSpec((tm, d), lambda i, j: (i, 0)),
        out_shape=jax.ShapeDtypeStruct((m, d), F32),
        scratch_shapes=[pltpu.VMEM((tm + 2 * HALO16, d), BF16), pltpu.VMEM((tm + 2 * HALO16, th), F32)],
        compiler_params=_params(("parallel", "arbitrary")),
        name="ffn",
    )(x, x, x, nw, mod, wup, cw, wdn)


def _rope_pad_cols(r):
    n = MLA_ROPE // 4
    z = jnp.zeros(r.shape[:-1] + (2 * n,), r.dtype)
    return jnp.concatenate([r[..., 0:n], r[..., 2 * n:3 * n], z, r[..., n:2 * n], r[..., 3 * n:4 * n], z], axis=-1)


def _qk_pad_cols(w):
    lead = w.shape[:-1]
    w = w.reshape(lead + (MLA_HEADS, MLA_QK))
    out = jnp.concatenate([w[..., :MLA_NOPE], _rope_pad_cols(w[..., MLA_NOPE:])], axis=-1)
    return out.reshape(lead + (MLA_HEADS * MLA_QK_PAD,))


def _rope_tables(rows):
    n = MLA_ROPE // 4
    row = jnp.repeat(jnp.arange(rows, dtype=F32), GRID_W)
    col = jnp.tile(jnp.arange(GRID_W, dtype=F32), rows)
    inv = ROPE_BASE ** (-jnp.arange(n, dtype=F32) / n)
    ang_r = row[:, None] * inv
    ang_c = col[:, None] * inv
    t = row.shape[0]
    one = jnp.ones((t, 2 * n), F32)
    zero = jnp.zeros((t, 2 * n), F32)
    cos = jnp.concatenate([jnp.cos(ang_r), jnp.cos(ang_c), one, jnp.cos(ang_r), jnp.cos(ang_c), one], axis=-1)
    sin = jnp.concatenate([-jnp.sin(ang_r), -jnp.sin(ang_c), zero, jnp.sin(ang_r), jnp.sin(ang_c), zero], axis=-1)
    return cos, sin


def _pick_tile(n, pref):
    t = min(pref, n)
    while n % t:
        t //= 2
    return t


def kernel(x, c, ctx, c_ctx, w_ada, b_ada, norm_mix, w_in, q_a_norm, w_q_b, kv_a_norm, w_kv_b, q_norm, k_norm,
           w_o_mla, dn_conv, dn_a_log, dn_dt_bias, dn_o_norm, w_o_dn, w_out, norm_ffn, w_ffn_up, ffn_conv,
           w_ffn_down):
    batch, seq, d = x.shape
    lc = ctx.shape[1]
    assert w_ada.shape[0] == 1, "single-layer stack"
    assert seq % GRID_W == 0 and seq % DN_CHUNK == 0 and lc % DN_CHUNK == 0
    h = DN_HEADS
    l = 0

    wi = w_in[l]
    o_qa, o_kva, o_kr = 0, MLA_Q_RANK, MLA_Q_RANK + MLA_KV_RANK
    o_dq = o_kr + MLA_ROPE
    o_da = o_dq + 4 * h * DN_DK
    o_ga = o_da + 4 * h
    w_a = jnp.concatenate([wi[:, o_qa:o_kr], _rope_pad_cols(wi[:, o_kr:o_dq]), wi[:, o_da:o_ga],
                           jnp.zeros((d, A_WIDTH - A_DB - 2 * h), F32)], axis=1).astype(BF16)
    w_b = jnp.concatenate([wi[:, o_dq:o_da], wi[:, o_ga:]], axis=1).astype(BF16)
    wq = _qk_pad_cols(w_q_b[l]).astype(BF16)
    wkv = w_kv_b[l].reshape(MLA_KV_RANK, MLA_HEADS, MLA_NOPE + MLA_V)
    wkv = jnp.concatenate([wkv[:, :, :MLA_NOPE].reshape(MLA_KV_RANK, -1),
                           wkv[:, :, MLA_NOPE:].reshape(MLA_KV_RANK, -1)], axis=1).astype(BF16)
    qn = jnp.concatenate([q_norm[l, :MLA_NOPE], _rope_pad_cols(q_norm[l, MLA_NOPE:])])[None, :]
    kn = jnp.concatenate([k_norm[l, :MLA_NOPE], _rope_pad_cols(k_norm[l, MLA_NOPE:])])[None, :]
    hidden = w_ffn_down.shape[1]
    nj = -(-hidden // FFN_TILE)
    hpad = nj * FFN_TILE - hidden
    wg = jnp.pad(w_ffn_up[l][:, :hidden], ((0, 0), (0, hpad))).reshape(d, nj, FFN_TILE)
    wv = jnp.pad(w_ffn_up[l][:, hidden:], ((0, 0), (0, hpad))).reshape(d, nj, FFN_TILE)
    wup = jnp.concatenate([wg, wv], axis=2).reshape(d, nj * 2 * FFN_TILE).astype(BF16)
    fcw = jnp.pad(ffn_conv[l], ((0, 0), (0, hpad)))
    wdn = jnp.pad(w_ffn_down[l], ((0, hpad), (0, 0))).astype(BF16)
    hp = jnp.concatenate([dn_a_log[l].T, dn_dt_bias[l].T], axis=1)[:, None, :]
    cos_l, sin_l = _rope_tables(seq // GRID_W)
    cos_c, sin_c = jnp.ones((lc, ROPE_PAD), F32), jnp.zeros((lc, ROPE_PAD), F32)

    rows = jnp.concatenate([c, c_ctx[None, :], jnp.zeros((8 - batch - 1, d), F32)], axis=0)
    mod = _ada(rows, w_ada[l], b_ada[l][None, :]).reshape(8, 6, d)

    xf = x.reshape(batch * seq, d)
    cf = ctx.reshape(batch * lc, d)

    def stream(tokens, n, tm_a, tm_b, row_of):
        nm = norm_mix[l][None, :]
        a = _norm_mod_matmul(tokens, nm, mod, row_of(tm_a), w_a, F32, tm_a, A_WIDTH, 0, 1, "in_proj_small")
        p = _norm_mod_matmul(tokens, nm, mod, row_of(tm_b), w_b, BF16, tm_b, 1024, 0, 1, "in_proj_wide")
        ab = a[:, A_DA:A_DA + 4 * h].reshape(batch * n, 4, h).transpose(2, 0, 1)
        return a, p, ab

    tm_a = _pick_tile(seq, 512)
    tm_b = _pick_tile(seq, 1024)
    lat_row = lambda tm: (lambda i: i // (seq // tm))
    a_l, p_l, ab_l = stream(xf, seq, tm_a, tm_b, lat_row)
    tc = _pick_tile(lc, 256)
    ctx_row = lambda tm: (lambda i: batch)
    a_c, p_c, ab_c = stream(cf, lc, tc, tc, ctx_row)

    qan, kvan = q_a_norm[l][None, :], kv_a_norm[l][None, :]
    tmp = _pick_tile(seq, 512)
    q_l, k_l, v_l = _mla_proj(a_l, cos_l, sin_l, qan, kvan, qn, kn, wq, wkv, tmp, seq)
    _, k_c, v_c = _mla_proj(a_c, cos_c, sin_c, qan, kvan, qn, kn, wq, wkv, tc, lc)
    tq = _pick_tile(seq, 512)
    tk = _pick_tile(seq, 512)
    qt = q_l.reshape(batch, seq, MLA_HEADS, MLA_QK_PAD).transpose(0, 2, 3, 1)
    vt = v_l.reshape(batch, seq // tk, tk, MLA_HEADS, MLA_V).transpose(0, 3, 1, 4, 2)
    vct = v_c.reshape(batch, lc, MLA_HEADS, MLA_V).transpose(0, 2, 3, 1)
    y_mla = _attention(qt, k_l.reshape(batch, seq, -1), vt, k_c.reshape(batch, lc, -1), vct, tq)
    y_mla = y_mla.reshape(batch * seq, MLA_HEADS * MLA_V)

    conv_w = dn_conv[l]
    prep_c = _dn_prep(p_c, ab_c, hp, conv_w, batch, lc, _pick_tile(lc, 256))
    s_zero = jnp.zeros((2, batch * h, DN_DK, DN_DV), F32)
    _, _, s_ctx = _dn_scan(s_zero, *prep_c, _pick_tile(lc, 256) // DN_CHUNK)
    prep_l = _dn_prep(p_l, ab_l, hp, conv_w, batch, seq, _pick_tile(seq, 512))
    o_f, o_b, _ = _dn_scan(s_ctx, *prep_l, _pick_tile(seq, 256) // DN_CHUNK)
    y_dn = _dn_out(o_f, o_b, p_l, dn_o_norm[l][None, :], _pick_tile(seq, 1024))

    tmm = _pick_tile(seq, 256)
    x1 = _merge(xf, mod, lat_row(tmm), y_mla, y_dn, p_l, w_o_mla[l].astype(BF16), w_o_dn[l].astype(BF16),
                w_out[l].astype(BF16), tmm, 2)

    tmf = _pick_tile(seq, 512)
    out = _ffn(x1, norm_ffn[l][None, :], mod, lat_row(tmf), wup, fcw, wdn, seq, tmf, 3, 4, 5)
    return out.reshape(batch, seq, d)
```

```python
import functools
import math

import jax
import jax.numpy as jnp
import numpy as np
from jax import lax
from jax.experimental import pallas as pl
from jax.experimental.pallas import tpu as pltpu

F32 = jnp.float32
BF16 = jnp.bfloat16
HIGHEST = lax.Precision.HIGHEST

EPS = 1e-6
GRID_W = 64
ROPE_BASE = 10000.0

MLA_HEADS = 8
MLA_Q_RANK = 512
MLA_KV_RANK = 512
MLA_NOPE = 128
MLA_ROPE = 64
MLA_V = 128
MLA_QK = MLA_NOPE + MLA_ROPE
MLA_QK_PAD = 256

DN_HEADS = 8
DN_DK = 128
DN_DV = 128
DN_CHUNK = 64

LANE = 128
SUBLANE = 8
ROPE_PAD = 128
VMEM_LIMIT = 56 * 1024 * 1024

A_QA, A_KVA, A_KR, A_DA, A_DB, A_WIDTH = 0, 512, 1024, 1152, 1168, 1280
B_DQ, B_DK, B_DV, B_DZ, B_GA, B_GB, B_WIDTH = 0, 1024, 2048, 3072, 4096, 6144, 8192

FFN_TILE = 512
HALO16 = 16


def _params(sem, vmem=VMEM_LIMIT):
    return pltpu.CompilerParams(dimension_semantics=sem, vmem_limit_bytes=vmem)


def _nt_dot(a, b):
    return lax.dot_general(a, b, (((1,), (1,)), ((), ())), preferred_element_type=F32)


def _silu(x):
    return x * jax.nn.sigmoid(x)


def _ada_kernel(c_ref, w_ref, b_ref, o_ref):
    a = _silu(c_ref[...]).astype(BF16)
    o_ref[...] = jnp.dot(a, w_ref[...].astype(BF16), preferred_element_type=F32) + b_ref[...]


def _ada(cs, w, b):
    m, d = cs.shape
    n = w.shape[1]
    tn = 512
    return pl.pallas_call(
        _ada_kernel,
        grid=(n // tn,),
        in_specs=[pl.BlockSpec((m, d), lambda j: (0, 0)),
                  pl.BlockSpec((d, tn), lambda j: (0, j)),
                  pl.BlockSpec((1, tn), lambda j: (0, j))],
        out_specs=pl.BlockSpec((m, tn), lambda j: (0, j)),
        out_shape=jax.ShapeDtypeStruct((m, n), F32),
        compiler_params=_params(("parallel",)),
        name="ada",
    )(cs, w, b)


def _norm_mod(x, nw, shift, scale):
    ms = jnp.mean(x * x, axis=-1, keepdims=True)
    y = x * lax.rsqrt(ms + EPS) * nw
    return y * (1.0 + scale) + shift


def _nmm_kernel(x_ref, nw_ref, mod_ref, w_ref, o_ref, xn_ref, *, k_shift, k_scale):
    @pl.when(pl.program_id(1) == 0)
    def _():
        y = _norm_mod(x_ref[...], nw_ref[...], mod_ref[0, k_shift:k_shift + 1, :],
                      mod_ref[0, k_scale:k_scale + 1, :])
        xn_ref[...] = y.astype(BF16)

    o_ref[...] = jnp.dot(xn_ref[...], w_ref[...], preferred_element_type=F32).astype(o_ref.dtype)


def _norm_mod_matmul(x, nw, mod, row_of_tile, w, out_dtype, tm, tn, k_shift, k_scale, name):
    m, d = x.shape
    n = w.shape[1]
    kern = functools.partial(_nmm_kernel, k_shift=k_shift, k_scale=k_scale)
    return pl.pallas_call(
        kern,
        grid=(m // tm, n // tn),
        in_specs=[pl.BlockSpec((tm, d), lambda i, j: (i, 0)),
                  pl.BlockSpec((1, d), lambda i, j: (0, 0)),
                  pl.BlockSpec((1, 6, d), lambda i, j: (row_of_tile(i), 0, 0)),
                  pl.BlockSpec((d, tn), lambda i, j: (0, j))],
        out_specs=pl.BlockSpec((tm, tn), lambda i, j: (i, j)),
        out_shape=jax.ShapeDtypeStruct((m, n), out_dtype),
        scratch_shapes=[pltpu.VMEM((tm, d), BF16)],
        compiler_params=_params(("parallel", "arbitrary")),
        name=name,
    )(x, nw, mod, w)


def _mla_proj_kernel(a_ref, cos_ref, sin_ref, qan_ref, kvan_ref, qn_ref, kn_ref, wq_ref, wkv_ref,
                     q_ref, k_ref, v_ref):
    cos = cos_ref[...]
    sin = sin_ref[...]

    def rope(r):
        return r * cos + pltpu.roll(r, 64, axis=1) * sin

    def rms_rows(t, w):
        return t * lax.rsqrt(jnp.mean(t * t, axis=-1, keepdims=True) + EPS) * w

    qa = rms_rows(a_ref[:, A_QA:A_QA + MLA_Q_RANK], qan_ref[...]).astype(BF16)
    q = jnp.dot(qa, wq_ref[...], preferred_element_type=F32)
    kva = rms_rows(a_ref[:, A_KVA:A_KVA + MLA_KV_RANK], kvan_ref[...]).astype(BF16)
    kv = jnp.dot(kva, wkv_ref[...], preferred_element_type=F32)
    kr = a_ref[:, A_KR:A_KR + ROPE_PAD]
    kr_ss = jnp.sum(kr * kr, axis=-1, keepdims=True)
    qn = qn_ref[...]
    kn = kn_ref[...]
    scale = MLA_QK ** -0.5 * math.log2(math.e)
    for h in range(MLA_HEADS):
        qh = q[:, h * MLA_QK_PAD:(h + 1) * MLA_QK_PAD]
        inv = lax.rsqrt(jnp.sum(qh * qh, axis=-1, keepdims=True) * (1.0 / MLA_QK) + EPS) * scale
        qh = qh * inv * qn
        q_ref[:, h * MLA_QK_PAD:h * MLA_QK_PAD + MLA_NOPE] = qh[:, :MLA_NOPE].astype(BF16)
        q_ref[:, h * MLA_QK_PAD + MLA_NOPE:(h + 1) * MLA_QK_PAD] = rope(qh[:, MLA_NOPE:]).astype(BF16)

        kh = kv[:, h * MLA_NOPE:(h + 1) * MLA_NOPE]
        inv = lax.rsqrt((jnp.sum(kh * kh, axis=-1, keepdims=True) + kr_ss) * (1.0 / MLA_QK) + EPS)
        k_ref[:, h * MLA_QK_PAD:h * MLA_QK_PAD + MLA_NOPE] = (kh * inv * kn[:, :MLA_NOPE]).astype(BF16)
        k_ref[:, h * MLA_QK_PAD + MLA_NOPE:(h + 1) * MLA_QK_PAD] = rope(
            kr * inv * kn[:, MLA_NOPE:]).astype(BF16)
    v_ref[...] = kv[:, MLA_HEADS * MLA_NOPE:].astype(BF16)


def _mla_proj(a, cos, sin, qan, kvan, qn, kn, wq, wkv, tm, seq):
    m = a.shape[0]
    nseq = seq // tm
    full = lambda arr: pl.BlockSpec(arr.shape, lambda i: (0, 0))
    return pl.pallas_call(
        _mla_proj_kernel,
        grid=(m // tm,),
        in_specs=[pl.BlockSpec((tm, A_WIDTH), lambda i: (i, 0)),
                  pl.BlockSpec((tm, ROPE_PAD), lambda i: (i % nseq, 0)),
                  pl.BlockSpec((tm, ROPE_PAD), lambda i: (i % nseq, 0)),
                  full(qan), full(kvan), full(qn), full(kn), full(wq), full(wkv)],
        out_specs=[pl.BlockSpec((tm, MLA_HEADS * MLA_QK_PAD), lambda i: (i, 0)),
                   pl.BlockSpec((tm, MLA_HEADS * MLA_QK_PAD), lambda i: (i, 0)),
                   pl.BlockSpec((tm, MLA_HEADS * MLA_V), lambda i: (i, 0))],
        out_shape=[jax.ShapeDtypeStruct((m, MLA_HEADS * MLA_QK_PAD), BF16),
                   jax.ShapeDtypeStruct((m, MLA_HEADS * MLA_QK_PAD), BF16),
                   jax.ShapeDtypeStruct((m, MLA_HEADS * MLA_V), BF16)],
        compiler_params=_params(("parallel",)),
        name="mla_proj",
    )(a, cos, sin, qan, kvan, qn, kn, wq, wkv)


ATTN_UNROLL = 8
ATTN_ROWS = 64


def _attn_kernel(qt_ref, kc_ref, vct_ref, k_ref, vt_ref, o_ref, s_ref, p_ref, acc_ref, *, nk, unroll):
    qt = qt_ref[0, 0]
    tq = qt.shape[1]
    tk = s_ref.shape[1]
    rows = ATTN_ROWS

    def scores(j):
        return jnp.dot(k_ref[0, pl.ds(pl.multiple_of(j * tk, tk), tk), :], qt, preferred_element_type=F32)

    s = jnp.dot(kc_ref[0], qt, preferred_element_type=F32)
    m = jnp.max(s, axis=0, keepdims=True)
    p = jnp.exp2(s - m)
    l = jnp.sum(p, axis=0, keepdims=True)
    acc_ref[...] = jnp.dot(vct_ref[0, 0], p.astype(BF16), preferred_element_type=F32)

    s_ref[0] = scores(0)
    p_ref[1] = jnp.zeros((tk, tq), BF16)

    def tile(j, slot, carry):
        m, l, alpha_prev = carry
        acc_ref[...] = alpha_prev * acc_ref[...] + jnp.dot(
            vt_ref[0, 0, jnp.maximum(j - 1, 0)], p_ref[1 - slot], preferred_element_type=F32)
        s_ref[1 - slot] = scores(jnp.minimum(j + 1, nk - 1))
        m8 = jnp.full((SUBLANE, tq), -jnp.inf, F32)
        for r in range(0, tk, rows):
            m8 = jnp.maximum(m8, jnp.max(s_ref[slot, r:r + rows, :].reshape(rows // SUBLANE, SUBLANE, tq), axis=0))
        m_new = jnp.maximum(m, jnp.max(m8, axis=0, keepdims=True))
        l8 = jnp.zeros((SUBLANE, tq), F32)
        for r in range(0, tk, rows):
            p = jnp.exp2(s_ref[slot, r:r + rows, :] - m_new)
            l8 = l8 + jnp.sum(p.reshape(rows // SUBLANE, SUBLANE, tq), axis=0)
            p_ref[slot, r:r + rows, :] = p.astype(BF16)
        alpha = jnp.exp2(m - m_new)
        return m_new, alpha * l + jnp.sum(l8, axis=0, keepdims=True), alpha

    def body(i, carry):
        for u in range(unroll):
            carry = tile(unroll * i + u, u % 2, carry)
        return carry

    m, l, alpha = lax.fori_loop(0, nk // unroll, body, (m, l, jnp.ones((1, tq), F32)))
    acc = alpha * acc_ref[...] + jnp.dot(vt_ref[0, 0, nk - 1], p_ref[1], preferred_element_type=F32)
    o_ref[0] = (acc / l).T.astype(o_ref.dtype)


def _attention(qt, k, vt, kc, vct, tq):
    b, t, _ = k.shape
    lc = kc.shape[1]
    nk, vrows, tk = vt.shape[2], vt.shape[3], vt.shape[4]
    unroll = ATTN_UNROLL if nk % ATTN_UNROLL == 0 else 2
    assert nk % unroll == 0
    kern = functools.partial(_attn_kernel, nk=nk, unroll=unroll)
    return pl.pallas_call(
        kern,
        grid=(b, MLA_HEADS, t // tq),
        in_specs=[pl.BlockSpec((1, 1, MLA_QK_PAD, tq), lambda bi, h, i: (bi, h, 0, i)),
                  pl.BlockSpec((1, lc, MLA_QK_PAD), lambda bi, h, i: (bi, 0, h)),
                  pl.BlockSpec((1, 1, vrows, lc), lambda bi, h, i: (bi, h, 0, 0)),
                  pl.BlockSpec((1, t, MLA_QK_PAD), lambda bi, h, i: (bi, 0, h)),
                  pl.BlockSpec((1, 1, nk, vrows, tk), lambda bi, h, i: (bi, h, 0, 0, 0))],
        out_specs=pl.BlockSpec((1, tq, MLA_V), lambda bi, h, i: (bi, i, h)),
        out_shape=jax.ShapeDtypeStruct((b, t, MLA_HEADS * MLA_V), BF16),
        scratch_shapes=[pltpu.VMEM((2, tk, tq), F32), pltpu.VMEM((2, tk, tq), BF16),
                        pltpu.VMEM((vrows, tq), F32)],
        compiler_params=_params(("parallel", "parallel", "arbitrary")),
        name="attention",
    )(qt, kc, vct, k, vt)


def _split2(x):
    hi = x.astype(BF16)
    return hi, (x - hi.astype(F32)).astype(BF16)


def _bmm(a, b):
    return lax.dot_general(a, b, (((2,), (1,)), ((0,), (0,))), preferred_element_type=F32)


def _bmm_nt(a, b):
    return lax.dot_general(a, b, (((2,), (2,)), ((0,), (0,))), preferred_element_type=F32)


def _bmm_hp(a, b):
    ah, al = _split2(a)
    bh, bl = _split2(b)
    return _bmm(ah, bh) + _bmm(al, bh) + _bmm(ah, bl)


def _tri_inverse(m, eye):
    assert m.shape[-1] == 64
    n = -m
    x = eye + n
    for _ in range(3):
        nb = n.astype(BF16)
        n = _bmm(nb, nb)
        x = x + _bmm(x.astype(BF16), n.astype(BF16))
    for _ in range(2):
        r = eye - x - _bmm_hp(m, x)
        x = x + _bmm(x.astype(BF16), r.astype(BF16))
    return x


def _dn_prep_kernel(qm_ref, qp_ref, qx_ref, km_ref, kp_ref, kx_ref, vm_ref, vp_ref, vx_ref,
                    cwq_ref, cwk_ref, cwv_ref, ab_ref, hp_ref,
                    u_ref, w_ref, ke_ref, qg_ref, a_ref, ge_ref, ext_ref, *, tm, tiles_per_seq):
    i = pl.program_id(0)
    first = (i % tiles_per_seq) == 0
    last = (i % tiles_per_seq) == tiles_per_seq - 1
    c = DN_CHUNK

    def conv_silu(main_ref, prev_ref, next_ref, cw_ref):
        ext_ref[0:HALO16, :] = jnp.where(first, 0.0, prev_ref[...].astype(F32))
        ext_ref[HALO16:HALO16 + tm, :] = main_ref[...].astype(F32)
        ext_ref[HALO16 + tm:, :] = jnp.where(last, 0.0, next_ref[...].astype(F32))
        y = (cw_ref[0:1, :] * ext_ref[HALO16 - 1:HALO16 - 1 + tm, :]
             + cw_ref[1:2, :] * ext_ref[HALO16:HALO16 + tm, :]
             + cw_ref[2:3, :] * ext_ref[HALO16 + 1:HALO16 + 1 + tm, :])
        return _silu(y)

    def l2n(t):
        return t * lax.rsqrt(jnp.sum(t * t, axis=-1, keepdims=True) + EPS)

    q = l2n(conv_silu(qm_ref, qp_ref, qx_ref, cwq_ref)) * (DN_DK ** -0.5)
    k = l2n(conv_silu(km_ref, kp_ref, kx_ref, cwk_ref))
    v = conv_silu(vm_ref, vp_ref, vx_ref, cwv_ref)

    hp = hp_ref[0]
    ab = ab_ref[0]
    z = ab[:, 0:2] + hp[:, 2:4]
    softplus = jnp.maximum(z, 0.0) + jnp.log(1.0 + jnp.exp(-jnp.abs(z)))
    g_all = -jnp.exp(hp[:, 0:2]) * softplus
    beta_all = jax.nn.sigmoid(ab[:, 2:4])

    nc = tm // c

    def per_chunk(t):
        return t.reshape(nc, c, t.shape[-1])

    def both(t):
        return jnp.concatenate([t, t], axis=0)

    def by_dir(t2):
        return jnp.concatenate([per_chunk(jnp.broadcast_to(t2[:, d:d + 1], (tm, LANE))) for d in range(2)],
                               axis=0)

    shape = (2 * nc, c, c)
    fwd = lax.broadcasted_iota(jnp.int32, shape, 0) < nc
    rows = lax.broadcasted_iota(jnp.int32, shape, 1)
    cols = lax.broadcasted_iota(jnp.int32, shape, 2)
    eye_mask = rows == cols
    ahead = jnp.where(fwd, cols - rows, rows - cols)
    incl = ahead <= 0
    strict = ahead < 0
    eye = jnp.where(eye_mask, 1.0, 0.0)
    tri = jnp.where(incl, 1.0, 0.0).astype(BF16)

    g = by_dir(g_all)
    g_hi = g.astype(BF16)
    g_r = g - g_hi.astype(F32)
    g_mid = g_r.astype(BF16)
    g_lo = (g_r - g_mid.astype(F32)).astype(BF16)
    gc = _bmm(tri, g_hi) + _bmm(tri, g_mid) + _bmm(tri, g_lo)
    gc_sq = gc[:, :, :c]
    gc_row = jnp.sum(jnp.where(eye_mask, gc_sq, 0.0), axis=1, keepdims=True)
    decay = jnp.where(incl, jnp.exp(jnp.where(incl, gc_sq - gc_row, 0.0)), 0.0)
    gtot = jnp.concatenate([gc[:nc, c - 1:c, :], gc[nc:, 0:1, :]], axis=0)
    eg = jnp.exp(gc)

    beta = by_dir(beta_all)
    k3 = per_chunk(k)
    q3 = per_chunk(q)
    k3b = k3.astype(BF16)
    kk = both(_bmm_nt(k3b, k3b))
    qk = both(_bmm_nt(q3.astype(BF16), k3b))
    k2 = both(k3)
    kb = k2 * beta
    m = jnp.where(strict, beta[:, :, :c] * kk * decay, 0.0)
    tinv = _tri_inverse(m, eye)
    sol = _bmm_hp(tinv, jnp.concatenate([both(per_chunk(v)) * beta, kb * eg], axis=2))
    ke = k2 * jnp.exp(gtot - gc)
    qg = both(q3) * eg
    att = qk * decay
    gend = jnp.exp(gtot)
    for d in range(2):
        sl = slice(d * nc, (d + 1) * nc)
        u_ref[d, 0, 0] = sol[sl, :, :DN_DV].reshape(tm, DN_DV)
        w_ref[d, 0, 0] = sol[sl, :, DN_DV:].reshape(tm, DN_DK).astype(BF16)
        ke_ref[d, 0, 0] = ke[sl].reshape(tm, DN_DK).astype(BF16)
        qg_ref[d, 0, 0] = qg[sl].reshape(tm, DN_DK).astype(BF16)
        a_ref[d, 0, 0] = att[sl].reshape(tm, c).astype(BF16)
        ge_ref[d, 0, 0] = gend[sl]


def _dn_prep(p, ab, hp, conv_w, batch, seq, tm):
    h = DN_HEADS
    nt = seq // tm
    nb16 = tm // HALO16
    total16 = batch * seq // HALO16
    kern = functools.partial(_dn_prep_kernel, tm=tm, tiles_per_seq=nt)

    def triple(col0):
        return [pl.BlockSpec((tm, LANE), lambda i, hh: (i, col0 + hh)),
                pl.BlockSpec((HALO16, LANE), lambda i, hh: (jnp.maximum(i * nb16 - 1, 0), col0 + hh)),
                pl.BlockSpec((HALO16, LANE), lambda i, hh: (jnp.minimum((i + 1) * nb16, total16 - 1), col0 + hh))]

    cq, ck, cv = B_DQ // LANE, B_DK // LANE, B_DV // LANE
    row = lambda shape: pl.BlockSpec(shape, lambda i, hh: (0, i // nt, hh, i % nt, 0))
    seq_shape = lambda width, dt: jax.ShapeDtypeStruct((2, batch, h, seq, width), dt)
    return pl.pallas_call(
        kern,
        grid=(batch * nt, h),
        in_specs=triple(cq) + triple(ck) + triple(cv) + [
            pl.BlockSpec((3, LANE), lambda i, hh: (0, hh)),
            pl.BlockSpec((3, LANE), lambda i, hh: (0, h + hh)),
            pl.BlockSpec((3, LANE), lambda i, hh: (0, 2 * h + hh)),
            pl.BlockSpec((1, tm, 4), lambda i, hh: (hh, i, 0)),
            pl.BlockSpec((1, 1, 4), lambda i, hh: (hh, 0, 0))],
        out_specs=[row((2, 1, 1, tm, DN_DV)), row((2, 1, 1, tm, DN_DK)), row((2, 1, 1, tm, DN_DK)),
                   row((2, 1, 1, tm, DN_DK)), row((2, 1, 1, tm, DN_CHUNK)),
                   pl.BlockSpec((2, 1, 1, tm // DN_CHUNK, 1, LANE), lambda i, hh: (0, i // nt, hh, i % nt, 0, 0))],
        out_shape=[seq_shape(DN_DV, F32), seq_shape(DN_DK, BF16), seq_shape(DN_DK, BF16),
                   seq_shape(DN_DK, BF16), seq_shape(DN_CHUNK, BF16),
                   jax.ShapeDtypeStruct((2, batch, h, seq // DN_CHUNK, 1, LANE), F32)],
        scratch_shapes=[pltpu.VMEM((tm + 2 * HALO16, LANE), F32)],
        compiler_params=_params(("parallel", "parallel")),
        name="dn_prep",
    )(p, p, p, p, p, p, p, p, p, conv_w, conv_w, conv_w, ab, hp)


def _dn_scan_kernel(s0_ref, uf_ref, wf_ref, kf_ref, qf_ref, af_ref, gf_ref,
                    ub_ref, wb_ref, kb_ref, qb_ref, ab_ref, gb_ref,
                    of_ref, ob_ref, sfin_ref, s_ref, *, cs):
    n = pl.program_id(1)
    c = DN_CHUNK
    h = DN_HEADS

    @pl.when(n == 0)
    def _():
        s_ref[0:h] = s0_ref[0, 0]
        s_ref[h:] = s0_ref[1, 0]

    for ci in range(cs):
        rf = ci * c
        rb = (cs - 1 - ci) * c

        def pair(f_ref, b_ref):
            return jnp.concatenate([f_ref[0, 0, :, rf:rf + c, :], b_ref[0, 0, :, rb:rb + c, :]], axis=0)

        s = s_ref[...]
        s16 = s.astype(BF16)
        ws_qs = _bmm(jnp.concatenate([pair(wf_ref, wb_ref), pair(qf_ref, qb_ref)], axis=1), s16)
        v_new = pair(uf_ref, ub_ref) - ws_qs[:, :c]
        v16 = v_new.astype(BF16)
        o = ws_qs[:, c:] + _bmm(pair(af_ref, ab_ref), v16)
        of_ref[0, :, rf:rf + c, :] = o[:h].astype(of_ref.dtype)
        ob_ref[0, :, rb:rb + c, :] = o[h:].astype(ob_ref.dtype)
        g = jnp.concatenate([gf_ref[0, 0, :, ci], gb_ref[0, 0, :, cs - 1 - ci]], axis=0)
        s_ref[...] = s * g + lax.dot_general(pair(kf_ref, kb_ref), v16, (((1,), (1,)), ((0,), (0,))),
                                             preferred_element_type=F32)

    @pl.when(n == pl.num_programs(1) - 1)
    def _():
        sfin_ref[0, 0] = s_ref[0:h]
        sfin_ref[1, 0] = s_ref[h:]


def _dn_scan(s0, u, w, ke, qg, a, ge, cs):
    _, batch, h, seq, _ = u.shape
    ts = cs * DN_CHUNK
    ns = seq // ts
    kern = functools.partial(_dn_scan_kernel, cs=cs)

    def specs(d):
        idx = (lambda n: n) if d == 0 else (lambda n: ns - 1 - n)
        seqs = [pl.BlockSpec((1, 1, h, ts, width), lambda b, n: (d, b, 0, idx(n), 0))
                for width in (DN_DV, DN_DK, DN_DK, DN_DK, DN_CHUNK)]
        return seqs + [pl.BlockSpec((1, 1, h, cs, 1, LANE), lambda b, n: (d, b, 0, idx(n), 0, 0))]

    o_spec = lambda d: pl.BlockSpec((1, h, ts, DN_DV), lambda b, n: (b, 0, n if d == 0 else ns - 1 - n, 0))
    state = pl.BlockSpec((2, 1, h, DN_DK, DN_DV), lambda b, n: (0, b, 0, 0, 0))
    return pl.pallas_call(
        kern,
        grid=(batch, ns),
        in_specs=[state] + specs(0) + specs(1),
        out_specs=[o_spec(0), o_spec(1), state],
        out_shape=[jax.ShapeDtypeStruct((batch, h, seq, DN_DV), BF16),
                   jax.ShapeDtypeStruct((batch, h, seq, DN_DV), BF16),
                   jax.ShapeDtypeStruct((2, batch, h, DN_DK, DN_DV), F32)],
        scratch_shapes=[pltpu.VMEM((2 * h, DN_DK, DN_DV), F32)],
        compiler_params=_params(("parallel", "arbitrary")),
        name="dn_scan",
    )(s0, u, w, ke, qg, a, ge, u, w, ke, qg, a, ge)


def _dn_out_kernel(of_ref, ob_ref, z_ref, nw_ref, y_ref):
    o = of_ref[0, 0].astype(F32) + ob_ref[0, 0].astype(F32)
    y = o * lax.rsqrt(jnp.mean(o * o, axis=-1, keepdims=True) + EPS) * nw_ref[...]
    y_ref[...] = (y * _silu(z_ref[...].astype(F32))).astype(y_ref.dtype)


def _dn_out(o_f, o_b, p, nw, tm):
    batch, h, seq, _ = o_f.shape
    nt = seq // tm
    zc = B_DZ // LANE
    return pl.pallas_call(
        _dn_out_kernel,
        grid=(batch, h, nt),
        in_specs=[pl.BlockSpec((1, 1, tm, DN_DV), lambda b, hh, i: (b, hh, i, 0)),
                  pl.BlockSpec((1, 1, tm, DN_DV), lambda b, hh, i: (b, hh, i, 0)),
                  pl.BlockSpec((tm, LANE), lambda b, hh, i: (b * nt + i, zc + hh)),
                  pl.BlockSpec((1, DN_DV), lambda b, hh, i: (0, 0))],
        out_specs=pl.BlockSpec((tm, LANE), lambda b, hh, i: (b * nt + i, hh)),
        out_shape=jax.ShapeDtypeStruct((batch * seq, h * DN_DV), BF16),
        compiler_params=_params(("parallel", "parallel", "parallel")),
        name="dn_out",
    )(o_f, o_b, p, nw)


def _merge_kernel(x_ref, mod_ref, ym_ref, yd_ref, ga_ref, gb_ref, wm_ref, wd_ref, wo_ref, o_ref, *, k_gate):
    pm = jnp.dot(ym_ref[...], wm_ref[...], preferred_element_type=F32)
    pd = jnp.dot(yd_ref[...], wd_ref[...], preferred_element_type=F32)
    merged = (jax.nn.sigmoid(ga_ref[...].astype(F32)) * pm
              + jax.nn.sigmoid(gb_ref[...].astype(F32)) * pd).astype(BF16)
    y = jnp.dot(merged, wo_ref[...], preferred_element_type=F32)
    o_ref[...] = x_ref[...] + mod_ref[0, k_gate:k_gate + 1, :] * y


def _merge(x, mod, row_of_tile, ym, yd, p, wm, wd, wo, tm, k_gate):
    m, d = x.shape
    kern = functools.partial(_merge_kernel, k_gate=k_gate)
    const = lambda arr: pl.BlockSpec(arr.shape, lambda i: (0, 0), pipeline_mode=pl.Buffered(1))
    return pl.pallas_call(
        kern,
        grid=(m // tm,),
        in_specs=[pl.BlockSpec((tm, d), lambda i: (i, 0)),
                  pl.BlockSpec((1, 6, d), lambda i: (row_of_tile(i), 0, 0)),
                  pl.BlockSpec((tm, ym.shape[1]), lambda i: (i, 0)),
                  pl.BlockSpec((tm, yd.shape[1]), lambda i: (i, 0)),
                  pl.BlockSpec((tm, d), lambda i: (i, B_GA // d)),
                  pl.BlockSpec((tm, d), lambda i: (i, B_GB // d)),
                  const(wm), const(wd), const(wo)],
        out_specs=pl.BlockSpec((tm, d), lambda i: (i, 0)),
        out_shape=jax.ShapeDtypeStruct((m, d), F32),
        compiler_params=_params(("parallel",)),
        name="merge",
    )(x, mod, ym, yd, p, p, wm, wd, wo)


def _ffn_kernel(x_ref, xp_ref, xx_ref, nw_ref, mod_ref, wup_ref, cw_ref, wdn_ref, o_ref, xn_ref, ge_ref,
                *, tm, tiles_per_seq, k_shift, k_scale, k_gate):
    i = pl.program_id(0)
    j = pl.program_id(1)
    th = FFN_TILE

    @pl.when(j == 0)
    def _():
        first = (i % tiles_per_seq) == 0
        last = (i % tiles_per_seq) == tiles_per_seq - 1
        nw = nw_ref[...]
        sh = mod_ref[0, k_shift:k_shift + 1, :]
        sc = mod_ref[0, k_scale:k_scale + 1, :]
        xn_ref[0:HALO16, :] = jnp.where(first, 0.0, _norm_mod(xp_ref[...], nw, sh, sc)).astype(BF16)
        xn_ref[HALO16:HALO16 + tm, :] = _norm_mod(x_ref[...], nw, sh, sc).astype(BF16)
        xn_ref[HALO16 + tm:, :] = jnp.where(last, 0.0, _norm_mod(xx_ref[...], nw, sh, sc)).astype(BF16)
        o_ref[...] = jnp.zeros_like(o_ref)

    up = jnp.dot(xn_ref[...], wup_ref[...], preferred_element_type=F32)
    ge_ref[...] = up[:, :th]
    conv = (cw_ref[0:1, :] * ge_ref[HALO16 - 1:HALO16 - 1 + tm, :]
            + cw_ref[1:2, :] * ge_ref[HALO16:HALO16 + tm, :]
            + cw_ref[2:3, :] * ge_ref[HALO16 + 1:HALO16 + 1 + tm, :])
    hid = (_silu(conv) * up[HALO16:HALO16 + tm, th:]).astype(BF16)
    o_ref[...] += jnp.dot(hid, wdn_ref[...], preferred_element_type=F32)

    @pl.when(j == pl.num_programs(1) - 1)
    def _():
        o_ref[...] = x_ref[...] + mod_ref[0, k_gate:k_gate + 1, :] * o_ref[...]


def _ffn(x, nw, mod, row_of_tile, wup, cw, wdn, seq, tm, k_shift, k_scale, k_gate):
    m, d = x.shape
    th = FFN_TILE
    nj = wdn.shape[0] // th
    nt = seq // tm
    nb16 = tm // HALO16
    total16 = m // HALO16
    kern = functools.partial(_ffn_kernel, tm=tm, tiles_per_seq=nt, k_shift=k_shift, k_scale=k_scale,
                             k_gate=k_gate)
    return pl.pallas_call(
        kern,
        grid=(m // tm, nj),
        in_specs=[pl.BlockSpec((tm, d), lambda i, j: (i, 0)),
                  pl.BlockSpec((HALO16, d), lambda i, j: (jnp.maximum(i * nb16 - 1, 0), 0)),
                  pl.BlockSpec((HALO16, d), lambda i, j: (jnp.minimum((i + 1) * nb16, total16 - 1), 0)),
                  pl.BlockSpec((1, d), lambda i, j: (0, 0)),
                  pl.BlockSpec((1, 6, d), lambda i, j: (row_of_tile(i), 0, 0)),
                  pl.BlockSpec((d, 2 * th), lambda i, j: (0, j)),
                  pl.BlockSpec((3, th), lambda i, j: (0, j)),
                  pl.BlockSpec((th, d), lambda i, j: (j, 0))],
        out_specs=pl.BlockSpec((tm, d), lambda i, j: (i, 0)),
        out_shape=jax.ShapeDtypeStruct((m, d), F32),
        scratch_shapes=[pltpu.VMEM((tm + 2 * HALO16, d), BF16), pltpu.VMEM((tm + 2 * HALO16, th), F32)],
        compiler_params=_params(("parallel", "arbitrary")),
        name="ffn",
    )(x, x, x, nw, mod, wup, cw, wdn)


def _rope_pad_cols(r):
    n = MLA_ROPE // 4
    z = jnp.zeros(r.shape[:-1] + (2 * n,), r.dtype)
    return jnp.concatenate([r[..., 0:n], r[..., 2 * n:3 * n], z, r[..., n:2 * n], r[..., 3 * n:4 * n], z], axis=-1)


def _qk_pad_cols(w):
    lead = w.shape[:-1]
    w = w.reshape(lead + (MLA_HEADS, MLA_QK))
    out = jnp.concatenate([w[..., :MLA_NOPE], _rope_pad_cols(w[..., MLA_NOPE:])], axis=-1)
    return out.reshape(lead + (MLA_HEADS * MLA_QK_PAD,))


def _rope_tables(rows):
    n = MLA_ROPE // 4
    row = jnp.repeat(jnp.arange(rows, dtype=F32), GRID_W)
    col = jnp.tile(jnp.arange(GRID_W, dtype=F32), rows)
    inv = ROPE_BASE ** (-jnp.arange(n, dtype=F32) / n)
    ang_r = row[:, None] * inv
    ang_c = col[:, None] * inv
    t = row.shape[0]
    one = jnp.ones((t, 2 * n), F32)
    zero = jnp.zeros((t, 2 * n), F32)
    cos = jnp.concatenate([jnp.cos(ang_r), jnp.cos(ang_c), one, jnp.cos(ang_r), jnp.cos(ang_c), one], axis=-1)
    sin = jnp.concatenate([-jnp.sin(ang_r), -jnp.sin(ang_c), zero, jnp.sin(ang_r), jnp.sin(ang_c), zero], axis=-1)
    return cos, sin


def _pick_tile(n, pref):
    t = min(pref, n)
    while n % t:
        t //= 2
    return t


def kernel(x, c, ctx, c_ctx, w_ada, b_ada, norm_mix, w_in, q_a_norm, w_q_b, kv_a_norm, w_kv_b, q_norm, k_norm,
           w_o_mla, dn_conv, dn_a_log, dn_dt_bias, dn_o_norm, w_o_dn, w_out, norm_ffn, w_ffn_up, ffn_conv,
           w_ffn_down):
    batch, seq, d = x.shape
    lc = ctx.shape[1]
    assert w_ada.shape[0] == 1, "single-layer stack"
    assert seq % GRID_W == 0 and seq % DN_CHUNK == 0 and lc % DN_CHUNK == 0
    h = DN_HEADS
    l = 0

    wi = w_in[l]
    o_qa, o_kva, o_kr = 0, MLA_Q_RANK, MLA_Q_RANK + MLA_KV_RANK
    o_dq = o_kr + MLA_ROPE
    o_da = o_dq + 4 * h * DN_DK
    o_ga = o_da + 4 * h
    w_a = jnp.concatenate([wi[:, o_qa:o_kr], _rope_pad_cols(wi[:, o_kr:o_dq]), wi[:, o_da:o_ga],
                           jnp.zeros((d, A_WIDTH - A_DB - 2 * h), F32)], axis=1).astype(BF16)
    w_b = jnp.concatenate([wi[:, o_dq:o_da], wi[:, o_ga:]], axis=1).astype(BF16)
    wq = _qk_pad_cols(w_q_b[l]).astype(BF16)
    wkv = w_kv_b[l].reshape(MLA_KV_RANK, MLA_HEADS, MLA_NOPE + MLA_V)
    wkv = jnp.concatenate([wkv[:, :, :MLA_NOPE].reshape(MLA_KV_RANK, -1),
                           wkv[:, :, MLA_NOPE:].reshape(MLA_KV_RANK, -1)], axis=1).astype(BF16)
    qn = jnp.concatenate([q_norm[l, :MLA_NOPE], _rope_pad_cols(q_norm[l, MLA_NOPE:])])[None, :]
    kn = jnp.concatenate([k_norm[l, :MLA_NOPE], _rope_pad_cols(k_norm[l, MLA_NOPE:])])[None, :]
    hidden = w_ffn_down.shape[1]
    nj = -(-hidden // FFN_TILE)
    hpad = nj * FFN_TILE - hidden
    wg = jnp.pad(w_ffn_up[l][:, :hidden], ((0, 0), (0, hpad))).reshape(d, nj, FFN_TILE)
    wv = jnp.pad(w_ffn_up[l][:, hidden:], ((0, 0), (0, hpad))).reshape(d, nj, FFN_TILE)
    wup = jnp.concatenate([wg, wv], axis=2).reshape(d, nj * 2 * FFN_TILE).astype(BF16)
    fcw = jnp.pad(ffn_conv[l], ((0, 0), (0, hpad)))
    wdn = jnp.pad(w_ffn_down[l], ((0, hpad), (0, 0))).astype(BF16)
    hp = jnp.concatenate([dn_a_log[l].T, dn_dt_bias[l].T], axis=1)[:, None, :]
    cos_l, sin_l = _rope_tables(seq // GRID_W)
    cos_c, sin_c = jnp.ones((lc, ROPE_PAD), F32), jnp.zeros((lc, ROPE_PAD), F32)

    rows = jnp.concatenate([c, c_ctx[None, :], jnp.zeros((8 - batch - 1, d), F32)], axis=0)
    mod = _ada(rows, w_ada[l], b_ada[l][None, :]).reshape(8, 6, d)

    xf = x.reshape(batch * seq, d)
    cf = ctx.reshape(batch * lc, d)

    def stream(tokens, n, tm_a, tm_b, row_of):
        nm = norm_mix[l][None, :]
        a = _norm_mod_matmul(tokens, nm, mod, row_of(tm_a), w_a, F32, tm_a, A_WIDTH, 0, 1, "in_proj_small")
        p = _norm_mod_matmul(tokens, nm, mod, row_of(tm_b), w_b, BF16, tm_b, 1024, 0, 1, "in_proj_wide")
        ab = a[:, A_DA:A_DA + 4 * h].reshape(batch * n, 4, h).transpose(2, 0, 1)
        return a, p, ab

    tm_a = _pick_tile(seq, 512)
    tm_b = _pick_tile(seq, 1024)
    lat_row = lambda tm: (lambda i: i // (seq // tm))
    a_l, p_l, ab_l = stream(xf, seq, tm_a, tm_b, lat_row)
    tc = _pick_tile(lc, 256)
    ctx_row = lambda tm: (lambda i: batch)
    a_c, p_c, ab_c = stream(cf, lc, tc, tc, ctx_row)

    qan, kvan = q_a_norm[l][None, :], kv_a_norm[l][None, :]
    tmp = _pick_tile(seq, 512)
    q_l, k_l, v_l = _mla_proj(a_l, cos_l, sin_l, qan, kvan, qn, kn, wq, wkv, tmp, seq)
    _, k_c, v_c = _mla_proj(a_c, cos_c, sin_c, qan, kvan, qn, kn, wq, wkv, tc, lc)
    tq = _pick_tile(seq, 512)
    tk = _pick_tile(seq, 512)
    qt = q_l.reshape(batch, seq, MLA_HEADS, MLA_QK_PAD).transpose(0, 2, 3, 1)
    vt = v_l.reshape(batch, seq // tk, tk, MLA_HEADS, MLA_V).transpose(0, 3, 1, 4, 2)
    vct = v_c.reshape(batch, lc, MLA_HEADS, MLA_V).transpose(0, 2, 3, 1)
    y_mla = _attention(qt, k_l.reshape(batch, seq, -1), vt, k_c.reshape(batch, lc, -1), vct, tq)
    y_mla = y_mla.reshape(batch * seq, MLA_HEADS * MLA_V)

    conv_w = dn_conv[l]
    prep_c = _dn_prep(p_c, ab_c, hp, conv_w, batch, lc, _pick_tile(lc, 256))
    s_zero = jnp.zeros((2, batch, h, DN_DK, DN_DV), F32)
    _, _, s_ctx = _dn_scan(s_zero, *prep_c, _pick_tile(lc, 256) // DN_CHUNK)
    prep_l = _dn_prep(p_l, ab_l, hp, conv_w, batch, seq, _pick_tile(seq, 1024))
    o_f, o_b, _ = _dn_scan(s_ctx, *prep_l, _pick_tile(seq, 256) // DN_CHUNK)
    y_dn = _dn_out(o_f, o_b, p_l, dn_o_norm[l][None, :], _pick_tile(seq, 1024))

    tmm = _pick_tile(seq, 256)
    x1 = _merge(xf, mod, lat_row(tmm), y_mla, y_dn, p_l, w_o_mla[l].astype(BF16), w_o_dn[l].astype(BF16),
                w_out[l].astype(BF16), tmm, 2)

    tmf = _pick_tile(seq, 512)
    out = _ffn(x1, norm_ffn[l][None, :], mod, lat_row(tmf), wup, fcw, wdn, seq, tmf, 3, 4, 5)
    return out.reshape(batch, seq, d)
```

```python
import functools
import math

import jax
import jax.numpy as jnp
import numpy as np
from jax import lax
from jax.experimental import pallas as pl
from jax.experimental.pallas import tpu as pltpu

F32 = jnp.float32
BF16 = jnp.bfloat16
HIGHEST = lax.Precision.HIGHEST

EPS = 1e-6
GRID_W = 64
ROPE_BASE = 10000.0

MLA_HEADS = 8
MLA_Q_RANK = 512
MLA_KV_RANK = 512
MLA_NOPE = 128
MLA_ROPE = 64
MLA_V = 128
MLA_QK = MLA_NOPE + MLA_ROPE
MLA_QK_PAD = 256

DN_HEADS = 8
DN_DK = 128
DN_DV = 128
DN_CHUNK = 64

LANE = 128
SUBLANE = 8
ROPE_PAD = 128
VMEM_LIMIT = 56 * 1024 * 1024

A_QA, A_KVA, A_KR, A_DA, A_DB, A_WIDTH = 0, 512, 1024, 1152, 1168, 1280
B_DQ, B_DK, B_DV, B_DZ, B_GA, B_GB, B_WIDTH = 0, 1024, 2048, 3072, 4096, 6144, 8192

FFN_TILE = 512
HALO16 = 16


def _params(sem, vmem=VMEM_LIMIT):
    return pltpu.CompilerParams(dimension_semantics=sem, vmem_limit_bytes=vmem)


def _nt_dot(a, b):
    return lax.dot_general(a, b, (((1,), (1,)), ((), ())), preferred_element_type=F32)


def _silu(x):
    return x * jax.nn.sigmoid(x)


def _ada_kernel(c_ref, w_ref, b_ref, o_ref):
    a = _silu(c_ref[...]).astype(BF16)
    o_ref[...] = jnp.dot(a, w_ref[...].astype(BF16), preferred_element_type=F32) + b_ref[...]


def _ada(cs, w, b):
    m, d = cs.shape
    n = w.shape[1]
    tn = 512
    return pl.pallas_call(
        _ada_kernel,
        grid=(n // tn,),
        in_specs=[pl.BlockSpec((m, d), lambda j: (0, 0)),
                  pl.BlockSpec((d, tn), lambda j: (0, j)),
                  pl.BlockSpec((1, tn), lambda j: (0, j))],
        out_specs=pl.BlockSpec((m, tn), lambda j: (0, j)),
        out_shape=jax.ShapeDtypeStruct((m, n), F32),
        compiler_params=_params(("parallel",)),
        name="ada",
    )(cs, w, b)


def _norm_mod(x, nw, shift, scale):
    ms = jnp.mean(x * x, axis=-1, keepdims=True)
    y = x * lax.rsqrt(ms + EPS) * nw
    return y * (1.0 + scale) + shift


def _nmm_kernel(x_ref, nw_ref, mod_ref, w_ref, o_ref, xn_ref, *, k_shift, k_scale):
    @pl.when(pl.program_id(1) == 0)
    def _():
        y = _norm_mod(x_ref[...], nw_ref[...], mod_ref[0, k_shift:k_shift + 1, :],
                      mod_ref[0, k_scale:k_scale + 1, :])
        xn_ref[...] = y.astype(BF16)

    o_ref[...] = jnp.dot(xn_ref[...], w_ref[...], preferred_element_type=F32).astype(o_ref.dtype)


def _norm_mod_matmul(x, nw, mod, row_of_tile, w, out_dtype, tm, tn, k_shift, k_scale, name):
    m, d = x.shape
    n = w.shape[1]
    kern = functools.partial(_nmm_kernel, k_shift=k_shift, k_scale=k_scale)
    return pl.pallas_call(
        kern,
        grid=(m // tm, n // tn),
        in_specs=[pl.BlockSpec((tm, d), lambda i, j: (i, 0)),
                  pl.BlockSpec((1, d), lambda i, j: (0, 0)),
                  pl.BlockSpec((1, 6, d), lambda i, j: (row_of_tile(i), 0, 0)),
                  pl.BlockSpec((d, tn), lambda i, j: (0, j))],
        out_specs=pl.BlockSpec((tm, tn), lambda i, j: (i, j)),
        out_shape=jax.ShapeDtypeStruct((m, n), out_dtype),
        scratch_shapes=[pltpu.VMEM((tm, d), BF16)],
        compiler_params=_params(("parallel", "arbitrary")),
        name=name,
    )(x, nw, mod, w)


def _mla_proj_kernel(a_ref, cos_ref, sin_ref, qan_ref, kvan_ref, qn_ref, kn_ref, wq_ref, wkv_ref,
                     q_ref, k_ref, v_ref):
    cos = cos_ref[...]
    sin = sin_ref[...]

    def rope(r):
        return r * cos + pltpu.roll(r, 64, axis=1) * sin

    def rms_rows(t, w):
        return t * lax.rsqrt(jnp.mean(t * t, axis=-1, keepdims=True) + EPS) * w

    qa = rms_rows(a_ref[:, A_QA:A_QA + MLA_Q_RANK], qan_ref[...]).astype(BF16)
    q = jnp.dot(qa, wq_ref[...], preferred_element_type=F32)
    kva = rms_rows(a_ref[:, A_KVA:A_KVA + MLA_KV_RANK], kvan_ref[...]).astype(BF16)
    kv = jnp.dot(kva, wkv_ref[...], preferred_element_type=F32)
    kr = a_ref[:, A_KR:A_KR + ROPE_PAD]
    kr_ss = jnp.sum(kr * kr, axis=-1, keepdims=True)
    qn = qn_ref[...]
    kn = kn_ref[...]
    scale = MLA_QK ** -0.5 * math.log2(math.e)
    for h in range(MLA_HEADS):
        qh = q[:, h * MLA_QK_PAD:(h + 1) * MLA_QK_PAD]
        inv = lax.rsqrt(jnp.sum(qh * qh, axis=-1, keepdims=True) * (1.0 / MLA_QK) + EPS) * scale
        qh = qh * inv * qn
        q_ref[:, h * MLA_QK_PAD:h * MLA_QK_PAD + MLA_NOPE] = qh[:, :MLA_NOPE].astype(BF16)
        q_ref[:, h * MLA_QK_PAD + MLA_NOPE:(h + 1) * MLA_QK_PAD] = rope(qh[:, MLA_NOPE:]).astype(BF16)

        kh = kv[:, h * MLA_NOPE:(h + 1) * MLA_NOPE]
        inv = lax.rsqrt((jnp.sum(kh * kh, axis=-1, keepdims=True) + kr_ss) * (1.0 / MLA_QK) + EPS)
        k_ref[:, h * MLA_QK_PAD:h * MLA_QK_PAD + MLA_NOPE] = (kh * inv * kn[:, :MLA_NOPE]).astype(BF16)
        k_ref[:, h * MLA_QK_PAD + MLA_NOPE:(h + 1) * MLA_QK_PAD] = rope(
            kr * inv * kn[:, MLA_NOPE:]).astype(BF16)
    v_ref[...] = kv[:, MLA_HEADS * MLA_NOPE:].astype(BF16)


def _mla_proj(a, cos, sin, qan, kvan, qn, kn, wq, wkv, tm, seq):
    m = a.shape[0]
    nseq = seq // tm
    full = lambda arr: pl.BlockSpec(arr.shape, lambda i: (0, 0))
    return pl.pallas_call(
        _mla_proj_kernel,
        grid=(m // tm,),
        in_specs=[pl.BlockSpec((tm, A_WIDTH), lambda i: (i, 0)),
                  pl.BlockSpec((tm, ROPE_PAD), lambda i: (i % nseq, 0)),
                  pl.BlockSpec((tm, ROPE_PAD), lambda i: (i % nseq, 0)),
                  full(qan), full(kvan), full(qn), full(kn), full(wq), full(wkv)],
        out_specs=[pl.BlockSpec((tm, MLA_HEADS * MLA_QK_PAD), lambda i: (i, 0)),
                   pl.BlockSpec((tm, MLA_HEADS * MLA_QK_PAD), lambda i: (i, 0)),
                   pl.BlockSpec((tm, MLA_HEADS * MLA_V), lambda i: (i, 0))],
        out_shape=[jax.ShapeDtypeStruct((m, MLA_HEADS * MLA_QK_PAD), BF16),
                   jax.ShapeDtypeStruct((m, MLA_HEADS * MLA_QK_PAD), BF16),
                   jax.ShapeDtypeStruct((m, MLA_HEADS * MLA_V), BF16)],
        compiler_params=_params(("parallel",)),
        name="mla_proj",
    )(a, cos, sin, qan, kvan, qn, kn, wq, wkv)


ATTN_UNROLL = 8
ATTN_ROWS = 64


def _attn_kernel(qt_ref, kc_ref, vct_ref, k_ref, vt_ref, o_ref, s0_ref, s1_ref, p0_ref, p1_ref, acc_ref,
                 *, nk, unroll):
    qt = qt_ref[0, 0]
    tq = qt.shape[1]
    tk = s0_ref.shape[0]
    rows = ATTN_ROWS
    s_refs = (s0_ref, s1_ref)
    p_refs = (p0_ref, p1_ref)

    def scores(j):
        return jnp.dot(k_ref[0, pl.ds(pl.multiple_of(j * tk, tk), tk), :], qt, preferred_element_type=F32)

    def probs(s, m):
        return jnp.exp2((s - m).astype(BF16))

    s = jnp.dot(kc_ref[0], qt, preferred_element_type=F32)
    m = jnp.max(s, axis=0, keepdims=True)
    acc_ref[...] = jnp.dot(vct_ref[0, 0], probs(s, m), preferred_element_type=F32)

    def store_scores(slot, s):
        s_refs[slot][...] = s
        return jnp.max(s.reshape(tk // SUBLANE, SUBLANE, tq), axis=0)

    m8_first = store_scores(0, scores(0))
    p1_ref[...] = jnp.zeros((tk, tq), BF16)

    def tile(j, slot, carry):
        m, alpha_prev, m8 = carry
        acc_ref[...] = alpha_prev * acc_ref[...] + jnp.dot(
            vt_ref[0, 0, jnp.maximum(j - 1, 0)], p_refs[1 - slot][...], preferred_element_type=F32)
        m8_next = store_scores(1 - slot, scores(jnp.minimum(j + 1, nk - 1)))
        m_new = jnp.maximum(m, jnp.max(m8, axis=0, keepdims=True))
        for r in range(0, tk, rows):
            p_refs[slot][r:r + rows, :] = probs(s_refs[slot][r:r + rows, :], m_new)
        return m_new, jnp.exp2(m - m_new), m8_next

    def body(i, carry):
        for u in range(unroll):
            carry = tile(unroll * i + u, u % 2, carry)
        return carry

    m, alpha, _ = lax.fori_loop(0, nk // unroll, body, (m, jnp.ones((1, tq), F32), m8_first))
    acc = alpha * acc_ref[...] + jnp.dot(vt_ref[0, 0, nk - 1], p1_ref[...], preferred_element_type=F32)
    o_ref[0] = (acc[:MLA_V] / acc[MLA_V:MLA_V + 1]).T.astype(o_ref.dtype)


def _attention(qt, k, vt, kc, vct, tq):
    b, t, _ = k.shape
    lc = kc.shape[1]
    nk, vrows, tk = vt.shape[2], vt.shape[3], vt.shape[4]
    unroll = ATTN_UNROLL if nk % ATTN_UNROLL == 0 else 2
    assert nk % unroll == 0
    kern = functools.partial(_attn_kernel, nk=nk, unroll=unroll)
    return pl.pallas_call(
        kern,
        grid=(b, MLA_HEADS, t // tq),
        in_specs=[pl.BlockSpec((1, 1, MLA_QK_PAD, tq), lambda bi, h, i: (bi, h, 0, i)),
                  pl.BlockSpec((1, lc, MLA_QK_PAD), lambda bi, h, i: (bi, 0, h)),
                  pl.BlockSpec((1, 1, vrows, lc), lambda bi, h, i: (bi, h, 0, 0)),
                  pl.BlockSpec((1, t, MLA_QK_PAD), lambda bi, h, i: (bi, 0, h)),
                  pl.BlockSpec((1, 1, nk, vrows, tk), lambda bi, h, i: (bi, h, 0, 0, 0))],
        out_specs=pl.BlockSpec((1, tq, MLA_V), lambda bi, h, i: (bi, i, h)),
        out_shape=jax.ShapeDtypeStruct((b, t, MLA_HEADS * MLA_V), BF16),
        scratch_shapes=[pltpu.VMEM((tk, tq), F32), pltpu.VMEM((tk, tq), F32),
                        pltpu.VMEM((tk, tq), BF16), pltpu.VMEM((tk, tq), BF16),
                        pltpu.VMEM((vrows, tq), F32)],
        compiler_params=_params(("parallel", "parallel", "arbitrary")),
        name="attention",
    )(qt, kc, vct, k, vt)


def _split2(x):
    hi = x.astype(BF16)
    return hi, (x - hi.astype(F32)).astype(BF16)


def _bmm(a, b):
    return lax.dot_general(a, b, (((2,), (1,)), ((0,), (0,))), preferred_element_type=F32)


def _bmm_nt(a, b):
    return lax.dot_general(a, b, (((2,), (2,)), ((0,), (0,))), preferred_element_type=F32)


def _bmm_hp(a, b):
    ah, al = _split2(a)
    bh, bl = _split2(b)
    return _bmm(ah, bh) + _bmm(al, bh) + _bmm(ah, bl)


def _tri_inverse(m, eye):
    assert m.shape[-1] == 64
    n = -m
    x = eye + n
    for _ in range(3):
        nb = n.astype(BF16)
        n = _bmm(nb, nb)
        x = x + _bmm(x.astype(BF16), n.astype(BF16))
    for _ in range(2):
        r = eye - x - _bmm_hp(m, x)
        x = x + _bmm(x.astype(BF16), r.astype(BF16))
    return x


def _dn_prep_kernel(qm_ref, qp_ref, qx_ref, km_ref, kp_ref, kx_ref, vm_ref, vp_ref, vx_ref,
                    cwq_ref, cwk_ref, cwv_ref, ab_ref, hp_ref,
                    u_ref, w_ref, ke_ref, qg_ref, a_ref, ge_ref, ext_ref, *, tm, tiles_per_seq):
    i = pl.program_id(0)
    first = (i % tiles_per_seq) == 0
    last = (i % tiles_per_seq) == tiles_per_seq - 1
    c = DN_CHUNK

    def conv_silu(main_ref, prev_ref, next_ref, cw_ref):
        ext_ref[0:HALO16, :] = jnp.where(first, 0.0, prev_ref[...].astype(F32))
        ext_ref[HALO16:HALO16 + tm, :] = main_ref[...].astype(F32)
        ext_ref[HALO16 + tm:, :] = jnp.where(last, 0.0, next_ref[...].astype(F32))
        y = (cw_ref[0:1, :] * ext_ref[HALO16 - 1:HALO16 - 1 + tm, :]
             + cw_ref[1:2, :] * ext_ref[HALO16:HALO16 + tm, :]
             + cw_ref[2:3, :] * ext_ref[HALO16 + 1:HALO16 + 1 + tm, :])
        return _silu(y)

    def l2n(t):
        return t * lax.rsqrt(jnp.sum(t * t, axis=-1, keepdims=True) + EPS)

    q = l2n(conv_silu(qm_ref, qp_ref, qx_ref, cwq_ref)) * (DN_DK ** -0.5)
    k = l2n(conv_silu(km_ref, kp_ref, kx_ref, cwk_ref))
    v = conv_silu(vm_ref, vp_ref, vx_ref, cwv_ref)

    hp = hp_ref[0]
    ab = ab_ref[0]
    z = ab[:, 0:2] + hp[:, 2:4]
    softplus = jnp.maximum(z, 0.0) + jnp.log(1.0 + jnp.exp(-jnp.abs(z)))
    g_all = -jnp.exp(hp[:, 0:2]) * softplus
    beta_all = jax.nn.sigmoid(ab[:, 2:4])

    nc = tm // c

    def per_chunk(t):
        return t.reshape(nc, c, t.shape[-1])

    def both(t):
        return jnp.concatenate([t, t], axis=0)

    def by_dir(t2):
        return jnp.concatenate([per_chunk(jnp.broadcast_to(t2[:, d:d + 1], (tm, LANE))) for d in range(2)],
                               axis=0)

    shape = (2 * nc, c, c)
    fwd = lax.broadcasted_iota(jnp.int32, shape, 0) < nc
    rows = lax.broadcasted_iota(jnp.int32, shape, 1)
    cols = lax.broadcasted_iota(jnp.int32, shape, 2)
    eye_mask = rows == cols
    ahead = jnp.where(fwd, cols - rows, rows - cols)
    incl = ahead <= 0
    strict = ahead < 0
    eye = jnp.where(eye_mask, 1.0, 0.0)
    tri = jnp.where(incl, 1.0, 0.0).astype(BF16)

    g = by_dir(g_all)
    g_hi = g.astype(BF16)
    g_r = g - g_hi.astype(F32)
    g_mid = g_r.astype(BF16)
    g_lo = (g_r - g_mid.astype(F32)).astype(BF16)
    gc = _bmm(tri, g_hi) + _bmm(tri, g_mid) + _bmm(tri, g_lo)
    gc_sq = gc[:, :, :c]
    gc_row = jnp.sum(jnp.where(eye_mask, gc_sq, 0.0), axis=1, keepdims=True)
    decay = jnp.where(incl, jnp.exp(jnp.where(incl, gc_sq - gc_row, 0.0)), 0.0)
    gtot = jnp.concatenate([gc[:nc, c - 1:c, :], gc[nc:, 0:1, :]], axis=0)
    eg = jnp.exp(gc)

    beta = by_dir(beta_all)
    k3 = per_chunk(k)
    q3 = per_chunk(q)
    k3b = k3.astype(BF16)
    kk = both(_bmm_nt(k3b, k3b))
    qk = both(_bmm_nt(q3.astype(BF16), k3b))
    k2 = both(k3)
    kb = k2 * beta
    m = jnp.where(strict, beta[:, :, :c] * kk * decay, 0.0)
    tinv = _tri_inverse(m, eye)
    sol = _bmm_hp(tinv, jnp.concatenate([both(per_chunk(v)) * beta, kb * eg], axis=2))
    ke = k2 * jnp.exp(gtot - gc)
    qg = both(q3) * eg
    att = qk * decay
    gend = jnp.exp(gtot)
    for d in range(2):
        sl = slice(d * nc, (d + 1) * nc)
        u_ref[d, 0, 0] = sol[sl, :, :DN_DV].reshape(tm, DN_DV)
        w_ref[d, 0, 0] = sol[sl, :, DN_DV:].reshape(tm, DN_DK).astype(BF16)
        ke_ref[d, 0, 0] = ke[sl].reshape(tm, DN_DK).astype(BF16)
        qg_ref[d, 0, 0] = qg[sl].reshape(tm, DN_DK).astype(BF16)
        a_ref[d, 0, 0] = att[sl].reshape(tm, c).astype(BF16)
        ge_ref[d, 0, 0] = gend[sl]


def _dn_prep(p, ab, hp, conv_w, batch, seq, tm):
    h = DN_HEADS
    nt = seq // tm
    nb16 = tm // HALO16
    total16 = batch * seq // HALO16
    kern = functools.partial(_dn_prep_kernel, tm=tm, tiles_per_seq=nt)

    def triple(col0):
        return [pl.BlockSpec((tm, LANE), lambda i, hh: (i, col0 + hh)),
                pl.BlockSpec((HALO16, LANE), lambda i, hh: (jnp.maximum(i * nb16 - 1, 0), col0 + hh)),
                pl.BlockSpec((HALO16, LANE), lambda i, hh: (jnp.minimum((i + 1) * nb16, total16 - 1), col0 + hh))]

    cq, ck, cv = B_DQ // LANE, B_DK // LANE, B_DV // LANE
    row = lambda shape: pl.BlockSpec(shape, lambda i, hh: (0, i // nt, hh, i % nt, 0))
    seq_shape = lambda width, dt: jax.ShapeDtypeStruct((2, batch, h, seq, width), dt)
    return pl.pallas_call(
        kern,
        grid=(batch * nt, h),
        in_specs=triple(cq) + triple(ck) + triple(cv) + [
            pl.BlockSpec((3, LANE), lambda i, hh: (0, hh)),
            pl.BlockSpec((3, LANE), lambda i, hh: (0, h + hh)),
            pl.BlockSpec((3, LANE), lambda i, hh: (0, 2 * h + hh)),
            pl.BlockSpec((1, tm, 4), lambda i, hh: (hh, i, 0)),
            pl.BlockSpec((1, 1, 4), lambda i, hh: (hh, 0, 0))],
        out_specs=[row((2, 1, 1, tm, DN_DV)), row((2, 1, 1, tm, DN_DK)), row((2, 1, 1, tm, DN_DK)),
                   row((2, 1, 1, tm, DN_DK)), row((2, 1, 1, tm, DN_CHUNK)),
                   pl.BlockSpec((2, 1, 1, tm // DN_CHUNK, 1, LANE), lambda i, hh: (0, i // nt, hh, i % nt, 0, 0))],
        out_shape=[seq_shape(DN_DV, F32), seq_shape(DN_DK, BF16), seq_shape(DN_DK, BF16),
                   seq_shape(DN_DK, BF16), seq_shape(DN_CHUNK, BF16),
                   jax.ShapeDtypeStruct((2, batch, h, seq // DN_CHUNK, 1, LANE), F32)],
        scratch_shapes=[pltpu.VMEM((tm + 2 * HALO16, LANE), F32)],
        compiler_params=_params(("parallel", "parallel")),
        name="dn_prep",
    )(p, p, p, p, p, p, p, p, p, conv_w, conv_w, conv_w, ab, hp)


def _dn_scan_kernel(s0_ref, uf_ref, wf_ref, kf_ref, qf_ref, af_ref, gf_ref,
                    ub_ref, wb_ref, kb_ref, qb_ref, ab_ref, gb_ref,
                    of_ref, ob_ref, sfin_ref, s_ref, *, cs):
    n = pl.program_id(1)
    c = DN_CHUNK
    h = DN_HEADS

    @pl.when(n == 0)
    def _():
        s_ref[0:h] = s0_ref[0, 0]
        s_ref[h:] = s0_ref[1, 0]

    for ci in range(cs):
        rf = ci * c
        rb = (cs - 1 - ci) * c

        def pair(f_ref, b_ref):
            return jnp.concatenate([f_ref[0, 0, :, rf:rf + c, :], b_ref[0, 0, :, rb:rb + c, :]], axis=0)

        s = s_ref[...]
        s16 = s.astype(BF16)
        ws_qs = _bmm(jnp.concatenate([pair(wf_ref, wb_ref), pair(qf_ref, qb_ref)], axis=1), s16)
        v_new = pair(uf_ref, ub_ref) - ws_qs[:, :c]
        v16 = v_new.astype(BF16)
        o = ws_qs[:, c:] + _bmm(pair(af_ref, ab_ref), v16)
        of_ref[0, :, rf:rf + c, :] = o[:h].astype(of_ref.dtype)
        ob_ref[0, :, rb:rb + c, :] = o[h:].astype(ob_ref.dtype)
        g = jnp.concatenate([gf_ref[0, 0, :, ci], gb_ref[0, 0, :, cs - 1 - ci]], axis=0)
        s_ref[...] = s * g + lax.dot_general(pair(kf_ref, kb_ref), v16, (((1,), (1,)), ((0,), (0,))),
                                             preferred_element_type=F32)

    @pl.when(n == pl.num_programs(1) - 1)
    def _():
        sfin_ref[0, 0] = s_ref[0:h]
        sfin_ref[1, 0] = s_ref[h:]


def _dn_scan(s0, u, w, ke, qg, a, ge, cs):
    _, batch, h, seq, _ = u.shape
    ts = cs * DN_CHUNK
    ns = seq // ts
    kern = functools.partial(_dn_scan_kernel, cs=cs)

    def specs(d):
        idx = (lambda n: n) if d == 0 else (lambda n: ns - 1 - n)
        seqs = [pl.BlockSpec((1, 1, h, ts, width), lambda b, n: (d, b, 0, idx(n), 0))
                for width in (DN_DV, DN_DK, DN_DK, DN_DK, DN_CHUNK)]
        return seqs + [pl.BlockSpec((1, 1, h, cs, 1, LANE), lambda b, n: (d, b, 0, idx(n), 0, 0))]

    o_spec = lambda d: pl.BlockSpec((1, h, ts, DN_DV), lambda b, n: (b, 0, n if d == 0 else ns - 1 - n, 0))
    state = pl.BlockSpec((2, 1, h, DN_DK, DN_DV), lambda b, n: (0, b, 0, 0, 0))
    return pl.pallas_call(
        kern,
        grid=(batch, ns),
        in_specs=[state] + specs(0) + specs(1),
        out_specs=[o_spec(0), o_spec(1), state],
        out_shape=[jax.ShapeDtypeStruct((batch, h, seq, DN_DV), BF16),
                   jax.ShapeDtypeStruct((batch, h, seq, DN_DV), BF16),
                   jax.ShapeDtypeStruct((2, batch, h, DN_DK, DN_DV), F32)],
        scratch_shapes=[pltpu.VMEM((2 * h, DN_DK, DN_DV), F32)],
        compiler_params=_params(("parallel", "arbitrary")),
        name="dn_scan",
    )(s0, u, w, ke, qg, a, ge, u, w, ke, qg, a, ge)


def _merge_kernel(x_ref, mod_ref, ym_ref, of_ref, ob_ref, z_ref, nw_ref, ga_ref, gb_ref, wm_ref, wd_ref, wo_ref,
                  o_ref, *, k_gate):
    nw = nw_ref[...]
    heads = []
    for h in range(DN_HEADS):
        o = of_ref[0, h].astype(F32) + ob_ref[0, h].astype(F32)
        y = o * lax.rsqrt(jnp.mean(o * o, axis=-1, keepdims=True) + EPS) * nw
        heads.append((y * _silu(z_ref[:, h * DN_DV:(h + 1) * DN_DV].astype(F32))).astype(BF16))
    yd = jnp.concatenate(heads, axis=1)
    pm = jnp.dot(ym_ref[...], wm_ref[...], preferred_element_type=F32)
    pd = jnp.dot(yd, wd_ref[...], preferred_element_type=F32)
    merged = (jax.nn.sigmoid(ga_ref[...].astype(F32)) * pm
              + jax.nn.sigmoid(gb_ref[...].astype(F32)) * pd).astype(BF16)
    y = jnp.dot(merged, wo_ref[...], preferred_element_type=F32)
    o_ref[...] = x_ref[...] + mod_ref[0, k_gate:k_gate + 1, :] * y


def _merge(x, mod, row_of_tile, ym, o_f, o_b, p, nw, wm, wd, wo, tm, k_gate):
    m, d = x.shape
    _, h, seq, _ = o_f.shape
    nt = seq // tm
    zw = h * DN_DV
    kern = functools.partial(_merge_kernel, k_gate=k_gate)
    const = lambda arr: pl.BlockSpec(arr.shape, lambda i: (0, 0), pipeline_mode=pl.Buffered(1))
    scan_out = pl.BlockSpec((1, h, tm, DN_DV), lambda i: (i // nt, 0, i % nt, 0))
    return pl.pallas_call(
        kern,
        grid=(m // tm,),
        in_specs=[pl.BlockSpec((tm, d), lambda i: (i, 0)),
                  pl.BlockSpec((1, 6, d), lambda i: (row_of_tile(i), 0, 0)),
                  pl.BlockSpec((tm, ym.shape[1]), lambda i: (i, 0)),
                  scan_out, scan_out,
                  pl.BlockSpec((tm, zw), lambda i: (i, B_DZ // zw)),
                  pl.BlockSpec((1, DN_DV), lambda i: (0, 0)),
                  pl.BlockSpec((tm, d), lambda i: (i, B_GA // d)),
                  pl.BlockSpec((tm, d), lambda i: (i, B_GB // d)),
                  const(wm), const(wd), const(wo)],
        out_specs=pl.BlockSpec((tm, d), lambda i: (i, 0)),
        out_shape=jax.ShapeDtypeStruct((m, d), F32),
        compiler_params=_params(("parallel",)),
        name="merge",
    )(x, mod, ym, o_f, o_b, p, nw, p, p, wm, wd, wo)


def _ffn_kernel(x_ref, xp_ref, xx_ref, nw_ref, mod_ref, wup_ref, cw_ref, wdn_ref, o_ref, xn_ref, ge_ref,
                *, tm, tiles_per_seq, k_shift, k_scale, k_gate):
    i = pl.program_id(0)
    j = pl.program_id(1)
    th = FFN_TILE

    @pl.when(j == 0)
    def _():
        first = (i % tiles_per_seq) == 0
        last = (i % tiles_per_seq) == tiles_per_seq - 1
        nw = nw_ref[...]
        sh = mod_ref[0, k_shift:k_shift + 1, :]
        sc = mod_ref[0, k_scale:k_scale + 1, :]
        xn_ref[0:HALO16, :] = jnp.where(first, 0.0, _norm_mod(xp_ref[...], nw, sh, sc)).astype(BF16)
        xn_ref[HALO16:HALO16 + tm, :] = _norm_mod(x_ref[...], nw, sh, sc).astype(BF16)
        xn_ref[HALO16 + tm:, :] = jnp.where(last, 0.0, _norm_mod(xx_ref[...], nw, sh, sc)).astype(BF16)
        o_ref[...] = jnp.zeros_like(o_ref)

    up = jnp.dot(xn_ref[...], wup_ref[...], preferred_element_type=F32)
    ge_ref[...] = up[:, :th]
    conv = (cw_ref[0:1, :] * ge_ref[HALO16 - 1:HALO16 - 1 + tm, :]
            + cw_ref[1:2, :] * ge_ref[HALO16:HALO16 + tm, :]
            + cw_ref[2:3, :] * ge_ref[HALO16 + 1:HALO16 + 1 + tm, :])
    hid = (_silu(conv) * up[HALO16:HALO16 + tm, th:]).astype(BF16)
    o_ref[...] += jnp.dot(hid, wdn_ref[...], preferred_element_type=F32)

    @pl.when(j == pl.num_programs(1) - 1)
    def _():
        o_ref[...] = x_ref[...] + mod_ref[0, k_gate:k_gate + 1, :] * o_ref[...]


def _ffn(x, nw, mod, row_of_tile, wup, cw, wdn, seq, tm, k_shift, k_scale, k_gate):
    m, d = x.shape
    th = FFN_TILE
    nj = wdn.shape[0] // th
    nt = seq // tm
    nb16 = tm // HALO16
    total16 = m // HALO16
    kern = functools.partial(_ffn_kernel, tm=tm, tiles_per_seq=nt, k_shift=k_shift, k_scale=k_scale,
                             k_gate=k_gate)
    return pl.pallas_call(
        kern,
        grid=(m // tm, nj),
        in_specs=[pl.BlockSpec((tm, d), lambda i, j: (i, 0)),
                  pl.BlockSpec((HALO16, d), lambda i, j: (jnp.maximum(i * nb16 - 1, 0), 0)),
                  pl.BlockSpec((HALO16, d), lambda i, j: (jnp.minimum((i + 1) * nb16, total16 - 1), 0)),
                  pl.BlockSpec((1, d), lambda i, j: (0, 0)),
                  pl.BlockSpec((1, 6, d), lambda i, j: (row_of_tile(i), 0, 0)),
                  pl.BlockSpec((d, 2 * th), lambda i, j: (0, j)),
                  pl.BlockSpec((3, th), lambda i, j: (0, j)),
                  pl.BlockSpec((th, d), lambda i, j: (j, 0))],
        out_specs=pl.BlockSpec((tm, d), lambda i, j: (i, 0)),
        out_shape=jax.ShapeDtypeStruct((m, d), F32),
        scratch_shapes=[pltpu.VMEM((tm + 2 * HALO16, d), BF16), pltpu.VMEM((tm + 2 * HALO16, th), F32)],
        compiler_params=_params(("parallel", "arbitrary")),
        name="ffn",
    )(x, x, x, nw, mod, wup, cw, wdn)


def _rope_pad_cols(r):
    n = MLA_ROPE // 4
    z = jnp.zeros(r.shape[:-1] + (2 * n,), r.dtype)
    return jnp.concatenate([r[..., 0:n], r[..., 2 * n:3 * n], z, r[..., n:2 * n], r[..., 3 * n:4 * n], z], axis=-1)


def _qk_pad_cols(w):
    lead = w.shape[:-1]
    w = w.reshape(lead + (MLA_HEADS, MLA_QK))
    out = jnp.concatenate([w[..., :MLA_NOPE], _rope_pad_cols(w[..., MLA_NOPE:])], axis=-1)
    return out.reshape(lead + (MLA_HEADS * MLA_QK_PAD,))


def _rope_tables(rows):
    n = MLA_ROPE // 4
    row = jnp.repeat(jnp.arange(rows, dtype=F32), GRID_W)
    col = jnp.tile(jnp.arange(GRID_W, dtype=F32), rows)
    inv = ROPE_BASE ** (-jnp.arange(n, dtype=F32) / n)
    ang_r = row[:, None] * inv
    ang_c = col[:, None] * inv
    t = row.shape[0]
    one = jnp.ones((t, 2 * n), F32)
    zero = jnp.zeros((t, 2 * n), F32)
    cos = jnp.concatenate([jnp.cos(ang_r), jnp.cos(ang_c), one, jnp.cos(ang_r), jnp.cos(ang_c), one], axis=-1)
    sin = jnp.concatenate([-jnp.sin(ang_r), -jnp.sin(ang_c), zero, jnp.sin(ang_r), jnp.sin(ang_c), zero], axis=-1)
    return cos, sin


def _pick_tile(n, pref):
    t = min(pref, n)
    while n % t:
        t //= 2
    return t


def kernel(x, c, ctx, c_ctx, w_ada, b_ada, norm_mix, w_in, q_a_norm, w_q_b, kv_a_norm, w_kv_b, q_norm, k_norm,
           w_o_mla, dn_conv, dn_a_log, dn_dt_bias, dn_o_norm, w_o_dn, w_out, norm_ffn, w_ffn_up, ffn_conv,
           w_ffn_down):
    batch, seq, d = x.shape
    lc = ctx.shape[1]
    assert w_ada.shape[0] == 1, "single-layer stack"
    assert seq % GRID_W == 0 and seq % DN_CHUNK == 0 and lc % DN_CHUNK == 0
    h = DN_HEADS
    l = 0

    wi = w_in[l]
    o_qa, o_kva, o_kr = 0, MLA_Q_RANK, MLA_Q_RANK + MLA_KV_RANK
    o_dq = o_kr + MLA_ROPE
    o_da = o_dq + 4 * h * DN_DK
    o_ga = o_da + 4 * h
    w_a = jnp.concatenate([wi[:, o_qa:o_kr], _rope_pad_cols(wi[:, o_kr:o_dq]), wi[:, o_da:o_ga],
                           jnp.zeros((d, A_WIDTH - A_DB - 2 * h), F32)], axis=1).astype(BF16)
    w_b = jnp.concatenate([wi[:, o_dq:o_da], wi[:, o_ga:]], axis=1).astype(BF16)
    wq = _qk_pad_cols(w_q_b[l]).astype(BF16)
    wkv = w_kv_b[l].reshape(MLA_KV_RANK, MLA_HEADS, MLA_NOPE + MLA_V)
    wkv = jnp.concatenate([wkv[:, :, :MLA_NOPE].reshape(MLA_KV_RANK, -1),
                           wkv[:, :, MLA_NOPE:].reshape(MLA_KV_RANK, -1)], axis=1).astype(BF16)
    qn = jnp.concatenate([q_norm[l, :MLA_NOPE], _rope_pad_cols(q_norm[l, MLA_NOPE:])])[None, :]
    kn = jnp.concatenate([k_norm[l, :MLA_NOPE], _rope_pad_cols(k_norm[l, MLA_NOPE:])])[None, :]
    hidden = w_ffn_down.shape[1]
    nj = -(-hidden // FFN_TILE)
    hpad = nj * FFN_TILE - hidden
    wg = jnp.pad(w_ffn_up[l][:, :hidden], ((0, 0), (0, hpad))).reshape(d, nj, FFN_TILE)
    wv = jnp.pad(w_ffn_up[l][:, hidden:], ((0, 0), (0, hpad))).reshape(d, nj, FFN_TILE)
    wup = jnp.concatenate([wg, wv], axis=2).reshape(d, nj * 2 * FFN_TILE).astype(BF16)
    fcw = jnp.pad(ffn_conv[l], ((0, 0), (0, hpad)))
    wdn = jnp.pad(w_ffn_down[l], ((0, hpad), (0, 0))).astype(BF16)
    hp = jnp.concatenate([dn_a_log[l].T, dn_dt_bias[l].T], axis=1)[:, None, :]
    cos_l, sin_l = _rope_tables(seq // GRID_W)
    cos_c, sin_c = jnp.ones((lc, ROPE_PAD), F32), jnp.zeros((lc, ROPE_PAD), F32)

    rows = jnp.concatenate([c, c_ctx[None, :], jnp.zeros((8 - batch - 1, d), F32)], axis=0)
    mod = _ada(rows, w_ada[l], b_ada[l][None, :]).reshape(8, 6, d)

    xf = x.reshape(batch * seq, d)
    cf = ctx.reshape(batch * lc, d)

    def stream(tokens, n, tm_a, tm_b, row_of):
        nm = norm_mix[l][None, :]
        a = _norm_mod_matmul(tokens, nm, mod, row_of(tm_a), w_a, F32, tm_a, A_WIDTH, 0, 1, "in_proj_small")
        p = _norm_mod_matmul(tokens, nm, mod, row_of(tm_b), w_b, BF16, tm_b, 1024, 0, 1, "in_proj_wide")
        ab = a[:, A_DA:A_DA + 4 * h].reshape(batch * n, 4, h).transpose(2, 0, 1)
        return a, p, ab

    tm_a = _pick_tile(seq, 512)
    tm_b = _pick_tile(seq, 1024)
    lat_row = lambda tm: (lambda i: i // (seq // tm))
    a_l, p_l, ab_l = stream(xf, seq, tm_a, tm_b, lat_row)
    tc = _pick_tile(lc, 256)
    ctx_row = lambda tm: (lambda i: batch)
    a_c, p_c, ab_c = stream(cf, lc, tc, tc, ctx_row)

    qan, kvan = q_a_norm[l][None, :], kv_a_norm[l][None, :]
    tmp = _pick_tile(seq, 512)
    q_l, k_l, v_l = _mla_proj(a_l, cos_l, sin_l, qan, kvan, qn, kn, wq, wkv, tmp, seq)
    _, k_c, v_c = _mla_proj(a_c, cos_c, sin_c, qan, kvan, qn, kn, wq, wkv, tc, lc)
    tq = _pick_tile(seq, 512)
    tk = _pick_tile(seq, 512)
    qt = q_l.reshape(batch, seq, MLA_HEADS, MLA_QK_PAD).transpose(0, 2, 3, 1)

    def with_ones_row(vt):
        lead, n = vt.shape[:-2], vt.shape[-1]
        return jnp.concatenate([vt, jnp.ones(lead + (1, n), BF16), jnp.zeros(lead + (HALO16 - 1, n), BF16)],
                               axis=-2)

    vt = with_ones_row(v_l.reshape(batch, seq // tk, tk, MLA_HEADS, MLA_V).transpose(0, 3, 1, 4, 2))
    vct = with_ones_row(v_c.reshape(batch, lc, MLA_HEADS, MLA_V).transpose(0, 2, 3, 1))
    y_mla = _attention(qt, k_l.reshape(batch, seq, -1), vt, k_c.reshape(batch, lc, -1), vct, tq)
    y_mla = y_mla.reshape(batch * seq, MLA_HEADS * MLA_V)

    conv_w = dn_conv[l]
    prep_c = _dn_prep(p_c, ab_c, hp, conv_w, batch, lc, _pick_tile(lc, 256))
    s_zero = jnp.zeros((2, batch, h, DN_DK, DN_DV), F32)
    _, _, s_ctx = _dn_scan(s_zero, *prep_c, _pick_tile(lc, 256) // DN_CHUNK)
    prep_l = _dn_prep(p_l, ab_l, hp, conv_w, batch, seq, _pick_tile(seq, 1024))
    o_f, o_b, _ = _dn_scan(s_ctx, *prep_l, _pick_tile(seq, 256) // DN_CHUNK)

    tmm = _pick_tile(seq, 256)
    x1 = _merge(xf, mod, lat_row(tmm), y_mla, o_f, o_b, p_l, dn_o_norm[l][None, :], w_o_mla[l].astype(BF16),
                w_o_dn[l].astype(BF16), w_out[l].astype(BF16), tmm, 2)

    tmf = _pick_tile(seq, 512)
    out = _ffn(x1, norm_ffn[l][None, :], mod, lat_row(tmf), wup, fcw, wdn, seq, tmf, 3, 4, 5)
    return out.reshape(batch, seq, d)
```

```python
import functools
import math

import jax
import jax.numpy as jnp
import numpy as np
from jax import lax
from jax.experimental import pallas as pl
from jax.experimental.pallas import tpu as pltpu

F32 = jnp.float32
BF16 = jnp.bfloat16
HIGHEST = lax.Precision.HIGHEST

EPS = 1e-6
GRID_W = 64
ROPE_BASE = 10000.0

MLA_HEADS = 8
MLA_Q_RANK = 512
MLA_KV_RANK = 512
MLA_NOPE = 128
MLA_ROPE = 64
MLA_V = 128
MLA_QK = MLA_NOPE + MLA_ROPE
MLA_QK_PAD = 256

DN_HEADS = 8
DN_DK = 128
DN_DV = 128
DN_CHUNK = 64

LANE = 128
SUBLANE = 8
ROPE_PAD = 128
VMEM_LIMIT = 56 * 1024 * 1024

A_QA, A_KVA, A_KR, A_DA, A_DB, A_WIDTH = 0, 512, 1024, 1152, 1168, 1280
B_DQ, B_DK, B_DV, B_DZ, B_GA, B_GB, B_WIDTH = 0, 1024, 2048, 3072, 4096, 6144, 8192

FFN_TILE = 512
HALO16 = 16


def _params(sem, vmem=VMEM_LIMIT):
    return pltpu.CompilerParams(dimension_semantics=sem, vmem_limit_bytes=vmem)


def _nt_dot(a, b):
    return lax.dot_general(a, b, (((1,), (1,)), ((), ())), preferred_element_type=F32)


def _silu(x):
    return x * jax.nn.sigmoid(x)


def _ada_kernel(c_ref, w_ref, b_ref, o_ref):
    a = _silu(c_ref[...]).astype(BF16)
    o_ref[...] = jnp.dot(a, w_ref[...].astype(BF16), preferred_element_type=F32) + b_ref[...]


def _ada(cs, w, b):
    m, d = cs.shape
    n = w.shape[1]
    tn = 512
    return pl.pallas_call(
        _ada_kernel,
        grid=(n // tn,),
        in_specs=[pl.BlockSpec((m, d), lambda j: (0, 0)),
                  pl.BlockSpec((d, tn), lambda j: (0, j)),
                  pl.BlockSpec((1, tn), lambda j: (0, j))],
        out_specs=pl.BlockSpec((m, tn), lambda j: (0, j)),
        out_shape=jax.ShapeDtypeStruct((m, n), F32),
        compiler_params=_params(("parallel",)),
        name="ada",
    )(cs, w, b)


def _norm_mod(x, nw, shift, scale):
    ms = jnp.mean(x * x, axis=-1, keepdims=True)
    y = x * lax.rsqrt(ms + EPS) * nw
    return y * (1.0 + scale) + shift


def _nmm_kernel(x_ref, nw_ref, mod_ref, w_ref, o_ref, xn_ref, *, k_shift, k_scale):
    @pl.when(pl.program_id(1) == 0)
    def _():
        y = _norm_mod(x_ref[...], nw_ref[...], mod_ref[0, k_shift:k_shift + 1, :],
                      mod_ref[0, k_scale:k_scale + 1, :])
        xn_ref[...] = y.astype(BF16)

    o_ref[...] = jnp.dot(xn_ref[...], w_ref[...], preferred_element_type=F32).astype(o_ref.dtype)


def _norm_mod_matmul(x, nw, mod, row_of_tile, w, out_dtype, tm, tn, k_shift, k_scale, name):
    m, d = x.shape
    n = w.shape[1]
    kern = functools.partial(_nmm_kernel, k_shift=k_shift, k_scale=k_scale)
    return pl.pallas_call(
        kern,
        grid=(m // tm, n // tn),
        in_specs=[pl.BlockSpec((tm, d), lambda i, j: (i, 0)),
                  pl.BlockSpec((1, d), lambda i, j: (0, 0)),
                  pl.BlockSpec((1, 6, d), lambda i, j: (row_of_tile(i), 0, 0)),
                  pl.BlockSpec((d, tn), lambda i, j: (0, j))],
        out_specs=pl.BlockSpec((tm, tn), lambda i, j: (i, j)),
        out_shape=jax.ShapeDtypeStruct((m, n), out_dtype),
        scratch_shapes=[pltpu.VMEM((tm, d), BF16)],
        compiler_params=_params(("parallel", "arbitrary")),
        name=name,
    )(x, nw, mod, w)


def _mla_proj_kernel(a_ref, cr_ref, sr_ref, cc_ref, sc_ref, qan_ref, kvan_ref, qn_ref, kn_ref, wq_ref, wkv_ref,
                     q_ref, k_ref, v_ref):
    tm = a_ref.shape[0]
    g = tm // GRID_W

    def table(row_ref, col_ref):
        r = jnp.broadcast_to(row_ref[...][:, None, :], (g, GRID_W, ROPE_PAD)).reshape(tm, ROPE_PAD)
        c = jnp.broadcast_to(col_ref[...][None], (g, GRID_W, ROPE_PAD)).reshape(tm, ROPE_PAD)
        return r + c

    cos = table(cr_ref, cc_ref)
    sin = table(sr_ref, sc_ref)

    def rope(r):
        return r * cos + pltpu.roll(r, 64, axis=1) * sin

    def rms_rows(t, w):
        return t * lax.rsqrt(jnp.mean(t * t, axis=-1, keepdims=True) + EPS) * w

    qa = rms_rows(a_ref[:, A_QA:A_QA + MLA_Q_RANK], qan_ref[...]).astype(BF16)
    q = jnp.dot(qa, wq_ref[...], preferred_element_type=F32)
    kva = rms_rows(a_ref[:, A_KVA:A_KVA + MLA_KV_RANK], kvan_ref[...]).astype(BF16)
    kv = jnp.dot(kva, wkv_ref[...], preferred_element_type=F32)
    kr = a_ref[:, A_KR:A_KR + ROPE_PAD]
    kr_ss = jnp.sum(kr * kr, axis=-1, keepdims=True)
    qn = qn_ref[...]
    kn = kn_ref[...]
    scale = MLA_QK ** -0.5 * math.log2(math.e)
    for h in range(MLA_HEADS):
        qh = q[:, h * MLA_QK_PAD:(h + 1) * MLA_QK_PAD]
        inv = lax.rsqrt(jnp.sum(qh * qh, axis=-1, keepdims=True) * (1.0 / MLA_QK) + EPS) * scale
        qh = qh * inv * qn
        q_ref[:, h * MLA_QK_PAD:h * MLA_QK_PAD + MLA_NOPE] = qh[:, :MLA_NOPE].astype(BF16)
        q_ref[:, h * MLA_QK_PAD + MLA_NOPE:(h + 1) * MLA_QK_PAD] = rope(qh[:, MLA_NOPE:]).astype(BF16)

        kh = kv[:, h * MLA_NOPE:(h + 1) * MLA_NOPE]
        inv = lax.rsqrt((jnp.sum(kh * kh, axis=-1, keepdims=True) + kr_ss) * (1.0 / MLA_QK) + EPS)
        k_ref[:, h * MLA_QK_PAD:h * MLA_QK_PAD + MLA_NOPE] = (kh * inv * kn[:, :MLA_NOPE]).astype(BF16)
        k_ref[:, h * MLA_QK_PAD + MLA_NOPE:(h + 1) * MLA_QK_PAD] = rope(
            kr * inv * kn[:, MLA_NOPE:]).astype(BF16)
    v_ref[...] = kv[:, MLA_HEADS * MLA_NOPE:].astype(BF16)


def _mla_proj(a, tables, qan, kvan, qn, kn, wq, wkv, tm, seq):
    m = a.shape[0]
    nseq = seq // tm
    row_cos, row_sin, col_cos, col_sin = tables
    full = lambda arr: pl.BlockSpec(arr.shape, lambda i: (0, 0))
    row_spec = pl.BlockSpec((tm // GRID_W, ROPE_PAD), lambda i: (i % nseq, 0))
    return pl.pallas_call(
        _mla_proj_kernel,
        grid=(m // tm,),
        in_specs=[pl.BlockSpec((tm, A_WIDTH), lambda i: (i, 0)),
                  row_spec, row_spec, full(col_cos), full(col_sin),
                  full(qan), full(kvan), full(qn), full(kn), full(wq), full(wkv)],
        out_specs=[pl.BlockSpec((tm, MLA_HEADS * MLA_QK_PAD), lambda i: (i, 0)),
                   pl.BlockSpec((tm, MLA_HEADS * MLA_QK_PAD), lambda i: (i, 0)),
                   pl.BlockSpec((tm, MLA_HEADS * MLA_V), lambda i: (i, 0))],
        out_shape=[jax.ShapeDtypeStruct((m, MLA_HEADS * MLA_QK_PAD), BF16),
                   jax.ShapeDtypeStruct((m, MLA_HEADS * MLA_QK_PAD), BF16),
                   jax.ShapeDtypeStruct((m, MLA_HEADS * MLA_V), BF16)],
        compiler_params=_params(("parallel",)),
        name="mla_proj",
    )(a, row_cos, row_sin, col_cos, col_sin, qan, kvan, qn, kn, wq, wkv)


ATTN_UNROLL = 8
ATTN_ROWS = 64


def _attn_kernel(qt_ref, kc_ref, vct_ref, k_ref, vt_ref, o_ref, s0_ref, s1_ref, p0_ref, p1_ref, acc_ref,
                 *, nk, unroll):
    qt = qt_ref[0, 0]
    tq = qt.shape[1]
    tk = s0_ref.shape[0]
    rows = ATTN_ROWS
    s_refs = (s0_ref, s1_ref)
    p_refs = (p0_ref, p1_ref)

    def scores(j):
        return jnp.dot(k_ref[0, pl.ds(pl.multiple_of(j * tk, tk), tk), :], qt, preferred_element_type=F32)

    def probs(s, m):
        return jnp.exp2((s - m).astype(BF16))

    s = jnp.dot(kc_ref[0], qt, preferred_element_type=F32)
    m = jnp.max(s, axis=0, keepdims=True)
    acc_ref[...] = jnp.dot(vct_ref[0, 0], probs(s, m), preferred_element_type=F32)

    def store_scores(slot, s):
        s_refs[slot][...] = s
        return jnp.max(s.reshape(tk // SUBLANE, SUBLANE, tq), axis=0)

    m8_first = store_scores(0, scores(0))
    p1_ref[...] = jnp.zeros((tk, tq), BF16)

    def tile(j, slot, carry):
        m, alpha_prev, m8 = carry
        acc_ref[...] = alpha_prev * acc_ref[...] + jnp.dot(
            vt_ref[0, 0, jnp.maximum(j - 1, 0)], p_refs[1 - slot][...], preferred_element_type=F32)
        m8_next = store_scores(1 - slot, scores(jnp.minimum(j + 1, nk - 1)))
        m_new = jnp.maximum(m, jnp.max(m8, axis=0, keepdims=True))
        for r in range(0, tk, rows):
            p_refs[slot][r:r + rows, :] = probs(s_refs[slot][r:r + rows, :], m_new)
        return m_new, jnp.exp2(m - m_new), m8_next

    def body(i, carry):
        for u in range(unroll):
            carry = tile(unroll * i + u, u % 2, carry)
        return carry

    m, alpha, _ = lax.fori_loop(0, nk // unroll, body, (m, jnp.ones((1, tq), F32), m8_first))
    acc = alpha * acc_ref[...] + jnp.dot(vt_ref[0, 0, nk - 1], p1_ref[...], preferred_element_type=F32)
    o_ref[0] = (acc[:MLA_V] / acc[MLA_V:MLA_V + 1]).T.astype(o_ref.dtype)


def _attention(qt, k, vt, kc, vct, tq):
    b, t, _ = k.shape
    lc = kc.shape[1]
    nk, vrows, tk = vt.shape[2], vt.shape[3], vt.shape[4]
    unroll = ATTN_UNROLL if nk % ATTN_UNROLL == 0 else 2
    assert nk % unroll == 0
    kern = functools.partial(_attn_kernel, nk=nk, unroll=unroll)
    return pl.pallas_call(
        kern,
        grid=(b, MLA_HEADS, t // tq),
        in_specs=[pl.BlockSpec((1, 1, MLA_QK_PAD, tq), lambda bi, h, i: (bi, h, 0, i)),
                  pl.BlockSpec((1, lc, MLA_QK_PAD), lambda bi, h, i: (bi, 0, h)),
                  pl.BlockSpec((1, 1, vrows, lc), lambda bi, h, i: (bi, h, 0, 0)),
                  pl.BlockSpec((1, t, MLA_QK_PAD), lambda bi, h, i: (bi, 0, h)),
                  pl.BlockSpec((1, 1, nk, vrows, tk), lambda bi, h, i: (bi, h, 0, 0, 0))],
        out_specs=pl.BlockSpec((1, tq, MLA_V), lambda bi, h, i: (bi, i, h)),
        out_shape=jax.ShapeDtypeStruct((b, t, MLA_HEADS * MLA_V), BF16),
        scratch_shapes=[pltpu.VMEM((tk, tq), F32), pltpu.VMEM((tk, tq), F32),
                        pltpu.VMEM((tk, tq), BF16), pltpu.VMEM((tk, tq), BF16),
                        pltpu.VMEM((vrows, tq), F32)],
        compiler_params=_params(("parallel", "parallel", "arbitrary")),
        name="attention",
    )(qt, kc, vct, k, vt)


def _split2(x):
    hi = x.astype(BF16)
    return hi, (x - hi.astype(F32)).astype(BF16)


def _bmm(a, b):
    return lax.dot_general(a, b, (((2,), (1,)), ((0,), (0,))), preferred_element_type=F32)


def _bmm_nt(a, b):
    return lax.dot_general(a, b, (((2,), (2,)), ((0,), (0,))), preferred_element_type=F32)


def _bmm_hp(a, b):
    ah, al = _split2(a)
    bh, bl = _split2(b)
    return _bmm(ah, bh) + _bmm(al, bh) + _bmm(ah, bl)


def _tri_inverse(m, eye):
    assert m.shape[-1] == 64
    n = -m
    x = eye + n
    for _ in range(3):
        nb = n.astype(BF16)
        n = _bmm(nb, nb)
        x = x + _bmm(x.astype(BF16), n.astype(BF16))
    xb = x.astype(BF16)
    r = eye - x - _bmm(m.astype(BF16), xb)
    x = x + _bmm(xb, r.astype(BF16))
    r = eye - x - _bmm_hp(m, x)
    return x + _bmm(x.astype(BF16), r.astype(BF16))


def _dn_prep_kernel(qm_ref, qp_ref, qx_ref, km_ref, kp_ref, kx_ref, vm_ref, vp_ref, vx_ref,
                    cwq_ref, cwk_ref, cwv_ref, ab_ref, hp_ref,
                    u_ref, w_ref, ke_ref, qg_ref, a_ref, ge_ref, ext_ref, *, tm, tiles_per_seq):
    i = pl.program_id(0)
    first = (i % tiles_per_seq) == 0
    last = (i % tiles_per_seq) == tiles_per_seq - 1
    c = DN_CHUNK

    def conv_silu(main_ref, prev_ref, next_ref, cw_ref):
        ext_ref[0:HALO16, :] = jnp.where(first, 0.0, prev_ref[...].astype(F32))
        ext_ref[HALO16:HALO16 + tm, :] = main_ref[...].astype(F32)
        ext_ref[HALO16 + tm:, :] = jnp.where(last, 0.0, next_ref[...].astype(F32))
        y = (cw_ref[0:1, :] * ext_ref[HALO16 - 1:HALO16 - 1 + tm, :]
             + cw_ref[1:2, :] * ext_ref[HALO16:HALO16 + tm, :]
             + cw_ref[2:3, :] * ext_ref[HALO16 + 1:HALO16 + 1 + tm, :])
        return _silu(y)

    def l2n(t):
        return t * lax.rsqrt(jnp.sum(t * t, axis=-1, keepdims=True) + EPS)

    q = l2n(conv_silu(qm_ref, qp_ref, qx_ref, cwq_ref)) * (DN_DK ** -0.5)
    k = l2n(conv_silu(km_ref, kp_ref, kx_ref, cwk_ref))
    v = conv_silu(vm_ref, vp_ref, vx_ref, cwv_ref)

    hp = hp_ref[0]
    ab = ab_ref[0]
    z = ab[:, 0:2] + hp[:, 2:4]
    softplus = jnp.maximum(z, 0.0) + jnp.log(1.0 + jnp.exp(-jnp.abs(z)))
    g_all = -jnp.exp(hp[:, 0:2]) * softplus
    beta_all = jax.nn.sigmoid(ab[:, 2:4])

    nc = tm // c

    def per_chunk(t):
        return t.reshape(nc, c, t.shape[-1])

    def both(t):
        return jnp.concatenate([t, t], axis=0)

    def by_dir(t2):
        return jnp.concatenate([per_chunk(jnp.broadcast_to(t2[:, d:d + 1], (tm, LANE))) for d in range(2)],
                               axis=0)

    shape = (2 * nc, c, c)
    fwd = lax.broadcasted_iota(jnp.int32, shape, 0) < nc
    rows = lax.broadcasted_iota(jnp.int32, shape, 1)
    cols = lax.broadcasted_iota(jnp.int32, shape, 2)
    eye_mask = rows == cols
    ahead = jnp.where(fwd, cols - rows, rows - cols)
    incl = ahead <= 0
    strict = ahead < 0
    eye = jnp.where(eye_mask, 1.0, 0.0)
    tri = jnp.where(incl, 1.0, 0.0).astype(BF16)

    g_hi, g_lo = _split2(by_dir(g_all))
    gc = _bmm(tri, g_hi) + _bmm(tri, g_lo)
    gc_sq = gc[:, :, :c]
    gc_row = jnp.sum(jnp.where(eye_mask, gc_sq, 0.0), axis=1, keepdims=True)
    decay = jnp.where(incl, jnp.exp(jnp.where(incl, gc_sq - gc_row, 0.0)), 0.0)
    gtot = jnp.concatenate([gc[:nc, c - 1:c, :], gc[nc:, 0:1, :]], axis=0)
    eg = jnp.exp(gc)

    beta = by_dir(beta_all)
    k3 = per_chunk(k)
    q3 = per_chunk(q)
    k3b = k3.astype(BF16)
    kk = both(_bmm_nt(k3b, k3b))
    qk = both(_bmm_nt(q3.astype(BF16), k3b))
    k2 = both(k3)
    kb = k2 * beta
    m = jnp.where(strict, beta[:, :, :c] * kk * decay, 0.0)
    tinv = _tri_inverse(m, eye)
    t_hi, t_lo = _split2(tinv)
    rhs = jnp.concatenate([both(per_chunk(v)) * beta, kb * eg], axis=2).astype(BF16)
    sol = _bmm(t_hi, rhs) + _bmm(t_lo, rhs)
    ke = k2 * jnp.exp(gtot - gc)
    qg = both(q3) * eg
    att = qk * decay
    gend = jnp.exp(gtot)
    for d in range(2):
        sl = slice(d * nc, (d + 1) * nc)
        u_ref[d, 0, 0] = sol[sl, :, :DN_DV].reshape(tm, DN_DV)
        w_ref[d, 0, 0] = sol[sl, :, DN_DV:].reshape(tm, DN_DK).astype(BF16)
        ke_ref[d, 0, 0] = ke[sl].reshape(tm, DN_DK).astype(BF16)
        qg_ref[d, 0, 0] = qg[sl].reshape(tm, DN_DK).astype(BF16)
        a_ref[d, 0, 0] = att[sl].reshape(tm, c).astype(BF16)
        ge_ref[d, 0, 0] = gend[sl]


def _dn_prep(p, ab, hp, conv_w, batch, seq, tm):
    h = DN_HEADS
    nt = seq // tm
    nb16 = tm // HALO16
    total16 = batch * seq // HALO16
    kern = functools.partial(_dn_prep_kernel, tm=tm, tiles_per_seq=nt)

    def triple(col0):
        return [pl.BlockSpec((tm, LANE), lambda i, hh: (i, col0 + hh)),
                pl.BlockSpec((HALO16, LANE), lambda i, hh: (jnp.maximum(i * nb16 - 1, 0), col0 + hh)),
                pl.BlockSpec((HALO16, LANE), lambda i, hh: (jnp.minimum((i + 1) * nb16, total16 - 1), col0 + hh))]

    cq, ck, cv = B_DQ // LANE, B_DK // LANE, B_DV // LANE
    row = lambda shape: pl.BlockSpec(shape, lambda i, hh: (0, i // nt, hh, i % nt, 0))
    seq_shape = lambda width, dt: jax.ShapeDtypeStruct((2, batch, h, seq, width), dt)
    return pl.pallas_call(
        kern,
        grid=(batch * nt, h),
        in_specs=triple(cq) + triple(ck) + triple(cv) + [
            pl.BlockSpec((3, LANE), lambda i, hh: (0, hh)),
            pl.BlockSpec((3, LANE), lambda i, hh: (0, h + hh)),
            pl.BlockSpec((3, LANE), lambda i, hh: (0, 2 * h + hh)),
            pl.BlockSpec((1, tm, 4), lambda i, hh: (hh, i, 0)),
            pl.BlockSpec((1, 1, 4), lambda i, hh: (hh, 0, 0))],
        out_specs=[row((2, 1, 1, tm, DN_DV)), row((2, 1, 1, tm, DN_DK)), row((2, 1, 1, tm, DN_DK)),
                   row((2, 1, 1, tm, DN_DK)), row((2, 1, 1, tm, DN_CHUNK)),
                   pl.BlockSpec((2, 1, 1, tm // DN_CHUNK, 1, LANE), lambda i, hh: (0, i // nt, hh, i % nt, 0, 0))],
        out_shape=[seq_shape(DN_DV, F32), seq_shape(DN_DK, BF16), seq_shape(DN_DK, BF16),
                   seq_shape(DN_DK, BF16), seq_shape(DN_CHUNK, BF16),
                   jax.ShapeDtypeStruct((2, batch, h, seq // DN_CHUNK, 1, LANE), F32)],
        scratch_shapes=[pltpu.VMEM((tm + 2 * HALO16, LANE), F32)],
        compiler_params=_params(("parallel", "parallel")),
        name="dn_prep",
    )(p, p, p, p, p, p, p, p, p, conv_w, conv_w, conv_w, ab, hp)


def _dn_scan_kernel(s0_ref, uf_ref, wf_ref, kf_ref, qf_ref, af_ref, gf_ref,
                    ub_ref, wb_ref, kb_ref, qb_ref, ab_ref, gb_ref,
                    of_ref, ob_ref, sfin_ref, s_ref, *, cs):
    n = pl.program_id(1)
    c = DN_CHUNK
    h = DN_HEADS

    @pl.when(n == 0)
    def _():
        s_ref[0:h] = s0_ref[0, 0]
        s_ref[h:] = s0_ref[1, 0]

    for ci in range(cs):
        rf = ci * c
        rb = (cs - 1 - ci) * c

        def pair(f_ref, b_ref):
            return jnp.concatenate([f_ref[0, 0, :, rf:rf + c, :], b_ref[0, 0, :, rb:rb + c, :]], axis=0)

        s = s_ref[...]
        s16 = s.astype(BF16)
        ws_qs = _bmm(jnp.concatenate([pair(wf_ref, wb_ref), pair(qf_ref, qb_ref)], axis=1), s16)
        v_new = pair(uf_ref, ub_ref) - ws_qs[:, :c]
        v16 = v_new.astype(BF16)
        o = ws_qs[:, c:] + _bmm(pair(af_ref, ab_ref), v16)
        of_ref[0, :, rf:rf + c, :] = o[:h].astype(of_ref.dtype)
        ob_ref[0, :, rb:rb + c, :] = o[h:].astype(ob_ref.dtype)
        g = jnp.concatenate([gf_ref[0, 0, :, ci], gb_ref[0, 0, :, cs - 1 - ci]], axis=0)
        s_ref[...] = s * g + lax.dot_general(pair(kf_ref, kb_ref), v16, (((1,), (1,)), ((0,), (0,))),
                                             preferred_element_type=F32)

    @pl.when(n == pl.num_programs(1) - 1)
    def _():
        sfin_ref[0, 0] = s_ref[0:h]
        sfin_ref[1, 0] = s_ref[h:]


def _dn_scan(s0, u, w, ke, qg, a, ge, cs):
    _, batch, h, seq, _ = u.shape
    ts = cs * DN_CHUNK
    ns = seq // ts
    kern = functools.partial(_dn_scan_kernel, cs=cs)

    def specs(d):
        idx = (lambda n: n) if d == 0 else (lambda n: ns - 1 - n)
        seqs = [pl.BlockSpec((1, 1, h, ts, width), lambda b, n: (d, b, 0, idx(n), 0))
                for width in (DN_DV, DN_DK, DN_DK, DN_DK, DN_CHUNK)]
        return seqs + [pl.BlockSpec((1, 1, h, cs, 1, LANE), lambda b, n: (d, b, 0, idx(n), 0, 0))]

    o_spec = lambda d: pl.BlockSpec((1, h, ts, DN_DV), lambda b, n: (b, 0, n if d == 0 else ns - 1 - n, 0))
    state = pl.BlockSpec((2, 1, h, DN_DK, DN_DV), lambda b, n: (0, b, 0, 0, 0))
    return pl.pallas_call(
        kern,
        grid=(batch, ns),
        in_specs=[state] + specs(0) + specs(1),
        out_specs=[o_spec(0), o_spec(1), state],
        out_shape=[jax.ShapeDtypeStruct((batch, h, seq, DN_DV), BF16),
                   jax.ShapeDtypeStruct((batch, h, seq, DN_DV), BF16),
                   jax.ShapeDtypeStruct((2, batch, h, DN_DK, DN_DV), F32)],
        scratch_shapes=[pltpu.VMEM((2 * h, DN_DK, DN_DV), F32)],
        compiler_params=_params(("parallel", "arbitrary")),
        name="dn_scan",
    )(s0, u, w, ke, qg, a, ge, u, w, ke, qg, a, ge)


def _merge_kernel(x_ref, mod_ref, ym_ref, of_ref, ob_ref, z_ref, nw_ref, ga_ref, gb_ref, wm_ref, wd_ref, wo_ref,
                  o_ref, *, k_gate):
    nw = nw_ref[...]
    heads = []
    for h in range(DN_HEADS):
        o = of_ref[0, h].astype(F32) + ob_ref[0, h].astype(F32)
        y = o * lax.rsqrt(jnp.mean(o * o, axis=-1, keepdims=True) + EPS) * nw
        heads.append((y * _silu(z_ref[:, h * DN_DV:(h + 1) * DN_DV].astype(F32))).astype(BF16))
    yd = jnp.concatenate(heads, axis=1)
    pm = jnp.dot(ym_ref[...], wm_ref[...], preferred_element_type=F32)
    pd = jnp.dot(yd, wd_ref[...], preferred_element_type=F32)
    merged = (jax.nn.sigmoid(ga_ref[...].astype(F32)) * pm
              + jax.nn.sigmoid(gb_ref[...].astype(F32)) * pd).astype(BF16)
    y = jnp.dot(merged, wo_ref[...], preferred_element_type=F32)
    o_ref[...] = x_ref[...] + mod_ref[0, k_gate:k_gate + 1, :] * y


def _merge(x, mod, row_of_tile, ym, o_f, o_b, p, nw, wm, wd, wo, tm, k_gate):
    m, d = x.shape
    _, h, seq, _ = o_f.shape
    nt = seq // tm
    zw = h * DN_DV
    kern = functools.partial(_merge_kernel, k_gate=k_gate)
    const = lambda arr: pl.BlockSpec(arr.shape, lambda i: (0, 0), pipeline_mode=pl.Buffered(1))
    scan_out = pl.BlockSpec((1, h, tm, DN_DV), lambda i: (i // nt, 0, i % nt, 0))
    return pl.pallas_call(
        kern,
        grid=(m // tm,),
        in_specs=[pl.BlockSpec((tm, d), lambda i: (i, 0)),
                  pl.BlockSpec((1, 6, d), lambda i: (row_of_tile(i), 0, 0)),
                  pl.BlockSpec((tm, ym.shape[1]), lambda i: (i, 0)),
                  scan_out, scan_out,
                  pl.BlockSpec((tm, zw), lambda i: (i, B_DZ // zw)),
                  pl.BlockSpec((1, DN_DV), lambda i: (0, 0)),
                  pl.BlockSpec((tm, d), lambda i: (i, B_GA // d)),
                  pl.BlockSpec((tm, d), lambda i: (i, B_GB // d)),
                  const(wm), const(wd), const(wo)],
        out_specs=pl.BlockSpec((tm, d), lambda i: (i, 0)),
        out_shape=jax.ShapeDtypeStruct((m, d), F32),
        compiler_params=_params(("parallel",)),
        name="merge",
    )(x, mod, ym, o_f, o_b, p, nw, p, p, wm, wd, wo)


def _ffn_kernel(x_ref, xp_ref, xx_ref, nw_ref, mod_ref, wg_ref, wv_ref, cw_ref, wdn_ref, o_ref, xn_ref, ge_ref,
                *, tm, tiles_per_seq, k_shift, k_scale, k_gate):
    i = pl.program_id(0)
    j = pl.program_id(1)
    th = FFN_TILE

    @pl.when(j == 0)
    def _():
        first = (i % tiles_per_seq) == 0
        last = (i % tiles_per_seq) == tiles_per_seq - 1
        nw = nw_ref[...]
        sh = mod_ref[0, k_shift:k_shift + 1, :]
        sc = mod_ref[0, k_scale:k_scale + 1, :]
        xn_ref[0:HALO16, :] = jnp.where(first, 0.0, _norm_mod(xp_ref[...], nw, sh, sc)).astype(BF16)
        xn_ref[HALO16:HALO16 + tm, :] = _norm_mod(x_ref[...], nw, sh, sc).astype(BF16)
        xn_ref[HALO16 + tm:, :] = jnp.where(last, 0.0, _norm_mod(xx_ref[...], nw, sh, sc)).astype(BF16)
        o_ref[...] = jnp.zeros_like(o_ref)

    ge_ref[...] = jnp.dot(xn_ref[...], wg_ref[...], preferred_element_type=F32)
    val = jnp.dot(xn_ref[HALO16:HALO16 + tm, :], wv_ref[...], preferred_element_type=F32)
    conv = (cw_ref[0:1, :] * ge_ref[HALO16 - 1:HALO16 - 1 + tm, :]
            + cw_ref[1:2, :] * ge_ref[HALO16:HALO16 + tm, :]
            + cw_ref[2:3, :] * ge_ref[HALO16 + 1:HALO16 + 1 + tm, :])
    hid = (_silu(conv) * val).astype(BF16)
    o_ref[...] += jnp.dot(hid, wdn_ref[...], preferred_element_type=F32)

    @pl.when(j == pl.num_programs(1) - 1)
    def _():
        o_ref[...] = x_ref[...] + mod_ref[0, k_gate:k_gate + 1, :] * o_ref[...]


def _ffn(x, nw, mod, row_of_tile, wg, wv, cw, wdn, seq, tm, k_shift, k_scale, k_gate):
    m, d = x.shape
    th = FFN_TILE
    nj = wdn.shape[0] // th
    nt = seq // tm
    nb16 = tm // HALO16
    total16 = m // HALO16
    kern = functools.partial(_ffn_kernel, tm=tm, tiles_per_seq=nt, k_shift=k_shift, k_scale=k_scale,
                             k_gate=k_gate)
    return pl.pallas_call(
        kern,
        grid=(m // tm, nj),
        in_specs=[pl.BlockSpec((tm, d), lambda i, j: (i, 0)),
                  pl.BlockSpec((HALO16, d), lambda i, j: (jnp.maximum(i * nb16 - 1, 0), 0)),
                  pl.BlockSpec((HALO16, d), lambda i, j: (jnp.minimum((i + 1) * nb16, total16 - 1), 0)),
                  pl.BlockSpec((1, d), lambda i, j: (0, 0)),
                  pl.BlockSpec((1, 6, d), lambda i, j: (row_of_tile(i), 0, 0)),
                  pl.BlockSpec((d, th), lambda i, j: (0, j)),
                  pl.BlockSpec((d, th), lambda i, j: (0, j)),
                  pl.BlockSpec((3, th), lambda i, j: (0, j)),
                  pl.BlockSpec((th, d), lambda i, j: (j, 0))],
        out_specs=pl.BlockSpec((tm, d), lambda i, j: (i, 0)),
        out_shape=jax.ShapeDtypeStruct((m, d), F32),
        scratch_shapes=[pltpu.VMEM((tm + 2 * HALO16, d), BF16), pltpu.VMEM((tm + 2 * HALO16, th), F32)],
        compiler_params=_params(("parallel", "arbitrary")),
        name="ffn",
    )(x, x, x, nw, mod, wg, wv, cw, wdn)


def _rope_pad_cols(r):
    n = MLA_ROPE // 4
    z = jnp.zeros(r.shape[:-1] + (2 * n,), r.dtype)
    return jnp.concatenate([r[..., 0:n], r[..., 2 * n:3 * n], z, r[..., n:2 * n], r[..., 3 * n:4 * n], z], axis=-1)


def _qk_pad_cols(w):
    lead = w.shape[:-1]
    w = w.reshape(lead + (MLA_HEADS, MLA_QK))
    out = jnp.concatenate([w[..., :MLA_NOPE], _rope_pad_cols(w[..., MLA_NOPE:])], axis=-1)
    return out.reshape(lead + (MLA_HEADS * MLA_QK_PAD,))


def _rope_tables(rows):
    n = MLA_ROPE // 4
    inv = ROPE_BASE ** (-jnp.arange(n, dtype=F32) / n)
    ang_r = jnp.arange(rows, dtype=F32)[:, None] * inv
    ang_c = jnp.arange(GRID_W, dtype=F32)[:, None] * inv

    def lanes(a_row, a_col, fill, count):
        return jnp.concatenate([a_row, a_col, jnp.full((count, 2 * n), fill, F32)] * 2, axis=-1)

    zr, zc = jnp.zeros((rows, n), F32), jnp.zeros((GRID_W, n), F32)
    row_cos = lanes(jnp.cos(ang_r), zr, 0.0, rows)
    row_sin = jnp.concatenate([-jnp.sin(ang_r), zr, jnp.zeros((rows, 2 * n), F32),
                               jnp.sin(ang_r), zr, jnp.zeros((rows, 2 * n), F32)], axis=-1)
    col_cos = lanes(zc, jnp.cos(ang_c), 1.0, GRID_W)
    col_sin = jnp.concatenate([zc, -jnp.sin(ang_c), jnp.zeros((GRID_W, 2 * n), F32),
                               zc, jnp.sin(ang_c), jnp.zeros((GRID_W, 2 * n), F32)], axis=-1)
    return row_cos, row_sin, col_cos, col_sin


def _pick_tile(n, pref):
    t = min(pref, n)
    while n % t:
        t //= 2
    return t


def kernel(x, c, ctx, c_ctx, w_ada, b_ada, norm_mix, w_in, q_a_norm, w_q_b, kv_a_norm, w_kv_b, q_norm, k_norm,
           w_o_mla, dn_conv, dn_a_log, dn_dt_bias, dn_o_norm, w_o_dn, w_out, norm_ffn, w_ffn_up, ffn_conv,
           w_ffn_down):
    batch, seq, d = x.shape
    lc = ctx.shape[1]
    assert w_ada.shape[0] == 1, "single-layer stack"
    assert seq % GRID_W == 0 and seq % DN_CHUNK == 0 and lc % DN_CHUNK == 0
    h = DN_HEADS
    l = 0

    wi = w_in[l]
    o_qa, o_kva, o_kr = 0, MLA_Q_RANK, MLA_Q_RANK + MLA_KV_RANK
    o_dq = o_kr + MLA_ROPE
    o_da = o_dq + 4 * h * DN_DK
    o_ga = o_da + 4 * h
    w_a = jnp.concatenate([wi[:, o_qa:o_kr], _rope_pad_cols(wi[:, o_kr:o_dq]), wi[:, o_da:o_ga],
                           jnp.zeros((d, A_WIDTH - A_DB - 2 * h), F32)], axis=1).astype(BF16)
    w_b = jnp.concatenate([wi[:, o_dq:o_da], wi[:, o_ga:]], axis=1).astype(BF16)
    wq = _qk_pad_cols(w_q_b[l]).astype(BF16)
    wkv = w_kv_b[l].reshape(MLA_KV_RANK, MLA_HEADS, MLA_NOPE + MLA_V)
    wkv = jnp.concatenate([wkv[:, :, :MLA_NOPE].reshape(MLA_KV_RANK, -1),
                           wkv[:, :, MLA_NOPE:].reshape(MLA_KV_RANK, -1)], axis=1).astype(BF16)
    qn = jnp.concatenate([q_norm[l, :MLA_NOPE], _rope_pad_cols(q_norm[l, MLA_NOPE:])])[None, :]
    kn = jnp.concatenate([k_norm[l, :MLA_NOPE], _rope_pad_cols(k_norm[l, MLA_NOPE:])])[None, :]
    hidden = w_ffn_down.shape[1]
    nj = -(-hidden // FFN_TILE)
    hpad = nj * FFN_TILE - hidden
    wg = jnp.pad(w_ffn_up[l][:, :hidden].astype(BF16), ((0, 0), (0, hpad)))
    wv = jnp.pad(w_ffn_up[l][:, hidden:].astype(BF16), ((0, 0), (0, hpad)))
    fcw = jnp.pad(ffn_conv[l], ((0, 0), (0, hpad)))
    wdn = jnp.pad(w_ffn_down[l], ((0, hpad), (0, 0))).astype(BF16)
    hp = jnp.concatenate([dn_a_log[l].T, dn_dt_bias[l].T], axis=1)[:, None, :]
    tables_l = _rope_tables(seq // GRID_W)
    tables_c = (jnp.zeros((lc // GRID_W, ROPE_PAD), F32), jnp.zeros((lc // GRID_W, ROPE_PAD), F32),
                jnp.ones((GRID_W, ROPE_PAD), F32), jnp.zeros((GRID_W, ROPE_PAD), F32))

    rows = jnp.concatenate([c, c_ctx[None, :], jnp.zeros((8 - batch - 1, d), F32)], axis=0)
    mod = _ada(rows, w_ada[l], b_ada[l][None, :]).reshape(8, 6, d)

    xf = x.reshape(batch * seq, d)
    cf = ctx.reshape(batch * lc, d)

    def stream(tokens, n, tm_a, tm_b, row_of):
        nm = norm_mix[l][None, :]
        a = _norm_mod_matmul(tokens, nm, mod, row_of(tm_a), w_a, F32, tm_a, A_WIDTH, 0, 1, "in_proj_small")
        p = _norm_mod_matmul(tokens, nm, mod, row_of(tm_b), w_b, BF16, tm_b, 1024, 0, 1, "in_proj_wide")
        ab = a[:, A_DA:A_DA + 4 * h].reshape(batch * n, 4, h).transpose(2, 0, 1)
        return a, p, ab

    tm_a = _pick_tile(seq, 512)
    tm_b = _pick_tile(seq, 1024)
    lat_row = lambda tm: (lambda i: i // (seq // tm))
    a_l, p_l, ab_l = stream(xf, seq, tm_a, tm_b, lat_row)
    tc = _pick_tile(lc, 256)
    ctx_row = lambda tm: (lambda i: batch)
    a_c, p_c, ab_c = stream(cf, lc, tc, tc, ctx_row)

    qan, kvan = q_a_norm[l][None, :], kv_a_norm[l][None, :]
    tmp = _pick_tile(seq, 512)
    q_l, k_l, v_l = _mla_proj(a_l, tables_l, qan, kvan, qn, kn, wq, wkv, tmp, seq)
    _, k_c, v_c = _mla_proj(a_c, tables_c, qan, kvan, qn, kn, wq, wkv, tc, lc)
    tq = _pick_tile(seq, 512)
    tk = _pick_tile(seq, 512)
    qt = q_l.reshape(batch, seq, MLA_HEADS, MLA_QK_PAD).transpose(0, 2, 3, 1)

    def with_ones_row(vt):
        lead, n = vt.shape[:-2], vt.shape[-1]
        return jnp.concatenate([vt, jnp.ones(lead + (1, n), BF16), jnp.zeros(lead + (HALO16 - 1, n), BF16)],
                               axis=-2)

    vt = with_ones_row(v_l.reshape(batch, seq // tk, tk, MLA_HEADS, MLA_V).transpose(0, 3, 1, 4, 2))
    vct = with_ones_row(v_c.reshape(batch, lc, MLA_HEADS, MLA_V).transpose(0, 2, 3, 1))
    y_mla = _attention(qt, k_l.reshape(batch, seq, -1), vt, k_c.reshape(batch, lc, -1), vct, tq)
    y_mla = y_mla.reshape(batch * seq, MLA_HEADS * MLA_V)

    conv_w = dn_conv[l]
    prep_c = _dn_prep(p_c, ab_c, hp, conv_w, batch, lc, _pick_tile(lc, 256))
    s_zero = jnp.zeros((2, batch, h, DN_DK, DN_DV), F32)
    _, _, s_ctx = _dn_scan(s_zero, *prep_c, _pick_tile(lc, 256) // DN_CHUNK)
    prep_l = _dn_prep(p_l, ab_l, hp, conv_w, batch, seq, _pick_tile(seq, 1024))
    o_f, o_b, _ = _dn_scan(s_ctx, *prep_l, _pick_tile(seq, 256) // DN_CHUNK)

    tmm = _pick_tile(seq, 256)
    x1 = _merge(xf, mod, lat_row(tmm), y_mla, o_f, o_b, p_l, dn_o_norm[l][None, :], w_o_mla[l].astype(BF16),
                w_o_dn[l].astype(BF16), w_out[l].astype(BF16), tmm, 2)

    tmf = _pick_tile(seq, 512)
    out = _ffn(x1, norm_ffn[l][None, :], mod, lat_row(tmf), wg, wv, fcw, wdn, seq, tmf, 3, 4, 5)
    return out.reshape(batch, seq, d)
```

```python
import functools
import math

import jax
import jax.numpy as jnp
import numpy as np
from jax import lax
from jax.experimental import pallas as pl
from jax.experimental.pallas import tpu as pltpu

F32 = jnp.float32
BF16 = jnp.bfloat16
HIGHEST = lax.Precision.HIGHEST

EPS = 1e-6
GRID_W = 64
ROPE_BASE = 10000.0

MLA_HEADS = 8
MLA_Q_RANK = 512
MLA_KV_RANK = 512
MLA_NOPE = 128
MLA_ROPE = 64
MLA_V = 128
MLA_QK = MLA_NOPE + MLA_ROPE
MLA_QK_PAD = 256

DN_HEADS = 8
DN_DK = 128
DN_DV = 128
DN_CHUNK = 64

LANE = 128
SUBLANE = 8
ROPE_PAD = 128
VMEM_LIMIT = 56 * 1024 * 1024

A_QA, A_KVA, A_KR, A_DA, A_DB, A_WIDTH = 0, 512, 1024, 1152, 1168, 1280
B_DQ, B_DK, B_DV, B_DZ, B_GA, B_GB, B_WIDTH = 0, 1024, 2048, 3072, 4096, 6144, 8192

FFN_TILE = 512
HALO16 = 16


def _params(sem, vmem=VMEM_LIMIT):
    return pltpu.CompilerParams(dimension_semantics=sem, vmem_limit_bytes=vmem)


def _nt_dot(a, b):
    return lax.dot_general(a, b, (((1,), (1,)), ((), ())), preferred_element_type=F32)


def _silu(x):
    return x * jax.nn.sigmoid(x)


def _ada_kernel(c_ref, w_ref, b_ref, o_ref):
    a = _silu(c_ref[...]).astype(BF16)
    o_ref[...] = jnp.dot(a, w_ref[...].astype(BF16), preferred_element_type=F32) + b_ref[...]


def _ada(cs, w, b):
    m, d = cs.shape
    n = w.shape[1]
    tn = 512
    return pl.pallas_call(
        _ada_kernel,
        grid=(n // tn,),
        in_specs=[pl.BlockSpec((m, d), lambda j: (0, 0)),
                  pl.BlockSpec((d, tn), lambda j: (0, j)),
                  pl.BlockSpec((1, tn), lambda j: (0, j))],
        out_specs=pl.BlockSpec((m, tn), lambda j: (0, j)),
        out_shape=jax.ShapeDtypeStruct((m, n), F32),
        compiler_params=_params(("parallel",)),
        name="ada",
    )(cs, w, b)


def _norm_mod(x, nw, shift, scale):
    ms = jnp.mean(x * x, axis=-1, keepdims=True)
    y = x * lax.rsqrt(ms + EPS) * nw
    return y * (1.0 + scale) + shift


def _nmm_kernel(x_ref, nw_ref, mod_ref, w_ref, o_ref, xn_ref, *, k_shift, k_scale):
    y = _norm_mod(x_ref[...], nw_ref[...], mod_ref[0, k_shift:k_shift + 1, :], mod_ref[0, k_scale:k_scale + 1, :])
    xn = y.astype(BF16)
    xn_ref[...] = xn
    o_ref[...] = jnp.dot(xn, w_ref[...], preferred_element_type=F32).astype(o_ref.dtype)


def _norm_mod_matmul(x, nw, mod, row_of_tile, w, out_dtype, tm, k_shift, k_scale, name):
    m, d = x.shape
    n = w.shape[1]
    kern = functools.partial(_nmm_kernel, k_shift=k_shift, k_scale=k_scale)
    return pl.pallas_call(
        kern,
        grid=(m // tm,),
        in_specs=[pl.BlockSpec((tm, d), lambda i: (i, 0)),
                  pl.BlockSpec((1, d), lambda i: (0, 0)),
                  pl.BlockSpec((1, 6, d), lambda i: (row_of_tile(i), 0, 0)),
                  pl.BlockSpec((d, n), lambda i: (0, 0))],
        out_specs=[pl.BlockSpec((tm, n), lambda i: (i, 0)), pl.BlockSpec((tm, d), lambda i: (i, 0))],
        out_shape=[jax.ShapeDtypeStruct((m, n), out_dtype), jax.ShapeDtypeStruct((m, d), BF16)],
        compiler_params=_params(("parallel",)),
        name=name,
    )(x, nw, mod, w)


def _mm_kernel(x_ref, w_ref, o_ref):
    o_ref[...] = jnp.dot(x_ref[...], w_ref[...], preferred_element_type=F32).astype(o_ref.dtype)


def _matmul(x, w, out_dtype, tm, tn, name):
    m, d = x.shape
    n = w.shape[1]
    return pl.pallas_call(
        _mm_kernel,
        grid=(m // tm, n // tn),
        in_specs=[pl.BlockSpec((tm, d), lambda i, j: (i, 0)),
                  pl.BlockSpec((d, tn), lambda i, j: (0, j))],
        out_specs=pl.BlockSpec((tm, tn), lambda i, j: (i, j)),
        out_shape=jax.ShapeDtypeStruct((m, n), out_dtype),
        compiler_params=_params(("parallel", "parallel")),
        name=name,
    )(x, w)


def _mla_proj_kernel(a_ref, cr_ref, sr_ref, cc_ref, sc_ref, qan_ref, kvan_ref, qn_ref, kn_ref, wq_ref, wkv_ref,
                     q_ref, k_ref, v_ref):
    tm = a_ref.shape[0]
    g = tm // GRID_W

    def table(row_ref, col_ref):
        r = jnp.broadcast_to(row_ref[...][:, None, :], (g, GRID_W, ROPE_PAD)).reshape(tm, ROPE_PAD)
        c = jnp.broadcast_to(col_ref[...][None], (g, GRID_W, ROPE_PAD)).reshape(tm, ROPE_PAD)
        return r + c

    cos = table(cr_ref, cc_ref)
    sin = table(sr_ref, sc_ref)

    def rope(r):
        return r * cos + pltpu.roll(r, 64, axis=1) * sin

    def rms_rows(t, w):
        return t * lax.rsqrt(jnp.mean(t * t, axis=-1, keepdims=True) + EPS) * w

    qa = rms_rows(a_ref[:, A_QA:A_QA + MLA_Q_RANK], qan_ref[...]).astype(BF16)
    q = jnp.dot(qa, wq_ref[...], preferred_element_type=F32)
    kva = rms_rows(a_ref[:, A_KVA:A_KVA + MLA_KV_RANK], kvan_ref[...]).astype(BF16)
    kv = jnp.dot(kva, wkv_ref[...], preferred_element_type=F32)
    kr = a_ref[:, A_KR:A_KR + ROPE_PAD]
    kr_ss = jnp.sum(kr * kr, axis=-1, keepdims=True)
    qn = qn_ref[...]
    kn = kn_ref[...]
    scale = MLA_QK ** -0.5 * math.log2(math.e)
    for h in range(MLA_HEADS):
        qh = q[:, h * MLA_QK_PAD:(h + 1) * MLA_QK_PAD]
        inv = lax.rsqrt(jnp.sum(qh * qh, axis=-1, keepdims=True) * (1.0 / MLA_QK) + EPS) * scale
        qh = qh * inv * qn
        q_ref[:, h * MLA_QK_PAD:h * MLA_QK_PAD + MLA_NOPE] = qh[:, :MLA_NOPE].astype(BF16)
        q_ref[:, h * MLA_QK_PAD + MLA_NOPE:(h + 1) * MLA_QK_PAD] = rope(qh[:, MLA_NOPE:]).astype(BF16)

        kh = kv[:, h * MLA_NOPE:(h + 1) * MLA_NOPE]
        inv = lax.rsqrt((jnp.sum(kh * kh, axis=-1, keepdims=True) + kr_ss) * (1.0 / MLA_QK) + EPS)
        k_ref[:, h * MLA_QK_PAD:h * MLA_QK_PAD + MLA_NOPE] = (kh * inv * kn[:, :MLA_NOPE]).astype(BF16)
        k_ref[:, h * MLA_QK_PAD + MLA_NOPE:(h + 1) * MLA_QK_PAD] = rope(
            kr * inv * kn[:, MLA_NOPE:]).astype(BF16)
    v_ref[...] = kv[:, MLA_HEADS * MLA_NOPE:].astype(BF16)


def _mla_proj(a, tables, qan, kvan, qn, kn, wq, wkv, tm, seq):
    m = a.shape[0]
    nseq = seq // tm
    row_cos, row_sin, col_cos, col_sin = tables
    full = lambda arr: pl.BlockSpec(arr.shape, lambda i: (0, 0))
    row_spec = pl.BlockSpec((tm // GRID_W, ROPE_PAD), lambda i: (i % nseq, 0))
    return pl.pallas_call(
        _mla_proj_kernel,
        grid=(m // tm,),
        in_specs=[pl.BlockSpec((tm, A_WIDTH), lambda i: (i, 0)),
                  row_spec, row_spec, full(col_cos), full(col_sin),
                  full(qan), full(kvan), full(qn), full(kn), full(wq), full(wkv)],
        out_specs=[pl.BlockSpec((tm, MLA_HEADS * MLA_QK_PAD), lambda i: (i, 0)),
                   pl.BlockSpec((tm, MLA_HEADS * MLA_QK_PAD), lambda i: (i, 0)),
                   pl.BlockSpec((tm, MLA_HEADS * MLA_V), lambda i: (i, 0))],
        out_shape=[jax.ShapeDtypeStruct((m, MLA_HEADS * MLA_QK_PAD), BF16),
                   jax.ShapeDtypeStruct((m, MLA_HEADS * MLA_QK_PAD), BF16),
                   jax.ShapeDtypeStruct((m, MLA_HEADS * MLA_V), BF16)],
        compiler_params=_params(("parallel",)),
        name="mla_proj",
    )(a, row_cos, row_sin, col_cos, col_sin, qan, kvan, qn, kn, wq, wkv)


ATTN_UNROLL = 8
ATTN_ROWS = 64


def _attn_kernel(qt_ref, kc_ref, vct_ref, k_ref, vt_ref, o_ref, s0_ref, s1_ref, p0_ref, p1_ref, acc_ref,
                 *, nk, unroll):
    qt = qt_ref[0, 0]
    tq = qt.shape[1]
    tk = s0_ref.shape[0]
    rows = ATTN_ROWS
    s_refs = (s0_ref, s1_ref)
    p_refs = (p0_ref, p1_ref)

    def scores(j):
        return jnp.dot(k_ref[0, pl.ds(pl.multiple_of(j * tk, tk), tk), :], qt, preferred_element_type=F32)

    def probs(s, m):
        return jnp.exp2((s - m).astype(BF16))

    def store_scores(slot, s):
        s_refs[slot][...] = s
        return jnp.max(s.reshape(tk // SUBLANE, SUBLANE, tq), axis=0)

    s = jnp.dot(kc_ref[0], qt, preferred_element_type=F32)
    m8_first = store_scores(0, scores(0))
    p1_ref[...] = jnp.zeros((tk, tq), BF16)
    m = jnp.max(s, axis=0, keepdims=True)
    acc_ref[...] = jnp.dot(vct_ref[0, 0], probs(s, m), preferred_element_type=F32)

    def tile(j, slot, carry):
        m, alpha_prev, m8 = carry
        acc_ref[...] = alpha_prev * acc_ref[...] + jnp.dot(
            vt_ref[0, 0, jnp.maximum(j - 1, 0)], p_refs[1 - slot][...], preferred_element_type=F32)
        m8_next = store_scores(1 - slot, scores(jnp.minimum(j + 1, nk - 1)))
        m_new = jnp.maximum(m, jnp.max(m8, axis=0, keepdims=True))
        for r in range(0, tk, rows):
            p_refs[slot][r:r + rows, :] = probs(s_refs[slot][r:r + rows, :], m_new)
        return m_new, jnp.exp2(m - m_new), m8_next

    def body(i, carry):
        for u in range(unroll):
            carry = tile(unroll * i + u, u % 2, carry)
        return carry

    m, alpha, _ = lax.fori_loop(0, nk // unroll, body, (m, jnp.ones((1, tq), F32), m8_first))
    acc = alpha * acc_ref[...] + jnp.dot(vt_ref[0, 0, nk - 1], p1_ref[...], preferred_element_type=F32)
    o_ref[0] = (acc[:MLA_V] * (1.0 / acc[MLA_V:MLA_V + 1])).T.astype(o_ref.dtype)


def _attention(qt, k, vt, kc, vct, tq):
    b, t, _ = k.shape
    lc = kc.shape[1]
    nk, vrows, tk = vt.shape[2], vt.shape[3], vt.shape[4]
    unroll = ATTN_UNROLL if nk % ATTN_UNROLL == 0 else 2
    assert nk % unroll == 0
    kern = functools.partial(_attn_kernel, nk=nk, unroll=unroll)
    return pl.pallas_call(
        kern,
        grid=(b, MLA_HEADS, t // tq),
        in_specs=[pl.BlockSpec((1, 1, MLA_QK_PAD, tq), lambda bi, h, i: (bi, h, 0, i)),
                  pl.BlockSpec((1, lc, MLA_QK_PAD), lambda bi, h, i: (bi, 0, h)),
                  pl.BlockSpec((1, 1, vrows, lc), lambda bi, h, i: (bi, h, 0, 0)),
                  pl.BlockSpec((1, t, MLA_QK_PAD), lambda bi, h, i: (bi, 0, h)),
                  pl.BlockSpec((1, 1, nk, vrows, tk), lambda bi, h, i: (bi, h, 0, 0, 0))],
        out_specs=pl.BlockSpec((1, tq, MLA_V), lambda bi, h, i: (bi, i, h)),
        out_shape=jax.ShapeDtypeStruct((b, t, MLA_HEADS * MLA_V), BF16),
        scratch_shapes=[pltpu.VMEM((tk, tq), F32), pltpu.VMEM((tk, tq), F32),
                        pltpu.VMEM((tk, tq), BF16), pltpu.VMEM((tk, tq), BF16),
                        pltpu.VMEM((vrows, tq), F32)],
        compiler_params=_params(("parallel", "parallel", "arbitrary")),
        name="attention",
    )(qt, kc, vct, k, vt)


def _split2(x):
    hi = x.astype(BF16)
    return hi, (x - hi.astype(F32)).astype(BF16)


def _bmm(a, b):
    return lax.dot_general(a, b, (((2,), (1,)), ((0,), (0,))), preferred_element_type=F32)


def _bmm_nt(a, b):
    return lax.dot_general(a, b, (((2,), (2,)), ((0,), (0,))), preferred_element_type=F32)


def _bmm_hp(a, b):
    ah, al = _split2(a)
    bh, bl = _split2(b)
    return _bmm(ah, bh) + _bmm(al, bh) + _bmm(ah, bl)


def _tri_inverse(m, eye):
    assert m.shape[-1] == 64
    n = -m
    x = eye + n
    for _ in range(3):
        nb = n.astype(BF16)
        n = _bmm(nb, nb)
        x = x + _bmm(x.astype(BF16), n.astype(BF16))
    xb = x.astype(BF16)
    r = eye - x - _bmm(m.astype(BF16), xb)
    x = x + _bmm(xb, r.astype(BF16))
    r = eye - x - _bmm_hp(m, x)
    return x + _bmm(x.astype(BF16), r.astype(BF16))


def _dn_prep_kernel(qm_ref, qp_ref, qx_ref, km_ref, kp_ref, kx_ref, vm_ref, vp_ref, vx_ref,
                    cwq_ref, cwk_ref, cwv_ref, ab_ref, hp_ref,
                    u_ref, w_ref, ke_ref, qg_ref, a_ref, ge_ref, ext_ref, *, tm, tiles_per_seq):
    i = pl.program_id(0)
    first = (i % tiles_per_seq) == 0
    last = (i % tiles_per_seq) == tiles_per_seq - 1
    c = DN_CHUNK

    def conv_silu(main_ref, prev_ref, next_ref, cw_ref):
        ext_ref[0:HALO16, :] = jnp.where(first, 0.0, prev_ref[...].astype(F32))
        ext_ref[HALO16:HALO16 + tm, :] = main_ref[...].astype(F32)
        ext_ref[HALO16 + tm:, :] = jnp.where(last, 0.0, next_ref[...].astype(F32))
        y = (cw_ref[0:1, :] * ext_ref[HALO16 - 1:HALO16 - 1 + tm, :]
             + cw_ref[1:2, :] * ext_ref[HALO16:HALO16 + tm, :]
             + cw_ref[2:3, :] * ext_ref[HALO16 + 1:HALO16 + 1 + tm, :])
        return _silu(y)

    def l2n(t):
        return t * lax.rsqrt(jnp.sum(t * t, axis=-1, keepdims=True) + EPS)

    q = l2n(conv_silu(qm_ref, qp_ref, qx_ref, cwq_ref)) * (DN_DK ** -0.5)
    k = l2n(conv_silu(km_ref, kp_ref, kx_ref, cwk_ref))
    v = conv_silu(vm_ref, vp_ref, vx_ref, cwv_ref)

    hp = hp_ref[0]
    ab = ab_ref[0]
    z = ab[:, 0:2] + hp[:, 2:4]
    softplus = jnp.maximum(z, 0.0) + jnp.log(1.0 + jnp.exp(-jnp.abs(z)))
    g_all = -jnp.exp(hp[:, 0:2]) * softplus
    beta_all = jax.nn.sigmoid(ab[:, 2:4])

    nc = tm // c

    def per_chunk(t):
        return t.reshape(nc, c, t.shape[-1])

    def both(t):
        return jnp.concatenate([t, t], axis=0)

    def by_dir(t2):
        return jnp.concatenate([per_chunk(jnp.broadcast_to(t2[:, d:d + 1], (tm, LANE))) for d in range(2)],
                               axis=0)

    shape = (2 * nc, c, c)
    fwd = lax.broadcasted_iota(jnp.int32, shape, 0) < nc
    rows = lax.broadcasted_iota(jnp.int32, shape, 1)
    cols = lax.broadcasted_iota(jnp.int32, shape, 2)
    eye_mask = rows == cols
    ahead = jnp.where(fwd, cols - rows, rows - cols)
    incl = ahead <= 0
    strict = ahead < 0
    eye = jnp.where(eye_mask, 1.0, 0.0)
    tri = jnp.where(incl, 1.0, 0.0).astype(BF16)

    g_hi, g_lo = _split2(by_dir(g_all))
    gc = _bmm(tri, g_hi) + _bmm(tri, g_lo)
    gc_sq = gc[:, :, :c]
    gc_row = jnp.sum(jnp.where(eye_mask, gc_sq, 0.0), axis=1, keepdims=True)
    decay = jnp.where(incl, jnp.exp(jnp.where(incl, gc_sq - gc_row, 0.0)), 0.0)
    gtot = jnp.concatenate([gc[:nc, c - 1:c, :], gc[nc:, 0:1, :]], axis=0)
    eg = jnp.exp(gc)

    beta = by_dir(beta_all)
    k3 = per_chunk(k)
    q3 = per_chunk(q)
    k3b = k3.astype(BF16)
    kk = both(_bmm_nt(k3b, k3b))
    qk = both(_bmm_nt(q3.astype(BF16), k3b))
    k2 = both(k3)
    kb = k2 * beta
    m = jnp.where(strict, beta[:, :, :c] * kk * decay, 0.0)
    tinv = _tri_inverse(m, eye)
    t_hi, t_lo = _split2(tinv)
    rhs = jnp.concatenate([both(per_chunk(v)) * beta, kb * eg], axis=2).astype(BF16)
    sol = _bmm(t_hi, rhs) + _bmm(t_lo, rhs)
    ke = k2 * jnp.exp(gtot - gc)
    qg = both(q3) * eg
    att = qk * decay
    gend = jnp.exp(gtot)
    for d in range(2):
        sl = slice(d * nc, (d + 1) * nc)
        u_ref[d, 0, 0] = sol[sl, :, :DN_DV].reshape(tm, DN_DV)
        w_ref[d, 0, 0] = sol[sl, :, DN_DV:].reshape(tm, DN_DK).astype(BF16)
        ke_ref[d, 0, 0] = ke[sl].reshape(tm, DN_DK).astype(BF16)
        qg_ref[d, 0, 0] = qg[sl].reshape(tm, DN_DK).astype(BF16)
        a_ref[d, 0, 0] = att[sl].reshape(tm, c).astype(BF16)
        ge_ref[d, 0, 0] = gend[sl]


def _dn_prep(p, ab, hp, conv_w, batch, seq, tm):
    h = DN_HEADS
    nt = seq // tm
    nb16 = tm // HALO16
    total16 = batch * seq // HALO16
    kern = functools.partial(_dn_prep_kernel, tm=tm, tiles_per_seq=nt)

    def triple(col0):
        return [pl.BlockSpec((tm, LANE), lambda i, hh: (i, col0 + hh)),
                pl.BlockSpec((HALO16, LANE), lambda i, hh: (jnp.maximum(i * nb16 - 1, 0), col0 + hh)),
                pl.BlockSpec((HALO16, LANE), lambda i, hh: (jnp.minimum((i + 1) * nb16, total16 - 1), col0 + hh))]

    cq, ck, cv = B_DQ // LANE, B_DK // LANE, B_DV // LANE
    row = lambda shape: pl.BlockSpec(shape, lambda i, hh: (0, i // nt, hh, i % nt, 0))
    seq_shape = lambda width, dt: jax.ShapeDtypeStruct((2, batch, h, seq, width), dt)
    return pl.pallas_call(
        kern,
        grid=(batch * nt, h),
        in_specs=triple(cq) + triple(ck) + triple(cv) + [
            pl.BlockSpec((3, LANE), lambda i, hh: (0, hh)),
            pl.BlockSpec((3, LANE), lambda i, hh: (0, h + hh)),
            pl.BlockSpec((3, LANE), lambda i, hh: (0, 2 * h + hh)),
            pl.BlockSpec((1, tm, 4), lambda i, hh: (hh, i, 0)),
            pl.BlockSpec((1, 1, 4), lambda i, hh: (hh, 0, 0))],
        out_specs=[row((2, 1, 1, tm, DN_DV)), row((2, 1, 1, tm, DN_DK)), row((2, 1, 1, tm, DN_DK)),
                   row((2, 1, 1, tm, DN_DK)), row((2, 1, 1, tm, DN_CHUNK)),
                   pl.BlockSpec((2, 1, 1, tm // DN_CHUNK, 1, LANE), lambda i, hh: (0, i // nt, hh, i % nt, 0, 0))],
        out_shape=[seq_shape(DN_DV, F32), seq_shape(DN_DK, BF16), seq_shape(DN_DK, BF16),
                   seq_shape(DN_DK, BF16), seq_shape(DN_CHUNK, BF16),
                   jax.ShapeDtypeStruct((2, batch, h, seq // DN_CHUNK, 1, LANE), F32)],
        scratch_shapes=[pltpu.VMEM((tm + 2 * HALO16, LANE), F32)],
        compiler_params=_params(("parallel", "parallel")),
        name="dn_prep",
    )(p, p, p, p, p, p, p, p, p, conv_w, conv_w, conv_w, ab, hp)


def _dn_scan_kernel(s0_ref, uf_ref, wf_ref, kf_ref, qf_ref, af_ref, gf_ref,
                    ub_ref, wb_ref, kb_ref, qb_ref, ab_ref, gb_ref,
                    of_ref, ob_ref, sfin_ref, s_ref, *, cs):
    n = pl.program_id(1)
    c = DN_CHUNK
    h = DN_HEADS

    @pl.when(n == 0)
    def _():
        s_ref[0:h] = s0_ref[0, 0]
        s_ref[h:] = s0_ref[1, 0]

    for ci in range(cs):
        rf = ci * c
        rb = (cs - 1 - ci) * c

        def pair(f_ref, b_ref):
            return jnp.concatenate([f_ref[0, 0, :, rf:rf + c, :], b_ref[0, 0, :, rb:rb + c, :]], axis=0)

        s = s_ref[...]
        s16 = s.astype(BF16)
        ws_qs = _bmm(jnp.concatenate([pair(wf_ref, wb_ref), pair(qf_ref, qb_ref)], axis=1), s16)
        v_new = pair(uf_ref, ub_ref) - ws_qs[:, :c]
        v16 = v_new.astype(BF16)
        o = ws_qs[:, c:] + _bmm(pair(af_ref, ab_ref), v16)
        of_ref[0, :, rf:rf + c, :] = o[:h].astype(of_ref.dtype)
        ob_ref[0, :, rb:rb + c, :] = o[h:].astype(ob_ref.dtype)
        g = jnp.concatenate([gf_ref[0, 0, :, ci], gb_ref[0, 0, :, cs - 1 - ci]], axis=0)
        s_ref[...] = s * g + lax.dot_general(pair(kf_ref, kb_ref), v16, (((1,), (1,)), ((0,), (0,))),
                                             preferred_element_type=F32)

    @pl.when(n == pl.num_programs(1) - 1)
    def _():
        sfin_ref[0, 0] = s_ref[0:h]
        sfin_ref[1, 0] = s_ref[h:]


def _dn_scan(s0, u, w, ke, qg, a, ge, cs):
    _, batch, h, seq, _ = u.shape
    ts = cs * DN_CHUNK
    ns = seq // ts
    kern = functools.partial(_dn_scan_kernel, cs=cs)

    def specs(d):
        idx = (lambda n: n) if d == 0 else (lambda n: ns - 1 - n)
        seqs = [pl.BlockSpec((1, 1, h, ts, width), lambda b, n: (d, b, 0, idx(n), 0))
                for width in (DN_DV, DN_DK, DN_DK, DN_DK, DN_CHUNK)]
        return seqs + [pl.BlockSpec((1, 1, h, cs, 1, LANE), lambda b, n: (d, b, 0, idx(n), 0, 0))]

    o_spec = lambda d: pl.BlockSpec((1, h, ts, DN_DV), lambda b, n: (b, 0, n if d == 0 else ns - 1 - n, 0))
    state = pl.BlockSpec((2, 1, h, DN_DK, DN_DV), lambda b, n: (0, b, 0, 0, 0))
    return pl.pallas_call(
        kern,
        grid=(batch, ns),
        in_specs=[state] + specs(0) + specs(1),
        out_specs=[o_spec(0), o_spec(1), state],
        out_shape=[jax.ShapeDtypeStruct((batch, h, seq, DN_DV), BF16),
                   jax.ShapeDtypeStruct((batch, h, seq, DN_DV), BF16),
                   jax.ShapeDtypeStruct((2, batch, h, DN_DK, DN_DV), F32)],
        scratch_shapes=[pltpu.VMEM((2 * h, DN_DK, DN_DV), F32)],
        compiler_params=_params(("parallel", "arbitrary")),
        name="dn_scan",
    )(s0, u, w, ke, qg, a, ge, u, w, ke, qg, a, ge)


def _merge_kernel(x_ref, mod_ref, ym_ref, of_ref, ob_ref, z_ref, nw_ref, ga_ref, gb_ref, wm_ref, wd_ref, wo_ref,
                  o_ref, *, k_gate):
    nw = nw_ref[...]
    heads = []
    for h in range(DN_HEADS):
        o = of_ref[0, h].astype(F32) + ob_ref[0, h].astype(F32)
        y = o * lax.rsqrt(jnp.mean(o * o, axis=-1, keepdims=True) + EPS) * nw
        heads.append((y * _silu(z_ref[:, h * DN_DV:(h + 1) * DN_DV].astype(F32))).astype(BF16))
    yd = jnp.concatenate(heads, axis=1)
    pm = jnp.dot(ym_ref[...], wm_ref[...], preferred_element_type=F32)
    pd = jnp.dot(yd, wd_ref[...], preferred_element_type=F32)
    merged = (jax.nn.sigmoid(ga_ref[...].astype(F32)) * pm
              + jax.nn.sigmoid(gb_ref[...].astype(F32)) * pd).astype(BF16)
    y = jnp.dot(merged, wo_ref[...], preferred_element_type=F32)
    o_ref[...] = x_ref[...] + mod_ref[0, k_gate:k_gate + 1, :] * y


def _merge(x, mod, row_of_tile, ym, o_f, o_b, p, nw, wm, wd, wo, tm, k_gate):
    m, d = x.shape
    _, h, seq, _ = o_f.shape
    nt = seq // tm
    zw = h * DN_DV
    kern = functools.partial(_merge_kernel, k_gate=k_gate)
    const = lambda arr: pl.BlockSpec(arr.shape, lambda i: (0, 0), pipeline_mode=pl.Buffered(1))
    scan_out = pl.BlockSpec((1, h, tm, DN_DV), lambda i: (i // nt, 0, i % nt, 0))
    return pl.pallas_call(
        kern,
        grid=(m // tm,),
        in_specs=[pl.BlockSpec((tm, d), lambda i: (i, 0)),
                  pl.BlockSpec((1, 6, d), lambda i: (row_of_tile(i), 0, 0)),
                  pl.BlockSpec((tm, ym.shape[1]), lambda i: (i, 0)),
                  scan_out, scan_out,
                  pl.BlockSpec((tm, zw), lambda i: (i, B_DZ // zw)),
                  pl.BlockSpec((1, DN_DV), lambda i: (0, 0)),
                  pl.BlockSpec((tm, d), lambda i: (i, B_GA // d)),
                  pl.BlockSpec((tm, d), lambda i: (i, B_GB // d)),
                  const(wm), const(wd), const(wo)],
        out_specs=pl.BlockSpec((tm, d), lambda i: (i, 0)),
        out_shape=jax.ShapeDtypeStruct((m, d), F32),
        compiler_params=_params(("parallel",)),
        name="merge",
    )(x, mod, ym, o_f, o_b, p, nw, p, p, wm, wd, wo)


def _ffn_kernel(x_ref, xp_ref, xx_ref, nw_ref, mod_ref, wg_ref, wv_ref, cw_ref, wdn_ref, o_ref, xn_ref, ge_ref,
                *, tm, tiles_per_seq, k_shift, k_scale, k_gate):
    i = pl.program_id(0)
    j = pl.program_id(1)
    th = FFN_TILE

    @pl.when(j == 0)
    def _():
        first = (i % tiles_per_seq) == 0
        last = (i % tiles_per_seq) == tiles_per_seq - 1
        nw = nw_ref[...]
        sh = mod_ref[0, k_shift:k_shift + 1, :]
        sc = mod_ref[0, k_scale:k_scale + 1, :]
        xn_ref[0:HALO16, :] = jnp.where(first, 0.0, _norm_mod(xp_ref[...], nw, sh, sc)).astype(BF16)
        xn_ref[HALO16:HALO16 + tm, :] = _norm_mod(x_ref[...], nw, sh, sc).astype(BF16)
        xn_ref[HALO16 + tm:, :] = jnp.where(last, 0.0, _norm_mod(xx_ref[...], nw, sh, sc)).astype(BF16)
        o_ref[...] = jnp.zeros_like(o_ref)

    ge_ref[...] = jnp.dot(xn_ref[...], wg_ref[...], preferred_element_type=F32)
    val = jnp.dot(xn_ref[HALO16:HALO16 + tm, :], wv_ref[...], preferred_element_type=F32)
    conv = (cw_ref[0:1, :] * ge_ref[HALO16 - 1:HALO16 - 1 + tm, :]
            + cw_ref[1:2, :] * ge_ref[HALO16:HALO16 + tm, :]
            + cw_ref[2:3, :] * ge_ref[HALO16 + 1:HALO16 + 1 + tm, :])
    hid = (_silu(conv) * val).astype(BF16)
    o_ref[...] += jnp.dot(hid, wdn_ref[...], preferred_element_type=F32)

    @pl.when(j == pl.num_programs(1) - 1)
    def _():
        o_ref[...] = x_ref[...] + mod_ref[0, k_gate:k_gate + 1, :] * o_ref[...]


def _ffn(x, nw, mod, row_of_tile, wg, wv, cw, wdn, seq, tm, k_shift, k_scale, k_gate):
    m, d = x.shape
    th = FFN_TILE
    nj = wdn.shape[0] // th
    nt = seq // tm
    nb16 = tm // HALO16
    total16 = m // HALO16
    kern = functools.partial(_ffn_kernel, tm=tm, tiles_per_seq=nt, k_shift=k_shift, k_scale=k_scale,
                             k_gate=k_gate)
    return pl.pallas_call(
        kern,
        grid=(m // tm, nj),
        in_specs=[pl.BlockSpec((tm, d), lambda i, j: (i, 0)),
                  pl.BlockSpec((HALO16, d), lambda i, j: (jnp.maximum(i * nb16 - 1, 0), 0)),
                  pl.BlockSpec((HALO16, d), lambda i, j: (jnp.minimum((i + 1) * nb16, total16 - 1), 0)),
                  pl.BlockSpec((1, d), lambda i, j: (0, 0)),
                  pl.BlockSpec((1, 6, d), lambda i, j: (row_of_tile(i), 0, 0)),
                  pl.BlockSpec((d, th), lambda i, j: (0, j)),
                  pl.BlockSpec((d, th), lambda i, j: (0, j)),
                  pl.BlockSpec((3, th), lambda i, j: (0, j)),
                  pl.BlockSpec((th, d), lambda i, j: (j, 0))],
        out_specs=pl.BlockSpec((tm, d), lambda i, j: (i, 0)),
        out_shape=jax.ShapeDtypeStruct((m, d), F32),
        scratch_shapes=[pltpu.VMEM((tm + 2 * HALO16, d), BF16), pltpu.VMEM((tm + 2 * HALO16, th), F32)],
        compiler_params=_params(("parallel", "arbitrary")),
        name="ffn",
    )(x, x, x, nw, mod, wg, wv, cw, wdn)


def _rope_pad_cols(r):
    n = MLA_ROPE // 4
    z = jnp.zeros(r.shape[:-1] + (2 * n,), r.dtype)
    return jnp.concatenate([r[..., 0:n], r[..., 2 * n:3 * n], z, r[..., n:2 * n], r[..., 3 * n:4 * n], z], axis=-1)


def _qk_pad_cols(w):
    lead = w.shape[:-1]
    w = w.reshape(lead + (MLA_HEADS, MLA_QK))
    out = jnp.concatenate([w[..., :MLA_NOPE], _rope_pad_cols(w[..., MLA_NOPE:])], axis=-1)
    return out.reshape(lead + (MLA_HEADS * MLA_QK_PAD,))


def _rope_tables(rows):
    n = MLA_ROPE // 4
    inv = ROPE_BASE ** (-jnp.arange(n, dtype=F32) / n)
    ang_r = jnp.arange(rows, dtype=F32)[:, None] * inv
    ang_c = jnp.arange(GRID_W, dtype=F32)[:, None] * inv

    def lanes(a_row, a_col, fill, count):
        return jnp.concatenate([a_row, a_col, jnp.full((count, 2 * n), fill, F32)] * 2, axis=-1)

    zr, zc = jnp.zeros((rows, n), F32), jnp.zeros((GRID_W, n), F32)
    row_cos = lanes(jnp.cos(ang_r), zr, 0.0, rows)
    row_sin = jnp.concatenate([-jnp.sin(ang_r), zr, jnp.zeros((rows, 2 * n), F32),
                               jnp.sin(ang_r), zr, jnp.zeros((rows, 2 * n), F32)], axis=-1)
    col_cos = lanes(zc, jnp.cos(ang_c), 1.0, GRID_W)
    col_sin = jnp.concatenate([zc, -jnp.sin(ang_c), jnp.zeros((GRID_W, 2 * n), F32),
                               zc, jnp.sin(ang_c), jnp.zeros((GRID_W, 2 * n), F32)], axis=-1)
    return row_cos, row_sin, col_cos, col_sin


def _pick_tile(n, pref):
    t = min(pref, n)
    while n % t:
        t //= 2
    return t


def kernel(x, c, ctx, c_ctx, w_ada, b_ada, norm_mix, w_in, q_a_norm, w_q_b, kv_a_norm, w_kv_b, q_norm, k_norm,
           w_o_mla, dn_conv, dn_a_log, dn_dt_bias, dn_o_norm, w_o_dn, w_out, norm_ffn, w_ffn_up, ffn_conv,
           w_ffn_down):
    batch, seq, d = x.shape
    lc = ctx.shape[1]
    assert w_ada.shape[0] == 1, "single-layer stack"
    assert seq % GRID_W == 0 and seq % DN_CHUNK == 0 and lc % DN_CHUNK == 0
    h = DN_HEADS
    l = 0

    wi = w_in[l]
    o_qa, o_kva, o_kr = 0, MLA_Q_RANK, MLA_Q_RANK + MLA_KV_RANK
    o_dq = o_kr + MLA_ROPE
    o_da = o_dq + 4 * h * DN_DK
    o_ga = o_da + 4 * h
    w_a = jnp.concatenate([wi[:, o_qa:o_kr], _rope_pad_cols(wi[:, o_kr:o_dq]), wi[:, o_da:o_ga],
                           jnp.zeros((d, A_WIDTH - A_DB - 2 * h), F32)], axis=1).astype(BF16)
    w_b = jnp.concatenate([wi[:, o_dq:o_da], wi[:, o_ga:]], axis=1).astype(BF16)
    wq = _qk_pad_cols(w_q_b[l]).astype(BF16)
    wkv = w_kv_b[l].reshape(MLA_KV_RANK, MLA_HEADS, MLA_NOPE + MLA_V)
    wkv = jnp.concatenate([wkv[:, :, :MLA_NOPE].reshape(MLA_KV_RANK, -1),
                           wkv[:, :, MLA_NOPE:].reshape(MLA_KV_RANK, -1)], axis=1).astype(BF16)
    qn = jnp.concatenate([q_norm[l, :MLA_NOPE], _rope_pad_cols(q_norm[l, MLA_NOPE:])])[None, :]
    kn = jnp.concatenate([k_norm[l, :MLA_NOPE], _rope_pad_cols(k_norm[l, MLA_NOPE:])])[None, :]
    hidden = w_ffn_down.shape[1]
    nj = -(-hidden // FFN_TILE)
    hpad = nj * FFN_TILE - hidden
    wg = jnp.pad(w_ffn_up[l][:, :hidden].astype(BF16), ((0, 0), (0, hpad)))
    wv = jnp.pad(w_ffn_up[l][:, hidden:].astype(BF16), ((0, 0), (0, hpad)))
    fcw = jnp.pad(ffn_conv[l], ((0, 0), (0, hpad)))
    wdn = jnp.pad(w_ffn_down[l], ((0, hpad), (0, 0))).astype(BF16)
    hp = jnp.concatenate([dn_a_log[l].T, dn_dt_bias[l].T], axis=1)[:, None, :]
    tables_l = _rope_tables(seq // GRID_W)
    tables_c = (jnp.zeros((lc // GRID_W, ROPE_PAD), F32), jnp.zeros((lc // GRID_W, ROPE_PAD), F32),
                jnp.ones((GRID_W, ROPE_PAD), F32), jnp.zeros((GRID_W, ROPE_PAD), F32))

    rows = jnp.concatenate([c, c_ctx[None, :], jnp.zeros((8 - batch - 1, d), F32)], axis=0)
    mod = _ada(rows, w_ada[l], b_ada[l][None, :]).reshape(8, 6, d)

    xf = x.reshape(batch * seq, d)
    cf = ctx.reshape(batch * lc, d)

    def stream(tokens, n, tm_a, tm_b, row_of):
        nm = norm_mix[l][None, :]
        a, xn = _norm_mod_matmul(tokens, nm, mod, row_of(tm_a), w_a, F32, tm_a, 0, 1, "in_proj_small")
        p = _matmul(xn, w_b, BF16, tm_b, 1024, "in_proj_wide")
        ab = a[:, A_DA:A_DA + 4 * h].reshape(batch * n, 4, h).transpose(2, 0, 1)
        return a, p, ab

    tm_a = _pick_tile(seq, 512)
    tm_b = _pick_tile(seq, 1024)
    lat_row = lambda tm: (lambda i: i // (seq // tm))
    a_l, p_l, ab_l = stream(xf, seq, tm_a, tm_b, lat_row)
    tc = _pick_tile(lc, 256)
    ctx_row = lambda tm: (lambda i: batch)
    a_c, p_c, ab_c = stream(cf, lc, tc, tc, ctx_row)

    qan, kvan = q_a_norm[l][None, :], kv_a_norm[l][None, :]
    tmp = _pick_tile(seq, 512)
    q_l, k_l, v_l = _mla_proj(a_l, tables_l, qan, kvan, qn, kn, wq, wkv, tmp, seq)
    _, k_c, v_c = _mla_proj(a_c, tables_c, qan, kvan, qn, kn, wq, wkv, tc, lc)
    tq = _pick_tile(seq, 512)
    tk = _pick_tile(seq, 512)
    qt = q_l.reshape(batch, seq, MLA_HEADS, MLA_QK_PAD).transpose(0, 2, 3, 1)

    def with_ones_row(vt):
        lead, n = vt.shape[:-2], vt.shape[-1]
        return jnp.concatenate([vt, jnp.ones(lead + (1, n), BF16), jnp.zeros(lead + (HALO16 - 1, n), BF16)],
                               axis=-2)

    vt = with_ones_row(v_l.reshape(batch, seq // tk, tk, MLA_HEADS, MLA_V).transpose(0, 3, 1, 4, 2))
    vct = with_ones_row(v_c.reshape(batch, lc, MLA_HEADS, MLA_V).transpose(0, 2, 3, 1))
    y_mla = _attention(qt, k_l.reshape(batch, seq, -1), vt, k_c.reshape(batch, lc, -1), vct, tq)
    y_mla = y_mla.reshape(batch * seq, MLA_HEADS * MLA_V)

    conv_w = dn_conv[l]
    prep_c = _dn_prep(p_c, ab_c, hp, conv_w, batch, lc, _pick_tile(lc, 256))
    s_zero = jnp.zeros((2, batch, h, DN_DK, DN_DV), F32)
    _, _, s_ctx = _dn_scan(s_zero, *prep_c, _pick_tile(lc, 256) // DN_CHUNK)
    prep_l = _dn_prep(p_l, ab_l, hp, conv_w, batch, seq, _pick_tile(seq, 1024))
    o_f, o_b, _ = _dn_scan(s_ctx, *prep_l, _pick_tile(seq, 256) // DN_CHUNK)

    tmm = _pick_tile(seq, 256)
    x1 = _merge(xf, mod, lat_row(tmm), y_mla, o_f, o_b, p_l, dn_o_norm[l][None, :], w_o_mla[l].astype(BF16),
                w_o_dn[l].astype(BF16), w_out[l].astype(BF16), tmm, 2)

    tmf = _pick_tile(seq, 512)
    out = _ffn(x1, norm_ffn[l][None, :], mod, lat_row(tmf), wg, wv, fcw, wdn, seq, tmf, 3, 4, 5)
    return out.reshape(batch, seq, d)
```

```python
import functools
import math

import jax
import jax.numpy as jnp
import numpy as np
from jax import lax
from jax.experimental import pallas as pl
from jax.experimental.pallas import tpu as pltpu

F32 = jnp.float32
BF16 = jnp.bfloat16
HIGHEST = lax.Precision.HIGHEST

EPS = 1e-6
GRID_W = 64
ROPE_BASE = 10000.0

MLA_HEADS = 8
MLA_Q_RANK = 512
MLA_KV_RANK = 512
MLA_NOPE = 128
MLA_ROPE = 64
MLA_V = 128
MLA_QK = MLA_NOPE + MLA_ROPE
MLA_QK_PAD = 256

DN_HEADS = 8
DN_DK = 128
DN_DV = 128
DN_CHUNK = 64

LANE = 128
SUBLANE = 8
ROPE_PAD = 128
VMEM_LIMIT = 56 * 1024 * 1024

A_QA, A_KVA, A_KR, A_DA, A_DB, A_WIDTH = 0, 512, 1024, 1152, 1168, 1280
B_DQ, B_DK, B_DV, B_DZ, B_GA, B_GB, B_WIDTH = 0, 1024, 2048, 3072, 4096, 6144, 8192

FFN_TILE = 512
HALO16 = 16


def _params(sem, vmem=VMEM_LIMIT):
    return pltpu.CompilerParams(dimension_semantics=sem, vmem_limit_bytes=vmem)


def _nt_dot(a, b):
    return lax.dot_general(a, b, (((1,), (1,)), ((), ())), preferred_element_type=F32)


def _silu(x):
    return x * jax.nn.sigmoid(x)


def _ada_kernel(c_ref, w_ref, b_ref, o_ref):
    a = _silu(c_ref[...]).astype(BF16)
    o_ref[...] = jnp.dot(a, w_ref[...].astype(BF16), preferred_element_type=F32) + b_ref[...]


def _ada(cs, w, b):
    m, d = cs.shape
    n = w.shape[1]
    tn = 512
    return pl.pallas_call(
        _ada_kernel,
        grid=(n // tn,),
        in_specs=[pl.BlockSpec((m, d), lambda j: (0, 0)),
                  pl.BlockSpec((d, tn), lambda j: (0, j)),
                  pl.BlockSpec((1, tn), lambda j: (0, j))],
        out_specs=pl.BlockSpec((m, tn), lambda j: (0, j)),
        out_shape=jax.ShapeDtypeStruct((m, n), F32),
        compiler_params=_params(("parallel",)),
        name="ada",
    )(cs, w, b)


def _norm_mod(x, nw, shift, scale):
    ms = jnp.mean(x * x, axis=-1, keepdims=True)
    y = x * lax.rsqrt(ms + EPS) * nw
    return y * (1.0 + scale) + shift


def _nmm_kernel(x_ref, nw_ref, mod_ref, w_ref, o_ref, xn_ref, *, k_shift, k_scale):
    y = _norm_mod(x_ref[...], nw_ref[...], mod_ref[0, k_shift:k_shift + 1, :], mod_ref[0, k_scale:k_scale + 1, :])
    xn = y.astype(BF16)
    xn_ref[...] = xn
    o_ref[...] = jnp.dot(xn, w_ref[...], preferred_element_type=F32).astype(o_ref.dtype)


def _norm_mod_matmul(x, nw, mod, row_of_tile, w, out_dtype, tm, k_shift, k_scale, name):
    m, d = x.shape
    n = w.shape[1]
    kern = functools.partial(_nmm_kernel, k_shift=k_shift, k_scale=k_scale)
    return pl.pallas_call(
        kern,
        grid=(m // tm,),
        in_specs=[pl.BlockSpec((tm, d), lambda i: (i, 0)),
                  pl.BlockSpec((1, d), lambda i: (0, 0)),
                  pl.BlockSpec((1, 6, d), lambda i: (row_of_tile(i), 0, 0)),
                  pl.BlockSpec((d, n), lambda i: (0, 0))],
        out_specs=[pl.BlockSpec((tm, n), lambda i: (i, 0)), pl.BlockSpec((tm, d), lambda i: (i, 0))],
        out_shape=[jax.ShapeDtypeStruct((m, n), out_dtype), jax.ShapeDtypeStruct((m, d), BF16)],
        compiler_params=_params(("parallel",)),
        name=name,
    )(x, nw, mod, w)


def _mm_kernel(x_ref, w_ref, o_ref):
    o_ref[...] = jnp.dot(x_ref[...], w_ref[...], preferred_element_type=F32).astype(o_ref.dtype)


def _matmul(x, w, out_dtype, tm, tn, name):
    m, d = x.shape
    n = w.shape[1]
    return pl.pallas_call(
        _mm_kernel,
        grid=(m // tm, n // tn),
        in_specs=[pl.BlockSpec((tm, d), lambda i, j: (i, 0)),
                  pl.BlockSpec((d, tn), lambda i, j: (0, j))],
        out_specs=pl.BlockSpec((tm, tn), lambda i, j: (i, j)),
        out_shape=jax.ShapeDtypeStruct((m, n), out_dtype),
        compiler_params=_params(("parallel", "parallel")),
        name=name,
    )(x, w)


def _mla_proj_kernel(a_ref, cr_ref, sr_ref, cc_ref, sc_ref, qan_ref, kvan_ref, qn_ref, kn_ref, wq_ref, wkv_ref,
                     q_ref, k_ref, v_ref):
    tm = a_ref.shape[0]
    g = tm // GRID_W

    def table(row_ref, col_ref):
        r = jnp.broadcast_to(row_ref[...][:, None, :], (g, GRID_W, ROPE_PAD)).reshape(tm, ROPE_PAD)
        c = jnp.broadcast_to(col_ref[...][None], (g, GRID_W, ROPE_PAD)).reshape(tm, ROPE_PAD)
        return r + c

    cos = table(cr_ref, cc_ref)
    sin = table(sr_ref, sc_ref)

    def rope(r):
        return r * cos + pltpu.roll(r, 64, axis=1) * sin

    def rms_rows(t, w):
        return t * lax.rsqrt(jnp.mean(t * t, axis=-1, keepdims=True) + EPS) * w

    qa = rms_rows(a_ref[:, A_QA:A_QA + MLA_Q_RANK], qan_ref[...]).astype(BF16)
    q = jnp.dot(qa, wq_ref[...], preferred_element_type=F32)
    kva = rms_rows(a_ref[:, A_KVA:A_KVA + MLA_KV_RANK], kvan_ref[...]).astype(BF16)
    kv = jnp.dot(kva, wkv_ref[...], preferred_element_type=F32)
    kr = a_ref[:, A_KR:A_KR + ROPE_PAD]
    kr_ss = jnp.sum(kr * kr, axis=-1, keepdims=True)
    qn = qn_ref[...]
    kn = kn_ref[...]
    scale = MLA_QK ** -0.5 * math.log2(math.e)
    for h in range(MLA_HEADS):
        qh = q[:, h * MLA_QK_PAD:(h + 1) * MLA_QK_PAD]
        inv = lax.rsqrt(jnp.sum(qh * qh, axis=-1, keepdims=True) * (1.0 / MLA_QK) + EPS) * scale
        qh = qh * inv * qn
        q_ref[:, h * MLA_QK_PAD:h * MLA_QK_PAD + MLA_NOPE] = qh[:, :MLA_NOPE].astype(BF16)
        q_ref[:, h * MLA_QK_PAD + MLA_NOPE:(h + 1) * MLA_QK_PAD] = rope(qh[:, MLA_NOPE:]).astype(BF16)

        kh = kv[:, h * MLA_NOPE:(h + 1) * MLA_NOPE]
        inv = lax.rsqrt((jnp.sum(kh * kh, axis=-1, keepdims=True) + kr_ss) * (1.0 / MLA_QK) + EPS)
        k_ref[:, h * MLA_QK_PAD:h * MLA_QK_PAD + MLA_NOPE] = (kh * inv * kn[:, :MLA_NOPE]).astype(BF16)
        k_ref[:, h * MLA_QK_PAD + MLA_NOPE:(h + 1) * MLA_QK_PAD] = rope(
            kr * inv * kn[:, MLA_NOPE:]).astype(BF16)
    v_ref[...] = kv[:, MLA_HEADS * MLA_NOPE:].astype(BF16)


def _mla_proj(a, tables, qan, kvan, qn, kn, wq, wkv, tm, seq):
    m = a.shape[0]
    nseq = seq // tm
    row_cos, row_sin, col_cos, col_sin = tables
    full = lambda arr: pl.BlockSpec(arr.shape, lambda i: (0, 0))
    row_spec = pl.BlockSpec((tm // GRID_W, ROPE_PAD), lambda i: (i % nseq, 0))
    return pl.pallas_call(
        _mla_proj_kernel,
        grid=(m // tm,),
        in_specs=[pl.BlockSpec((tm, A_WIDTH), lambda i: (i, 0)),
                  row_spec, row_spec, full(col_cos), full(col_sin),
                  full(qan), full(kvan), full(qn), full(kn), full(wq), full(wkv)],
        out_specs=[pl.BlockSpec((tm, MLA_HEADS * MLA_QK_PAD), lambda i: (i, 0)),
                   pl.BlockSpec((tm, MLA_HEADS * MLA_QK_PAD), lambda i: (i, 0)),
                   pl.BlockSpec((tm, MLA_HEADS * MLA_V), lambda i: (i, 0))],
        out_shape=[jax.ShapeDtypeStruct((m, MLA_HEADS * MLA_QK_PAD), BF16),
                   jax.ShapeDtypeStruct((m, MLA_HEADS * MLA_QK_PAD), BF16),
                   jax.ShapeDtypeStruct((m, MLA_HEADS * MLA_V), BF16)],
        compiler_params=_params(("parallel",)),
        name="mla_proj",
    )(a, row_cos, row_sin, col_cos, col_sin, qan, kvan, qn, kn, wq, wkv)


ATTN_UNROLL = 8
ATTN_ROWS = 64


def _attn_kernel(qt_ref, kc_ref, vct_ref, k_ref, vt_ref, o_ref, s0_ref, s1_ref, acc_ref, *, nk, unroll):
    qt = qt_ref[0, 0]
    tq = qt.shape[1]
    tk = s0_ref.shape[0]
    rows = ATTN_ROWS
    s_refs = (s0_ref, s1_ref)

    def scores(j):
        return jnp.dot(k_ref[0, pl.ds(pl.multiple_of(j * tk, tk), tk), :], qt, preferred_element_type=F32)

    def probs(s, m):
        return jnp.exp2((s - m).astype(BF16))

    def store_scores(slot, s):
        s_refs[slot][...] = s
        return jnp.max(s.reshape(tk // SUBLANE, SUBLANE, tq), axis=0)

    s = jnp.dot(kc_ref[0], qt, preferred_element_type=F32)
    m8_first = store_scores(0, scores(0))
    m = jnp.max(s, axis=0, keepdims=True)
    acc_ref[...] = jnp.dot(vct_ref[0, 0], probs(s, m), preferred_element_type=F32)

    def tile(j, slot, carry):
        m, m8 = carry
        m8_next = store_scores(1 - slot, scores(jnp.minimum(j + 1, nk - 1)))
        m_new = jnp.maximum(m, jnp.max(m8, axis=0, keepdims=True))
        p = jnp.concatenate([probs(s_refs[slot][r:r + rows, :], m_new) for r in range(0, tk, rows)], axis=0)
        acc_ref[...] = jnp.exp2(m - m_new) * acc_ref[...] + jnp.dot(vt_ref[0, 0, j], p, preferred_element_type=F32)
        return m_new, m8_next

    def body(i, carry):
        for u in range(unroll):
            carry = tile(unroll * i + u, u % 2, carry)
        return carry

    lax.fori_loop(0, nk // unroll, body, (m, m8_first))
    acc = acc_ref[...]
    o_ref[0] = (acc[:MLA_V] * (1.0 / acc[MLA_V:MLA_V + 1])).T.astype(o_ref.dtype)


def _attention(qt, k, vt, kc, vct, tq):
    b, t, _ = k.shape
    lc = kc.shape[1]
    nk, vrows, tk = vt.shape[2], vt.shape[3], vt.shape[4]
    unroll = ATTN_UNROLL if nk % ATTN_UNROLL == 0 else 2
    assert nk % unroll == 0
    kern = functools.partial(_attn_kernel, nk=nk, unroll=unroll)
    return pl.pallas_call(
        kern,
        grid=(b, MLA_HEADS, t // tq),
        in_specs=[pl.BlockSpec((1, 1, MLA_QK_PAD, tq), lambda bi, h, i: (bi, h, 0, i)),
                  pl.BlockSpec((1, lc, MLA_QK_PAD), lambda bi, h, i: (bi, 0, h)),
                  pl.BlockSpec((1, 1, vrows, lc), lambda bi, h, i: (bi, h, 0, 0)),
                  pl.BlockSpec((1, t, MLA_QK_PAD), lambda bi, h, i: (bi, 0, h)),
                  pl.BlockSpec((1, 1, nk, vrows, tk), lambda bi, h, i: (bi, h, 0, 0, 0))],
        out_specs=pl.BlockSpec((1, tq, MLA_V), lambda bi, h, i: (bi, i, h)),
        out_shape=jax.ShapeDtypeStruct((b, t, MLA_HEADS * MLA_V), BF16),
        scratch_shapes=[pltpu.VMEM((tk, tq), F32), pltpu.VMEM((tk, tq), F32), pltpu.VMEM((vrows, tq), F32)],
        compiler_params=_params(("parallel", "parallel", "arbitrary")),
        name="attention",
    )(qt, kc, vct, k, vt)


def _split2(x):
    hi = x.astype(BF16)
    return hi, (x - hi.astype(F32)).astype(BF16)


def _bmm(a, b):
    return lax.dot_general(a, b, (((2,), (1,)), ((0,), (0,))), preferred_element_type=F32)


def _bmm_nt(a, b):
    return lax.dot_general(a, b, (((2,), (2,)), ((0,), (0,))), preferred_element_type=F32)


def _bmm_hp(a, b):
    ah, al = _split2(a)
    bh, bl = _split2(b)
    return _bmm(ah, bh) + _bmm(al, bh) + _bmm(ah, bl)


def _tri_inverse(m, eye):
    assert m.shape[-1] == 64
    n = -m
    x = eye + n
    for _ in range(3):
        nb = n.astype(BF16)
        n = _bmm(nb, nb)
        x = x + _bmm(x.astype(BF16), n.astype(BF16))
    xb = x.astype(BF16)
    r = eye - x - _bmm(m.astype(BF16), xb)
    x = x + _bmm(xb, r.astype(BF16))
    r = eye - x - _bmm_hp(m, x)
    return x + _bmm(x.astype(BF16), r.astype(BF16))


def _dn_prep_kernel(qm_ref, qp_ref, qx_ref, km_ref, kp_ref, kx_ref, vm_ref, vp_ref, vx_ref,
                    cwq_ref, cwk_ref, cwv_ref, ab_ref, hp_ref,
                    u_ref, w_ref, ke_ref, qg_ref, a_ref, ge_ref, ext_ref, *, tm, tiles_per_seq):
    i = pl.program_id(0)
    first = (i % tiles_per_seq) == 0
    last = (i % tiles_per_seq) == tiles_per_seq - 1
    c = DN_CHUNK

    def conv_silu(main_ref, prev_ref, next_ref, cw_ref):
        ext_ref[0:HALO16, :] = jnp.where(first, 0.0, prev_ref[...].astype(F32))
        ext_ref[HALO16:HALO16 + tm, :] = main_ref[...].astype(F32)
        ext_ref[HALO16 + tm:, :] = jnp.where(last, 0.0, next_ref[...].astype(F32))
        y = (cw_ref[0:1, :] * ext_ref[HALO16 - 1:HALO16 - 1 + tm, :]
             + cw_ref[1:2, :] * ext_ref[HALO16:HALO16 + tm, :]
             + cw_ref[2:3, :] * ext_ref[HALO16 + 1:HALO16 + 1 + tm, :])
        return _silu(y)

    def l2n(t):
        return t * lax.rsqrt(jnp.sum(t * t, axis=-1, keepdims=True) + EPS)

    q = l2n(conv_silu(qm_ref, qp_ref, qx_ref, cwq_ref)) * (DN_DK ** -0.5)
    k = l2n(conv_silu(km_ref, kp_ref, kx_ref, cwk_ref))
    v = conv_silu(vm_ref, vp_ref, vx_ref, cwv_ref)

    hp = hp_ref[0]
    ab = ab_ref[0]
    z = ab[:, 0:2] + hp[:, 2:4]
    softplus = jnp.maximum(z, 0.0) + jnp.log(1.0 + jnp.exp(-jnp.abs(z)))
    g_all = -jnp.exp(hp[:, 0:2]) * softplus
    beta_all = jax.nn.sigmoid(ab[:, 2:4])

    nc = tm // c

    def per_chunk(t):
        return t.reshape(nc, c, t.shape[-1])

    def both(t):
        return jnp.concatenate([t, t], axis=0)

    def by_dir(t2):
        return jnp.concatenate([per_chunk(jnp.broadcast_to(t2[:, d:d + 1], (tm, LANE))) for d in range(2)],
                               axis=0)

    shape = (2 * nc, c, c)
    fwd = lax.broadcasted_iota(jnp.int32, shape, 0) < nc
    rows = lax.broadcasted_iota(jnp.int32, shape, 1)
    cols = lax.broadcasted_iota(jnp.int32, shape, 2)
    eye_mask = rows == cols
    ahead = jnp.where(fwd, cols - rows, rows - cols)
    incl = ahead <= 0
    strict = ahead < 0
    eye = jnp.where(eye_mask, 1.0, 0.0)
    tri = jnp.where(incl, 1.0, 0.0).astype(BF16)

    g_hi, g_lo = _split2(by_dir(g_all))
    gc = _bmm(tri, g_hi) + _bmm(tri, g_lo)
    gc_sq = gc[:, :, :c]
    gc_row = jnp.sum(jnp.where(eye_mask, gc_sq, 0.0), axis=1, keepdims=True)
    decay = jnp.where(incl, jnp.exp(jnp.where(incl, gc_sq - gc_row, 0.0)), 0.0)
    gtot = jnp.concatenate([gc[:nc, c - 1:c, :], gc[nc:, 0:1, :]], axis=0)
    eg = jnp.exp(gc)

    beta = by_dir(beta_all)
    k3 = per_chunk(k)
    q3 = per_chunk(q)
    k3b = k3.astype(BF16)
    kk = both(_bmm_nt(k3b, k3b))
    qk = both(_bmm_nt(q3.astype(BF16), k3b))
    k2 = both(k3)
    kb = k2 * beta
    m = jnp.where(strict, beta[:, :, :c] * kk * decay, 0.0)
    tinv = _tri_inverse(m, eye)
    t_hi, t_lo = _split2(tinv)
    rhs = jnp.concatenate([both(per_chunk(v)) * beta, kb * eg], axis=2).astype(BF16)
    sol = _bmm(t_hi, rhs) + _bmm(t_lo, rhs)
    ke = k2 * jnp.exp(gtot - gc)
    qg = both(q3) * eg
    att = qk * decay
    gend = jnp.exp(gtot)
    for d in range(2):
        sl = slice(d * nc, (d + 1) * nc)
        u_ref[d, 0, 0] = sol[sl, :, :DN_DV].reshape(tm, DN_DV)
        w_ref[d, 0, 0] = sol[sl, :, DN_DV:].reshape(tm, DN_DK).astype(BF16)
        ke_ref[d, 0, 0] = ke[sl].reshape(tm, DN_DK).astype(BF16)
        qg_ref[d, 0, 0] = qg[sl].reshape(tm, DN_DK).astype(BF16)
        a_ref[d, 0, 0] = att[sl].reshape(tm, c).astype(BF16)
        ge_ref[d, 0, 0] = gend[sl]


def _dn_prep(p, ab, hp, conv_w, batch, seq, tm):
    h = DN_HEADS
    nt = seq // tm
    nb16 = tm // HALO16
    total16 = batch * seq // HALO16
    kern = functools.partial(_dn_prep_kernel, tm=tm, tiles_per_seq=nt)

    def triple(col0):
        return [pl.BlockSpec((tm, LANE), lambda i, hh: (i, col0 + hh)),
                pl.BlockSpec((HALO16, LANE), lambda i, hh: (jnp.maximum(i * nb16 - 1, 0), col0 + hh)),
                pl.BlockSpec((HALO16, LANE), lambda i, hh: (jnp.minimum((i + 1) * nb16, total16 - 1), col0 + hh))]

    cq, ck, cv = B_DQ // LANE, B_DK // LANE, B_DV // LANE
    row = lambda shape: pl.BlockSpec(shape, lambda i, hh: (0, i // nt, hh, i % nt, 0))
    seq_shape = lambda width, dt: jax.ShapeDtypeStruct((2, batch, h, seq, width), dt)
    return pl.pallas_call(
        kern,
        grid=(batch * nt, h),
        in_specs=triple(cq) + triple(ck) + triple(cv) + [
            pl.BlockSpec((3, LANE), lambda i, hh: (0, hh)),
            pl.BlockSpec((3, LANE), lambda i, hh: (0, h + hh)),
            pl.BlockSpec((3, LANE), lambda i, hh: (0, 2 * h + hh)),
            pl.BlockSpec((1, tm, 4), lambda i, hh: (hh, i, 0)),
            pl.BlockSpec((1, 1, 4), lambda i, hh: (hh, 0, 0))],
        out_specs=[row((2, 1, 1, tm, DN_DV)), row((2, 1, 1, tm, DN_DK)), row((2, 1, 1, tm, DN_DK)),
                   row((2, 1, 1, tm, DN_DK)), row((2, 1, 1, tm, DN_CHUNK)),
                   pl.BlockSpec((2, 1, 1, tm // DN_CHUNK, 1, LANE), lambda i, hh: (0, i // nt, hh, i % nt, 0, 0))],
        out_shape=[seq_shape(DN_DV, F32), seq_shape(DN_DK, BF16), seq_shape(DN_DK, BF16),
                   seq_shape(DN_DK, BF16), seq_shape(DN_CHUNK, BF16),
                   jax.ShapeDtypeStruct((2, batch, h, seq // DN_CHUNK, 1, LANE), F32)],
        scratch_shapes=[pltpu.VMEM((tm + 2 * HALO16, LANE), F32)],
        compiler_params=_params(("parallel", "parallel")),
        name="dn_prep",
    )(p, p, p, p, p, p, p, p, p, conv_w, conv_w, conv_w, ab, hp)


def _dn_scan_kernel(s0_ref, uf_ref, wf_ref, kf_ref, qf_ref, af_ref, gf_ref,
                    ub_ref, wb_ref, kb_ref, qb_ref, ab_ref, gb_ref,
                    of_ref, ob_ref, sfin_ref, s_ref, *, cs):
    n = pl.program_id(1)
    c = DN_CHUNK
    h = DN_HEADS

    @pl.when(n == 0)
    def _():
        s_ref[0:h] = s0_ref[0, 0]
        s_ref[h:] = s0_ref[1, 0]

    for ci in range(cs):
        rf = ci * c
        rb = (cs - 1 - ci) * c

        def pair(f_ref, b_ref):
            return jnp.concatenate([f_ref[0, 0, :, rf:rf + c, :], b_ref[0, 0, :, rb:rb + c, :]], axis=0)

        s = s_ref[...]
        s16 = s.astype(BF16)
        ws_qs = _bmm(jnp.concatenate([pair(wf_ref, wb_ref), pair(qf_ref, qb_ref)], axis=1), s16)
        v_new = pair(uf_ref, ub_ref) - ws_qs[:, :c]
        v16 = v_new.astype(BF16)
        o = ws_qs[:, c:] + _bmm(pair(af_ref, ab_ref), v16)
        of_ref[0, :, rf:rf + c, :] = o[:h].astype(of_ref.dtype)
        ob_ref[0, :, rb:rb + c, :] = o[h:].astype(ob_ref.dtype)
        g = jnp.concatenate([gf_ref[0, 0, :, ci], gb_ref[0, 0, :, cs - 1 - ci]], axis=0)
        s_ref[...] = s * g + lax.dot_general(pair(kf_ref, kb_ref), v16, (((1,), (1,)), ((0,), (0,))),
                                             preferred_element_type=F32)

    @pl.when(n == pl.num_programs(1) - 1)
    def _():
        sfin_ref[0, 0] = s_ref[0:h]
        sfin_ref[1, 0] = s_ref[h:]


def _dn_scan(s0, u, w, ke, qg, a, ge, cs):
    _, batch, h, seq, _ = u.shape
    ts = cs * DN_CHUNK
    ns = seq // ts
    kern = functools.partial(_dn_scan_kernel, cs=cs)

    def specs(d):
        idx = (lambda n: n) if d == 0 else (lambda n: ns - 1 - n)
        seqs = [pl.BlockSpec((1, 1, h, ts, width), lambda b, n: (d, b, 0, idx(n), 0))
                for width in (DN_DV, DN_DK, DN_DK, DN_DK, DN_CHUNK)]
        return seqs + [pl.BlockSpec((1, 1, h, cs, 1, LANE), lambda b, n: (d, b, 0, idx(n), 0, 0))]

    o_spec = lambda d: pl.BlockSpec((1, h, ts, DN_DV), lambda b, n: (b, 0, n if d == 0 else ns - 1 - n, 0))
    state = pl.BlockSpec((2, 1, h, DN_DK, DN_DV), lambda b, n: (0, b, 0, 0, 0))
    return pl.pallas_call(
        kern,
        grid=(batch, ns),
        in_specs=[state] + specs(0) + specs(1),
        out_specs=[o_spec(0), o_spec(1), state],
        out_shape=[jax.ShapeDtypeStruct((batch, h, seq, DN_DV), BF16),
                   jax.ShapeDtypeStruct((batch, h, seq, DN_DV), BF16),
                   jax.ShapeDtypeStruct((2, batch, h, DN_DK, DN_DV), F32)],
        scratch_shapes=[pltpu.VMEM((2 * h, DN_DK, DN_DV), F32)],
        compiler_params=_params(("parallel", "arbitrary")),
        name="dn_scan",
    )(s0, u, w, ke, qg, a, ge, u, w, ke, qg, a, ge)


def _merge_kernel(x_ref, mod_ref, ym_ref, of_ref, ob_ref, z_ref, nw_ref, ga_ref, gb_ref, wm_ref, wd_ref, wo_ref,
                  o_ref, *, k_gate):
    nw = nw_ref[...]
    heads = []
    for h in range(DN_HEADS):
        o = of_ref[0, h].astype(F32) + ob_ref[0, h].astype(F32)
        y = o * lax.rsqrt(jnp.mean(o * o, axis=-1, keepdims=True) + EPS) * nw
        heads.append((y * _silu(z_ref[:, h * DN_DV:(h + 1) * DN_DV].astype(F32))).astype(BF16))
    yd = jnp.concatenate(heads, axis=1)
    pm = jnp.dot(ym_ref[...], wm_ref[...], preferred_element_type=F32)
    pd = jnp.dot(yd, wd_ref[...], preferred_element_type=F32)
    merged = (jax.nn.sigmoid(ga_ref[...].astype(F32)) * pm
              + jax.nn.sigmoid(gb_ref[...].astype(F32)) * pd).astype(BF16)
    y = jnp.dot(merged, wo_ref[...], preferred_element_type=F32)
    o_ref[...] = x_ref[...] + mod_ref[0, k_gate:k_gate + 1, :] * y


def _merge(x, mod, row_of_tile, ym, o_f, o_b, p, nw, wm, wd, wo, tm, k_gate):
    m, d = x.shape
    _, h, seq, _ = o_f.shape
    nt = seq // tm
    zw = h * DN_DV
    kern = functools.partial(_merge_kernel, k_gate=k_gate)
    const = lambda arr: pl.BlockSpec(arr.shape, lambda i: (0, 0), pipeline_mode=pl.Buffered(1))
    scan_out = pl.BlockSpec((1, h, tm, DN_DV), lambda i: (i // nt, 0, i % nt, 0))
    return pl.pallas_call(
        kern,
        grid=(m // tm,),
        in_specs=[pl.BlockSpec((tm, d), lambda i: (i, 0)),
                  pl.BlockSpec((1, 6, d), lambda i: (row_of_tile(i), 0, 0)),
                  pl.BlockSpec((tm, ym.shape[1]), lambda i: (i, 0)),
                  scan_out, scan_out,
                  pl.BlockSpec((tm, zw), lambda i: (i, B_DZ // zw)),
                  pl.BlockSpec((1, DN_DV), lambda i: (0, 0)),
                  pl.BlockSpec((tm, d), lambda i: (i, B_GA // d)),
                  pl.BlockSpec((tm, d), lambda i: (i, B_GB // d)),
                  const(wm), const(wd), const(wo)],
        out_specs=pl.BlockSpec((tm, d), lambda i: (i, 0)),
        out_shape=jax.ShapeDtypeStruct((m, d), F32),
        compiler_params=_params(("parallel",)),
        name="merge",
    )(x, mod, ym, o_f, o_b, p, nw, p, p, wm, wd, wo)


def _ffn_kernel(x_ref, xp_ref, xx_ref, nw_ref, mod_ref, wg_ref, wv_ref, cw_ref, wdn_ref, o_ref, xn_ref, ge_ref,
                *, tm, tiles_per_seq, k_shift, k_scale, k_gate):
    i = pl.program_id(0)
    j = pl.program_id(1)
    th = FFN_TILE

    @pl.when(j == 0)
    def _():
        first = (i % tiles_per_seq) == 0
        last = (i % tiles_per_seq) == tiles_per_seq - 1
        nw = nw_ref[...]
        sh = mod_ref[0, k_shift:k_shift + 1, :]
        sc = mod_ref[0, k_scale:k_scale + 1, :]
        xn_ref[0:HALO16, :] = jnp.where(first, 0.0, _norm_mod(xp_ref[...], nw, sh, sc)).astype(BF16)
        xn_ref[HALO16:HALO16 + tm, :] = _norm_mod(x_ref[...], nw, sh, sc).astype(BF16)
        xn_ref[HALO16 + tm:, :] = jnp.where(last, 0.0, _norm_mod(xx_ref[...], nw, sh, sc)).astype(BF16)
        o_ref[...] = jnp.zeros_like(o_ref)

    ge_ref[...] = jnp.dot(xn_ref[...], wg_ref[...], preferred_element_type=F32)
    val = jnp.dot(xn_ref[HALO16:HALO16 + tm, :], wv_ref[...], preferred_element_type=F32)
    conv = (cw_ref[0:1, :] * ge_ref[HALO16 - 1:HALO16 - 1 + tm, :]
            + cw_ref[1:2, :] * ge_ref[HALO16:HALO16 + tm, :]
            + cw_ref[2:3, :] * ge_ref[HALO16 + 1:HALO16 + 1 + tm, :])
    hid = (_silu(conv) * val).astype(BF16)
    o_ref[...] += jnp.dot(hid, wdn_ref[...], preferred_element_type=F32)

    @pl.when(j == pl.num_programs(1) - 1)
    def _():
        o_ref[...] = x_ref[...] + mod_ref[0, k_gate:k_gate + 1, :] * o_ref[...]


def _ffn(x, nw, mod, row_of_tile, wg, wv, cw, wdn, seq, tm, k_shift, k_scale, k_gate):
    m, d = x.shape
    th = FFN_TILE
    nj = wdn.shape[0] // th
    nt = seq // tm
    nb16 = tm // HALO16
    total16 = m // HALO16
    kern = functools.partial(_ffn_kernel, tm=tm, tiles_per_seq=nt, k_shift=k_shift, k_scale=k_scale,
                             k_gate=k_gate)
    return pl.pallas_call(
        kern,
        grid=(m // tm, nj),
        in_specs=[pl.BlockSpec((tm, d), lambda i, j: (i, 0)),
                  pl.BlockSpec((HALO16, d), lambda i, j: (jnp.maximum(i * nb16 - 1, 0), 0)),
                  pl.BlockSpec((HALO16, d), lambda i, j: (jnp.minimum((i + 1) * nb16, total16 - 1), 0)),
                  pl.BlockSpec((1, d), lambda i, j: (0, 0)),
                  pl.BlockSpec((1, 6, d), lambda i, j: (row_of_tile(i), 0, 0)),
                  pl.BlockSpec((d, th), lambda i, j: (0, j)),
                  pl.BlockSpec((d, th), lambda i, j: (0, j)),
                  pl.BlockSpec((3, th), lambda i, j: (0, j)),
                  pl.BlockSpec((th, d), lambda i, j: (j, 0))],
        out_specs=pl.BlockSpec((tm, d), lambda i, j: (i, 0)),
        out_shape=jax.ShapeDtypeStruct((m, d), F32),
        scratch_shapes=[pltpu.VMEM((tm + 2 * HALO16, d), BF16), pltpu.VMEM((tm + 2 * HALO16, th), F32)],
        compiler_params=_params(("parallel", "arbitrary")),
        name="ffn",
    )(x, x, x, nw, mod, wg, wv, cw, wdn)


def _rope_pad_cols(r):
    n = MLA_ROPE // 4
    z = jnp.zeros(r.shape[:-1] + (2 * n,), r.dtype)
    return jnp.concatenate([r[..., 0:n], r[..., 2 * n:3 * n], z, r[..., n:2 * n], r[..., 3 * n:4 * n], z], axis=-1)


def _qk_pad_cols(w):
    lead = w.shape[:-1]
    w = w.reshape(lead + (MLA_HEADS, MLA_QK))
    out = jnp.concatenate([w[..., :MLA_NOPE], _rope_pad_cols(w[..., MLA_NOPE:])], axis=-1)
    return out.reshape(lead + (MLA_HEADS * MLA_QK_PAD,))


def _rope_tables(rows):
    n = MLA_ROPE // 4
    inv = ROPE_BASE ** (-jnp.arange(n, dtype=F32) / n)
    ang_r = jnp.arange(rows, dtype=F32)[:, None] * inv
    ang_c = jnp.arange(GRID_W, dtype=F32)[:, None] * inv

    def lanes(a_row, a_col, fill, count):
        return jnp.concatenate([a_row, a_col, jnp.full((count, 2 * n), fill, F32)] * 2, axis=-1)

    zr, zc = jnp.zeros((rows, n), F32), jnp.zeros((GRID_W, n), F32)
    row_cos = lanes(jnp.cos(ang_r), zr, 0.0, rows)
    row_sin = jnp.concatenate([-jnp.sin(ang_r), zr, jnp.zeros((rows, 2 * n), F32),
                               jnp.sin(ang_r), zr, jnp.zeros((rows, 2 * n), F32)], axis=-1)
    col_cos = lanes(zc, jnp.cos(ang_c), 1.0, GRID_W)
    col_sin = jnp.concatenate([zc, -jnp.sin(ang_c), jnp.zeros((GRID_W, 2 * n), F32),
                               zc, jnp.sin(ang_c), jnp.zeros((GRID_W, 2 * n), F32)], axis=-1)
    return row_cos, row_sin, col_cos, col_sin


def _pick_tile(n, pref):
    t = min(pref, n)
    while n % t:
        t //= 2
    return t


def kernel(x, c, ctx, c_ctx, w_ada, b_ada, norm_mix, w_in, q_a_norm, w_q_b, kv_a_norm, w_kv_b, q_norm, k_norm,
           w_o_mla, dn_conv, dn_a_log, dn_dt_bias, dn_o_norm, w_o_dn, w_out, norm_ffn, w_ffn_up, ffn_conv,
           w_ffn_down):
    batch, seq, d = x.shape
    lc = ctx.shape[1]
    assert w_ada.shape[0] == 1, "single-layer stack"
    assert seq % GRID_W == 0 and seq % DN_CHUNK == 0 and lc % DN_CHUNK == 0
    h = DN_HEADS
    l = 0

    wi = w_in[l]
    o_qa, o_kva, o_kr = 0, MLA_Q_RANK, MLA_Q_RANK + MLA_KV_RANK
    o_dq = o_kr + MLA_ROPE
    o_da = o_dq + 4 * h * DN_DK
    o_ga = o_da + 4 * h
    w_a = jnp.concatenate([wi[:, o_qa:o_kr], _rope_pad_cols(wi[:, o_kr:o_dq]), wi[:, o_da:o_ga],
                           jnp.zeros((d, A_WIDTH - A_DB - 2 * h), F32)], axis=1).astype(BF16)
    w_b = jnp.concatenate([wi[:, o_dq:o_da], wi[:, o_ga:]], axis=1).astype(BF16)
    wq = _qk_pad_cols(w_q_b[l]).astype(BF16)
    wkv = w_kv_b[l].reshape(MLA_KV_RANK, MLA_HEADS, MLA_NOPE + MLA_V)
    wkv = jnp.concatenate([wkv[:, :, :MLA_NOPE].reshape(MLA_KV_RANK, -1),
                           wkv[:, :, MLA_NOPE:].reshape(MLA_KV_RANK, -1)], axis=1).astype(BF16)
    qn = jnp.concatenate([q_norm[l, :MLA_NOPE], _rope_pad_cols(q_norm[l, MLA_NOPE:])])[None, :]
    kn = jnp.concatenate([k_norm[l, :MLA_NOPE], _rope_pad_cols(k_norm[l, MLA_NOPE:])])[None, :]
    hidden = w_ffn_down.shape[1]
    nj = -(-hidden // FFN_TILE)
    hpad = nj * FFN_TILE - hidden
    wg = jnp.pad(w_ffn_up[l][:, :hidden].astype(BF16), ((0, 0), (0, hpad)))
    wv = jnp.pad(w_ffn_up[l][:, hidden:].astype(BF16), ((0, 0), (0, hpad)))
    fcw = jnp.pad(ffn_conv[l], ((0, 0), (0, hpad)))
    wdn = jnp.pad(w_ffn_down[l], ((0, hpad), (0, 0))).astype(BF16)
    hp = jnp.concatenate([dn_a_log[l].T, dn_dt_bias[l].T], axis=1)[:, None, :]
    tables_l = _rope_tables(seq // GRID_W)
    tables_c = (jnp.zeros((lc // GRID_W, ROPE_PAD), F32), jnp.zeros((lc // GRID_W, ROPE_PAD), F32),
                jnp.ones((GRID_W, ROPE_PAD), F32), jnp.zeros((GRID_W, ROPE_PAD), F32))

    rows = jnp.concatenate([c, c_ctx[None, :], jnp.zeros((8 - batch - 1, d), F32)], axis=0)
    mod = _ada(rows, w_ada[l], b_ada[l][None, :]).reshape(8, 6, d)

    xf = x.reshape(batch * seq, d)
    cf = ctx.reshape(batch * lc, d)

    def stream(tokens, n, tm_a, tm_b, row_of):
        nm = norm_mix[l][None, :]
        a, xn = _norm_mod_matmul(tokens, nm, mod, row_of(tm_a), w_a, F32, tm_a, 0, 1, "in_proj_small")
        p = _matmul(xn, w_b, BF16, tm_b, 1024, "in_proj_wide")
        ab = a[:, A_DA:A_DA + 4 * h].reshape(batch * n, 4, h).transpose(2, 0, 1)
        return a, p, ab

    tm_a = _pick_tile(seq, 512)
    tm_b = _pick_tile(seq, 1024)
    lat_row = lambda tm: (lambda i: i // (seq // tm))
    a_l, p_l, ab_l = stream(xf, seq, tm_a, tm_b, lat_row)
    tc = _pick_tile(lc, 256)
    ctx_row = lambda tm: (lambda i: batch)
    a_c, p_c, ab_c = stream(cf, lc, tc, tc, ctx_row)

    qan, kvan = q_a_norm[l][None, :], kv_a_norm[l][None, :]
    tmp = _pick_tile(seq, 512)
    q_l, k_l, v_l = _mla_proj(a_l, tables_l, qan, kvan, qn, kn, wq, wkv, tmp, seq)
    _, k_c, v_c = _mla_proj(a_c, tables_c, qan, kvan, qn, kn, wq, wkv, tc, lc)
    tq = _pick_tile(seq, 512)
    tk = _pick_tile(seq, 512)
    qt = q_l.reshape(batch, seq, MLA_HEADS, MLA_QK_PAD).transpose(0, 2, 3, 1)

    def with_ones_row(vt):
        lead, n = vt.shape[:-2], vt.shape[-1]
        return jnp.concatenate([vt, jnp.ones(lead + (1, n), BF16), jnp.zeros(lead + (HALO16 - 1, n), BF16)],
                               axis=-2)

    vt = with_ones_row(v_l.reshape(batch, seq // tk, tk, MLA_HEADS, MLA_V).transpose(0, 3, 1, 4, 2))
    vct = with_ones_row(v_c.reshape(batch, lc, MLA_HEADS, MLA_V).transpose(0, 2, 3, 1))
    y_mla = _attention(qt, k_l.reshape(batch, seq, -1), vt, k_c.reshape(batch, lc, -1), vct, tq)
    y_mla = y_mla.reshape(batch * seq, MLA_HEADS * MLA_V)

    conv_w = dn_conv[l]
    prep_c = _dn_prep(p_c, ab_c, hp, conv_w, batch, lc, _pick_tile(lc, 256))
    s_zero = jnp.zeros((2, batch, h, DN_DK, DN_DV), F32)
    _, _, s_ctx = _dn_scan(s_zero, *prep_c, _pick_tile(lc, 256) // DN_CHUNK)
    prep_l = _dn_prep(p_l, ab_l, hp, conv_w, batch, seq, _pick_tile(seq, 1024))
    o_f, o_b, _ = _dn_scan(s_ctx, *prep_l, _pick_tile(seq, 256) // DN_CHUNK)

    tmm = _pick_tile(seq, 256)
    x1 = _merge(xf, mod, lat_row(tmm), y_mla, o_f, o_b, p_l, dn_o_norm[l][None, :], w_o_mla[l].astype(BF16),
                w_o_dn[l].astype(BF16), w_out[l].astype(BF16), tmm, 2)

    tmf = _pick_tile(seq, 512)
    out = _ffn(x1, norm_ffn[l][None, :], mod, lat_row(tmf), wg, wv, fcw, wdn, seq, tmf, 3, 4, 5)
    return out.reshape(batch, seq, d)
```

```python
import functools
import math

import jax
import jax.numpy as jnp
import numpy as np
from jax import lax
from jax.experimental import pallas as pl
from jax.experimental.pallas import tpu as pltpu

F32 = jnp.float32
BF16 = jnp.bfloat16
HIGHEST = lax.Precision.HIGHEST

EPS = 1e-6
GRID_W = 64
ROPE_BASE = 10000.0

MLA_HEADS = 8
MLA_Q_RANK = 512
MLA_KV_RANK = 512
MLA_NOPE = 128
MLA_ROPE = 64
MLA_V = 128
MLA_QK = MLA_NOPE + MLA_ROPE
MLA_QK_PAD = 256

DN_HEADS = 8
DN_DK = 128
DN_DV = 128
DN_CHUNK = 64

LANE = 128
SUBLANE = 8
ROPE_PAD = 128
VMEM_LIMIT = 56 * 1024 * 1024

A_QA, A_KVA, A_KR, A_DA, A_DB, A_WIDTH = 0, 512, 1024, 1152, 1168, 1280
B_DQ, B_DK, B_DV, B_DZ, B_GA, B_GB, B_WIDTH = 0, 1024, 2048, 3072, 4096, 6144, 8192

FFN_TILE = 512
HALO16 = 16


def _params(sem, vmem=VMEM_LIMIT):
    return pltpu.CompilerParams(dimension_semantics=sem, vmem_limit_bytes=vmem)


def _nt_dot(a, b):
    return lax.dot_general(a, b, (((1,), (1,)), ((), ())), preferred_element_type=F32)


def _silu(x):
    return x * jax.nn.sigmoid(x)


def _ada_kernel(c_ref, w_ref, b_ref, o_ref):
    a = _silu(c_ref[...]).astype(BF16)
    o_ref[...] = jnp.dot(a, w_ref[...].astype(BF16), preferred_element_type=F32) + b_ref[...]


def _ada(cs, w, b):
    m, d = cs.shape
    n = w.shape[1]
    tn = 512
    return pl.pallas_call(
        _ada_kernel,
        grid=(n // tn,),
        in_specs=[pl.BlockSpec((m, d), lambda j: (0, 0)),
                  pl.BlockSpec((d, tn), lambda j: (0, j)),
                  pl.BlockSpec((1, tn), lambda j: (0, j))],
        out_specs=pl.BlockSpec((m, tn), lambda j: (0, j)),
        out_shape=jax.ShapeDtypeStruct((m, n), F32),
        compiler_params=_params(("parallel",)),
        name="ada",
    )(cs, w, b)


def _norm_mod(x, nw, shift, scale):
    ms = jnp.mean(x * x, axis=-1, keepdims=True)
    y = x * lax.rsqrt(ms + EPS) * nw
    return y * (1.0 + scale) + shift


def _nmm_kernel(x_ref, nw_ref, mod_ref, w_ref, o_ref, xn_ref, *, k_shift, k_scale):
    y = _norm_mod(x_ref[...], nw_ref[...], mod_ref[0, k_shift:k_shift + 1, :], mod_ref[0, k_scale:k_scale + 1, :])
    xn = y.astype(BF16)
    xn_ref[...] = xn
    o_ref[...] = jnp.dot(xn, w_ref[...], preferred_element_type=F32).astype(o_ref.dtype)


def _norm_mod_matmul(x, nw, mod, row_of_tile, w, out_dtype, tm, k_shift, k_scale, name):
    m, d = x.shape
    n = w.shape[1]
    kern = functools.partial(_nmm_kernel, k_shift=k_shift, k_scale=k_scale)
    return pl.pallas_call(
        kern,
        grid=(m // tm,),
        in_specs=[pl.BlockSpec((tm, d), lambda i: (i, 0)),
                  pl.BlockSpec((1, d), lambda i: (0, 0)),
                  pl.BlockSpec((1, 6, d), lambda i: (row_of_tile(i), 0, 0)),
                  pl.BlockSpec((d, n), lambda i: (0, 0))],
        out_specs=[pl.BlockSpec((tm, n), lambda i: (i, 0)), pl.BlockSpec((tm, d), lambda i: (i, 0))],
        out_shape=[jax.ShapeDtypeStruct((m, n), out_dtype), jax.ShapeDtypeStruct((m, d), BF16)],
        compiler_params=_params(("parallel",)),
        name=name,
    )(x, nw, mod, w)


def _mm_kernel(x_ref, w_ref, o_ref):
    o_ref[...] = jnp.dot(x_ref[...], w_ref[...], preferred_element_type=F32).astype(o_ref.dtype)


def _matmul(x, w, out_dtype, tm, tn, name):
    m, d = x.shape
    n = w.shape[1]
    return pl.pallas_call(
        _mm_kernel,
        grid=(m // tm, n // tn),
        in_specs=[pl.BlockSpec((tm, d), lambda i, j: (i, 0)),
                  pl.BlockSpec((d, tn), lambda i, j: (0, j))],
        out_specs=pl.BlockSpec((tm, tn), lambda i, j: (i, j)),
        out_shape=jax.ShapeDtypeStruct((m, n), out_dtype),
        compiler_params=_params(("parallel", "parallel")),
        name=name,
    )(x, w)


def _mla_proj_kernel(a_ref, cr_ref, sr_ref, cc_ref, sc_ref, qan_ref, kvan_ref, qn_ref, kn_ref, wq_ref, wkv_ref,
                     q_ref, k_ref, v_ref):
    tm = a_ref.shape[0]
    g = tm // GRID_W

    def table(row_ref, col_ref):
        r = jnp.broadcast_to(row_ref[...][:, None, :], (g, GRID_W, ROPE_PAD)).reshape(tm, ROPE_PAD)
        c = jnp.broadcast_to(col_ref[...][None], (g, GRID_W, ROPE_PAD)).reshape(tm, ROPE_PAD)
        return r + c

    cos = table(cr_ref, cc_ref)
    sin = table(sr_ref, sc_ref)

    def rope(r):
        return r * cos + pltpu.roll(r, 64, axis=1) * sin

    def rms_rows(t, w):
        return t * lax.rsqrt(jnp.mean(t * t, axis=-1, keepdims=True) + EPS) * w

    qa = rms_rows(a_ref[:, A_QA:A_QA + MLA_Q_RANK], qan_ref[...]).astype(BF16)
    q = jnp.dot(qa, wq_ref[...], preferred_element_type=F32)
    kva = rms_rows(a_ref[:, A_KVA:A_KVA + MLA_KV_RANK], kvan_ref[...]).astype(BF16)
    kv = jnp.dot(kva, wkv_ref[...], preferred_element_type=F32)
    kr = a_ref[:, A_KR:A_KR + ROPE_PAD]
    kr_ss = jnp.sum(kr * kr, axis=-1, keepdims=True)
    qn = qn_ref[...]
    kn = kn_ref[...]
    scale = MLA_QK ** -0.5 * math.log2(math.e)
    for h in range(MLA_HEADS):
        qh = q[:, h * MLA_QK_PAD:(h + 1) * MLA_QK_PAD]
        inv = lax.rsqrt(jnp.sum(qh * qh, axis=-1, keepdims=True) * (1.0 / MLA_QK) + EPS) * scale
        qh = qh * inv * qn
        q_ref[:, h * MLA_QK_PAD:h * MLA_QK_PAD + MLA_NOPE] = qh[:, :MLA_NOPE].astype(BF16)
        q_ref[:, h * MLA_QK_PAD + MLA_NOPE:(h + 1) * MLA_QK_PAD] = rope(qh[:, MLA_NOPE:]).astype(BF16)

        kh = kv[:, h * MLA_NOPE:(h + 1) * MLA_NOPE]
        inv = lax.rsqrt((jnp.sum(kh * kh, axis=-1, keepdims=True) + kr_ss) * (1.0 / MLA_QK) + EPS)
        k_ref[:, h * MLA_QK_PAD:h * MLA_QK_PAD + MLA_NOPE] = (kh * inv * kn[:, :MLA_NOPE]).astype(BF16)
        k_ref[:, h * MLA_QK_PAD + MLA_NOPE:(h + 1) * MLA_QK_PAD] = rope(
            kr * inv * kn[:, MLA_NOPE:]).astype(BF16)
    v_ref[...] = kv[:, MLA_HEADS * MLA_NOPE:].astype(BF16)


def _mla_proj(a, tables, qan, kvan, qn, kn, wq, wkv, tm, seq):
    m = a.shape[0]
    nseq = seq // tm
    row_cos, row_sin, col_cos, col_sin = tables
    full = lambda arr: pl.BlockSpec(arr.shape, lambda i: (0, 0))
    row_spec = pl.BlockSpec((tm // GRID_W, ROPE_PAD), lambda i: (i % nseq, 0))
    return pl.pallas_call(
        _mla_proj_kernel,
        grid=(m // tm,),
        in_specs=[pl.BlockSpec((tm, A_WIDTH), lambda i: (i, 0)),
                  row_spec, row_spec, full(col_cos), full(col_sin),
                  full(qan), full(kvan), full(qn), full(kn), full(wq), full(wkv)],
        out_specs=[pl.BlockSpec((tm, MLA_HEADS * MLA_QK_PAD), lambda i: (i, 0)),
                   pl.BlockSpec((tm, MLA_HEADS * MLA_QK_PAD), lambda i: (i, 0)),
                   pl.BlockSpec((tm, MLA_HEADS * MLA_V), lambda i: (i, 0))],
        out_shape=[jax.ShapeDtypeStruct((m, MLA_HEADS * MLA_QK_PAD), BF16),
                   jax.ShapeDtypeStruct((m, MLA_HEADS * MLA_QK_PAD), BF16),
                   jax.ShapeDtypeStruct((m, MLA_HEADS * MLA_V), BF16)],
        compiler_params=_params(("parallel",)),
        name="mla_proj",
    )(a, row_cos, row_sin, col_cos, col_sin, qan, kvan, qn, kn, wq, wkv)


ATTN_UNROLL = 8
ATTN_ROWS = 64
ATTN_MAX_LAG = 4.0


def _attn_kernel(qt_ref, kc_ref, vct_ref, k_ref, vt_ref, o_ref, s0_ref, s1_ref, acc_ref, *, nk, unroll):
    qt = qt_ref[0, 0]
    tq = qt.shape[1]
    tk = s0_ref.shape[0]
    rows = ATTN_ROWS
    s_refs = (s0_ref, s1_ref)

    def scores(j):
        return jnp.dot(k_ref[0, pl.ds(pl.multiple_of(j * tk, tk), tk), :], qt, preferred_element_type=F32)

    def probs(s, m):
        return jnp.exp2((s - m).astype(BF16))

    def column_max8(s):
        return jnp.max(s.reshape(tk // SUBLANE, SUBLANE, tq), axis=0)

    def context_step():
        s = jnp.dot(kc_ref[0], qt, preferred_element_type=F32)
        m = jnp.max(s, axis=0, keepdims=True)
        acc_ref[...] = jnp.dot(vct_ref[0, 0], probs(s, m), preferred_element_type=F32)
        return m

    def write_out():
        acc = acc_ref[...]
        o_ref[0] = (acc[:MLA_V] * (1.0 / acc[MLA_V:MLA_V + 1])).T.astype(o_ref.dtype)

    def lagged_tile(j, carry):
        ref, alpha, lag = carry
        s = scores(j)
        m_t = jnp.max(column_max8(s), axis=0, keepdims=True)
        acc_ref[...] = alpha * acc_ref[...] + jnp.dot(vt_ref[0, 0, j], probs(s, ref), preferred_element_type=F32)
        ref_next = jnp.maximum(ref, m_t)
        return ref_next, jnp.exp2(ref - ref_next), jnp.maximum(lag, m_t - ref)

    def lagged_body(i, carry):
        for u in range(unroll):
            carry = lagged_tile(unroll * i + u, carry)
        return carry

    one = jnp.ones((1, tq), F32)
    _, _, lag = lax.fori_loop(0, nk // unroll, lagged_body, (context_step(), one, jnp.zeros((1, tq), F32)))
    write_out()

    @pl.when(jnp.max(lag) > ATTN_MAX_LAG)
    def _():
        def store_scores(slot, s):
            s_refs[slot][...] = s
            return column_max8(s)

        def tile(j, slot, carry):
            m, m8 = carry
            m8_next = store_scores(1 - slot, scores(jnp.minimum(j + 1, nk - 1)))
            m_new = jnp.maximum(m, jnp.max(m8, axis=0, keepdims=True))
            p = jnp.concatenate([probs(s_refs[slot][r:r + rows, :], m_new) for r in range(0, tk, rows)], axis=0)
            acc_ref[...] = (jnp.exp2(m - m_new) * acc_ref[...]
                            + jnp.dot(vt_ref[0, 0, j], p, preferred_element_type=F32))
            return m_new, m8_next

        def body(i, carry):
            for u in range(unroll):
                carry = tile(unroll * i + u, u % 2, carry)
            return carry

        m8_first = store_scores(0, scores(0))
        lax.fori_loop(0, nk // unroll, body, (context_step(), m8_first))
        write_out()


def _attention(qt, k, vt, kc, vct, tq):
    b, t, _ = k.shape
    lc = kc.shape[1]
    nk, vrows, tk = vt.shape[2], vt.shape[3], vt.shape[4]
    unroll = ATTN_UNROLL if nk % ATTN_UNROLL == 0 else 2
    assert nk % unroll == 0
    kern = functools.partial(_attn_kernel, nk=nk, unroll=unroll)
    return pl.pallas_call(
        kern,
        grid=(b, MLA_HEADS, t // tq),
        in_specs=[pl.BlockSpec((1, 1, MLA_QK_PAD, tq), lambda bi, h, i: (bi, h, 0, i)),
                  pl.BlockSpec((1, lc, MLA_QK_PAD), lambda bi, h, i: (bi, 0, h)),
                  pl.BlockSpec((1, 1, vrows, lc), lambda bi, h, i: (bi, h, 0, 0)),
                  pl.BlockSpec((1, t, MLA_QK_PAD), lambda bi, h, i: (bi, 0, h)),
                  pl.BlockSpec((1, 1, nk, vrows, tk), lambda bi, h, i: (bi, h, 0, 0, 0))],
        out_specs=pl.BlockSpec((1, tq, MLA_V), lambda bi, h, i: (bi, i, h)),
        out_shape=jax.ShapeDtypeStruct((b, t, MLA_HEADS * MLA_V), BF16),
        scratch_shapes=[pltpu.VMEM((tk, tq), F32), pltpu.VMEM((tk, tq), F32), pltpu.VMEM((vrows, tq), F32)],
        compiler_params=_params(("parallel", "parallel", "arbitrary")),
        name="attention",
    )(qt, kc, vct, k, vt)


def _split2(x):
    hi = x.astype(BF16)
    return hi, (x - hi.astype(F32)).astype(BF16)


def _bmm(a, b):
    return lax.dot_general(a, b, (((2,), (1,)), ((0,), (0,))), preferred_element_type=F32)


def _bmm_nt(a, b):
    return lax.dot_general(a, b, (((2,), (2,)), ((0,), (0,))), preferred_element_type=F32)


def _bmm_hp(a, b):
    ah, al = _split2(a)
    bh, bl = _split2(b)
    return _bmm(ah, bh) + _bmm(al, bh) + _bmm(ah, bl)


def _tri_inverse(m, eye):
    assert m.shape[-1] == 64
    n = -m
    x = eye + n
    for _ in range(3):
        nb = n.astype(BF16)
        n = _bmm(nb, nb)
        x = x + _bmm(x.astype(BF16), n.astype(BF16))
    xb = x.astype(BF16)
    r = eye - x - _bmm(m.astype(BF16), xb)
    x = x + _bmm(xb, r.astype(BF16))
    r = eye - x - _bmm_hp(m, x)
    return x + _bmm(x.astype(BF16), r.astype(BF16))


def _dn_prep_kernel(qm_ref, qp_ref, qx_ref, km_ref, kp_ref, kx_ref, vm_ref, vp_ref, vx_ref,
                    cwq_ref, cwk_ref, cwv_ref, ab_ref, hp_ref,
                    u_ref, w_ref, ke_ref, qg_ref, a_ref, ge_ref, ext_ref, *, tm, tiles_per_seq):
    i = pl.program_id(0)
    first = (i % tiles_per_seq) == 0
    last = (i % tiles_per_seq) == tiles_per_seq - 1
    c = DN_CHUNK

    def conv_silu(main_ref, prev_ref, next_ref, cw_ref):
        ext_ref[0:HALO16, :] = jnp.where(first, 0.0, prev_ref[...].astype(F32))
        ext_ref[HALO16:HALO16 + tm, :] = main_ref[...].astype(F32)
        ext_ref[HALO16 + tm:, :] = jnp.where(last, 0.0, next_ref[...].astype(F32))
        y = (cw_ref[0:1, :] * ext_ref[HALO16 - 1:HALO16 - 1 + tm, :]
             + cw_ref[1:2, :] * ext_ref[HALO16:HALO16 + tm, :]
             + cw_ref[2:3, :] * ext_ref[HALO16 + 1:HALO16 + 1 + tm, :])
        return _silu(y)

    def l2n(t):
        return t * lax.rsqrt(jnp.sum(t * t, axis=-1, keepdims=True) + EPS)

    q = l2n(conv_silu(qm_ref, qp_ref, qx_ref, cwq_ref)) * (DN_DK ** -0.5)
    k = l2n(conv_silu(km_ref, kp_ref, kx_ref, cwk_ref))
    v = conv_silu(vm_ref, vp_ref, vx_ref, cwv_ref)

    hp = hp_ref[0]
    ab = ab_ref[0]
    z = ab[:, 0:2] + hp[:, 2:4]
    softplus = jnp.maximum(z, 0.0) + jnp.log(1.0 + jnp.exp(-jnp.abs(z)))
    g_all = -jnp.exp(hp[:, 0:2]) * softplus
    beta_all = jax.nn.sigmoid(ab[:, 2:4])

    nc = tm // c

    def per_chunk(t):
        return t.reshape(nc, c, t.shape[-1])

    def both(t):
        return jnp.concatenate([t, t], axis=0)

    def by_dir(t2):
        return jnp.concatenate([per_chunk(jnp.broadcast_to(t2[:, d:d + 1], (tm, LANE))) for d in range(2)],
                               axis=0)

    shape = (2 * nc, c, c)
    fwd = lax.broadcasted_iota(jnp.int32, shape, 0) < nc
    rows = lax.broadcasted_iota(jnp.int32, shape, 1)
    cols = lax.broadcasted_iota(jnp.int32, shape, 2)
    eye_mask = rows == cols
    ahead = jnp.where(fwd, cols - rows, rows - cols)
    incl = ahead <= 0
    strict = ahead < 0
    eye = jnp.where(eye_mask, 1.0, 0.0)
    tri = jnp.where(incl, 1.0, 0.0).astype(BF16)

    g_hi, g_lo = _split2(by_dir(g_all))
    gc = _bmm(tri, g_hi) + _bmm(tri, g_lo)
    gc_sq = gc[:, :, :c]
    gc_row = jnp.sum(jnp.where(eye_mask, gc_sq, 0.0), axis=1, keepdims=True)
    decay = jnp.where(incl, jnp.exp(jnp.where(incl, gc_sq - gc_row, 0.0)), 0.0)
    gtot = jnp.concatenate([gc[:nc, c - 1:c, :], gc[nc:, 0:1, :]], axis=0)
    eg = jnp.exp(gc)

    beta = by_dir(beta_all)
    k3 = per_chunk(k)
    q3 = per_chunk(q)
    k3b = k3.astype(BF16)
    kk = both(_bmm_nt(k3b, k3b))
    qk = both(_bmm_nt(q3.astype(BF16), k3b))
    k2 = both(k3)
    kb = k2 * beta
    m = jnp.where(strict, beta[:, :, :c] * kk * decay, 0.0)
    tinv = _tri_inverse(m, eye)
    t_hi, t_lo = _split2(tinv)
    rhs = jnp.concatenate([both(per_chunk(v)) * beta, kb * eg], axis=2).astype(BF16)
    sol = _bmm(t_hi, rhs) + _bmm(t_lo, rhs)
    ke = k2 * jnp.exp(gtot - gc)
    qg = both(q3) * eg
    att = qk * decay
    gend = jnp.exp(gtot)
    for d in range(2):
        sl = slice(d * nc, (d + 1) * nc)
        u_ref[d, 0, 0] = sol[sl, :, :DN_DV].reshape(tm, DN_DV)
        w_ref[d, 0, 0] = sol[sl, :, DN_DV:].reshape(tm, DN_DK).astype(BF16)
        ke_ref[d, 0, 0] = ke[sl].reshape(tm, DN_DK).astype(BF16)
        qg_ref[d, 0, 0] = qg[sl].reshape(tm, DN_DK).astype(BF16)
        a_ref[d, 0, 0] = att[sl].reshape(tm, c).astype(BF16)
        ge_ref[d, 0, 0] = gend[sl]


def _dn_prep(p, ab, hp, conv_w, batch, seq, tm):
    h = DN_HEADS
    nt = seq // tm
    nb16 = tm // HALO16
    total16 = batch * seq // HALO16
    kern = functools.partial(_dn_prep_kernel, tm=tm, tiles_per_seq=nt)

    def triple(col0):
        return [pl.BlockSpec((tm, LANE), lambda i, hh: (i, col0 + hh)),
                pl.BlockSpec((HALO16, LANE), lambda i, hh: (jnp.maximum(i * nb16 - 1, 0), col0 + hh)),
                pl.BlockSpec((HALO16, LANE), lambda i, hh: (jnp.minimum((i + 1) * nb16, total16 - 1), col0 + hh))]

    cq, ck, cv = B_DQ // LANE, B_DK // LANE, B_DV // LANE
    row = lambda shape: pl.BlockSpec(shape, lambda i, hh: (0, i // nt, hh, i % nt, 0))
    seq_shape = lambda width, dt: jax.ShapeDtypeStruct((2, batch, h, seq, width), dt)
    return pl.pallas_call(
        kern,
        grid=(batch * nt, h),
        in_specs=triple(cq) + triple(ck) + triple(cv) + [
            pl.BlockSpec((3, LANE), lambda i, hh: (0, hh)),
            pl.BlockSpec((3, LANE), lambda i, hh: (0, h + hh)),
            pl.BlockSpec((3, LANE), lambda i, hh: (0, 2 * h + hh)),
            pl.BlockSpec((1, tm, 4), lambda i, hh: (hh, i, 0)),
            pl.BlockSpec((1, 1, 4), lambda i, hh: (hh, 0, 0))],
        out_specs=[row((2, 1, 1, tm, DN_DV)), row((2, 1, 1, tm, DN_DK)), row((2, 1, 1, tm, DN_DK)),
                   row((2, 1, 1, tm, DN_DK)), row((2, 1, 1, tm, DN_CHUNK)),
                   pl.BlockSpec((2, 1, 1, tm // DN_CHUNK, 1, LANE), lambda i, hh: (0, i // nt, hh, i % nt, 0, 0))],
        out_shape=[seq_shape(DN_DV, F32), seq_shape(DN_DK, BF16), seq_shape(DN_DK, BF16),
                   seq_shape(DN_DK, BF16), seq_shape(DN_CHUNK, BF16),
                   jax.ShapeDtypeStruct((2, batch, h, seq // DN_CHUNK, 1, LANE), F32)],
        scratch_shapes=[pltpu.VMEM((tm + 2 * HALO16, LANE), F32)],
        compiler_params=_params(("parallel", "parallel")),
        name="dn_prep",
    )(p, p, p, p, p, p, p, p, p, conv_w, conv_w, conv_w, ab, hp)


def _dn_scan_kernel(s0_ref, uf_ref, wf_ref, kf_ref, qf_ref, af_ref, gf_ref,
                    ub_ref, wb_ref, kb_ref, qb_ref, ab_ref, gb_ref,
                    of_ref, ob_ref, sfin_ref, s_ref, *, cs):
    n = pl.program_id(1)
    c = DN_CHUNK
    h = DN_HEADS

    @pl.when(n == 0)
    def _():
        s_ref[0:h] = s0_ref[0, 0]
        s_ref[h:] = s0_ref[1, 0]

    for ci in range(cs):
        rf = ci * c
        rb = (cs - 1 - ci) * c

        def pair(f_ref, b_ref):
            return jnp.concatenate([f_ref[0, 0, :, rf:rf + c, :], b_ref[0, 0, :, rb:rb + c, :]], axis=0)

        s = s_ref[...]
        s16 = s.astype(BF16)
        ws_qs = _bmm(jnp.concatenate([pair(wf_ref, wb_ref), pair(qf_ref, qb_ref)], axis=1), s16)
        v_new = pair(uf_ref, ub_ref) - ws_qs[:, :c]
        v16 = v_new.astype(BF16)
        o = ws_qs[:, c:] + _bmm(pair(af_ref, ab_ref), v16)
        of_ref[0, :, rf:rf + c, :] = o[:h].astype(of_ref.dtype)
        ob_ref[0, :, rb:rb + c, :] = o[h:].astype(ob_ref.dtype)
        g = jnp.concatenate([gf_ref[0, 0, :, ci], gb_ref[0, 0, :, cs - 1 - ci]], axis=0)
        s_ref[...] = s * g + lax.dot_general(pair(kf_ref, kb_ref), v16, (((1,), (1,)), ((0,), (0,))),
                                             preferred_element_type=F32)

    @pl.when(n == pl.num_programs(1) - 1)
    def _():
        sfin_ref[0, 0] = s_ref[0:h]
        sfin_ref[1, 0] = s_ref[h:]


def _dn_scan(s0, u, w, ke, qg, a, ge, cs):
    _, batch, h, seq, _ = u.shape
    ts = cs * DN_CHUNK
    ns = seq // ts
    kern = functools.partial(_dn_scan_kernel, cs=cs)

    def specs(d):
        idx = (lambda n: n) if d == 0 else (lambda n: ns - 1 - n)
        seqs = [pl.BlockSpec((1, 1, h, ts, width), lambda b, n: (d, b, 0, idx(n), 0))
                for width in (DN_DV, DN_DK, DN_DK, DN_DK, DN_CHUNK)]
        return seqs + [pl.BlockSpec((1, 1, h, cs, 1, LANE), lambda b, n: (d, b, 0, idx(n), 0, 0))]

    o_spec = lambda d: pl.BlockSpec((1, h, ts, DN_DV), lambda b, n: (b, 0, n if d == 0 else ns - 1 - n, 0))
    state = pl.BlockSpec((2, 1, h, DN_DK, DN_DV), lambda b, n: (0, b, 0, 0, 0))
    return pl.pallas_call(
        kern,
        grid=(batch, ns),
        in_specs=[state] + specs(0) + specs(1),
        out_specs=[o_spec(0), o_spec(1), state],
        out_shape=[jax.ShapeDtypeStruct((batch, h, seq, DN_DV), BF16),
                   jax.ShapeDtypeStruct((batch, h, seq, DN_DV), BF16),
                   jax.ShapeDtypeStruct((2, batch, h, DN_DK, DN_DV), F32)],
        scratch_shapes=[pltpu.VMEM((2 * h, DN_DK, DN_DV), F32)],
        compiler_params=_params(("parallel", "arbitrary")),
        name="dn_scan",
    )(s0, u, w, ke, qg, a, ge, u, w, ke, qg, a, ge)


def _merge_kernel(x_ref, mod_ref, ym_ref, of_ref, ob_ref, z_ref, nw_ref, ga_ref, gb_ref, wm_ref, wd_ref, wo_ref,
                  o_ref, *, k_gate):
    nw = nw_ref[...]
    heads = []
    for h in range(DN_HEADS):
        o = of_ref[0, h].astype(F32) + ob_ref[0, h].astype(F32)
        y = o * lax.rsqrt(jnp.mean(o * o, axis=-1, keepdims=True) + EPS) * nw
        heads.append((y * _silu(z_ref[:, h * DN_DV:(h + 1) * DN_DV].astype(F32))).astype(BF16))
    yd = jnp.concatenate(heads, axis=1)
    pm = jnp.dot(ym_ref[...], wm_ref[...], preferred_element_type=F32)
    pd = jnp.dot(yd, wd_ref[...], preferred_element_type=F32)
    merged = (jax.nn.sigmoid(ga_ref[...].astype(F32)) * pm
              + jax.nn.sigmoid(gb_ref[...].astype(F32)) * pd).astype(BF16)
    y = jnp.dot(merged, wo_ref[...], preferred_element_type=F32)
    o_ref[...] = x_ref[...] + mod_ref[0, k_gate:k_gate + 1, :] * y


def _merge(x, mod, row_of_tile, ym, o_f, o_b, p, nw, wm, wd, wo, tm, k_gate):
    m, d = x.shape
    _, h, seq, _ = o_f.shape
    nt = seq // tm
    zw = h * DN_DV
    kern = functools.partial(_merge_kernel, k_gate=k_gate)
    const = lambda arr: pl.BlockSpec(arr.shape, lambda i: (0, 0), pipeline_mode=pl.Buffered(1))
    scan_out = pl.BlockSpec((1, h, tm, DN_DV), lambda i: (i // nt, 0, i % nt, 0))
    return pl.pallas_call(
        kern,
        grid=(m // tm,),
        in_specs=[pl.BlockSpec((tm, d), lambda i: (i, 0)),
                  pl.BlockSpec((1, 6, d), lambda i: (row_of_tile(i), 0, 0)),
                  pl.BlockSpec((tm, ym.shape[1]), lambda i: (i, 0)),
                  scan_out, scan_out,
                  pl.BlockSpec((tm, zw), lambda i: (i, B_DZ // zw)),
                  pl.BlockSpec((1, DN_DV), lambda i: (0, 0)),
                  pl.BlockSpec((tm, d), lambda i: (i, B_GA // d)),
                  pl.BlockSpec((tm, d), lambda i: (i, B_GB // d)),
                  const(wm), const(wd), const(wo)],
        out_specs=pl.BlockSpec((tm, d), lambda i: (i, 0)),
        out_shape=jax.ShapeDtypeStruct((m, d), F32),
        compiler_params=_params(("parallel",)),
        name="merge",
    )(x, mod, ym, o_f, o_b, p, nw, p, p, wm, wd, wo)


def _ffn_kernel(x_ref, xp_ref, xx_ref, nw_ref, mod_ref, wg_ref, wv_ref, cw_ref, wdn_ref, o_ref, xn_ref, ge_ref,
                *, tm, tiles_per_seq, k_shift, k_scale, k_gate):
    i = pl.program_id(0)
    j = pl.program_id(1)
    th = FFN_TILE

    @pl.when(j == 0)
    def _():
        first = (i % tiles_per_seq) == 0
        last = (i % tiles_per_seq) == tiles_per_seq - 1
        nw = nw_ref[...]
        sh = mod_ref[0, k_shift:k_shift + 1, :]
        sc = mod_ref[0, k_scale:k_scale + 1, :]
        xn_ref[0:HALO16, :] = jnp.where(first, 0.0, _norm_mod(xp_ref[...], nw, sh, sc)).astype(BF16)
        xn_ref[HALO16:HALO16 + tm, :] = _norm_mod(x_ref[...], nw, sh, sc).astype(BF16)
        xn_ref[HALO16 + tm:, :] = jnp.where(last, 0.0, _norm_mod(xx_ref[...], nw, sh, sc)).astype(BF16)
        o_ref[...] = jnp.zeros_like(o_ref)

    ge_ref[...] = jnp.dot(xn_ref[...], wg_ref[...], preferred_element_type=F32)
    val = jnp.dot(xn_ref[HALO16:HALO16 + tm, :], wv_ref[...], preferred_element_type=F32)
    conv = (cw_ref[0:1, :] * ge_ref[HALO16 - 1:HALO16 - 1 + tm, :]
            + cw_ref[1:2, :] * ge_ref[HALO16:HALO16 + tm, :]
            + cw_ref[2:3, :] * ge_ref[HALO16 + 1:HALO16 + 1 + tm, :])
    hid = (_silu(conv) * val).astype(BF16)
    o_ref[...] += jnp.dot(hid, wdn_ref[...], preferred_element_type=F32)

    @pl.when(j == pl.num_programs(1) - 1)
    def _():
        o_ref[...] = x_ref[...] + mod_ref[0, k_gate:k_gate + 1, :] * o_ref[...]


def _ffn(x, nw, mod, row_of_tile, wg, wv, cw, wdn, seq, tm, k_shift, k_scale, k_gate):
    m, d = x.shape
    th = FFN_TILE
    nj = wdn.shape[0] // th
    nt = seq // tm
    nb16 = tm // HALO16
    total16 = m // HALO16
    kern = functools.partial(_ffn_kernel, tm=tm, tiles_per_seq=nt, k_shift=k_shift, k_scale=k_scale,
                             k_gate=k_gate)
    return pl.pallas_call(
        kern,
        grid=(m // tm, nj),
        in_specs=[pl.BlockSpec((tm, d), lambda i, j: (i, 0)),
                  pl.BlockSpec((HALO16, d), lambda i, j: (jnp.maximum(i * nb16 - 1, 0), 0)),
                  pl.BlockSpec((HALO16, d), lambda i, j: (jnp.minimum((i + 1) * nb16, total16 - 1), 0)),
                  pl.BlockSpec((1, d), lambda i, j: (0, 0)),
                  pl.BlockSpec((1, 6, d), lambda i, j: (row_of_tile(i), 0, 0)),
                  pl.BlockSpec((d, th), lambda i, j: (0, j)),
                  pl.BlockSpec((d, th), lambda i, j: (0, j)),
                  pl.BlockSpec((3, th), lambda i, j: (0, j)),
                  pl.BlockSpec((th, d), lambda i, j: (j, 0))],
        out_specs=pl.BlockSpec((tm, d), lambda i, j: (i, 0)),
        out_shape=jax.ShapeDtypeStruct((m, d), F32),
        scratch_shapes=[pltpu.VMEM((tm + 2 * HALO16, d), BF16), pltpu.VMEM((tm + 2 * HALO16, th), F32)],
        compiler_params=_params(("parallel", "arbitrary")),
        name="ffn",
    )(x, x, x, nw, mod, wg, wv, cw, wdn)


def _rope_pad_cols(r):
    n = MLA_ROPE // 4
    z = jnp.zeros(r.shape[:-1] + (2 * n,), r.dtype)
    return jnp.concatenate([r[..., 0:n], r[..., 2 * n:3 * n], z, r[..., n:2 * n], r[..., 3 * n:4 * n], z], axis=-1)


def _qk_pad_cols(w):
    lead = w.shape[:-1]
    w = w.reshape(lead + (MLA_HEADS, MLA_QK))
    out = jnp.concatenate([w[..., :MLA_NOPE], _rope_pad_cols(w[..., MLA_NOPE:])], axis=-1)
    return out.reshape(lead + (MLA_HEADS * MLA_QK_PAD,))


def _rope_tables(rows):
    n = MLA_ROPE // 4
    inv = ROPE_BASE ** (-jnp.arange(n, dtype=F32) / n)
    ang_r = jnp.arange(rows, dtype=F32)[:, None] * inv
    ang_c = jnp.arange(GRID_W, dtype=F32)[:, None] * inv

    def lanes(a_row, a_col, fill, count):
        return jnp.concatenate([a_row, a_col, jnp.full((count, 2 * n), fill, F32)] * 2, axis=-1)

    zr, zc = jnp.zeros((rows, n), F32), jnp.zeros((GRID_W, n), F32)
    row_cos = lanes(jnp.cos(ang_r), zr, 0.0, rows)
    row_sin = jnp.concatenate([-jnp.sin(ang_r), zr, jnp.zeros((rows, 2 * n), F32),
                               jnp.sin(ang_r), zr, jnp.zeros((rows, 2 * n), F32)], axis=-1)
    col_cos = lanes(zc, jnp.cos(ang_c), 1.0, GRID_W)
    col_sin = jnp.concatenate([zc, -jnp.sin(ang_c), jnp.zeros((GRID_W, 2 * n), F32),
                               zc, jnp.sin(ang_c), jnp.zeros((GRID_W, 2 * n), F32)], axis=-1)
    return row_cos, row_sin, col_cos, col_sin


def _pick_tile(n, pref):
    t = min(pref, n)
    while n % t:
        t //= 2
    return t


def kernel(x, c, ctx, c_ctx, w_ada, b_ada, norm_mix, w_in, q_a_norm, w_q_b, kv_a_norm, w_kv_b, q_norm, k_norm,
           w_o_mla, dn_conv, dn_a_log, dn_dt_bias, dn_o_norm, w_o_dn, w_out, norm_ffn, w_ffn_up, ffn_conv,
           w_ffn_down):
    batch, seq, d = x.shape
    lc = ctx.shape[1]
    assert w_ada.shape[0] == 1, "single-layer stack"
    assert seq % GRID_W == 0 and seq % DN_CHUNK == 0 and lc % DN_CHUNK == 0
    h = DN_HEADS
    l = 0

    wi = w_in[l]
    o_qa, o_kva, o_kr = 0, MLA_Q_RANK, MLA_Q_RANK + MLA_KV_RANK
    o_dq = o_kr + MLA_ROPE
    o_da = o_dq + 4 * h * DN_DK
    o_ga = o_da + 4 * h
    w_a = jnp.concatenate([wi[:, o_qa:o_kr], _rope_pad_cols(wi[:, o_kr:o_dq]), wi[:, o_da:o_ga],
                           jnp.zeros((d, A_WIDTH - A_DB - 2 * h), F32)], axis=1).astype(BF16)
    w_b = jnp.concatenate([wi[:, o_dq:o_da], wi[:, o_ga:]], axis=1).astype(BF16)
    wq = _qk_pad_cols(w_q_b[l]).astype(BF16)
    wkv = w_kv_b[l].reshape(MLA_KV_RANK, MLA_HEADS, MLA_NOPE + MLA_V)
    wkv = jnp.concatenate([wkv[:, :, :MLA_NOPE].reshape(MLA_KV_RANK, -1),
                           wkv[:, :, MLA_NOPE:].reshape(MLA_KV_RANK, -1)], axis=1).astype(BF16)
    qn = jnp.concatenate([q_norm[l, :MLA_NOPE], _rope_pad_cols(q_norm[l, MLA_NOPE:])])[None, :]
    kn = jnp.concatenate([k_norm[l, :MLA_NOPE], _rope_pad_cols(k_norm[l, MLA_NOPE:])])[None, :]
    hidden = w_ffn_down.shape[1]
    nj = -(-hidden // FFN_TILE)
    hpad = nj * FFN_TILE - hidden
    wg = jnp.pad(w_ffn_up[l][:, :hidden].astype(BF16), ((0, 0), (0, hpad)))
    wv = jnp.pad(w_ffn_up[l][:, hidden:].astype(BF16), ((0, 0), (0, hpad)))
    fcw = jnp.pad(ffn_conv[l], ((0, 0), (0, hpad)))
    wdn = jnp.pad(w_ffn_down[l], ((0, hpad), (0, 0))).astype(BF16)
    hp = jnp.concatenate([dn_a_log[l].T, dn_dt_bias[l].T], axis=1)[:, None, :]
    tables_l = _rope_tables(seq // GRID_W)
    tables_c = (jnp.zeros((lc // GRID_W, ROPE_PAD), F32), jnp.zeros((lc // GRID_W, ROPE_PAD), F32),
                jnp.ones((GRID_W, ROPE_PAD), F32), jnp.zeros((GRID_W, ROPE_PAD), F32))

    rows = jnp.concatenate([c, c_ctx[None, :], jnp.zeros((8 - batch - 1, d), F32)], axis=0)
    mod = _ada(rows, w_ada[l], b_ada[l][None, :]).reshape(8, 6, d)

    xf = x.reshape(batch * seq, d)
    cf = ctx.reshape(batch * lc, d)

    def stream(tokens, n, tm_a, tm_b, row_of):
        nm = norm_mix[l][None, :]
        a, xn = _norm_mod_matmul(tokens, nm, mod, row_of(tm_a), w_a, F32, tm_a, 0, 1, "in_proj_small")
        p = _matmul(xn, w_b, BF16, tm_b, 1024, "in_proj_wide")
        ab = a[:, A_DA:A_DA + 4 * h].reshape(batch * n, 4, h).transpose(2, 0, 1)
        return a, p, ab

    tm_a = _pick_tile(seq, 512)
    tm_b = _pick_tile(seq, 1024)
    lat_row = lambda tm: (lambda i: i // (seq // tm))
    a_l, p_l, ab_l = stream(xf, seq, tm_a, tm_b, lat_row)
    tc = _pick_tile(lc, 256)
    ctx_row = lambda tm: (lambda i: batch)
    a_c, p_c, ab_c = stream(cf, lc, tc, tc, ctx_row)

    qan, kvan = q_a_norm[l][None, :], kv_a_norm[l][None, :]
    tmp = _pick_tile(seq, 512)
    q_l, k_l, v_l = _mla_proj(a_l, tables_l, qan, kvan, qn, kn, wq, wkv, tmp, seq)
    _, k_c, v_c = _mla_proj(a_c, tables_c, qan, kvan, qn, kn, wq, wkv, tc, lc)
    tq = _pick_tile(seq, 512)
    tk = _pick_tile(seq, 512)
    qt = q_l.reshape(batch, seq, MLA_HEADS, MLA_QK_PAD).transpose(0, 2, 3, 1)

    def with_ones_row(vt):
        lead, n = vt.shape[:-2], vt.shape[-1]
        return jnp.concatenate([vt, jnp.ones(lead + (1, n), BF16), jnp.zeros(lead + (HALO16 - 1, n), BF16)],
                               axis=-2)

    vt = with_ones_row(v_l.reshape(batch, seq // tk, tk, MLA_HEADS, MLA_V).transpose(0, 3, 1, 4, 2))
    vct = with_ones_row(v_c.reshape(batch, lc, MLA_HEADS, MLA_V).transpose(0, 2, 3, 1))
    y_mla = _attention(qt, k_l.reshape(batch, seq, -1), vt, k_c.reshape(batch, lc, -1), vct, tq)
    y_mla = y_mla.reshape(batch * seq, MLA_HEADS * MLA_V)

    conv_w = dn_conv[l]
    prep_c = _dn_prep(p_c, ab_c, hp, conv_w, batch, lc, _pick_tile(lc, 256))
    s_zero = jnp.zeros((2, batch, h, DN_DK, DN_DV), F32)
    _, _, s_ctx = _dn_scan(s_zero, *prep_c, _pick_tile(lc, 256) // DN_CHUNK)
    prep_l = _dn_prep(p_l, ab_l, hp, conv_w, batch, seq, _pick_tile(seq, 1024))
    o_f, o_b, _ = _dn_scan(s_ctx, *prep_l, _pick_tile(seq, 256) // DN_CHUNK)

    tmm = _pick_tile(seq, 256)
    x1 = _merge(xf, mod, lat_row(tmm), y_mla, o_f, o_b, p_l, dn_o_norm[l][None, :], w_o_mla[l].astype(BF16),
                w_o_dn[l].astype(BF16), w_out[l].astype(BF16), tmm, 2)

    tmf = _pick_tile(seq, 512)
    out = _ffn(x1, norm_ffn[l][None, :], mod, lat_row(tmf), wg, wv, fcw, wdn, seq, tmf, 3, 4, 5)
    return out.reshape(batch, seq, d)
```

```python
import functools
import math

import jax
import jax.numpy as jnp
import numpy as np
from jax import lax
from jax.experimental import pallas as pl
from jax.experimental.pallas import tpu as pltpu

F32 = jnp.float32
BF16 = jnp.bfloat16
HIGHEST = lax.Precision.HIGHEST

EPS = 1e-6
GRID_W = 64
ROPE_BASE = 10000.0

MLA_HEADS = 8
MLA_Q_RANK = 512
MLA_KV_RANK = 512
MLA_NOPE = 128
MLA_ROPE = 64
MLA_V = 128
MLA_QK = MLA_NOPE + MLA_ROPE
MLA_QK_PAD = 256

DN_HEADS = 8
DN_DK = 128
DN_DV = 128
DN_CHUNK = 64

LANE = 128
SUBLANE = 8
ROPE_PAD = 128
VMEM_LIMIT = 56 * 1024 * 1024

A_QA, A_KVA, A_KR, A_DA, A_DB, A_WIDTH = 0, 512, 1024, 1152, 1168, 1280
B_DQ, B_DK, B_DV, B_DZ, B_GA, B_GB, B_WIDTH = 0, 1024, 2048, 3072, 4096, 6144, 8192

FFN_TILE = 512
HALO16 = 16


def _params(sem, vmem=VMEM_LIMIT):
    return pltpu.CompilerParams(dimension_semantics=sem, vmem_limit_bytes=vmem)


def _nt_dot(a, b):
    return lax.dot_general(a, b, (((1,), (1,)), ((), ())), preferred_element_type=F32)


def _silu(x):
    return x * jax.nn.sigmoid(x)


def _ada_kernel(c_ref, w_ref, b_ref, o_ref):
    a = _silu(c_ref[...]).astype(BF16)
    o_ref[...] = jnp.dot(a, w_ref[...].astype(BF16), preferred_element_type=F32) + b_ref[...]


def _ada(cs, w, b):
    m, d = cs.shape
    n = w.shape[1]
    tn = 512
    return pl.pallas_call(
        _ada_kernel,
        grid=(n // tn,),
        in_specs=[pl.BlockSpec((m, d), lambda j: (0, 0)),
                  pl.BlockSpec((d, tn), lambda j: (0, j)),
                  pl.BlockSpec((1, tn), lambda j: (0, j))],
        out_specs=pl.BlockSpec((m, tn), lambda j: (0, j)),
        out_shape=jax.ShapeDtypeStruct((m, n), F32),
        compiler_params=_params(("parallel",)),
        name="ada",
    )(cs, w, b)


def _norm_mod(x, nw, shift, scale):
    ms = jnp.mean(x * x, axis=-1, keepdims=True)
    y = x * lax.rsqrt(ms + EPS) * nw
    return y * (1.0 + scale) + shift


def _nmm_kernel(x_ref, nw_ref, mod_ref, w_ref, o_ref, xn_ref, *, k_shift, k_scale):
    y = _norm_mod(x_ref[...], nw_ref[...], mod_ref[0, k_shift:k_shift + 1, :], mod_ref[0, k_scale:k_scale + 1, :])
    xn = y.astype(BF16)
    xn_ref[...] = xn
    o_ref[...] = jnp.dot(xn, w_ref[...], preferred_element_type=F32).astype(o_ref.dtype)


def _norm_mod_matmul(x, nw, mod, row_of_tile, w, out_dtype, tm, k_shift, k_scale, name):
    m, d = x.shape
    n = w.shape[1]
    kern = functools.partial(_nmm_kernel, k_shift=k_shift, k_scale=k_scale)
    return pl.pallas_call(
        kern,
        grid=(m // tm,),
        in_specs=[pl.BlockSpec((tm, d), lambda i: (i, 0)),
                  pl.BlockSpec((1, d), lambda i: (0, 0)),
                  pl.BlockSpec((1, 6, d), lambda i: (row_of_tile(i), 0, 0)),
                  pl.BlockSpec((d, n), lambda i: (0, 0))],
        out_specs=[pl.BlockSpec((tm, n), lambda i: (i, 0)), pl.BlockSpec((tm, d), lambda i: (i, 0))],
        out_shape=[jax.ShapeDtypeStruct((m, n), out_dtype), jax.ShapeDtypeStruct((m, d), BF16)],
        compiler_params=_params(("parallel",)),
        name=name,
    )(x, nw, mod, w)


def _mm_kernel(x_ref, w_ref, o_ref):
    o_ref[...] = jnp.dot(x_ref[...], w_ref[...], preferred_element_type=F32).astype(o_ref.dtype)


def _matmul(x, w, out_dtype, tm, tn, name):
    m, d = x.shape
    n = w.shape[1]
    return pl.pallas_call(
        _mm_kernel,
        grid=(m // tm, n // tn),
        in_specs=[pl.BlockSpec((tm, d), lambda i, j: (i, 0)),
                  pl.BlockSpec((d, tn), lambda i, j: (0, j))],
        out_specs=pl.BlockSpec((tm, tn), lambda i, j: (i, j)),
        out_shape=jax.ShapeDtypeStruct((m, n), out_dtype),
        compiler_params=_params(("parallel", "parallel")),
        name=name,
    )(x, w)


def _mla_proj_kernel(a_ref, cr_ref, sr_ref, cc_ref, sc_ref, qan_ref, kvan_ref, qn_ref, kn_ref, wq_ref, wkv_ref,
                     q_ref, k_ref, v_ref):
    tm = a_ref.shape[0]
    g = tm // GRID_W

    def table(row_ref, col_ref):
        r = jnp.broadcast_to(row_ref[...][:, None, :], (g, GRID_W, ROPE_PAD)).reshape(tm, ROPE_PAD)
        c = jnp.broadcast_to(col_ref[...][None], (g, GRID_W, ROPE_PAD)).reshape(tm, ROPE_PAD)
        return r + c

    cos = table(cr_ref, cc_ref)
    sin = table(sr_ref, sc_ref)

    def rope(r):
        return r * cos + pltpu.roll(r, 64, axis=1) * sin

    def rms_rows(t, w):
        return t * lax.rsqrt(jnp.mean(t * t, axis=-1, keepdims=True) + EPS) * w

    qa = rms_rows(a_ref[:, A_QA:A_QA + MLA_Q_RANK], qan_ref[...]).astype(BF16)
    q = jnp.dot(qa, wq_ref[...], preferred_element_type=F32)
    kva = rms_rows(a_ref[:, A_KVA:A_KVA + MLA_KV_RANK], kvan_ref[...]).astype(BF16)
    kv = jnp.dot(kva, wkv_ref[...], preferred_element_type=F32)
    kr = a_ref[:, A_KR:A_KR + ROPE_PAD]
    kr_ss = jnp.sum(kr * kr, axis=-1, keepdims=True)
    qn = qn_ref[...]
    kn = kn_ref[...]
    scale = MLA_QK ** -0.5 * math.log2(math.e)
    for h in range(MLA_HEADS):
        qh = q[:, h * MLA_QK_PAD:(h + 1) * MLA_QK_PAD]
        inv = lax.rsqrt(jnp.sum(qh * qh, axis=-1, keepdims=True) * (1.0 / MLA_QK) + EPS) * scale
        qh = qh * inv * qn
        q_ref[:, h * MLA_QK_PAD:h * MLA_QK_PAD + MLA_NOPE] = qh[:, :MLA_NOPE].astype(BF16)
        q_ref[:, h * MLA_QK_PAD + MLA_NOPE:(h + 1) * MLA_QK_PAD] = rope(qh[:, MLA_NOPE:]).astype(BF16)

        kh = kv[:, h * MLA_NOPE:(h + 1) * MLA_NOPE]
        inv = lax.rsqrt((jnp.sum(kh * kh, axis=-1, keepdims=True) + kr_ss) * (1.0 / MLA_QK) + EPS)
        k_ref[:, h * MLA_QK_PAD:h * MLA_QK_PAD + MLA_NOPE] = (kh * inv * kn[:, :MLA_NOPE]).astype(BF16)
        k_ref[:, h * MLA_QK_PAD + MLA_NOPE:(h + 1) * MLA_QK_PAD] = rope(
            kr * inv * kn[:, MLA_NOPE:]).astype(BF16)
    v_ref[...] = kv[:, MLA_HEADS * MLA_NOPE:].astype(BF16)


def _mla_proj(a, tables, qan, kvan, qn, kn, wq, wkv, tm, seq):
    m = a.shape[0]
    nseq = seq // tm
    row_cos, row_sin, col_cos, col_sin = tables
    full = lambda arr: pl.BlockSpec(arr.shape, lambda i: (0, 0))
    row_spec = pl.BlockSpec((tm // GRID_W, ROPE_PAD), lambda i: (i % nseq, 0))
    return pl.pallas_call(
        _mla_proj_kernel,
        grid=(m // tm,),
        in_specs=[pl.BlockSpec((tm, A_WIDTH), lambda i: (i, 0)),
                  row_spec, row_spec, full(col_cos), full(col_sin),
                  full(qan), full(kvan), full(qn), full(kn), full(wq), full(wkv)],
        out_specs=[pl.BlockSpec((tm, MLA_HEADS * MLA_QK_PAD), lambda i: (i, 0)),
                   pl.BlockSpec((tm, MLA_HEADS * MLA_QK_PAD), lambda i: (i, 0)),
                   pl.BlockSpec((tm, MLA_HEADS * MLA_V), lambda i: (i, 0))],
        out_shape=[jax.ShapeDtypeStruct((m, MLA_HEADS * MLA_QK_PAD), BF16),
                   jax.ShapeDtypeStruct((m, MLA_HEADS * MLA_QK_PAD), BF16),
                   jax.ShapeDtypeStruct((m, MLA_HEADS * MLA_V), BF16)],
        compiler_params=_params(("parallel",)),
        name="mla_proj",
    )(a, row_cos, row_sin, col_cos, col_sin, qan, kvan, qn, kn, wq, wkv)


ATTN_UNROLL = 8
ATTN_ROWS = 64


def _attn_kernel(qt_ref, kc_ref, vct_ref, k_ref, vt_ref, o_ref, s0_ref, s1_ref, acc_ref, *, nk, unroll):
    qt = qt_ref[0, 0]
    tq = qt.shape[1]
    tk = s0_ref.shape[0]
    rows = ATTN_ROWS
    s_refs = (s0_ref, s1_ref)

    def scores(j):
        return jnp.dot(k_ref[0, pl.ds(pl.multiple_of(j * tk, tk), tk), :], qt, preferred_element_type=F32)

    def probs(s, m):
        return jnp.exp2((s - m).astype(BF16))

    def store_scores(slot, s):
        s_refs[slot][...] = s
        return jnp.max(s.reshape(tk // SUBLANE, SUBLANE, tq), axis=0)

    s = jnp.dot(kc_ref[0], qt, preferred_element_type=F32)
    m8_first = store_scores(0, scores(0))
    m = jnp.max(s, axis=0, keepdims=True)
    acc_ref[...] = jnp.dot(vct_ref[0, 0], probs(s, m), preferred_element_type=F32)

    def tile(j, slot, carry):
        m, m8 = carry
        m8_next = store_scores(1 - slot, scores(jnp.minimum(j + 1, nk - 1)))
        m_new = jnp.maximum(m, jnp.max(m8, axis=0, keepdims=True))
        p = jnp.concatenate([probs(s_refs[slot][r:r + rows, :], m_new) for r in range(0, tk, rows)], axis=0)
        acc_ref[...] = jnp.exp2(m - m_new) * acc_ref[...] + jnp.dot(vt_ref[0, 0, j], p, preferred_element_type=F32)
        return m_new, m8_next

    def body(i, carry):
        for u in range(unroll):
            carry = tile(unroll * i + u, u % 2, carry)
        return carry

    lax.fori_loop(0, nk // unroll, body, (m, m8_first))
    acc = acc_ref[...]
    o_ref[0] = (acc[:MLA_V] * (1.0 / acc[MLA_V:MLA_V + 1])).T.astype(o_ref.dtype)


def _attention(qt, k, vt, kc, vct, tq):
    b, t, _ = k.shape
    lc = kc.shape[1]
    nk, vrows, tk = vt.shape[2], vt.shape[3], vt.shape[4]
    unroll = ATTN_UNROLL if nk % ATTN_UNROLL == 0 else 2
    assert nk % unroll == 0
    kern = functools.partial(_attn_kernel, nk=nk, unroll=unroll)
    return pl.pallas_call(
        kern,
        grid=(b, MLA_HEADS, t // tq),
        in_specs=[pl.BlockSpec((1, 1, MLA_QK_PAD, tq), lambda bi, h, i: (bi, h, 0, i)),
                  pl.BlockSpec((1, lc, MLA_QK_PAD), lambda bi, h, i: (bi, 0, h)),
                  pl.BlockSpec((1, 1, vrows, lc), lambda bi, h, i: (bi, h, 0, 0)),
                  pl.BlockSpec((1, t, MLA_QK_PAD), lambda bi, h, i: (bi, 0, h)),
                  pl.BlockSpec((1, 1, nk, vrows, tk), lambda bi, h, i: (bi, h, 0, 0, 0))],
        out_specs=pl.BlockSpec((1, tq, MLA_V), lambda bi, h, i: (bi, i, h)),
        out_shape=jax.ShapeDtypeStruct((b, t, MLA_HEADS * MLA_V), BF16),
        scratch_shapes=[pltpu.VMEM((tk, tq), F32), pltpu.VMEM((tk, tq), F32), pltpu.VMEM((vrows, tq), F32)],
        compiler_params=_params(("parallel", "parallel", "arbitrary")),
        name="attention",
    )(qt, kc, vct, k, vt)


def _split2(x):
    hi = x.astype(BF16)
    return hi, (x - hi.astype(F32)).astype(BF16)


def _bmm(a, b):
    return lax.dot_general(a, b, (((2,), (1,)), ((0,), (0,))), preferred_element_type=F32)


def _bmm_nt(a, b):
    return lax.dot_general(a, b, (((2,), (2,)), ((0,), (0,))), preferred_element_type=F32)


def _bmm_hp(a, b):
    ah, al = _split2(a)
    bh, bl = _split2(b)
    return _bmm(ah, bh) + _bmm(al, bh) + _bmm(ah, bl)


def _tri_inverse(m, eye):
    assert m.shape[-1] == 64
    n = -m
    x = eye + n
    for _ in range(3):
        nb = n.astype(BF16)
        n = _bmm(nb, nb)
        x = x + _bmm(x.astype(BF16), n.astype(BF16))
    xb = x.astype(BF16)
    r = eye - x - _bmm(m.astype(BF16), xb)
    x = x + _bmm(xb, r.astype(BF16))
    r = eye - x - _bmm_hp(m, x)
    return x + _bmm(x.astype(BF16), r.astype(BF16))


def _dn_prep_kernel(qm_ref, qp_ref, qx_ref, km_ref, kp_ref, kx_ref, vm_ref, vp_ref, vx_ref,
                    cwq_ref, cwk_ref, cwv_ref, ab_ref, hp_ref,
                    u_ref, w_ref, ke_ref, qg_ref, a_ref, ge_ref, ext_ref, *, tm, tiles_per_seq):
    i = pl.program_id(0)
    first = (i % tiles_per_seq) == 0
    last = (i % tiles_per_seq) == tiles_per_seq - 1
    c = DN_CHUNK

    def conv_silu(main_ref, prev_ref, next_ref, cw_ref):
        ext_ref[0:HALO16, :] = jnp.where(first, 0.0, prev_ref[...].astype(F32))
        ext_ref[HALO16:HALO16 + tm, :] = main_ref[...].astype(F32)
        ext_ref[HALO16 + tm:, :] = jnp.where(last, 0.0, next_ref[...].astype(F32))
        y = (cw_ref[0:1, :] * ext_ref[HALO16 - 1:HALO16 - 1 + tm, :]
             + cw_ref[1:2, :] * ext_ref[HALO16:HALO16 + tm, :]
             + cw_ref[2:3, :] * ext_ref[HALO16 + 1:HALO16 + 1 + tm, :])
        return _silu(y)

    def l2n(t):
        return t * lax.rsqrt(jnp.sum(t * t, axis=-1, keepdims=True) + EPS)

    q = l2n(conv_silu(qm_ref, qp_ref, qx_ref, cwq_ref)) * (DN_DK ** -0.5)
    k = l2n(conv_silu(km_ref, kp_ref, kx_ref, cwk_ref))
    v = conv_silu(vm_ref, vp_ref, vx_ref, cwv_ref)

    hp = hp_ref[0]
    ab = ab_ref[0]
    z = ab[:, 0:2] + hp[:, 2:4]
    softplus = jnp.maximum(z, 0.0) + jnp.log(1.0 + jnp.exp(-jnp.abs(z)))
    g_all = -jnp.exp(hp[:, 0:2]) * softplus
    beta_all = jax.nn.sigmoid(ab[:, 2:4])

    nc = tm // c

    def per_chunk(t):
        return t.reshape(nc, c, t.shape[-1])

    def both(t):
        return jnp.concatenate([t, t], axis=0)

    def by_dir(t2):
        return jnp.concatenate([per_chunk(jnp.broadcast_to(t2[:, d:d + 1], (tm, LANE))) for d in range(2)],
                               axis=0)

    shape = (2 * nc, c, c)
    fwd = lax.broadcasted_iota(jnp.int32, shape, 0) < nc
    rows = lax.broadcasted_iota(jnp.int32, shape, 1)
    cols = lax.broadcasted_iota(jnp.int32, shape, 2)
    eye_mask = rows == cols
    ahead = jnp.where(fwd, cols - rows, rows - cols)
    incl = ahead <= 0
    strict = ahead < 0
    eye = jnp.where(eye_mask, 1.0, 0.0)
    tri = jnp.where(incl, 1.0, 0.0).astype(BF16)

    g_hi, g_lo = _split2(by_dir(g_all))
    gc = _bmm(tri, g_hi) + _bmm(tri, g_lo)
    gc_sq = gc[:, :, :c]
    gc_row = jnp.sum(jnp.where(eye_mask, gc_sq, 0.0), axis=1, keepdims=True)
    decay = jnp.where(incl, jnp.exp(jnp.where(incl, gc_sq - gc_row, 0.0)), 0.0)
    gtot = jnp.concatenate([gc[:nc, c - 1:c, :], gc[nc:, 0:1, :]], axis=0)
    eg = jnp.exp(gc)

    beta = by_dir(beta_all)
    k3 = per_chunk(k)
    q3 = per_chunk(q)
    k3b = k3.astype(BF16)
    kk = both(_bmm_nt(k3b, k3b))
    qk = both(_bmm_nt(q3.astype(BF16), k3b))
    k2 = both(k3)
    kb = k2 * beta
    m = jnp.where(strict, beta[:, :, :c] * kk * decay, 0.0)
    tinv = _tri_inverse(m, eye)
    t_hi, t_lo = _split2(tinv)
    rhs = jnp.concatenate([both(per_chunk(v)) * beta, kb * eg], axis=2).astype(BF16)
    sol = _bmm(t_hi, rhs) + _bmm(t_lo, rhs)
    ke = k2 * jnp.exp(gtot - gc)
    qg = both(q3) * eg
    att = qk * decay
    gend = jnp.exp(gtot)
    for d in range(2):
        sl = slice(d * nc, (d + 1) * nc)
        u_ref[d, 0, 0] = sol[sl, :, :DN_DV].reshape(tm, DN_DV)
        w_ref[d, 0, 0] = sol[sl, :, DN_DV:].reshape(tm, DN_DK).astype(BF16)
        ke_ref[d, 0, 0] = ke[sl].reshape(tm, DN_DK).astype(BF16)
        qg_ref[d, 0, 0] = qg[sl].reshape(tm, DN_DK).astype(BF16)
        a_ref[d, 0, 0] = att[sl].reshape(tm, c).astype(BF16)
        ge_ref[d, 0, 0] = gend[sl]


def _dn_prep(p, ab, hp, conv_w, batch, seq, tm):
    h = DN_HEADS
    nt = seq // tm
    nb16 = tm // HALO16
    total16 = batch * seq // HALO16
    kern = functools.partial(_dn_prep_kernel, tm=tm, tiles_per_seq=nt)

    def triple(col0):
        return [pl.BlockSpec((tm, LANE), lambda i, hh: (i, col0 + hh)),
                pl.BlockSpec((HALO16, LANE), lambda i, hh: (jnp.maximum(i * nb16 - 1, 0), col0 + hh)),
                pl.BlockSpec((HALO16, LANE), lambda i, hh: (jnp.minimum((i + 1) * nb16, total16 - 1), col0 + hh))]

    cq, ck, cv = B_DQ // LANE, B_DK // LANE, B_DV // LANE
    row = lambda shape: pl.BlockSpec(shape, lambda i, hh: (0, i // nt, hh, i % nt, 0))
    seq_shape = lambda width, dt: jax.ShapeDtypeStruct((2, batch, h, seq, width), dt)
    return pl.pallas_call(
        kern,
        grid=(batch * nt, h),
        in_specs=triple(cq) + triple(ck) + triple(cv) + [
            pl.BlockSpec((3, LANE), lambda i, hh: (0, hh)),
            pl.BlockSpec((3, LANE), lambda i, hh: (0, h + hh)),
            pl.BlockSpec((3, LANE), lambda i, hh: (0, 2 * h + hh)),
            pl.BlockSpec((1, tm, 4), lambda i, hh: (hh, i, 0)),
            pl.BlockSpec((1, 1, 4), lambda i, hh: (hh, 0, 0))],
        out_specs=[row((2, 1, 1, tm, DN_DV)), row((2, 1, 1, tm, DN_DK)), row((2, 1, 1, tm, DN_DK)),
                   row((2, 1, 1, tm, DN_DK)), row((2, 1, 1, tm, DN_CHUNK)),
                   pl.BlockSpec((2, 1, 1, tm // DN_CHUNK, 1, LANE), lambda i, hh: (0, i // nt, hh, i % nt, 0, 0))],
        out_shape=[seq_shape(DN_DV, F32), seq_shape(DN_DK, BF16), seq_shape(DN_DK, BF16),
                   seq_shape(DN_DK, BF16), seq_shape(DN_CHUNK, BF16),
                   jax.ShapeDtypeStruct((2, batch, h, seq // DN_CHUNK, 1, LANE), F32)],
        scratch_shapes=[pltpu.VMEM((tm + 2 * HALO16, LANE), F32)],
        compiler_params=_params(("parallel", "parallel")),
        name="dn_prep",
    )(p, p, p, p, p, p, p, p, p, conv_w, conv_w, conv_w, ab, hp)


def _dn_scan_kernel(s0_ref, uf_ref, wf_ref, kf_ref, qf_ref, af_ref, gf_ref,
                    ub_ref, wb_ref, kb_ref, qb_ref, ab_ref, gb_ref,
                    of_ref, ob_ref, sfin_ref, s_ref, *, cs):
    n = pl.program_id(1)
    c = DN_CHUNK
    h = DN_HEADS

    @pl.when(n == 0)
    def _():
        s_ref[0:h] = s0_ref[0, 0]
        s_ref[h:] = s0_ref[1, 0]

    for ci in range(cs):
        rf = ci * c
        rb = (cs - 1 - ci) * c

        def pair(f_ref, b_ref):
            return jnp.concatenate([f_ref[0, 0, :, rf:rf + c, :], b_ref[0, 0, :, rb:rb + c, :]], axis=0)

        s = s_ref[...]
        s16 = s.astype(BF16)
        ws_qs = _bmm(jnp.concatenate([pair(wf_ref, wb_ref), pair(qf_ref, qb_ref)], axis=1), s16)
        v_new = pair(uf_ref, ub_ref) - ws_qs[:, :c]
        v16 = v_new.astype(BF16)
        o = ws_qs[:, c:] + _bmm(pair(af_ref, ab_ref), v16)
        of_ref[0, :, rf:rf + c, :] = o[:h].astype(of_ref.dtype)
        ob_ref[0, :, rb:rb + c, :] = o[h:].astype(ob_ref.dtype)
        g = jnp.concatenate([gf_ref[0, 0, :, ci], gb_ref[0, 0, :, cs - 1 - ci]], axis=0)
        s_ref[...] = s * g + lax.dot_general(pair(kf_ref, kb_ref), v16, (((1,), (1,)), ((0,), (0,))),
                                             preferred_element_type=F32)

    @pl.when(n == pl.num_programs(1) - 1)
    def _():
        sfin_ref[0, 0] = s_ref[0:h]
        sfin_ref[1, 0] = s_ref[h:]


def _dn_scan(s0, u, w, ke, qg, a, ge, cs):
    _, batch, h, seq, _ = u.shape
    ts = cs * DN_CHUNK
    ns = seq // ts
    kern = functools.partial(_dn_scan_kernel, cs=cs)

    def specs(d):
        idx = (lambda n: n) if d == 0 else (lambda n: ns - 1 - n)
        seqs = [pl.BlockSpec((1, 1, h, ts, width), lambda b, n: (d, b, 0, idx(n), 0))
                for width in (DN_DV, DN_DK, DN_DK, DN_DK, DN_CHUNK)]
        return seqs + [pl.BlockSpec((1, 1, h, cs, 1, LANE), lambda b, n: (d, b, 0, idx(n), 0, 0))]

    o_spec = lambda d: pl.BlockSpec((1, h, ts, DN_DV), lambda b, n: (b, 0, n if d == 0 else ns - 1 - n, 0))
    state = pl.BlockSpec((2, 1, h, DN_DK, DN_DV), lambda b, n: (0, b, 0, 0, 0))
    return pl.pallas_call(
        kern,
        grid=(batch, ns),
        in_specs=[state] + specs(0) + specs(1),
        out_specs=[o_spec(0), o_spec(1), state],
        out_shape=[jax.ShapeDtypeStruct((batch, h, seq, DN_DV), BF16),
                   jax.ShapeDtypeStruct((batch, h, seq, DN_DV), BF16),
                   jax.ShapeDtypeStruct((2, batch, h, DN_DK, DN_DV), F32)],
        scratch_shapes=[pltpu.VMEM((2 * h, DN_DK, DN_DV), F32)],
        compiler_params=_params(("parallel", "arbitrary")),
        name="dn_scan",
    )(s0, u, w, ke, qg, a, ge, u, w, ke, qg, a, ge)


def _merge_kernel(x_ref, mod_ref, ym_ref, of_ref, ob_ref, z_ref, nw_ref, ga_ref, gb_ref, wm_ref, wd_ref, wo_ref,
                  o_ref, *, k_gate):
    nw = nw_ref[...]
    heads = []
    for h in range(DN_HEADS):
        o = of_ref[0, h].astype(F32) + ob_ref[0, h].astype(F32)
        y = o * lax.rsqrt(jnp.mean(o * o, axis=-1, keepdims=True) + EPS) * nw
        heads.append((y * _silu(z_ref[:, h * DN_DV:(h + 1) * DN_DV].astype(F32))).astype(BF16))
    yd = jnp.concatenate(heads, axis=1)
    pm = jnp.dot(ym_ref[...], wm_ref[...], preferred_element_type=F32)
    pd = jnp.dot(yd, wd_ref[...], preferred_element_type=F32)
    merged = (jax.nn.sigmoid(ga_ref[...].astype(F32)) * pm
              + jax.nn.sigmoid(gb_ref[...].astype(F32)) * pd).astype(BF16)
    y = jnp.dot(merged, wo_ref[...], preferred_element_type=F32)
    o_ref[...] = x_ref[...] + mod_ref[0, k_gate:k_gate + 1, :] * y


def _merge(x, mod, row_of_tile, ym, o_f, o_b, p, nw, wm, wd, wo, tm, k_gate):
    m, d = x.shape
    _, h, seq, _ = o_f.shape
    nt = seq // tm
    zw = h * DN_DV
    kern = functools.partial(_merge_kernel, k_gate=k_gate)
    const = lambda arr: pl.BlockSpec(arr.shape, lambda i: (0, 0), pipeline_mode=pl.Buffered(1))
    scan_out = pl.BlockSpec((1, h, tm, DN_DV), lambda i: (i // nt, 0, i % nt, 0))
    return pl.pallas_call(
        kern,
        grid=(m // tm,),
        in_specs=[pl.BlockSpec((tm, d), lambda i: (i, 0)),
                  pl.BlockSpec((1, 6, d), lambda i: (row_of_tile(i), 0, 0)),
                  pl.BlockSpec((tm, ym.shape[1]), lambda i: (i, 0)),
                  scan_out, scan_out,
                  pl.BlockSpec((tm, zw), lambda i: (i, B_DZ // zw)),
                  pl.BlockSpec((1, DN_DV), lambda i: (0, 0)),
                  pl.BlockSpec((tm, d), lambda i: (i, B_GA // d)),
                  pl.BlockSpec((tm, d), lambda i: (i, B_GB // d)),
                  const(wm), const(wd), const(wo)],
        out_specs=pl.BlockSpec((tm, d), lambda i: (i, 0)),
        out_shape=jax.ShapeDtypeStruct((m, d), F32),
        compiler_params=_params(("parallel",)),
        name="merge",
    )(x, mod, ym, o_f, o_b, p, nw, p, p, wm, wd, wo)


def _ffn_kernel(x_ref, xp_ref, xx_ref, nw_ref, mod_ref, wg_ref, wv_ref, cw_ref, wdn_ref, o_ref, xn_ref, ge_ref,
                *, tm, tiles_per_seq, k_shift, k_scale, k_gate):
    i = pl.program_id(0)
    j = pl.program_id(1)
    th = FFN_TILE

    @pl.when(j == 0)
    def _():
        first = (i % tiles_per_seq) == 0
        last = (i % tiles_per_seq) == tiles_per_seq - 1
        nw = nw_ref[...]
        sh = mod_ref[0, k_shift:k_shift + 1, :]
        sc = mod_ref[0, k_scale:k_scale + 1, :]
        xn_ref[0:HALO16, :] = jnp.where(first, 0.0, _norm_mod(xp_ref[...], nw, sh, sc)).astype(BF16)
        xn_ref[HALO16:HALO16 + tm, :] = _norm_mod(x_ref[...], nw, sh, sc).astype(BF16)
        xn_ref[HALO16 + tm:, :] = jnp.where(last, 0.0, _norm_mod(xx_ref[...], nw, sh, sc)).astype(BF16)
        o_ref[...] = jnp.zeros_like(o_ref)

    ge_ref[...] = jnp.dot(xn_ref[...], wg_ref[...], preferred_element_type=F32)
    val = jnp.dot(xn_ref[HALO16:HALO16 + tm, :], wv_ref[...], preferred_element_type=F32)
    conv = (cw_ref[0:1, :] * ge_ref[HALO16 - 1:HALO16 - 1 + tm, :]
            + cw_ref[1:2, :] * ge_ref[HALO16:HALO16 + tm, :]
            + cw_ref[2:3, :] * ge_ref[HALO16 + 1:HALO16 + 1 + tm, :])
    hid = (_silu(conv) * val).astype(BF16)
    o_ref[...] += jnp.dot(hid, wdn_ref[...], preferred_element_type=F32)

    @pl.when(j == pl.num_programs(1) - 1)
    def _():
        o_ref[...] = x_ref[...] + mod_ref[0, k_gate:k_gate + 1, :] * o_ref[...]


def _ffn(x, nw, mod, row_of_tile, wg, wv, cw, wdn, seq, tm, k_shift, k_scale, k_gate):
    m, d = x.shape
    th = FFN_TILE
    nj = wdn.shape[0] // th
    nt = seq // tm
    nb16 = tm // HALO16
    total16 = m // HALO16
    kern = functools.partial(_ffn_kernel, tm=tm, tiles_per_seq=nt, k_shift=k_shift, k_scale=k_scale,
                             k_gate=k_gate)
    return pl.pallas_call(
        kern,
        grid=(m // tm, nj),
        in_specs=[pl.BlockSpec((tm, d), lambda i, j: (i, 0)),
                  pl.BlockSpec((HALO16, d), lambda i, j: (jnp.maximum(i * nb16 - 1, 0), 0)),
                  pl.BlockSpec((HALO16, d), lambda i, j: (jnp.minimum((i + 1) * nb16, total16 - 1), 0)),
                  pl.BlockSpec((1, d), lambda i, j: (0, 0)),
                  pl.BlockSpec((1, 6, d), lambda i, j: (row_of_tile(i), 0, 0)),
                  pl.BlockSpec((d, th), lambda i, j: (0, j)),
                  pl.BlockSpec((d, th), lambda i, j: (0, j)),
                  pl.BlockSpec((3, th), lambda i, j: (0, j)),
                  pl.BlockSpec((th, d), lambda i, j: (j, 0))],
        out_specs=pl.BlockSpec((tm, d), lambda i, j: (i, 0)),
        out_shape=jax.ShapeDtypeStruct((m, d), F32),
        scratch_shapes=[pltpu.VMEM((tm + 2 * HALO16, d), BF16), pltpu.VMEM((tm + 2 * HALO16, th), F32)],
        compiler_params=_params(("parallel", "arbitrary")),
        name="ffn",
    )(x, x, x, nw, mod, wg, wv, cw, wdn)


def _rope_pad_cols(r):
    n = MLA_ROPE // 4
    z = jnp.zeros(r.shape[:-1] + (2 * n,), r.dtype)
    return jnp.concatenate([r[..., 0:n], r[..., 2 * n:3 * n], z, r[..., n:2 * n], r[..., 3 * n:4 * n], z], axis=-1)


def _qk_pad_cols(w):
    lead = w.shape[:-1]
    w = w.reshape(lead + (MLA_HEADS, MLA_QK))
    out = jnp.concatenate([w[..., :MLA_NOPE], _rope_pad_cols(w[..., MLA_NOPE:])], axis=-1)
    return out.reshape(lead + (MLA_HEADS * MLA_QK_PAD,))


def _rope_tables(rows):
    n = MLA_ROPE // 4
    inv = ROPE_BASE ** (-jnp.arange(n, dtype=F32) / n)
    ang_r = jnp.arange(rows, dtype=F32)[:, None] * inv
    ang_c = jnp.arange(GRID_W, dtype=F32)[:, None] * inv

    def lanes(a_row, a_col, fill, count):
        return jnp.concatenate([a_row, a_col, jnp.full((count, 2 * n), fill, F32)] * 2, axis=-1)

    zr, zc = jnp.zeros((rows, n), F32), jnp.zeros((GRID_W, n), F32)
    row_cos = lanes(jnp.cos(ang_r), zr, 0.0, rows)
    row_sin = jnp.concatenate([-jnp.sin(ang_r), zr, jnp.zeros((rows, 2 * n), F32),
                               jnp.sin(ang_r), zr, jnp.zeros((rows, 2 * n), F32)], axis=-1)
    col_cos = lanes(zc, jnp.cos(ang_c), 1.0, GRID_W)
    col_sin = jnp.concatenate([zc, -jnp.sin(ang_c), jnp.zeros((GRID_W, 2 * n), F32),
                               zc, jnp.sin(ang_c), jnp.zeros((GRID_W, 2 * n), F32)], axis=-1)
    return row_cos, row_sin, col_cos, col_sin


def _pick_tile(n, pref):
    t = min(pref, n)
    while n % t:
        t //= 2
    return t


def kernel(x, c, ctx, c_ctx, w_ada, b_ada, norm_mix, w_in, q_a_norm, w_q_b, kv_a_norm, w_kv_b, q_norm, k_norm,
           w_o_mla, dn_conv, dn_a_log, dn_dt_bias, dn_o_norm, w_o_dn, w_out, norm_ffn, w_ffn_up, ffn_conv,
           w_ffn_down):
    batch, seq, d = x.shape
    lc = ctx.shape[1]
    assert w_ada.shape[0] == 1, "single-layer stack"
    assert seq % GRID_W == 0 and seq % DN_CHUNK == 0 and lc % DN_CHUNK == 0
    h = DN_HEADS
    l = 0

    wi = w_in[l]
    o_qa, o_kva, o_kr = 0, MLA_Q_RANK, MLA_Q_RANK + MLA_KV_RANK
    o_dq = o_kr + MLA_ROPE
    o_da = o_dq + 4 * h * DN_DK
    o_ga = o_da + 4 * h
    w_a = jnp.concatenate([wi[:, o_qa:o_kr], _rope_pad_cols(wi[:, o_kr:o_dq]), wi[:, o_da:o_ga],
                           jnp.zeros((d, A_WIDTH - A_DB - 2 * h), F32)], axis=1).astype(BF16)
    w_b = jnp.concatenate([wi[:, o_dq:o_da], wi[:, o_ga:]], axis=1).astype(BF16)
    wq = _qk_pad_cols(w_q_b[l]).astype(BF16)
    wkv = w_kv_b[l].reshape(MLA_KV_RANK, MLA_HEADS, MLA_NOPE + MLA_V)
    wkv = jnp.concatenate([wkv[:, :, :MLA_NOPE].reshape(MLA_KV_RANK, -1),
                           wkv[:, :, MLA_NOPE:].reshape(MLA_KV_RANK, -1)], axis=1).astype(BF16)
    qn = jnp.concatenate([q_norm[l, :MLA_NOPE], _rope_pad_cols(q_norm[l, MLA_NOPE:])])[None, :]
    kn = jnp.concatenate([k_norm[l, :MLA_NOPE], _rope_pad_cols(k_norm[l, MLA_NOPE:])])[None, :]
    hidden = w_ffn_down.shape[1]
    nj = -(-hidden // FFN_TILE)
    hpad = nj * FFN_TILE - hidden
    wg = jnp.pad(w_ffn_up[l][:, :hidden].astype(BF16), ((0, 0), (0, hpad)))
    wv = jnp.pad(w_ffn_up[l][:, hidden:].astype(BF16), ((0, 0), (0, hpad)))
    fcw = jnp.pad(ffn_conv[l], ((0, 0), (0, hpad)))
    wdn = jnp.pad(w_ffn_down[l], ((0, hpad), (0, 0))).astype(BF16)
    hp = jnp.concatenate([dn_a_log[l].T, dn_dt_bias[l].T], axis=1)[:, None, :]
    tables_l = _rope_tables(seq // GRID_W)
    tables_c = (jnp.zeros((lc // GRID_W, ROPE_PAD), F32), jnp.zeros((lc // GRID_W, ROPE_PAD), F32),
                jnp.ones((GRID_W, ROPE_PAD), F32), jnp.zeros((GRID_W, ROPE_PAD), F32))

    rows = jnp.concatenate([c, c_ctx[None, :], jnp.zeros((8 - batch - 1, d), F32)], axis=0)
    mod = _ada(rows, w_ada[l], b_ada[l][None, :]).reshape(8, 6, d)

    xf = x.reshape(batch * seq, d)
    cf = ctx.reshape(batch * lc, d)

    def stream(tokens, n, tm_a, tm_b, row_of):
        nm = norm_mix[l][None, :]
        a, xn = _norm_mod_matmul(tokens, nm, mod, row_of(tm_a), w_a, F32, tm_a, 0, 1, "in_proj_small")
        p = _matmul(xn, w_b, BF16, tm_b, 1024, "in_proj_wide")
        ab = a[:, A_DA:A_DA + 4 * h].reshape(batch * n, 4, h).transpose(2, 0, 1)
        return a, p, ab

    tm_a = _pick_tile(seq, 512)
    tm_b = _pick_tile(seq, 1024)
    lat_row = lambda tm: (lambda i: i // (seq // tm))
    a_l, p_l, ab_l = stream(xf, seq, tm_a, tm_b, lat_row)
    tc = _pick_tile(lc, 256)
    ctx_row = lambda tm: (lambda i: batch)
    a_c, p_c, ab_c = stream(cf, lc, tc, tc, ctx_row)

    qan, kvan = q_a_norm[l][None, :], kv_a_norm[l][None, :]
    tmp = _pick_tile(seq, 512)
    q_l, k_l, v_l = _mla_proj(a_l, tables_l, qan, kvan, qn, kn, wq, wkv, tmp, seq)
    _, k_c, v_c = _mla_proj(a_c, tables_c, qan, kvan, qn, kn, wq, wkv, tc, lc)
    tq = _pick_tile(seq, 512)
    tk = _pick_tile(seq, 1024)
    qt = q_l.reshape(batch, seq, MLA_HEADS, MLA_QK_PAD).transpose(0, 2, 3, 1)

    def with_ones_row(vt):
        lead, n = vt.shape[:-2], vt.shape[-1]
        return jnp.concatenate([vt, jnp.ones(lead + (1, n), BF16), jnp.zeros(lead + (HALO16 - 1, n), BF16)],
                               axis=-2)

    vt = with_ones_row(v_l.reshape(batch, seq // tk, tk, MLA_HEADS, MLA_V).transpose(0, 3, 1, 4, 2))
    vct = with_ones_row(v_c.reshape(batch, lc, MLA_HEADS, MLA_V).transpose(0, 2, 3, 1))
    y_mla = _attention(qt, k_l.reshape(batch, seq, -1), vt, k_c.reshape(batch, lc, -1), vct, tq)
    y_mla = y_mla.reshape(batch * seq, MLA_HEADS * MLA_V)

    conv_w = dn_conv[l]
    prep_c = _dn_prep(p_c, ab_c, hp, conv_w, batch, lc, _pick_tile(lc, 256))
    s_zero = jnp.zeros((2, batch, h, DN_DK, DN_DV), F32)
    _, _, s_ctx = _dn_scan(s_zero, *prep_c, _pick_tile(lc, 256) // DN_CHUNK)
    prep_l = _dn_prep(p_l, ab_l, hp, conv_w, batch, seq, _pick_tile(seq, 1024))
    o_f, o_b, _ = _dn_scan(s_ctx, *prep_l, _pick_tile(seq, 256) // DN_CHUNK)

    tmm = _pick_tile(seq, 256)
    x1 = _merge(xf, mod, lat_row(tmm), y_mla, o_f, o_b, p_l, dn_o_norm[l][None, :], w_o_mla[l].astype(BF16),
                w_o_dn[l].astype(BF16), w_out[l].astype(BF16), tmm, 2)

    tmf = _pick_tile(seq, 512)
    out = _ffn(x1, norm_ffn[l][None, :], mod, lat_row(tmf), wg, wv, fcw, wdn, seq, tmf, 3, 4, 5)
    return out.reshape(batch, seq, d)
```

```python
import functools
import math

import jax
import jax.numpy as jnp
import numpy as np
from jax import lax
from jax.experimental import pallas as pl
from jax.experimental.pallas import tpu as pltpu

F32 = jnp.float32
BF16 = jnp.bfloat16
HIGHEST = lax.Precision.HIGHEST

EPS = 1e-6
GRID_W = 64
ROPE_BASE = 10000.0

MLA_HEADS = 8
MLA_Q_RANK = 512
MLA_KV_RANK = 512
MLA_NOPE = 128
MLA_ROPE = 64
MLA_V = 128
MLA_QK = MLA_NOPE + MLA_ROPE
MLA_QK_PAD = 256

DN_HEADS = 8
DN_DK = 128
DN_DV = 128
DN_CHUNK = 64

LANE = 128
SUBLANE = 8
ROPE_PAD = 128
VMEM_LIMIT = 56 * 1024 * 1024

A_QA, A_KVA, A_KR, A_DA, A_DB, A_WIDTH = 0, 512, 1024, 1152, 1168, 1280
B_DQ, B_DK, B_DV, B_DZ, B_GA, B_GB, B_WIDTH = 0, 1024, 2048, 3072, 4096, 6144, 8192

FFN_TILE = 512
HALO16 = 16


def _params(sem, vmem=VMEM_LIMIT):
    return pltpu.CompilerParams(dimension_semantics=sem, vmem_limit_bytes=vmem)


def _nt_dot(a, b):
    return lax.dot_general(a, b, (((1,), (1,)), ((), ())), preferred_element_type=F32)


def _silu(x):
    return x * jax.nn.sigmoid(x)


def _ada_kernel(c_ref, w_ref, b_ref, o_ref):
    a = _silu(c_ref[...]).astype(BF16)
    o_ref[...] = jnp.dot(a, w_ref[...].astype(BF16), preferred_element_type=F32) + b_ref[...]


def _ada(cs, w, b):
    m, d = cs.shape
    n = w.shape[1]
    tn = 512
    return pl.pallas_call(
        _ada_kernel,
        grid=(n // tn,),
        in_specs=[pl.BlockSpec((m, d), lambda j: (0, 0)),
                  pl.BlockSpec((d, tn), lambda j: (0, j)),
                  pl.BlockSpec((1, tn), lambda j: (0, j))],
        out_specs=pl.BlockSpec((m, tn), lambda j: (0, j)),
        out_shape=jax.ShapeDtypeStruct((m, n), F32),
        compiler_params=_params(("parallel",)),
        name="ada",
    )(cs, w, b)


def _norm_mod(x, nw, shift, scale):
    ms = jnp.mean(x * x, axis=-1, keepdims=True)
    y = x * lax.rsqrt(ms + EPS) * nw
    return y * (1.0 + scale) + shift


def _nmm_kernel(x_ref, nw_ref, mod_ref, w_ref, o_ref, xn_ref, *, k_shift, k_scale):
    y = _norm_mod(x_ref[...], nw_ref[...], mod_ref[0, k_shift:k_shift + 1, :], mod_ref[0, k_scale:k_scale + 1, :])
    xn = y.astype(BF16)
    xn_ref[...] = xn
    o_ref[...] = jnp.dot(xn, w_ref[...], preferred_element_type=F32).astype(o_ref.dtype)


def _norm_mod_matmul(x, nw, mod, row_of_tile, w, out_dtype, tm, k_shift, k_scale, name):
    m, d = x.shape
    n = w.shape[1]
    kern = functools.partial(_nmm_kernel, k_shift=k_shift, k_scale=k_scale)
    return pl.pallas_call(
        kern,
        grid=(m // tm,),
        in_specs=[pl.BlockSpec((tm, d), lambda i: (i, 0)),
                  pl.BlockSpec((1, d), lambda i: (0, 0)),
                  pl.BlockSpec((1, 6, d), lambda i: (row_of_tile(i), 0, 0)),
                  pl.BlockSpec((d, n), lambda i: (0, 0))],
        out_specs=[pl.BlockSpec((tm, n), lambda i: (i, 0)), pl.BlockSpec((tm, d), lambda i: (i, 0))],
        out_shape=[jax.ShapeDtypeStruct((m, n), out_dtype), jax.ShapeDtypeStruct((m, d), BF16)],
        compiler_params=_params(("parallel",)),
        name=name,
    )(x, nw, mod, w)


def _mm_kernel(x_ref, w_ref, o_ref):
    o_ref[...] = jnp.dot(x_ref[...], w_ref[...], preferred_element_type=F32).astype(o_ref.dtype)


def _matmul(x, w, out_dtype, tm, tn, name):
    m, d = x.shape
    n = w.shape[1]
    return pl.pallas_call(
        _mm_kernel,
        grid=(m // tm, n // tn),
        in_specs=[pl.BlockSpec((tm, d), lambda i, j: (i, 0)),
                  pl.BlockSpec((d, tn), lambda i, j: (0, j))],
        out_specs=pl.BlockSpec((tm, tn), lambda i, j: (i, j)),
        out_shape=jax.ShapeDtypeStruct((m, n), out_dtype),
        compiler_params=_params(("parallel", "parallel")),
        name=name,
    )(x, w)


def _mla_proj_kernel(a_ref, cr_ref, sr_ref, cc_ref, sc_ref, qan_ref, kvan_ref, qn_ref, kn_ref, wq_ref, wkv_ref,
                     q_ref, k_ref, v_ref):
    tm = a_ref.shape[0]
    g = tm // GRID_W

    def table(row_ref, col_ref):
        r = jnp.broadcast_to(row_ref[...][:, None, :], (g, GRID_W, ROPE_PAD)).reshape(tm, ROPE_PAD)
        c = jnp.broadcast_to(col_ref[...][None], (g, GRID_W, ROPE_PAD)).reshape(tm, ROPE_PAD)
        return r + c

    cos = table(cr_ref, cc_ref)
    sin = table(sr_ref, sc_ref)

    def rope(r):
        return r * cos + pltpu.roll(r, 64, axis=1) * sin

    def rms_rows(t, w):
        return t * lax.rsqrt(jnp.mean(t * t, axis=-1, keepdims=True) + EPS) * w

    qa = rms_rows(a_ref[:, A_QA:A_QA + MLA_Q_RANK], qan_ref[...]).astype(BF16)
    q = jnp.dot(qa, wq_ref[...], preferred_element_type=F32)
    kva = rms_rows(a_ref[:, A_KVA:A_KVA + MLA_KV_RANK], kvan_ref[...]).astype(BF16)
    kv = jnp.dot(kva, wkv_ref[...], preferred_element_type=F32)
    kr = a_ref[:, A_KR:A_KR + ROPE_PAD]
    kr_ss = jnp.sum(kr * kr, axis=-1, keepdims=True)
    qn = qn_ref[...]
    kn = kn_ref[...]
    scale = MLA_QK ** -0.5 * math.log2(math.e)
    for h in range(MLA_HEADS):
        qh = q[:, h * MLA_QK_PAD:(h + 1) * MLA_QK_PAD]
        inv = lax.rsqrt(jnp.sum(qh * qh, axis=-1, keepdims=True) * (1.0 / MLA_QK) + EPS) * scale
        qh = qh * inv * qn
        q_ref[:, h * MLA_QK_PAD:h * MLA_QK_PAD + MLA_NOPE] = qh[:, :MLA_NOPE].astype(BF16)
        q_ref[:, h * MLA_QK_PAD + MLA_NOPE:(h + 1) * MLA_QK_PAD] = rope(qh[:, MLA_NOPE:]).astype(BF16)

        kh = kv[:, h * MLA_NOPE:(h + 1) * MLA_NOPE]
        inv = lax.rsqrt((jnp.sum(kh * kh, axis=-1, keepdims=True) + kr_ss) * (1.0 / MLA_QK) + EPS)
        k_ref[:, h * MLA_QK_PAD:h * MLA_QK_PAD + MLA_NOPE] = (kh * inv * kn[:, :MLA_NOPE]).astype(BF16)
        k_ref[:, h * MLA_QK_PAD + MLA_NOPE:(h + 1) * MLA_QK_PAD] = rope(
            kr * inv * kn[:, MLA_NOPE:]).astype(BF16)
    v_ref[...] = kv[:, MLA_HEADS * MLA_NOPE:].astype(BF16)


def _mla_proj(a, tables, qan, kvan, qn, kn, wq, wkv, tm, seq):
    m = a.shape[0]
    nseq = seq // tm
    row_cos, row_sin, col_cos, col_sin = tables
    full = lambda arr: pl.BlockSpec(arr.shape, lambda i: (0, 0))
    row_spec = pl.BlockSpec((tm // GRID_W, ROPE_PAD), lambda i: (i % nseq, 0))
    return pl.pallas_call(
        _mla_proj_kernel,
        grid=(m // tm,),
        in_specs=[pl.BlockSpec((tm, A_WIDTH), lambda i: (i, 0)),
                  row_spec, row_spec, full(col_cos), full(col_sin),
                  full(qan), full(kvan), full(qn), full(kn), full(wq), full(wkv)],
        out_specs=[pl.BlockSpec((tm, MLA_HEADS * MLA_QK_PAD), lambda i: (i, 0)),
                   pl.BlockSpec((tm, MLA_HEADS * MLA_QK_PAD), lambda i: (i, 0)),
                   pl.BlockSpec((tm, MLA_HEADS * MLA_V), lambda i: (i, 0))],
        out_shape=[jax.ShapeDtypeStruct((m, MLA_HEADS * MLA_QK_PAD), BF16),
                   jax.ShapeDtypeStruct((m, MLA_HEADS * MLA_QK_PAD), BF16),
                   jax.ShapeDtypeStruct((m, MLA_HEADS * MLA_V), BF16)],
        compiler_params=_params(("parallel",)),
        name="mla_proj",
    )(a, row_cos, row_sin, col_cos, col_sin, qan, kvan, qn, kn, wq, wkv)


ATTN_UNROLL = 8
ATTN_ROWS = 64


def _attn_kernel(qt_ref, kc_ref, vct_ref, k_ref, vt_ref, o_ref, s0_ref, s1_ref, acc_ref, *, nk, unroll):
    qt = qt_ref[0, 0]
    tq = qt.shape[1]
    tk = s0_ref.shape[0]
    rows = ATTN_ROWS
    s_refs = (s0_ref, s1_ref)

    def scores(j):
        return jnp.dot(k_ref[0, pl.ds(pl.multiple_of(j * tk, tk), tk), :], qt, preferred_element_type=F32)

    def probs(s, m):
        return jnp.exp2((s - m).astype(BF16))

    def store_scores(slot, s):
        s_refs[slot][...] = s
        return jnp.max(s.reshape(tk // SUBLANE, SUBLANE, tq), axis=0)

    s = jnp.dot(kc_ref[0], qt, preferred_element_type=F32)
    m8_first = store_scores(0, scores(0))
    m = jnp.max(s, axis=0, keepdims=True)
    acc_ref[...] = jnp.dot(vct_ref[0, 0], probs(s, m), preferred_element_type=F32)

    def tile(j, slot, carry):
        m, m8 = carry
        m8_next = store_scores(1 - slot, scores(jnp.minimum(j + 1, nk - 1)))
        m_new = jnp.maximum(m, jnp.max(m8, axis=0, keepdims=True))
        p = jnp.concatenate([probs(s_refs[slot][r:r + rows, :], m_new) for r in range(0, tk, rows)], axis=0)
        acc_ref[...] = jnp.exp2(m - m_new) * acc_ref[...] + jnp.dot(vt_ref[0, 0, j], p, preferred_element_type=F32)
        return m_new, m8_next

    def body(i, carry):
        for u in range(unroll):
            carry = tile(unroll * i + u, u % 2, carry)
        return carry

    lax.fori_loop(0, nk // unroll, body, (m, m8_first))
    acc = acc_ref[...]
    o_ref[0] = (acc[:MLA_V] * (1.0 / acc[MLA_V:MLA_V + 1])).T.astype(o_ref.dtype)


def _attention(qt, k, vt, kc, vct, tq):
    b, t, _ = k.shape
    lc = kc.shape[1]
    nk, vrows, tk = vt.shape[2], vt.shape[3], vt.shape[4]
    unroll = ATTN_UNROLL if nk % ATTN_UNROLL == 0 else 2
    assert nk % unroll == 0
    kern = functools.partial(_attn_kernel, nk=nk, unroll=unroll)
    return pl.pallas_call(
        kern,
        grid=(b, MLA_HEADS, t // tq),
        in_specs=[pl.BlockSpec((1, 1, MLA_QK_PAD, tq), lambda bi, h, i: (bi, h, 0, i)),
                  pl.BlockSpec((1, lc, MLA_QK_PAD), lambda bi, h, i: (bi, 0, h)),
                  pl.BlockSpec((1, 1, vrows, lc), lambda bi, h, i: (bi, h, 0, 0)),
                  pl.BlockSpec((1, t, MLA_QK_PAD), lambda bi, h, i: (bi, 0, h)),
                  pl.BlockSpec((1, 1, nk, vrows, tk), lambda bi, h, i: (bi, h, 0, 0, 0))],
        out_specs=pl.BlockSpec((1, tq, MLA_V), lambda bi, h, i: (bi, i, h)),
        out_shape=jax.ShapeDtypeStruct((b, t, MLA_HEADS * MLA_V), BF16),
        scratch_shapes=[pltpu.VMEM((tk, tq), F32), pltpu.VMEM((tk, tq), F32), pltpu.VMEM((vrows, tq), F32)],
        compiler_params=_params(("parallel", "parallel", "arbitrary")),
        name="attention",
    )(qt, kc, vct, k, vt)


def _split2(x):
    hi = x.astype(BF16)
    return hi, (x - hi.astype(F32)).astype(BF16)


def _bmm(a, b):
    return lax.dot_general(a, b, (((2,), (1,)), ((0,), (0,))), preferred_element_type=F32)


def _bmm_nt(a, b):
    return lax.dot_general(a, b, (((2,), (2,)), ((0,), (0,))), preferred_element_type=F32)


def _bmm_hp(a, b):
    ah, al = _split2(a)
    bh, bl = _split2(b)
    return _bmm(ah, bh) + _bmm(al, bh) + _bmm(ah, bl)


def _bd2(b):
    lane = lax.broadcasted_iota(jnp.int32, b.shape, 2)
    zero = jnp.zeros_like(b)
    return jnp.concatenate([jnp.where(lane < DN_CHUNK, b, zero), jnp.where(lane >= DN_CHUNK, b, zero)], axis=1)


def _tri_inverse(m, eye):
    assert m.shape[-2] == 64
    x = eye - m
    nb = (-m).astype(BF16)
    bd = _bd2(nb)
    for _ in range(3):
        nb = _bmm(nb, bd).astype(BF16)
        bd = _bd2(nb)
        x = x + _bmm(x.astype(BF16), bd)
    xb = x.astype(BF16)
    r = eye - x - _bmm(m.astype(BF16), _bd2(xb))
    x = x + _bmm(xb, _bd2(r.astype(BF16)))
    mh, ml = _split2(m)
    xh, xl = _split2(x)
    bdh = _bd2(xh)
    r = eye - x - (_bmm(mh, bdh) + _bmm(ml, bdh) + _bmm(mh, _bd2(xl)))
    return x + _bmm(xh, _bd2(r.astype(BF16)))


def _dn_prep_kernel(qm_ref, qp_ref, qx_ref, km_ref, kp_ref, kx_ref, vm_ref, vp_ref, vx_ref,
                    cwq_ref, cwk_ref, cwv_ref, ab_ref, hp_ref,
                    u_ref, w_ref, ke_ref, qg_ref, a_ref, ge_ref, ext_ref, *, tm, tiles_per_seq):
    i = pl.program_id(0)
    first = (i % tiles_per_seq) == 0
    last = (i % tiles_per_seq) == tiles_per_seq - 1
    c = DN_CHUNK

    def conv_silu(main_ref, prev_ref, next_ref, cw_ref):
        ext_ref[0:HALO16, :] = jnp.where(first, 0.0, prev_ref[...].astype(F32))
        ext_ref[HALO16:HALO16 + tm, :] = main_ref[...].astype(F32)
        ext_ref[HALO16 + tm:, :] = jnp.where(last, 0.0, next_ref[...].astype(F32))
        y = (cw_ref[0:1, :] * ext_ref[HALO16 - 1:HALO16 - 1 + tm, :]
             + cw_ref[1:2, :] * ext_ref[HALO16:HALO16 + tm, :]
             + cw_ref[2:3, :] * ext_ref[HALO16 + 1:HALO16 + 1 + tm, :])
        return _silu(y)

    def l2n(t):
        return t * lax.rsqrt(jnp.sum(t * t, axis=-1, keepdims=True) + EPS)

    q = l2n(conv_silu(qm_ref, qp_ref, qx_ref, cwq_ref)) * (DN_DK ** -0.5)
    k = l2n(conv_silu(km_ref, kp_ref, kx_ref, cwk_ref))
    v = conv_silu(vm_ref, vp_ref, vx_ref, cwv_ref)

    hp = hp_ref[0]
    ab = ab_ref[0]
    z = ab[:, 0:2] + hp[:, 2:4]
    softplus = jnp.maximum(z, 0.0) + jnp.log(1.0 + jnp.exp(-jnp.abs(z)))
    g_all = -jnp.exp(hp[:, 0:2]) * softplus
    beta_all = jax.nn.sigmoid(ab[:, 2:4])

    nc = tm // c

    def per_chunk(t):
        return t.reshape(nc, c, t.shape[-1])

    def all_lanes(t2, d):
        return per_chunk(jnp.broadcast_to(t2[:, d:d + 1], (tm, LANE)))

    shape = (nc, c, LANE)
    rows = lax.broadcasted_iota(jnp.int32, shape, 1)
    lane = lax.broadcasted_iota(jnp.int32, shape, 2)
    bwd = lane >= c
    col = jnp.where(bwd, lane - c, lane)
    eye_mask = rows == col
    ahead = jnp.where(bwd, rows - col, col - rows)
    incl = ahead <= 0
    strict = ahead < 0
    eye = jnp.where(eye_mask, 1.0, 0.0)
    tri = jnp.where(incl, 1.0, 0.0).astype(BF16)

    def halves(t2):
        return jnp.where(bwd, all_lanes(t2, 1), all_lanes(t2, 0))

    g_hi, g_lo = _split2(halves(g_all))
    gc = _bmm(tri, _bd2(g_hi)) + _bmm(tri, _bd2(g_lo))
    gc_row = jnp.sum(jnp.where(eye_mask, gc, 0.0), axis=1, keepdims=True)
    decay = jnp.where(incl, jnp.exp(jnp.where(incl, gc - gc_row, 0.0)), 0.0)

    k3 = per_chunk(k)
    q3 = per_chunk(q)
    v3 = per_chunk(v)
    k3b = k3.astype(BF16)
    kdup = jnp.concatenate([k3b, k3b], axis=1)
    kk = _bmm_nt(k3b, kdup)
    att = _bmm_nt(q3.astype(BF16), kdup) * decay
    m = jnp.where(strict, halves(beta_all) * kk * decay, 0.0)
    t_hi, t_lo = _split2(_tri_inverse(m, eye))

    swapped = pltpu.roll(gc.reshape(tm, LANE), c, axis=1).reshape(shape)
    gc_dir = (jnp.where(bwd, swapped, gc), jnp.where(bwd, gc, swapped))
    rhs, ke, qg, gend = [], [], [], []
    for d in range(2):
        gcd = gc_dir[d]
        beta = all_lanes(beta_all, d)
        eg = jnp.exp(gcd)
        gtot = gcd[:, c - 1:c, :] if d == 0 else gcd[:, 0:1, :]
        rhs.append(jnp.concatenate([v3 * beta, k3 * beta * eg], axis=2))
        ke.append(k3 * jnp.exp(gtot - gcd))
        qg.append(q3 * eg)
        gend.append(jnp.exp(gtot))
    rhs = jnp.concatenate(rhs, axis=1).astype(BF16)
    sol = _bmm(_bd2(t_hi), rhs) + _bmm(_bd2(t_lo), rhs)
    att_dir = (att, pltpu.roll(att.reshape(tm, LANE), c, axis=1).reshape(shape))
    for d in range(2):
        sl = slice(d * c, (d + 1) * c)
        u_ref[d, 0, 0] = sol[:, sl, :DN_DV].reshape(tm, DN_DV)
        w_ref[d, 0, 0] = sol[:, sl, DN_DV:].reshape(tm, DN_DK).astype(BF16)
        ke_ref[d, 0, 0] = ke[d].reshape(tm, DN_DK).astype(BF16)
        qg_ref[d, 0, 0] = qg[d].reshape(tm, DN_DK).astype(BF16)
        a_ref[d, 0, 0] = att_dir[d][:, :, :c].reshape(tm, c).astype(BF16)
        ge_ref[d, 0, 0] = gend[d]


def _dn_prep(p, ab, hp, conv_w, batch, seq, tm):
    h = DN_HEADS
    nt = seq // tm
    nb16 = tm // HALO16
    total16 = batch * seq // HALO16
    kern = functools.partial(_dn_prep_kernel, tm=tm, tiles_per_seq=nt)

    def triple(col0):
        return [pl.BlockSpec((tm, LANE), lambda i, hh: (i, col0 + hh)),
                pl.BlockSpec((HALO16, LANE), lambda i, hh: (jnp.maximum(i * nb16 - 1, 0), col0 + hh)),
                pl.BlockSpec((HALO16, LANE), lambda i, hh: (jnp.minimum((i + 1) * nb16, total16 - 1), col0 + hh))]

    cq, ck, cv = B_DQ // LANE, B_DK // LANE, B_DV // LANE
    row = lambda shape: pl.BlockSpec(shape, lambda i, hh: (0, i // nt, hh, i % nt, 0))
    seq_shape = lambda width, dt: jax.ShapeDtypeStruct((2, batch, h, seq, width), dt)
    return pl.pallas_call(
        kern,
        grid=(batch * nt, h),
        in_specs=triple(cq) + triple(ck) + triple(cv) + [
            pl.BlockSpec((3, LANE), lambda i, hh: (0, hh)),
            pl.BlockSpec((3, LANE), lambda i, hh: (0, h + hh)),
            pl.BlockSpec((3, LANE), lambda i, hh: (0, 2 * h + hh)),
            pl.BlockSpec((1, tm, 4), lambda i, hh: (hh, i, 0)),
            pl.BlockSpec((1, 1, 4), lambda i, hh: (hh, 0, 0))],
        out_specs=[row((2, 1, 1, tm, DN_DV)), row((2, 1, 1, tm, DN_DK)), row((2, 1, 1, tm, DN_DK)),
                   row((2, 1, 1, tm, DN_DK)), row((2, 1, 1, tm, DN_CHUNK)),
                   pl.BlockSpec((2, 1, 1, tm // DN_CHUNK, 1, LANE), lambda i, hh: (0, i // nt, hh, i % nt, 0, 0))],
        out_shape=[seq_shape(DN_DV, F32), seq_shape(DN_DK, BF16), seq_shape(DN_DK, BF16),
                   seq_shape(DN_DK, BF16), seq_shape(DN_CHUNK, BF16),
                   jax.ShapeDtypeStruct((2, batch, h, seq // DN_CHUNK, 1, LANE), F32)],
        scratch_shapes=[pltpu.VMEM((tm + 2 * HALO16, LANE), F32)],
        compiler_params=_params(("parallel", "parallel")),
        name="dn_prep",
    )(p, p, p, p, p, p, p, p, p, conv_w, conv_w, conv_w, ab, hp)


def _dn_scan_kernel(s0_ref, uf_ref, wf_ref, kf_ref, qf_ref, af_ref, gf_ref,
                    ub_ref, wb_ref, kb_ref, qb_ref, ab_ref, gb_ref,
                    of_ref, ob_ref, sfin_ref, s_ref, *, cs):
    n = pl.program_id(1)
    c = DN_CHUNK
    h = DN_HEADS

    @pl.when(n == 0)
    def _():
        s_ref[0:h] = s0_ref[0, 0]
        s_ref[h:] = s0_ref[1, 0]

    for ci in range(cs):
        rf = ci * c
        rb = (cs - 1 - ci) * c

        def pair(f_ref, b_ref):
            return jnp.concatenate([f_ref[0, 0, :, rf:rf + c, :], b_ref[0, 0, :, rb:rb + c, :]], axis=0)

        s = s_ref[...]
        s16 = s.astype(BF16)
        ws_qs = _bmm(jnp.concatenate([pair(wf_ref, wb_ref), pair(qf_ref, qb_ref)], axis=1), s16)
        v_new = pair(uf_ref, ub_ref) - ws_qs[:, :c]
        v16 = v_new.astype(BF16)
        o = ws_qs[:, c:] + _bmm(pair(af_ref, ab_ref), v16)
        of_ref[0, :, rf:rf + c, :] = o[:h].astype(of_ref.dtype)
        ob_ref[0, :, rb:rb + c, :] = o[h:].astype(ob_ref.dtype)
        g = jnp.concatenate([gf_ref[0, 0, :, ci], gb_ref[0, 0, :, cs - 1 - ci]], axis=0)
        s_ref[...] = s * g + lax.dot_general(pair(kf_ref, kb_ref), v16, (((1,), (1,)), ((0,), (0,))),
                                             preferred_element_type=F32)

    @pl.when(n == pl.num_programs(1) - 1)
    def _():
        sfin_ref[0, 0] = s_ref[0:h]
        sfin_ref[1, 0] = s_ref[h:]


def _dn_scan(s0, u, w, ke, qg, a, ge, cs):
    _, batch, h, seq, _ = u.shape
    ts = cs * DN_CHUNK
    ns = seq // ts
    kern = functools.partial(_dn_scan_kernel, cs=cs)

    def specs(d):
        idx = (lambda n: n) if d == 0 else (lambda n: ns - 1 - n)
        seqs = [pl.BlockSpec((1, 1, h, ts, width), lambda b, n: (d, b, 0, idx(n), 0))
                for width in (DN_DV, DN_DK, DN_DK, DN_DK, DN_CHUNK)]
        return seqs + [pl.BlockSpec((1, 1, h, cs, 1, LANE), lambda b, n: (d, b, 0, idx(n), 0, 0))]

    o_spec = lambda d: pl.BlockSpec((1, h, ts, DN_DV), lambda b, n: (b, 0, n if d == 0 else ns - 1 - n, 0))
    state = pl.BlockSpec((2, 1, h, DN_DK, DN_DV), lambda b, n: (0, b, 0, 0, 0))
    return pl.pallas_call(
        kern,
        grid=(batch, ns),
        in_specs=[state] + specs(0) + specs(1),
        out_specs=[o_spec(0), o_spec(1), state],
        out_shape=[jax.ShapeDtypeStruct((batch, h, seq, DN_DV), BF16),
                   jax.ShapeDtypeStruct((batch, h, seq, DN_DV), BF16),
                   jax.ShapeDtypeStruct((2, batch, h, DN_DK, DN_DV), F32)],
        scratch_shapes=[pltpu.VMEM((2 * h, DN_DK, DN_DV), F32)],
        compiler_params=_params(("parallel", "arbitrary")),
        name="dn_scan",
    )(s0, u, w, ke, qg, a, ge, u, w, ke, qg, a, ge)


def _merge_kernel(x_ref, mod_ref, ym_ref, of_ref, ob_ref, z_ref, nw_ref, ga_ref, gb_ref, wm_ref, wd_ref, wo_ref,
                  o_ref, *, k_gate):
    nw = nw_ref[...]
    heads = []
    for h in range(DN_HEADS):
        o = of_ref[0, h].astype(F32) + ob_ref[0, h].astype(F32)
        y = o * lax.rsqrt(jnp.mean(o * o, axis=-1, keepdims=True) + EPS) * nw
        heads.append((y * _silu(z_ref[:, h * DN_DV:(h + 1) * DN_DV].astype(F32))).astype(BF16))
    yd = jnp.concatenate(heads, axis=1)
    pm = jnp.dot(ym_ref[...], wm_ref[...], preferred_element_type=F32)
    pd = jnp.dot(yd, wd_ref[...], preferred_element_type=F32)
    merged = (jax.nn.sigmoid(ga_ref[...].astype(F32)) * pm
              + jax.nn.sigmoid(gb_ref[...].astype(F32)) * pd).astype(BF16)
    y = jnp.dot(merged, wo_ref[...], preferred_element_type=F32)
    o_ref[...] = x_ref[...] + mod_ref[0, k_gate:k_gate + 1, :] * y


def _merge(x, mod, row_of_tile, ym, o_f, o_b, p, nw, wm, wd, wo, tm, k_gate):
    m, d = x.shape
    _, h, seq, _ = o_f.shape
    nt = seq // tm
    zw = h * DN_DV
    kern = functools.partial(_merge_kernel, k_gate=k_gate)
    const = lambda arr: pl.BlockSpec(arr.shape, lambda i: (0, 0), pipeline_mode=pl.Buffered(1))
    scan_out = pl.BlockSpec((1, h, tm, DN_DV), lambda i: (i // nt, 0, i % nt, 0))
    return pl.pallas_call(
        kern,
        grid=(m // tm,),
        in_specs=[pl.BlockSpec((tm, d), lambda i: (i, 0)),
                  pl.BlockSpec((1, 6, d), lambda i: (row_of_tile(i), 0, 0)),
                  pl.BlockSpec((tm, ym.shape[1]), lambda i: (i, 0)),
                  scan_out, scan_out,
                  pl.BlockSpec((tm, zw), lambda i: (i, B_DZ // zw)),
                  pl.BlockSpec((1, DN_DV), lambda i: (0, 0)),
                  pl.BlockSpec((tm, d), lambda i: (i, B_GA // d)),
                  pl.BlockSpec((tm, d), lambda i: (i, B_GB // d)),
                  const(wm), const(wd), const(wo)],
        out_specs=pl.BlockSpec((tm, d), lambda i: (i, 0)),
        out_shape=jax.ShapeDtypeStruct((m, d), F32),
        compiler_params=_params(("parallel",)),
        name="merge",
    )(x, mod, ym, o_f, o_b, p, nw, p, p, wm, wd, wo)


def _ffn_kernel(x_ref, xp_ref, xx_ref, nw_ref, mod_ref, wg_ref, wv_ref, cw_ref, wdn_ref, o_ref, xn_ref, ge_ref,
                *, tm, tiles_per_seq, k_shift, k_scale, k_gate):
    i = pl.program_id(0)
    j = pl.program_id(1)
    th = FFN_TILE

    @pl.when(j == 0)
    def _():
        first = (i % tiles_per_seq) == 0
        last = (i % tiles_per_seq) == tiles_per_seq - 1
        nw = nw_ref[...]
        sh = mod_ref[0, k_shift:k_shift + 1, :]
        sc = mod_ref[0, k_scale:k_scale + 1, :]
        xn_ref[0:HALO16, :] = jnp.where(first, 0.0, _norm_mod(xp_ref[...], nw, sh, sc)).astype(BF16)
        xn_ref[HALO16:HALO16 + tm, :] = _norm_mod(x_ref[...], nw, sh, sc).astype(BF16)
        xn_ref[HALO16 + tm:, :] = jnp.where(last, 0.0, _norm_mod(xx_ref[...], nw, sh, sc)).astype(BF16)
        o_ref[...] = jnp.zeros_like(o_ref)

    ge_ref[...] = jnp.dot(xn_ref[...], wg_ref[...], preferred_element_type=F32)
    val = jnp.dot(xn_ref[HALO16:HALO16 + tm, :], wv_ref[...], preferred_element_type=F32)
    conv = (cw_ref[0:1, :] * ge_ref[HALO16 - 1:HALO16 - 1 + tm, :]
            + cw_ref[1:2, :] * ge_ref[HALO16:HALO16 + tm, :]
            + cw_ref[2:3, :] * ge_ref[HALO16 + 1:HALO16 + 1 + tm, :])
    hid = (_silu(conv) * val).astype(BF16)
    o_ref[...] += jnp.dot(hid, wdn_ref[...], preferred_element_type=F32)

    @pl.when(j == pl.num_programs(1) - 1)
    def _():
        o_ref[...] = x_ref[...] + mod_ref[0, k_gate:k_gate + 1, :] * o_ref[...]


def _ffn(x, nw, mod, row_of_tile, wg, wv, cw, wdn, seq, tm, k_shift, k_scale, k_gate):
    m, d = x.shape
    th = FFN_TILE
    nj = wdn.shape[0] // th
    nt = seq // tm
    nb16 = tm // HALO16
    total16 = m // HALO16
    kern = functools.partial(_ffn_kernel, tm=tm, tiles_per_seq=nt, k_shift=k_shift, k_scale=k_scale,
                             k_gate=k_gate)
    return pl.pallas_call(
        kern,
        grid=(m // tm, nj),
        in_specs=[pl.BlockSpec((tm, d), lambda i, j: (i, 0)),
                  pl.BlockSpec((HALO16, d), lambda i, j: (jnp.maximum(i * nb16 - 1, 0), 0)),
                  pl.BlockSpec((HALO16, d), lambda i, j: (jnp.minimum((i + 1) * nb16, total16 - 1), 0)),
                  pl.BlockSpec((1, d), lambda i, j: (0, 0)),
                  pl.BlockSpec((1, 6, d), lambda i, j: (row_of_tile(i), 0, 0)),
                  pl.BlockSpec((d, th), lambda i, j: (0, j)),
                  pl.BlockSpec((d, th), lambda i, j: (0, j)),
                  pl.BlockSpec((3, th), lambda i, j: (0, j)),
                  pl.BlockSpec((th, d), lambda i, j: (j, 0))],
        out_specs=pl.BlockSpec((tm, d), lambda i, j: (i, 0)),
        out_shape=jax.ShapeDtypeStruct((m, d), F32),
        scratch_shapes=[pltpu.VMEM((tm + 2 * HALO16, d), BF16), pltpu.VMEM((tm + 2 * HALO16, th), F32)],
        compiler_params=_params(("parallel", "arbitrary")),
        name="ffn",
    )(x, x, x, nw, mod, wg, wv, cw, wdn)


def _rope_pad_cols(r):
    n = MLA_ROPE // 4
    z = jnp.zeros(r.shape[:-1] + (2 * n,), r.dtype)
    return jnp.concatenate([r[..., 0:n], r[..., 2 * n:3 * n], z, r[..., n:2 * n], r[..., 3 * n:4 * n], z], axis=-1)


def _qk_pad_cols(w):
    lead = w.shape[:-1]
    w = w.reshape(lead + (MLA_HEADS, MLA_QK))
    out = jnp.concatenate([w[..., :MLA_NOPE], _rope_pad_cols(w[..., MLA_NOPE:])], axis=-1)
    return out.reshape(lead + (MLA_HEADS * MLA_QK_PAD,))


def _rope_tables(rows):
    n = MLA_ROPE // 4
    inv = ROPE_BASE ** (-jnp.arange(n, dtype=F32) / n)
    ang_r = jnp.arange(rows, dtype=F32)[:, None] * inv
    ang_c = jnp.arange(GRID_W, dtype=F32)[:, None] * inv

    def lanes(a_row, a_col, fill, count):
        return jnp.concatenate([a_row, a_col, jnp.full((count, 2 * n), fill, F32)] * 2, axis=-1)

    zr, zc = jnp.zeros((rows, n), F32), jnp.zeros((GRID_W, n), F32)
    row_cos = lanes(jnp.cos(ang_r), zr, 0.0, rows)
    row_sin = jnp.concatenate([-jnp.sin(ang_r), zr, jnp.zeros((rows, 2 * n), F32),
                               jnp.sin(ang_r), zr, jnp.zeros((rows, 2 * n), F32)], axis=-1)
    col_cos = lanes(zc, jnp.cos(ang_c), 1.0, GRID_W)
    col_sin = jnp.concatenate([zc, -jnp.sin(ang_c), jnp.zeros((GRID_W, 2 * n), F32),
                               zc, jnp.sin(ang_c), jnp.zeros((GRID_W, 2 * n), F32)], axis=-1)
    return row_cos, row_sin, col_cos, col_sin


def _pick_tile(n, pref):
    t = min(pref, n)
    while n % t:
        t //= 2
    return t


def kernel(x, c, ctx, c_ctx, w_ada, b_ada, norm_mix, w_in, q_a_norm, w_q_b, kv_a_norm, w_kv_b, q_norm, k_norm,
           w_o_mla, dn_conv, dn_a_log, dn_dt_bias, dn_o_norm, w_o_dn, w_out, norm_ffn, w_ffn_up, ffn_conv,
           w_ffn_down):
    batch, seq, d = x.shape
    lc = ctx.shape[1]
    assert w_ada.shape[0] == 1, "single-layer stack"
    assert seq % GRID_W == 0 and seq % DN_CHUNK == 0 and lc % DN_CHUNK == 0
    h = DN_HEADS
    l = 0

    wi = w_in[l]
    o_qa, o_kva, o_kr = 0, MLA_Q_RANK, MLA_Q_RANK + MLA_KV_RANK
    o_dq = o_kr + MLA_ROPE
    o_da = o_dq + 4 * h * DN_DK
    o_ga = o_da + 4 * h
    w_a = jnp.concatenate([wi[:, o_qa:o_kr], _rope_pad_cols(wi[:, o_kr:o_dq]), wi[:, o_da:o_ga],
                           jnp.zeros((d, A_WIDTH - A_DB - 2 * h), F32)], axis=1).astype(BF16)
    w_b = jnp.concatenate([wi[:, o_dq:o_da], wi[:, o_ga:]], axis=1).astype(BF16)
    wq = _qk_pad_cols(w_q_b[l]).astype(BF16)
    wkv = w_kv_b[l].reshape(MLA_KV_RANK, MLA_HEADS, MLA_NOPE + MLA_V)
    wkv = jnp.concatenate([wkv[:, :, :MLA_NOPE].reshape(MLA_KV_RANK, -1),
                           wkv[:, :, MLA_NOPE:].reshape(MLA_KV_RANK, -1)], axis=1).astype(BF16)
    qn = jnp.concatenate([q_norm[l, :MLA_NOPE], _rope_pad_cols(q_norm[l, MLA_NOPE:])])[None, :]
    kn = jnp.concatenate([k_norm[l, :MLA_NOPE], _rope_pad_cols(k_norm[l, MLA_NOPE:])])[None, :]
    hidden = w_ffn_down.shape[1]
    nj = -(-hidden // FFN_TILE)
    hpad = nj * FFN_TILE - hidden
    wg = jnp.pad(w_ffn_up[l][:, :hidden].astype(BF16), ((0, 0), (0, hpad)))
    wv = jnp.pad(w_ffn_up[l][:, hidden:].astype(BF16), ((0, 0), (0, hpad)))
    fcw = jnp.pad(ffn_conv[l], ((0, 0), (0, hpad)))
    wdn = jnp.pad(w_ffn_down[l], ((0, hpad), (0, 0))).astype(BF16)
    hp = jnp.concatenate([dn_a_log[l].T, dn_dt_bias[l].T], axis=1)[:, None, :]
    tables_l = _rope_tables(seq // GRID_W)
    tables_c = (jnp.zeros((lc // GRID_W, ROPE_PAD), F32), jnp.zeros((lc // GRID_W, ROPE_PAD), F32),
                jnp.ones((GRID_W, ROPE_PAD), F32), jnp.zeros((GRID_W, ROPE_PAD), F32))

    rows = jnp.concatenate([c, c_ctx[None, :], jnp.zeros((8 - batch - 1, d), F32)], axis=0)
    mod = _ada(rows, w_ada[l], b_ada[l][None, :]).reshape(8, 6, d)

    xf = x.reshape(batch * seq, d)
    cf = ctx.reshape(batch * lc, d)

    def stream(tokens, n, tm_a, tm_b, row_of):
        nm = norm_mix[l][None, :]
        a, xn = _norm_mod_matmul(tokens, nm, mod, row_of(tm_a), w_a, F32, tm_a, 0, 1, "in_proj_small")
        p = _matmul(xn, w_b, BF16, tm_b, 1024, "in_proj_wide")
        ab = a[:, A_DA:A_DA + 4 * h].reshape(batch * n, 4, h).transpose(2, 0, 1)
        return a, p, ab

    tm_a = _pick_tile(seq, 512)
    tm_b = _pick_tile(seq, 1024)
    lat_row = lambda tm: (lambda i: i // (seq // tm))
    a_l, p_l, ab_l = stream(xf, seq, tm_a, tm_b, lat_row)
    tc = _pick_tile(lc, 256)
    ctx_row = lambda tm: (lambda i: batch)
    a_c, p_c, ab_c = stream(cf, lc, tc, tc, ctx_row)

    qan, kvan = q_a_norm[l][None, :], kv_a_norm[l][None, :]
    tmp = _pick_tile(seq, 512)
    q_l, k_l, v_l = _mla_proj(a_l, tables_l, qan, kvan, qn, kn, wq, wkv, tmp, seq)
    _, k_c, v_c = _mla_proj(a_c, tables_c, qan, kvan, qn, kn, wq, wkv, tc, lc)
    tq = _pick_tile(seq, 512)
    tk = _pick_tile(seq, 512)
    qt = q_l.reshape(batch, seq, MLA_HEADS, MLA_QK_PAD).transpose(0, 2, 3, 1)

    def with_ones_row(vt):
        lead, n = vt.shape[:-2], vt.shape[-1]
        return jnp.concatenate([vt, jnp.ones(lead + (1, n), BF16), jnp.zeros(lead + (HALO16 - 1, n), BF16)],
                               axis=-2)

    vt = with_ones_row(v_l.reshape(batch, seq // tk, tk, MLA_HEADS, MLA_V).transpose(0, 3, 1, 4, 2))
    vct = with_ones_row(v_c.reshape(batch, lc, MLA_HEADS, MLA_V).transpose(0, 2, 3, 1))
    y_mla = _attention(qt, k_l.reshape(batch, seq, -1), vt, k_c.reshape(batch, lc, -1), vct, tq)
    y_mla = y_mla.reshape(batch * seq, MLA_HEADS * MLA_V)

    conv_w = dn_conv[l]
    prep_c = _dn_prep(p_c, ab_c, hp, conv_w, batch, lc, _pick_tile(lc, 256))
    s_zero = jnp.zeros((2, batch, h, DN_DK, DN_DV), F32)
    _, _, s_ctx = _dn_scan(s_zero, *prep_c, _pick_tile(lc, 256) // DN_CHUNK)
    prep_l = _dn_prep(p_l, ab_l, hp, conv_w, batch, seq, _pick_tile(seq, 1024))
    o_f, o_b, _ = _dn_scan(s_ctx, *prep_l, _pick_tile(seq, 256) // DN_CHUNK)

    tmm = _pick_tile(seq, 256)
    x1 = _merge(xf, mod, lat_row(tmm), y_mla, o_f, o_b, p_l, dn_o_norm[l][None, :], w_o_mla[l].astype(BF16),
                w_o_dn[l].astype(BF16), w_out[l].astype(BF16), tmm, 2)

    tmf = _pick_tile(seq, 512)
    out = _ffn(x1, norm_ffn[l][None, :], mod, lat_row(tmf), wg, wv, fcw, wdn, seq, tmf, 3, 4, 5)
    return out.reshape(batch, seq, d)
```

```python
import functools
import math

import jax
import jax.numpy as jnp
import numpy as np
from jax import lax
from jax.experimental import pallas as pl
from jax.experimental.pallas import tpu as pltpu

F32 = jnp.float32
BF16 = jnp.bfloat16
HIGHEST = lax.Precision.HIGHEST

EPS = 1e-6
GRID_W = 64
ROPE_BASE = 10000.0

MLA_HEADS = 8
MLA_Q_RANK = 512
MLA_KV_RANK = 512
MLA_NOPE = 128
MLA_ROPE = 64
MLA_V = 128
MLA_QK = MLA_NOPE + MLA_ROPE
MLA_QK_PAD = 256

DN_HEADS = 8
DN_DK = 128
DN_DV = 128
DN_CHUNK = 64

LANE = 128
SUBLANE = 8
ROPE_PAD = 128
VMEM_LIMIT = 56 * 1024 * 1024

A_QA, A_KVA, A_KR, A_DA, A_DB, A_WIDTH = 0, 512, 1024, 1152, 1168, 1280
B_DQ, B_DK, B_DV, B_DZ, B_GA, B_GB, B_WIDTH = 0, 1024, 2048, 3072, 4096, 6144, 8192

FFN_TILE = 512
HALO16 = 16


def _params(sem, vmem=VMEM_LIMIT):
    return pltpu.CompilerParams(dimension_semantics=sem, vmem_limit_bytes=vmem)


def _nt_dot(a, b):
    return lax.dot_general(a, b, (((1,), (1,)), ((), ())), preferred_element_type=F32)


def _silu(x):
    return x * jax.nn.sigmoid(x)


def _ada_kernel(c_ref, w_ref, b_ref, o_ref):
    a = _silu(c_ref[...]).astype(BF16)
    o_ref[...] = jnp.dot(a, w_ref[...].astype(BF16), preferred_element_type=F32) + b_ref[...]


def _ada(cs, w, b):
    m, d = cs.shape
    n = w.shape[1]
    tn = 512
    return pl.pallas_call(
        _ada_kernel,
        grid=(n // tn,),
        in_specs=[pl.BlockSpec((m, d), lambda j: (0, 0)),
                  pl.BlockSpec((d, tn), lambda j: (0, j)),
                  pl.BlockSpec((1, tn), lambda j: (0, j))],
        out_specs=pl.BlockSpec((m, tn), lambda j: (0, j)),
        out_shape=jax.ShapeDtypeStruct((m, n), F32),
        compiler_params=_params(("parallel",)),
        name="ada",
    )(cs, w, b)


def _norm_mod(x, nw, shift, scale):
    ms = jnp.mean(x * x, axis=-1, keepdims=True)
    y = x * lax.rsqrt(ms + EPS) * nw
    return y * (1.0 + scale) + shift


def _nmm_kernel(x_ref, nw_ref, mod_ref, w_ref, o_ref, xn_ref, *, k_shift, k_scale):
    y = _norm_mod(x_ref[...], nw_ref[...], mod_ref[0, k_shift:k_shift + 1, :], mod_ref[0, k_scale:k_scale + 1, :])
    xn = y.astype(BF16)
    xn_ref[...] = xn
    o_ref[...] = jnp.dot(xn, w_ref[...], preferred_element_type=F32).astype(o_ref.dtype)


def _norm_mod_matmul(x, nw, mod, row_of_tile, w, out_dtype, tm, k_shift, k_scale, name):
    m, d = x.shape
    n = w.shape[1]
    kern = functools.partial(_nmm_kernel, k_shift=k_shift, k_scale=k_scale)
    return pl.pallas_call(
        kern,
        grid=(m // tm,),
        in_specs=[pl.BlockSpec((tm, d), lambda i: (i, 0)),
                  pl.BlockSpec((1, d), lambda i: (0, 0)),
                  pl.BlockSpec((1, 6, d), lambda i: (row_of_tile(i), 0, 0)),
                  pl.BlockSpec((d, n), lambda i: (0, 0))],
        out_specs=[pl.BlockSpec((tm, n), lambda i: (i, 0)), pl.BlockSpec((tm, d), lambda i: (i, 0))],
        out_shape=[jax.ShapeDtypeStruct((m, n), out_dtype), jax.ShapeDtypeStruct((m, d), BF16)],
        compiler_params=_params(("parallel",)),
        name=name,
    )(x, nw, mod, w)


def _mm_kernel(x_ref, w_ref, o_ref):
    o_ref[...] = jnp.dot(x_ref[...], w_ref[...], preferred_element_type=F32).astype(o_ref.dtype)


def _matmul(x, w, out_dtype, tm, tn, name):
    m, d = x.shape
    n = w.shape[1]
    return pl.pallas_call(
        _mm_kernel,
        grid=(m // tm, n // tn),
        in_specs=[pl.BlockSpec((tm, d), lambda i, j: (i, 0)),
                  pl.BlockSpec((d, tn), lambda i, j: (0, j))],
        out_specs=pl.BlockSpec((tm, tn), lambda i, j: (i, j)),
        out_shape=jax.ShapeDtypeStruct((m, n), out_dtype),
        compiler_params=_params(("parallel", "parallel")),
        name=name,
    )(x, w)


def _mla_proj_kernel(a_ref, cr_ref, sr_ref, cc_ref, sc_ref, qan_ref, kvan_ref, qn_ref, kn_ref, wq_ref, wkv_ref,
                     q_ref, k_ref, v_ref):
    tm = a_ref.shape[0]
    g = tm // GRID_W

    def table(row_ref, col_ref):
        r = jnp.broadcast_to(row_ref[...][:, None, :], (g, GRID_W, ROPE_PAD)).reshape(tm, ROPE_PAD)
        c = jnp.broadcast_to(col_ref[...][None], (g, GRID_W, ROPE_PAD)).reshape(tm, ROPE_PAD)
        return r + c

    cos = table(cr_ref, cc_ref)
    sin = table(sr_ref, sc_ref)

    def rope(r):
        return r * cos + pltpu.roll(r, 64, axis=1) * sin

    def rms_rows(t, w):
        return t * lax.rsqrt(jnp.mean(t * t, axis=-1, keepdims=True) + EPS) * w

    qa = rms_rows(a_ref[:, A_QA:A_QA + MLA_Q_RANK], qan_ref[...]).astype(BF16)
    q = jnp.dot(qa, wq_ref[...], preferred_element_type=F32)
    kva = rms_rows(a_ref[:, A_KVA:A_KVA + MLA_KV_RANK], kvan_ref[...]).astype(BF16)
    kv = jnp.dot(kva, wkv_ref[...], preferred_element_type=F32)
    kr = a_ref[:, A_KR:A_KR + ROPE_PAD]
    kr_ss = jnp.sum(kr * kr, axis=-1, keepdims=True)
    qn = qn_ref[...]
    kn = kn_ref[...]
    scale = MLA_QK ** -0.5 * math.log2(math.e)
    for h in range(MLA_HEADS):
        qh = q[:, h * MLA_QK_PAD:(h + 1) * MLA_QK_PAD]
        inv = lax.rsqrt(jnp.sum(qh * qh, axis=-1, keepdims=True) * (1.0 / MLA_QK) + EPS) * scale
        qh = qh * inv * qn
        q_ref[:, h * MLA_QK_PAD:h * MLA_QK_PAD + MLA_NOPE] = qh[:, :MLA_NOPE].astype(BF16)
        q_ref[:, h * MLA_QK_PAD + MLA_NOPE:(h + 1) * MLA_QK_PAD] = rope(qh[:, MLA_NOPE:]).astype(BF16)

        kh = kv[:, h * MLA_NOPE:(h + 1) * MLA_NOPE]
        inv = lax.rsqrt((jnp.sum(kh * kh, axis=-1, keepdims=True) + kr_ss) * (1.0 / MLA_QK) + EPS)
        k_ref[:, h * MLA_QK_PAD:h * MLA_QK_PAD + MLA_NOPE] = (kh * inv * kn[:, :MLA_NOPE]).astype(BF16)
        k_ref[:, h * MLA_QK_PAD + MLA_NOPE:(h + 1) * MLA_QK_PAD] = rope(
            kr * inv * kn[:, MLA_NOPE:]).astype(BF16)
    v_ref[...] = kv[:, MLA_HEADS * MLA_NOPE:].astype(BF16)


def _mla_proj(a, tables, qan, kvan, qn, kn, wq, wkv, tm, seq):
    m = a.shape[0]
    nseq = seq // tm
    row_cos, row_sin, col_cos, col_sin = tables
    full = lambda arr: pl.BlockSpec(arr.shape, lambda i: (0, 0))
    row_spec = pl.BlockSpec((tm // GRID_W, ROPE_PAD), lambda i: (i % nseq, 0))
    return pl.pallas_call(
        _mla_proj_kernel,
        grid=(m // tm,),
        in_specs=[pl.BlockSpec((tm, A_WIDTH), lambda i: (i, 0)),
                  row_spec, row_spec, full(col_cos), full(col_sin),
                  full(qan), full(kvan), full(qn), full(kn), full(wq), full(wkv)],
        out_specs=[pl.BlockSpec((tm, MLA_HEADS * MLA_QK_PAD), lambda i: (i, 0)),
                   pl.BlockSpec((tm, MLA_HEADS * MLA_QK_PAD), lambda i: (i, 0)),
                   pl.BlockSpec((tm, MLA_HEADS * MLA_V), lambda i: (i, 0))],
        out_shape=[jax.ShapeDtypeStruct((m, MLA_HEADS * MLA_QK_PAD), BF16),
                   jax.ShapeDtypeStruct((m, MLA_HEADS * MLA_QK_PAD), BF16),
                   jax.ShapeDtypeStruct((m, MLA_HEADS * MLA_V), BF16)],
        compiler_params=_params(("parallel",)),
        name="mla_proj",
    )(a, row_cos, row_sin, col_cos, col_sin, qan, kvan, qn, kn, wq, wkv)


ATTN_UNROLL = 8
ATTN_ROWS = 64


def _attn_kernel(q_ref, kc_ref, vct_ref, k_ref, vt_ref, o_ref, qt_ref, s0_ref, s1_ref, acc_ref, *, nk, unroll):
    qt_ref[...] = q_ref[0].T
    tq = qt_ref.shape[1]
    tk = s0_ref.shape[0]
    rows = ATTN_ROWS
    s_refs = (s0_ref, s1_ref)

    def scores(j):
        return jnp.dot(k_ref[0, pl.ds(pl.multiple_of(j * tk, tk), tk), :], qt_ref[...], preferred_element_type=F32)

    def probs(s, m):
        return jnp.exp2((s - m).astype(BF16))

    def store_scores(slot, s):
        s_refs[slot][...] = s
        return jnp.max(s.reshape(tk // SUBLANE, SUBLANE, tq), axis=0)

    s = jnp.dot(kc_ref[0], qt_ref[...], preferred_element_type=F32)
    m8_first = store_scores(0, scores(0))
    m = jnp.max(s, axis=0, keepdims=True)
    acc_ref[...] = jnp.dot(vct_ref[0, 0], probs(s, m), preferred_element_type=F32)

    def tile(j, slot, carry):
        m, m8 = carry
        m8_next = store_scores(1 - slot, scores(jnp.minimum(j + 1, nk - 1)))
        m_new = jnp.maximum(m, jnp.max(m8, axis=0, keepdims=True))
        p = jnp.concatenate([probs(s_refs[slot][r:r + rows, :], m_new) for r in range(0, tk, rows)], axis=0)
        acc_ref[...] = jnp.exp2(m - m_new) * acc_ref[...] + jnp.dot(vt_ref[0, 0, j], p, preferred_element_type=F32)
        return m_new, m8_next

    def body(i, carry):
        for u in range(unroll):
            carry = tile(unroll * i + u, u % 2, carry)
        return carry

    lax.fori_loop(0, nk // unroll, body, (m, m8_first))
    acc = acc_ref[...]
    o_ref[0] = (acc[:MLA_V] * (1.0 / acc[MLA_V:MLA_V + 1])).T.astype(o_ref.dtype)


def _attention(q, k, vt, kc, vct, tq):
    b, t, _ = k.shape
    lc = kc.shape[1]
    nk, vrows, tk = vt.shape[2], vt.shape[3], vt.shape[4]
    unroll = ATTN_UNROLL if nk % ATTN_UNROLL == 0 else 2
    assert nk % unroll == 0
    kern = functools.partial(_attn_kernel, nk=nk, unroll=unroll)
    return pl.pallas_call(
        kern,
        grid=(b, MLA_HEADS, t // tq),
        in_specs=[pl.BlockSpec((1, tq, MLA_QK_PAD), lambda bi, h, i: (bi, i, h)),
                  pl.BlockSpec((1, lc, MLA_QK_PAD), lambda bi, h, i: (bi, 0, h)),
                  pl.BlockSpec((1, 1, vrows, lc), lambda bi, h, i: (bi, h, 0, 0)),
                  pl.BlockSpec((1, t, MLA_QK_PAD), lambda bi, h, i: (bi, 0, h)),
                  pl.BlockSpec((1, 1, nk, vrows, tk), lambda bi, h, i: (bi, h, 0, 0, 0))],
        out_specs=pl.BlockSpec((1, tq, MLA_V), lambda bi, h, i: (bi, i, h)),
        out_shape=jax.ShapeDtypeStruct((b, t, MLA_HEADS * MLA_V), BF16),
        scratch_shapes=[pltpu.VMEM((MLA_QK_PAD, tq), BF16), pltpu.VMEM((tk, tq), F32), pltpu.VMEM((tk, tq), F32),
                        pltpu.VMEM((vrows, tq), F32)],
        compiler_params=_params(("parallel", "parallel", "arbitrary")),
        name="attention",
    )(q, kc, vct, k, vt)


def _split2(x):
    hi = x.astype(BF16)
    return hi, (x - hi.astype(F32)).astype(BF16)


def _bmm(a, b):
    return lax.dot_general(a, b, (((2,), (1,)), ((0,), (0,))), preferred_element_type=F32)


def _bmm_nt(a, b):
    return lax.dot_general(a, b, (((2,), (2,)), ((0,), (0,))), preferred_element_type=F32)


def _bmm_hp(a, b):
    ah, al = _split2(a)
    bh, bl = _split2(b)
    return _bmm(ah, bh) + _bmm(al, bh) + _bmm(ah, bl)


def _bd2(b):
    lane = lax.broadcasted_iota(jnp.int32, b.shape, 2)
    zero = jnp.zeros_like(b)
    return jnp.concatenate([jnp.where(lane < DN_CHUNK, b, zero), jnp.where(lane >= DN_CHUNK, b, zero)], axis=1)


def _tri_inverse(m, eye):
    assert m.shape[-2] == 64
    x = eye - m
    nb = (-m).astype(BF16)
    bd = _bd2(nb)
    for _ in range(3):
        nb = _bmm(nb, bd).astype(BF16)
        bd = _bd2(nb)
        x = x + _bmm(x.astype(BF16), bd)
    xb = x.astype(BF16)
    r = eye - x - _bmm(m.astype(BF16), _bd2(xb))
    x = x + _bmm(xb, _bd2(r.astype(BF16)))
    mh, ml = _split2(m)
    xh, xl = _split2(x)
    bdh = _bd2(xh)
    r = eye - x - (_bmm(mh, bdh) + _bmm(ml, bdh) + _bmm(mh, _bd2(xl)))
    return x + _bmm(xh, _bd2(r.astype(BF16)))


def _dn_prep_kernel(qm_ref, qp_ref, qx_ref, km_ref, kp_ref, kx_ref, vm_ref, vp_ref, vx_ref,
                    cwq_ref, cwk_ref, cwv_ref, ab_ref, hp_ref,
                    u_ref, w_ref, ke_ref, qg_ref, a_ref, ge_ref, ext_ref, *, tm, tiles_per_seq):
    i = pl.program_id(0)
    first = (i % tiles_per_seq) == 0
    last = (i % tiles_per_seq) == tiles_per_seq - 1
    c = DN_CHUNK

    def conv_silu(main_ref, prev_ref, next_ref, cw_ref):
        ext_ref[0:HALO16, :] = jnp.where(first, 0.0, prev_ref[...].astype(F32))
        ext_ref[HALO16:HALO16 + tm, :] = main_ref[...].astype(F32)
        ext_ref[HALO16 + tm:, :] = jnp.where(last, 0.0, next_ref[...].astype(F32))
        y = (cw_ref[0:1, :] * ext_ref[HALO16 - 1:HALO16 - 1 + tm, :]
             + cw_ref[1:2, :] * ext_ref[HALO16:HALO16 + tm, :]
             + cw_ref[2:3, :] * ext_ref[HALO16 + 1:HALO16 + 1 + tm, :])
        return _silu(y)

    def l2n(t):
        return t * lax.rsqrt(jnp.sum(t * t, axis=-1, keepdims=True) + EPS)

    q = l2n(conv_silu(qm_ref, qp_ref, qx_ref, cwq_ref)) * (DN_DK ** -0.5)
    k = l2n(conv_silu(km_ref, kp_ref, kx_ref, cwk_ref))
    v = conv_silu(vm_ref, vp_ref, vx_ref, cwv_ref)

    hp = hp_ref[0]
    ab = ab_ref[0]
    z = ab[:, 0:2] + hp[:, 2:4]
    softplus = jnp.maximum(z, 0.0) + jnp.log(1.0 + jnp.exp(-jnp.abs(z)))
    g_all = -jnp.exp(hp[:, 0:2]) * softplus
    beta_all = jax.nn.sigmoid(ab[:, 2:4])

    nc = tm // c

    def per_chunk(t):
        return t.reshape(nc, c, t.shape[-1])

    def all_lanes(t2, d):
        return per_chunk(jnp.broadcast_to(t2[:, d:d + 1], (tm, LANE)))

    shape = (nc, c, LANE)
    rows = lax.broadcasted_iota(jnp.int32, shape, 1)
    lane = lax.broadcasted_iota(jnp.int32, shape, 2)
    bwd = lane >= c
    col = jnp.where(bwd, lane - c, lane)
    eye_mask = rows == col
    ahead = jnp.where(bwd, rows - col, col - rows)
    incl = ahead <= 0
    strict = ahead < 0
    eye = jnp.where(eye_mask, 1.0, 0.0)
    tri = jnp.where(incl, 1.0, 0.0).astype(BF16)

    def halves(t2):
        return jnp.where(bwd, all_lanes(t2, 1), all_lanes(t2, 0))

    g_hi, g_lo = _split2(halves(g_all))
    gc = _bmm(tri, _bd2(g_hi)) + _bmm(tri, _bd2(g_lo))
    gc_row = jnp.sum(jnp.where(eye_mask, gc, 0.0), axis=1, keepdims=True)
    decay = jnp.where(incl, jnp.exp(jnp.where(incl, gc - gc_row, 0.0)), 0.0)

    k3 = per_chunk(k)
    q3 = per_chunk(q)
    v3 = per_chunk(v)
    k3b = k3.astype(BF16)
    kdup = jnp.concatenate([k3b, k3b], axis=1)
    kk = _bmm_nt(k3b, kdup)
    att = _bmm_nt(q3.astype(BF16), kdup) * decay
    m = jnp.where(strict, halves(beta_all) * kk * decay, 0.0)
    t_hi, t_lo = _split2(_tri_inverse(m, eye))

    swapped = pltpu.roll(gc.reshape(tm, LANE), c, axis=1).reshape(shape)
    gc_dir = (jnp.where(bwd, swapped, gc), jnp.where(bwd, gc, swapped))
    rhs, ke, qg, gend = [], [], [], []
    for d in range(2):
        gcd = gc_dir[d]
        beta = all_lanes(beta_all, d)
        eg = jnp.exp(gcd)
        gtot = gcd[:, c - 1:c, :] if d == 0 else gcd[:, 0:1, :]
        rhs.append(jnp.concatenate([v3 * beta, k3 * beta * eg], axis=2))
        ke.append(k3 * jnp.exp(gtot - gcd))
        qg.append(q3 * eg)
        gend.append(jnp.exp(gtot))
    rhs = jnp.concatenate(rhs, axis=1).astype(BF16)
    sol = _bmm(_bd2(t_hi), rhs) + _bmm(_bd2(t_lo), rhs)
    att_dir = (att, pltpu.roll(att.reshape(tm, LANE), c, axis=1).reshape(shape))
    for d in range(2):
        sl = slice(d * c, (d + 1) * c)
        u_ref[d, 0, 0] = sol[:, sl, :DN_DV].reshape(tm, DN_DV)
        w_ref[d, 0, 0] = sol[:, sl, DN_DV:].reshape(tm, DN_DK).astype(BF16)
        ke_ref[d, 0, 0] = ke[d].reshape(tm, DN_DK).astype(BF16)
        qg_ref[d, 0, 0] = qg[d].reshape(tm, DN_DK).astype(BF16)
        a_ref[d, 0, 0] = att_dir[d][:, :, :c].reshape(tm, c).astype(BF16)
        ge_ref[d, 0, 0] = gend[d]


def _dn_prep(p, ab, hp, conv_w, batch, seq, tm):
    h = DN_HEADS
    nt = seq // tm
    nb16 = tm // HALO16
    total16 = batch * seq // HALO16
    kern = functools.partial(_dn_prep_kernel, tm=tm, tiles_per_seq=nt)

    def triple(col0):
        return [pl.BlockSpec((tm, LANE), lambda i, hh: (i, col0 + hh)),
                pl.BlockSpec((HALO16, LANE), lambda i, hh: (jnp.maximum(i * nb16 - 1, 0), col0 + hh)),
                pl.BlockSpec((HALO16, LANE), lambda i, hh: (jnp.minimum((i + 1) * nb16, total16 - 1), col0 + hh))]

    cq, ck, cv = B_DQ // LANE, B_DK // LANE, B_DV // LANE
    row = lambda shape: pl.BlockSpec(shape, lambda i, hh: (0, i // nt, hh, i % nt, 0))
    seq_shape = lambda width, dt: jax.ShapeDtypeStruct((2, batch, h, seq, width), dt)
    return pl.pallas_call(
        kern,
        grid=(batch * nt, h),
        in_specs=triple(cq) + triple(ck) + triple(cv) + [
            pl.BlockSpec((3, LANE), lambda i, hh: (0, hh)),
            pl.BlockSpec((3, LANE), lambda i, hh: (0, h + hh)),
            pl.BlockSpec((3, LANE), lambda i, hh: (0, 2 * h + hh)),
            pl.BlockSpec((1, tm, 4), lambda i, hh: (hh, i, 0)),
            pl.BlockSpec((1, 1, 4), lambda i, hh: (hh, 0, 0))],
        out_specs=[row((2, 1, 1, tm, DN_DV)), row((2, 1, 1, tm, DN_DK)), row((2, 1, 1, tm, DN_DK)),
                   row((2, 1, 1, tm, DN_DK)), row((2, 1, 1, tm, DN_CHUNK)),
                   pl.BlockSpec((2, 1, 1, tm // DN_CHUNK, 1, LANE), lambda i, hh: (0, i // nt, hh, i % nt, 0, 0))],
        out_shape=[seq_shape(DN_DV, F32), seq_shape(DN_DK, BF16), seq_shape(DN_DK, BF16),
                   seq_shape(DN_DK, BF16), seq_shape(DN_CHUNK, BF16),
                   jax.ShapeDtypeStruct((2, batch, h, seq // DN_CHUNK, 1, LANE), F32)],
        scratch_shapes=[pltpu.VMEM((tm + 2 * HALO16, LANE), F32)],
        compiler_params=_params(("parallel", "parallel")),
        name="dn_prep",
    )(p, p, p, p, p, p, p, p, p, conv_w, conv_w, conv_w, ab, hp)


def _dn_scan_kernel(s0_ref, uf_ref, wf_ref, kf_ref, qf_ref, af_ref, gf_ref,
                    ub_ref, wb_ref, kb_ref, qb_ref, ab_ref, gb_ref,
                    of_ref, ob_ref, sfin_ref, s_ref, *, cs):
    n = pl.program_id(1)
    c = DN_CHUNK
    h = DN_HEADS

    @pl.when(n == 0)
    def _():
        s_ref[0:h] = s0_ref[0, 0]
        s_ref[h:] = s0_ref[1, 0]

    for ci in range(cs):
        rf = ci * c
        rb = (cs - 1 - ci) * c

        def pair(f_ref, b_ref):
            return jnp.concatenate([f_ref[0, 0, :, rf:rf + c, :], b_ref[0, 0, :, rb:rb + c, :]], axis=0)

        s = s_ref[...]
        s16 = s.astype(BF16)
        ws_qs = _bmm(jnp.concatenate([pair(wf_ref, wb_ref), pair(qf_ref, qb_ref)], axis=1), s16)
        v_new = pair(uf_ref, ub_ref) - ws_qs[:, :c]
        v16 = v_new.astype(BF16)
        o = ws_qs[:, c:] + _bmm(pair(af_ref, ab_ref), v16)
        of_ref[0, :, rf:rf + c, :] = o[:h].astype(of_ref.dtype)
        ob_ref[0, :, rb:rb + c, :] = o[h:].astype(ob_ref.dtype)
        g = jnp.concatenate([gf_ref[0, 0, :, ci], gb_ref[0, 0, :, cs - 1 - ci]], axis=0)
        s_ref[...] = s * g + lax.dot_general(pair(kf_ref, kb_ref), v16, (((1,), (1,)), ((0,), (0,))),
                                             preferred_element_type=F32)

    @pl.when(n == pl.num_programs(1) - 1)
    def _():
        sfin_ref[0, 0] = s_ref[0:h]
        sfin_ref[1, 0] = s_ref[h:]


def _dn_scan(s0, u, w, ke, qg, a, ge, cs):
    _, batch, h, seq, _ = u.shape
    ts = cs * DN_CHUNK
    ns = seq // ts
    kern = functools.partial(_dn_scan_kernel, cs=cs)

    def specs(d):
        idx = (lambda n: n) if d == 0 else (lambda n: ns - 1 - n)
        seqs = [pl.BlockSpec((1, 1, h, ts, width), lambda b, n: (d, b, 0, idx(n), 0))
                for width in (DN_DV, DN_DK, DN_DK, DN_DK, DN_CHUNK)]
        return seqs + [pl.BlockSpec((1, 1, h, cs, 1, LANE), lambda b, n: (d, b, 0, idx(n), 0, 0))]

    o_spec = lambda d: pl.BlockSpec((1, h, ts, DN_DV), lambda b, n: (b, 0, n if d == 0 else ns - 1 - n, 0))
    state = pl.BlockSpec((2, 1, h, DN_DK, DN_DV), lambda b, n: (0, b, 0, 0, 0))
    return pl.pallas_call(
        kern,
        grid=(batch, ns),
        in_specs=[state] + specs(0) + specs(1),
        out_specs=[o_spec(0), o_spec(1), state],
        out_shape=[jax.ShapeDtypeStruct((batch, h, seq, DN_DV), BF16),
                   jax.ShapeDtypeStruct((batch, h, seq, DN_DV), BF16),
                   jax.ShapeDtypeStruct((2, batch, h, DN_DK, DN_DV), F32)],
        scratch_shapes=[pltpu.VMEM((2 * h, DN_DK, DN_DV), F32)],
        compiler_params=_params(("parallel", "arbitrary")),
        name="dn_scan",
    )(s0, u, w, ke, qg, a, ge, u, w, ke, qg, a, ge)


def _merge_kernel(x_ref, mod_ref, ym_ref, of_ref, ob_ref, z_ref, nw_ref, ga_ref, gb_ref, wm_ref, wd_ref, wo_ref,
                  o_ref, *, k_gate):
    nw = nw_ref[...]
    heads = []
    for h in range(DN_HEADS):
        o = of_ref[0, h].astype(F32) + ob_ref[0, h].astype(F32)
        y = o * lax.rsqrt(jnp.mean(o * o, axis=-1, keepdims=True) + EPS) * nw
        heads.append((y * _silu(z_ref[:, h * DN_DV:(h + 1) * DN_DV].astype(F32))).astype(BF16))
    yd = jnp.concatenate(heads, axis=1)
    pm = jnp.dot(ym_ref[...], wm_ref[...], preferred_element_type=F32)
    pd = jnp.dot(yd, wd_ref[...], preferred_element_type=F32)
    merged = (jax.nn.sigmoid(ga_ref[...].astype(F32)) * pm
              + jax.nn.sigmoid(gb_ref[...].astype(F32)) * pd).astype(BF16)
    y = jnp.dot(merged, wo_ref[...], preferred_element_type=F32)
    o_ref[...] = x_ref[...] + mod_ref[0, k_gate:k_gate + 1, :] * y


def _merge(x, mod, row_of_tile, ym, o_f, o_b, p, nw, wm, wd, wo, tm, k_gate):
    m, d = x.shape
    _, h, seq, _ = o_f.shape
    nt = seq // tm
    zw = h * DN_DV
    kern = functools.partial(_merge_kernel, k_gate=k_gate)
    const = lambda arr: pl.BlockSpec(arr.shape, lambda i: (0, 0), pipeline_mode=pl.Buffered(1))
    scan_out = pl.BlockSpec((1, h, tm, DN_DV), lambda i: (i // nt, 0, i % nt, 0))
    return pl.pallas_call(
        kern,
        grid=(m // tm,),
        in_specs=[pl.BlockSpec((tm, d), lambda i: (i, 0)),
                  pl.BlockSpec((1, 6, d), lambda i: (row_of_tile(i), 0, 0)),
                  pl.BlockSpec((tm, ym.shape[1]), lambda i: (i, 0)),
                  scan_out, scan_out,
                  pl.BlockSpec((tm, zw), lambda i: (i, B_DZ // zw)),
                  pl.BlockSpec((1, DN_DV), lambda i: (0, 0)),
                  pl.BlockSpec((tm, d), lambda i: (i, B_GA // d)),
                  pl.BlockSpec((tm, d), lambda i: (i, B_GB // d)),
                  const(wm), const(wd), const(wo)],
        out_specs=pl.BlockSpec((tm, d), lambda i: (i, 0)),
        out_shape=jax.ShapeDtypeStruct((m, d), F32),
        compiler_params=_params(("parallel",)),
        name="merge",
    )(x, mod, ym, o_f, o_b, p, nw, p, p, wm, wd, wo)


def _ffn_kernel(x_ref, xp_ref, xx_ref, nw_ref, mod_ref, wg_ref, wv_ref, cw_ref, wdn_ref, o_ref, xn_ref, ge_ref,
                *, tm, tiles_per_seq, k_shift, k_scale, k_gate):
    i = pl.program_id(0)
    j = pl.program_id(1)
    th = FFN_TILE

    @pl.when(j == 0)
    def _():
        first = (i % tiles_per_seq) == 0
        last = (i % tiles_per_seq) == tiles_per_seq - 1
        nw = nw_ref[...]
        sh = mod_ref[0, k_shift:k_shift + 1, :]
        sc = mod_ref[0, k_scale:k_scale + 1, :]
        xn_ref[0:HALO16, :] = jnp.where(first, 0.0, _norm_mod(xp_ref[...], nw, sh, sc)).astype(BF16)
        xn_ref[HALO16:HALO16 + tm, :] = _norm_mod(x_ref[...], nw, sh, sc).astype(BF16)
        xn_ref[HALO16 + tm:, :] = jnp.where(last, 0.0, _norm_mod(xx_ref[...], nw, sh, sc)).astype(BF16)
        o_ref[...] = jnp.zeros_like(o_ref)

    ge_ref[...] = jnp.dot(xn_ref[...], wg_ref[...], preferred_element_type=F32)
    val = jnp.dot(xn_ref[HALO16:HALO16 + tm, :], wv_ref[...], preferred_element_type=F32)
    conv = (cw_ref[0:1, :] * ge_ref[HALO16 - 1:HALO16 - 1 + tm, :]
            + cw_ref[1:2, :] * ge_ref[HALO16:HALO16 + tm, :]
            + cw_ref[2:3, :] * ge_ref[HALO16 + 1:HALO16 + 1 + tm, :])
    hid = (_silu(conv) * val).astype(BF16)
    o_ref[...] += jnp.dot(hid, wdn_ref[...], preferred_element_type=F32)

    @pl.when(j == pl.num_programs(1) - 1)
    def _():
        o_ref[...] = x_ref[...] + mod_ref[0, k_gate:k_gate + 1, :] * o_ref[...]


def _ffn(x, nw, mod, row_of_tile, wg, wv, cw, wdn, seq, tm, k_shift, k_scale, k_gate):
    m, d = x.shape
    th = FFN_TILE
    nj = wdn.shape[0] // th
    nt = seq // tm
    nb16 = tm // HALO16
    total16 = m // HALO16
    kern = functools.partial(_ffn_kernel, tm=tm, tiles_per_seq=nt, k_shift=k_shift, k_scale=k_scale,
                             k_gate=k_gate)
    return pl.pallas_call(
        kern,
        grid=(m // tm, nj),
        in_specs=[pl.BlockSpec((tm, d), lambda i, j: (i, 0)),
                  pl.BlockSpec((HALO16, d), lambda i, j: (jnp.maximum(i * nb16 - 1, 0), 0)),
                  pl.BlockSpec((HALO16, d), lambda i, j: (jnp.minimum((i + 1) * nb16, total16 - 1), 0)),
                  pl.BlockSpec((1, d), lambda i, j: (0, 0)),
                  pl.BlockSpec((1, 6, d), lambda i, j: (row_of_tile(i), 0, 0)),
                  pl.BlockSpec((d, th), lambda i, j: (0, j)),
                  pl.BlockSpec((d, th), lambda i, j: (0, j)),
                  pl.BlockSpec((3, th), lambda i, j: (0, j)),
                  pl.BlockSpec((th, d), lambda i, j: (j, 0))],
        out_specs=pl.BlockSpec((tm, d), lambda i, j: (i, 0)),
        out_shape=jax.ShapeDtypeStruct((m, d), F32),
        scratch_shapes=[pltpu.VMEM((tm + 2 * HALO16, d), BF16), pltpu.VMEM((tm + 2 * HALO16, th), F32)],
        compiler_params=_params(("parallel", "arbitrary")),
        name="ffn",
    )(x, x, x, nw, mod, wg, wv, cw, wdn)


def _rope_pad_cols(r):
    n = MLA_ROPE // 4
    z = jnp.zeros(r.shape[:-1] + (2 * n,), r.dtype)
    return jnp.concatenate([r[..., 0:n], r[..., 2 * n:3 * n], z, r[..., n:2 * n], r[..., 3 * n:4 * n], z], axis=-1)


def _qk_pad_cols(w):
    lead = w.shape[:-1]
    w = w.reshape(lead + (MLA_HEADS, MLA_QK))
    out = jnp.concatenate([w[..., :MLA_NOPE], _rope_pad_cols(w[..., MLA_NOPE:])], axis=-1)
    return out.reshape(lead + (MLA_HEADS * MLA_QK_PAD,))


def _rope_tables(rows):
    n = MLA_ROPE // 4
    inv = ROPE_BASE ** (-jnp.arange(n, dtype=F32) / n)
    ang_r = jnp.arange(rows, dtype=F32)[:, None] * inv
    ang_c = jnp.arange(GRID_W, dtype=F32)[:, None] * inv

    def lanes(a_row, a_col, fill, count):
        return jnp.concatenate([a_row, a_col, jnp.full((count, 2 * n), fill, F32)] * 2, axis=-1)

    zr, zc = jnp.zeros((rows, n), F32), jnp.zeros((GRID_W, n), F32)
    row_cos = lanes(jnp.cos(ang_r), zr, 0.0, rows)
    row_sin = jnp.concatenate([-jnp.sin(ang_r), zr, jnp.zeros((rows, 2 * n), F32),
                               jnp.sin(ang_r), zr, jnp.zeros((rows, 2 * n), F32)], axis=-1)
    col_cos = lanes(zc, jnp.cos(ang_c), 1.0, GRID_W)
    col_sin = jnp.concatenate([zc, -jnp.sin(ang_c), jnp.zeros((GRID_W, 2 * n), F32),
                               zc, jnp.sin(ang_c), jnp.zeros((GRID_W, 2 * n), F32)], axis=-1)
    return row_cos, row_sin, col_cos, col_sin


def _pick_tile(n, pref):
    t = min(pref, n)
    while n % t:
        t //= 2
    return t


def kernel(x, c, ctx, c_ctx, w_ada, b_ada, norm_mix, w_in, q_a_norm, w_q_b, kv_a_norm, w_kv_b, q_norm, k_norm,
           w_o_mla, dn_conv, dn_a_log, dn_dt_bias, dn_o_norm, w_o_dn, w_out, norm_ffn, w_ffn_up, ffn_conv,
           w_ffn_down):
    batch, seq, d = x.shape
    lc = ctx.shape[1]
    assert w_ada.shape[0] == 1, "single-layer stack"
    assert seq % GRID_W == 0 and seq % DN_CHUNK == 0 and lc % DN_CHUNK == 0
    h = DN_HEADS
    l = 0

    wi = w_in[l]
    o_qa, o_kva, o_kr = 0, MLA_Q_RANK, MLA_Q_RANK + MLA_KV_RANK
    o_dq = o_kr + MLA_ROPE
    o_da = o_dq + 4 * h * DN_DK
    o_ga = o_da + 4 * h
    w_a = jnp.concatenate([wi[:, o_qa:o_kr], _rope_pad_cols(wi[:, o_kr:o_dq]), wi[:, o_da:o_ga],
                           jnp.zeros((d, A_WIDTH - A_DB - 2 * h), F32)], axis=1).astype(BF16)
    w_b = jnp.concatenate([wi[:, o_dq:o_da], wi[:, o_ga:]], axis=1).astype(BF16)
    wq = _qk_pad_cols(w_q_b[l]).astype(BF16)
    wkv = w_kv_b[l].reshape(MLA_KV_RANK, MLA_HEADS, MLA_NOPE + MLA_V)
    wkv = jnp.concatenate([wkv[:, :, :MLA_NOPE].reshape(MLA_KV_RANK, -1),
                           wkv[:, :, MLA_NOPE:].reshape(MLA_KV_RANK, -1)], axis=1).astype(BF16)
    qn = jnp.concatenate([q_norm[l, :MLA_NOPE], _rope_pad_cols(q_norm[l, MLA_NOPE:])])[None, :]
    kn = jnp.concatenate([k_norm[l, :MLA_NOPE], _rope_pad_cols(k_norm[l, MLA_NOPE:])])[None, :]
    hidden = w_ffn_down.shape[1]
    nj = -(-hidden // FFN_TILE)
    hpad = nj * FFN_TILE - hidden
    wg = jnp.pad(w_ffn_up[l][:, :hidden].astype(BF16), ((0, 0), (0, hpad)))
    wv = jnp.pad(w_ffn_up[l][:, hidden:].astype(BF16), ((0, 0), (0, hpad)))
    fcw = jnp.pad(ffn_conv[l], ((0, 0), (0, hpad)))
    wdn = jnp.pad(w_ffn_down[l], ((0, hpad), (0, 0))).astype(BF16)
    hp = jnp.concatenate([dn_a_log[l].T, dn_dt_bias[l].T], axis=1)[:, None, :]
    tables_l = _rope_tables(seq // GRID_W)
    tables_c = (jnp.zeros((lc // GRID_W, ROPE_PAD), F32), jnp.zeros((lc // GRID_W, ROPE_PAD), F32),
                jnp.ones((GRID_W, ROPE_PAD), F32), jnp.zeros((GRID_W, ROPE_PAD), F32))

    rows = jnp.concatenate([c, c_ctx[None, :], jnp.zeros((8 - batch - 1, d), F32)], axis=0)
    mod = _ada(rows, w_ada[l], b_ada[l][None, :]).reshape(8, 6, d)

    xf = x.reshape(batch * seq, d)
    cf = ctx.reshape(batch * lc, d)

    def stream(tokens, n, tm_a, tm_b, row_of):
        nm = norm_mix[l][None, :]
        a, xn = _norm_mod_matmul(tokens, nm, mod, row_of(tm_a), w_a, F32, tm_a, 0, 1, "in_proj_small")
        p = _matmul(xn, w_b, BF16, tm_b, 1024, "in_proj_wide")
        ab = a[:, A_DA:A_DA + 4 * h].reshape(batch * n, 4, h).transpose(2, 0, 1)
        return a, p, ab

    tm_a = _pick_tile(seq, 512)
    tm_b = _pick_tile(seq, 1024)
    lat_row = lambda tm: (lambda i: i // (seq // tm))
    a_l, p_l, ab_l = stream(xf, seq, tm_a, tm_b, lat_row)
    tc = _pick_tile(lc, 256)
    ctx_row = lambda tm: (lambda i: batch)
    a_c, p_c, ab_c = stream(cf, lc, tc, tc, ctx_row)

    qan, kvan = q_a_norm[l][None, :], kv_a_norm[l][None, :]
    tmp = _pick_tile(seq, 512)
    q_l, k_l, v_l = _mla_proj(a_l, tables_l, qan, kvan, qn, kn, wq, wkv, tmp, seq)
    _, k_c, v_c = _mla_proj(a_c, tables_c, qan, kvan, qn, kn, wq, wkv, tc, lc)
    tq = _pick_tile(seq, 512)
    tk = _pick_tile(seq, 512)

    def with_ones_row(vt):
        lead, n = vt.shape[:-2], vt.shape[-1]
        return jnp.concatenate([vt, jnp.ones(lead + (1, n), BF16), jnp.zeros(lead + (HALO16 - 1, n), BF16)],
                               axis=-2)

    vt = with_ones_row(v_l.reshape(batch, seq // tk, tk, MLA_HEADS, MLA_V).transpose(0, 3, 1, 4, 2))
    vct = with_ones_row(v_c.reshape(batch, lc, MLA_HEADS, MLA_V).transpose(0, 2, 3, 1))
    y_mla = _attention(q_l.reshape(batch, seq, -1), k_l.reshape(batch, seq, -1), vt, k_c.reshape(batch, lc, -1), vct, tq)
    y_mla = y_mla.reshape(batch * seq, MLA_HEADS * MLA_V)

    conv_w = dn_conv[l]
    prep_c = _dn_prep(p_c, ab_c, hp, conv_w, batch, lc, _pick_tile(lc, 256))
    s_zero = jnp.zeros((2, batch, h, DN_DK, DN_DV), F32)
    _, _, s_ctx = _dn_scan(s_zero, *prep_c, _pick_tile(lc, 256) // DN_CHUNK)
    prep_l = _dn_prep(p_l, ab_l, hp, conv_w, batch, seq, _pick_tile(seq, 1024))
    o_f, o_b, _ = _dn_scan(s_ctx, *prep_l, _pick_tile(seq, 256) // DN_CHUNK)

    tmm = _pick_tile(seq, 256)
    x1 = _merge(xf, mod, lat_row(tmm), y_mla, o_f, o_b, p_l, dn_o_norm[l][None, :], w_o_mla[l].astype(BF16),
                w_o_dn[l].astype(BF16), w_out[l].astype(BF16), tmm, 2)

    tmf = _pick_tile(seq, 512)
    out = _ffn(x1, norm_ffn[l][None, :], mod, lat_row(tmf), wg, wv, fcw, wdn, seq, tmf, 3, 4, 5)
    return out.reshape(batch, seq, d)
```

```python
import functools
import math

import jax
import jax.numpy as jnp
import numpy as np
from jax import lax
from jax.experimental import pallas as pl
from jax.experimental.pallas import tpu as pltpu

F32 = jnp.float32
BF16 = jnp.bfloat16
HIGHEST = lax.Precision.HIGHEST

EPS = 1e-6
GRID_W = 64
ROPE_BASE = 10000.0

MLA_HEADS = 8
MLA_Q_RANK = 512
MLA_KV_RANK = 512
MLA_NOPE = 128
MLA_ROPE = 64
MLA_V = 128
MLA_QK = MLA_NOPE + MLA_ROPE
MLA_QK_PAD = 256

DN_HEADS = 8
DN_DK = 128
DN_DV = 128
DN_CHUNK = 64

LANE = 128
SUBLANE = 8
ROPE_PAD = 128
VMEM_LIMIT = 56 * 1024 * 1024

A_QA, A_KVA, A_KR, A_DA, A_DB, A_WIDTH = 0, 512, 1024, 1152, 1168, 1280
B_DQ, B_DK, B_DV, B_DZ, B_GA, B_GB, B_WIDTH = 0, 1024, 2048, 3072, 4096, 6144, 8192

FFN_TILE = 512
HALO16 = 16


def _params(sem, vmem=VMEM_LIMIT):
    return pltpu.CompilerParams(dimension_semantics=sem, vmem_limit_bytes=vmem)


def _nt_dot(a, b):
    return lax.dot_general(a, b, (((1,), (1,)), ((), ())), preferred_element_type=F32)


def _silu(x):
    return x * jax.nn.sigmoid(x)


def _ada_kernel(c_ref, w_ref, b_ref, o_ref):
    a = _silu(c_ref[...]).astype(BF16)
    o_ref[...] = jnp.dot(a, w_ref[...].astype(BF16), preferred_element_type=F32) + b_ref[...]


def _ada(cs, w, b):
    m, d = cs.shape
    n = w.shape[1]
    tn = 512
    return pl.pallas_call(
        _ada_kernel,
        grid=(n // tn,),
        in_specs=[pl.BlockSpec((m, d), lambda j: (0, 0)),
                  pl.BlockSpec((d, tn), lambda j: (0, j)),
                  pl.BlockSpec((1, tn), lambda j: (0, j))],
        out_specs=pl.BlockSpec((m, tn), lambda j: (0, j)),
        out_shape=jax.ShapeDtypeStruct((m, n), F32),
        compiler_params=_params(("parallel",)),
        name="ada",
    )(cs, w, b)


def _norm_mod(x, nw, shift, scale):
    ms = jnp.mean(x * x, axis=-1, keepdims=True)
    y = x * lax.rsqrt(ms + EPS) * nw
    return y * (1.0 + scale) + shift


def _nmm_kernel(x_ref, nw_ref, mod_ref, w_ref, o_ref, xn_ref, *, k_shift, k_scale):
    y = _norm_mod(x_ref[...], nw_ref[...], mod_ref[0, k_shift:k_shift + 1, :], mod_ref[0, k_scale:k_scale + 1, :])
    xn = y.astype(BF16)
    xn_ref[...] = xn
    o_ref[...] = jnp.dot(xn, w_ref[...], preferred_element_type=F32).astype(o_ref.dtype)


def _norm_mod_matmul(x, nw, mod, row_of_tile, w, out_dtype, tm, k_shift, k_scale, name):
    m, d = x.shape
    n = w.shape[1]
    kern = functools.partial(_nmm_kernel, k_shift=k_shift, k_scale=k_scale)
    return pl.pallas_call(
        kern,
        grid=(m // tm,),
        in_specs=[pl.BlockSpec((tm, d), lambda i: (i, 0)),
                  pl.BlockSpec((1, d), lambda i: (0, 0)),
                  pl.BlockSpec((1, 6, d), lambda i: (row_of_tile(i), 0, 0)),
                  pl.BlockSpec((d, n), lambda i: (0, 0))],
        out_specs=[pl.BlockSpec((tm, n), lambda i: (i, 0)), pl.BlockSpec((tm, d), lambda i: (i, 0))],
        out_shape=[jax.ShapeDtypeStruct((m, n), out_dtype), jax.ShapeDtypeStruct((m, d), BF16)],
        compiler_params=_params(("parallel",)),
        name=name,
    )(x, nw, mod, w)


def _mm_kernel(x_ref, w_ref, o_ref):
    o_ref[...] = jnp.dot(x_ref[...], w_ref[...], preferred_element_type=F32).astype(o_ref.dtype)


def _matmul(x, w, out_dtype, tm, tn, name):
    m, d = x.shape
    n = w.shape[1]
    return pl.pallas_call(
        _mm_kernel,
        grid=(m // tm, n // tn),
        in_specs=[pl.BlockSpec((tm, d), lambda i, j: (i, 0)),
                  pl.BlockSpec((d, tn), lambda i, j: (0, j))],
        out_specs=pl.BlockSpec((tm, tn), lambda i, j: (i, j)),
        out_shape=jax.ShapeDtypeStruct((m, n), out_dtype),
        compiler_params=_params(("parallel", "parallel")),
        name=name,
    )(x, w)


def _mla_proj_kernel(a_ref, cr_ref, sr_ref, cc_ref, sc_ref, qan_ref, kvan_ref, qn_ref, kn_ref, wq_ref, wkv_ref,
                     q_ref, k_ref, v_ref):
    tm = a_ref.shape[0]
    g = tm // GRID_W

    def table(row_ref, col_ref):
        r = jnp.broadcast_to(row_ref[...][:, None, :], (g, GRID_W, ROPE_PAD)).reshape(tm, ROPE_PAD)
        c = jnp.broadcast_to(col_ref[...][None], (g, GRID_W, ROPE_PAD)).reshape(tm, ROPE_PAD)
        return r + c

    cos = table(cr_ref, cc_ref)
    sin = table(sr_ref, sc_ref)

    def rope(r):
        return r * cos + pltpu.roll(r, 64, axis=1) * sin

    def rms_rows(t, w):
        return t * lax.rsqrt(jnp.mean(t * t, axis=-1, keepdims=True) + EPS) * w

    qa = rms_rows(a_ref[:, A_QA:A_QA + MLA_Q_RANK], qan_ref[...]).astype(BF16)
    q = jnp.dot(qa, wq_ref[...], preferred_element_type=F32)
    kva = rms_rows(a_ref[:, A_KVA:A_KVA + MLA_KV_RANK], kvan_ref[...]).astype(BF16)
    kv = jnp.dot(kva, wkv_ref[...], preferred_element_type=F32)
    kr = a_ref[:, A_KR:A_KR + ROPE_PAD]
    kr_ss = jnp.sum(kr * kr, axis=-1, keepdims=True)
    qn = qn_ref[...]
    kn = kn_ref[...]
    scale = MLA_QK ** -0.5 * math.log2(math.e)
    for h in range(MLA_HEADS):
        qh = q[:, h * MLA_QK_PAD:(h + 1) * MLA_QK_PAD]
        inv = lax.rsqrt(jnp.sum(qh * qh, axis=-1, keepdims=True) * (1.0 / MLA_QK) + EPS) * scale
        qh = qh * inv * qn
        q_ref[:, h * MLA_QK_PAD:h * MLA_QK_PAD + MLA_NOPE] = qh[:, :MLA_NOPE].astype(BF16)
        q_ref[:, h * MLA_QK_PAD + MLA_NOPE:(h + 1) * MLA_QK_PAD] = rope(qh[:, MLA_NOPE:]).astype(BF16)

        kh = kv[:, h * MLA_NOPE:(h + 1) * MLA_NOPE]
        inv = lax.rsqrt((jnp.sum(kh * kh, axis=-1, keepdims=True) + kr_ss) * (1.0 / MLA_QK) + EPS)
        k_ref[:, h * MLA_QK_PAD:h * MLA_QK_PAD + MLA_NOPE] = (kh * inv * kn[:, :MLA_NOPE]).astype(BF16)
        k_ref[:, h * MLA_QK_PAD + MLA_NOPE:(h + 1) * MLA_QK_PAD] = rope(
            kr * inv * kn[:, MLA_NOPE:]).astype(BF16)
    v_ref[...] = kv[:, MLA_HEADS * MLA_NOPE:].astype(BF16)


def _mla_proj(a, tables, qan, kvan, qn, kn, wq, wkv, tm, seq):
    m = a.shape[0]
    nseq = seq // tm
    row_cos, row_sin, col_cos, col_sin = tables
    full = lambda arr: pl.BlockSpec(arr.shape, lambda i: (0, 0))
    row_spec = pl.BlockSpec((tm // GRID_W, ROPE_PAD), lambda i: (i % nseq, 0))
    return pl.pallas_call(
        _mla_proj_kernel,
        grid=(m // tm,),
        in_specs=[pl.BlockSpec((tm, A_WIDTH), lambda i: (i, 0)),
                  row_spec, row_spec, full(col_cos), full(col_sin),
                  full(qan), full(kvan), full(qn), full(kn), full(wq), full(wkv)],
        out_specs=[pl.BlockSpec((tm, MLA_HEADS * MLA_QK_PAD), lambda i: (i, 0)),
                   pl.BlockSpec((tm, MLA_HEADS * MLA_QK_PAD), lambda i: (i, 0)),
                   pl.BlockSpec((tm, MLA_HEADS * MLA_V), lambda i: (i, 0))],
        out_shape=[jax.ShapeDtypeStruct((m, MLA_HEADS * MLA_QK_PAD), BF16),
                   jax.ShapeDtypeStruct((m, MLA_HEADS * MLA_QK_PAD), BF16),
                   jax.ShapeDtypeStruct((m, MLA_HEADS * MLA_V), BF16)],
        compiler_params=_params(("parallel",)),
        name="mla_proj",
    )(a, row_cos, row_sin, col_cos, col_sin, qan, kvan, qn, kn, wq, wkv)


ATTN_UNROLL = 8
ATTN_ROWS = 64


def _attn_kernel(q_ref, kc_ref, vct_ref, k_ref, vt_ref, o_ref, qt_ref, s0_ref, s1_ref, acc_ref, *, nk, unroll):
    qt_ref[...] = q_ref[0].T
    tq = qt_ref.shape[1]
    tk = s0_ref.shape[0]
    rows = ATTN_ROWS
    s_refs = (s0_ref, s1_ref)

    def scores(j):
        return jnp.dot(k_ref[0, pl.ds(pl.multiple_of(j * tk, tk), tk), :], qt_ref[...], preferred_element_type=F32)

    def probs(s, m):
        return jnp.exp2((s - m).astype(BF16))

    def store_scores(slot, s):
        s_refs[slot][...] = s
        return jnp.max(s.reshape(tk // SUBLANE, SUBLANE, tq), axis=0)

    s = jnp.dot(kc_ref[0], qt_ref[...], preferred_element_type=F32)
    m8_first = store_scores(0, scores(0))
    m = jnp.max(s, axis=0, keepdims=True)
    acc_ref[...] = jnp.dot(vct_ref[0, 0], probs(s, m), preferred_element_type=F32)

    def tile(j, slot, carry):
        m, m8 = carry
        m8_next = store_scores(1 - slot, scores(jnp.minimum(j + 1, nk - 1)))
        m_new = jnp.maximum(m, jnp.max(m8, axis=0, keepdims=True))
        p = jnp.concatenate([probs(s_refs[slot][r:r + rows, :], m_new) for r in range(0, tk, rows)], axis=0)
        acc_ref[...] = jnp.exp2(m - m_new) * acc_ref[...] + jnp.dot(vt_ref[0, 0, j], p, preferred_element_type=F32)
        return m_new, m8_next

    def body(i, carry):
        for u in range(unroll):
            carry = tile(unroll * i + u, u % 2, carry)
        return carry

    lax.fori_loop(0, nk // unroll, body, (m, m8_first))
    acc = acc_ref[...]
    o_ref[0] = (acc[:MLA_V] * (1.0 / acc[MLA_V:MLA_V + 1])).T.astype(o_ref.dtype)


def _attention(q, k, vt, kc, vct, tq):
    b, t, _ = k.shape
    lc = kc.shape[1]
    nk, vrows, tk = vt.shape[2], vt.shape[3], vt.shape[4]
    unroll = ATTN_UNROLL if nk % ATTN_UNROLL == 0 else 2
    assert nk % unroll == 0
    kern = functools.partial(_attn_kernel, nk=nk, unroll=unroll)
    return pl.pallas_call(
        kern,
        grid=(b, MLA_HEADS, t // tq),
        in_specs=[pl.BlockSpec((1, tq, MLA_QK_PAD), lambda bi, h, i: (bi, i, h)),
                  pl.BlockSpec((1, lc, MLA_QK_PAD), lambda bi, h, i: (bi, 0, h)),
                  pl.BlockSpec((1, 1, vrows, lc), lambda bi, h, i: (bi, h, 0, 0)),
                  pl.BlockSpec((1, t, MLA_QK_PAD), lambda bi, h, i: (bi, 0, h)),
                  pl.BlockSpec((1, 1, nk, vrows, tk), lambda bi, h, i: (bi, h, 0, 0, 0))],
        out_specs=pl.BlockSpec((1, tq, MLA_V), lambda bi, h, i: (bi, i, h)),
        out_shape=jax.ShapeDtypeStruct((b, t, MLA_HEADS * MLA_V), BF16),
        scratch_shapes=[pltpu.VMEM((MLA_QK_PAD, tq), BF16), pltpu.VMEM((tk, tq), F32), pltpu.VMEM((tk, tq), F32),
                        pltpu.VMEM((vrows, tq), F32)],
        compiler_params=_params(("parallel", "parallel", "arbitrary")),
        name="attention",
    )(q, kc, vct, k, vt)


def _split2(x):
    hi = x.astype(BF16)
    return hi, (x - hi.astype(F32)).astype(BF16)


def _bmm(a, b):
    return lax.dot_general(a, b, (((2,), (1,)), ((0,), (0,))), preferred_element_type=F32)


def _bmm_nt(a, b):
    return lax.dot_general(a, b, (((2,), (2,)), ((0,), (0,))), preferred_element_type=F32)


def _bmm_hp(a, b):
    ah, al = _split2(a)
    bh, bl = _split2(b)
    return _bmm(ah, bh) + _bmm(al, bh) + _bmm(ah, bl)


def _bd2(b):
    lane = lax.broadcasted_iota(jnp.int32, b.shape, 2)
    zero = jnp.zeros_like(b)
    return jnp.concatenate([jnp.where(lane < DN_CHUNK, b, zero), jnp.where(lane >= DN_CHUNK, b, zero)], axis=1)


def _tri_inverse(m, eye):
    assert m.shape[-2] == 64
    x = eye - m
    nb = (-m).astype(BF16)
    bd = _bd2(nb)
    for _ in range(3):
        nb = _bmm(nb, bd).astype(BF16)
        bd = _bd2(nb)
        x = x + _bmm(x.astype(BF16), bd)
    xb = x.astype(BF16)
    r = eye - x - _bmm(m.astype(BF16), _bd2(xb))
    x = x + _bmm(xb, _bd2(r.astype(BF16)))
    mh, ml = _split2(m)
    xh, xl = _split2(x)
    bdh = _bd2(xh)
    r = eye - x - (_bmm(mh, bdh) + _bmm(ml, bdh) + _bmm(mh, _bd2(xl)))
    return x + _bmm(xh, _bd2(r.astype(BF16)))


def _dn_prep_kernel(qm_ref, qp_ref, qx_ref, km_ref, kp_ref, kx_ref, vm_ref, vp_ref, vx_ref,
                    cwq_ref, cwk_ref, cwv_ref, ab_ref, hp_ref,
                    u_ref, w_ref, ke_ref, qg_ref, a_ref, ge_ref, ext_ref, *, tm, tiles_per_seq):
    i = pl.program_id(0)
    first = (i % tiles_per_seq) == 0
    last = (i % tiles_per_seq) == tiles_per_seq - 1
    c = DN_CHUNK

    def conv_silu(main_ref, prev_ref, next_ref, cw_ref):
        ext_ref[0:HALO16, :] = jnp.where(first, 0.0, prev_ref[...].astype(F32))
        ext_ref[HALO16:HALO16 + tm, :] = main_ref[...].astype(F32)
        ext_ref[HALO16 + tm:, :] = jnp.where(last, 0.0, next_ref[...].astype(F32))
        y = (cw_ref[0:1, :] * ext_ref[HALO16 - 1:HALO16 - 1 + tm, :]
             + cw_ref[1:2, :] * ext_ref[HALO16:HALO16 + tm, :]
             + cw_ref[2:3, :] * ext_ref[HALO16 + 1:HALO16 + 1 + tm, :])
        return _silu(y)

    def l2n(t):
        return t * lax.rsqrt(jnp.sum(t * t, axis=-1, keepdims=True) + EPS)

    q = l2n(conv_silu(qm_ref, qp_ref, qx_ref, cwq_ref)) * (DN_DK ** -0.5)
    k = l2n(conv_silu(km_ref, kp_ref, kx_ref, cwk_ref))
    v = conv_silu(vm_ref, vp_ref, vx_ref, cwv_ref)

    hp = hp_ref[0]
    ab = ab_ref[0]
    z = ab[:, 0:2] + hp[:, 2:4]
    softplus = jnp.maximum(z, 0.0) + jnp.log(1.0 + jnp.exp(-jnp.abs(z)))
    g_all = -jnp.exp(hp[:, 0:2]) * softplus
    beta_all = jax.nn.sigmoid(ab[:, 2:4])

    nc = tm // c

    def per_chunk(t):
        return t.reshape(nc, c, t.shape[-1])

    def all_lanes(t2, d):
        return per_chunk(jnp.broadcast_to(t2[:, d:d + 1], (tm, LANE)))

    shape = (nc, c, LANE)
    rows = lax.broadcasted_iota(jnp.int32, shape, 1)
    lane = lax.broadcasted_iota(jnp.int32, shape, 2)
    bwd = lane >= c
    col = jnp.where(bwd, lane - c, lane)
    eye_mask = rows == col
    ahead = jnp.where(bwd, rows - col, col - rows)
    incl = ahead <= 0
    strict = ahead < 0
    eye = jnp.where(eye_mask, 1.0, 0.0)
    tri = jnp.where(incl, 1.0, 0.0).astype(BF16)

    def halves(t2):
        return jnp.where(bwd, all_lanes(t2, 1), all_lanes(t2, 0))

    g_hi, g_lo = _split2(halves(g_all))
    gc = _bmm(tri, _bd2(g_hi)) + _bmm(tri, _bd2(g_lo))
    gc_row = jnp.sum(jnp.where(eye_mask, gc, 0.0), axis=1, keepdims=True)
    decay = jnp.where(incl, jnp.exp(jnp.where(incl, gc - gc_row, 0.0)), 0.0)

    k3 = per_chunk(k)
    q3 = per_chunk(q)
    v3 = per_chunk(v)
    k3b = k3.astype(BF16)
    kdup = jnp.concatenate([k3b, k3b], axis=1)
    kk = _bmm_nt(k3b, kdup)
    att = _bmm_nt(q3.astype(BF16), kdup) * decay
    m = jnp.where(strict, halves(beta_all) * kk * decay, 0.0)
    t_hi, t_lo = _split2(_tri_inverse(m, eye))

    swapped = pltpu.roll(gc.reshape(tm, LANE), c, axis=1).reshape(shape)
    gc_dir = (jnp.where(bwd, swapped, gc), jnp.where(bwd, gc, swapped))
    rhs, ke, qg, gend = [], [], [], []
    for d in range(2):
        gcd = gc_dir[d]
        beta = all_lanes(beta_all, d)
        eg = jnp.exp(gcd)
        gtot = gcd[:, c - 1:c, :] if d == 0 else gcd[:, 0:1, :]
        rhs.append(jnp.concatenate([v3 * beta, k3 * beta * eg], axis=2))
        ke.append(k3 * jnp.exp(gtot - gcd))
        qg.append(q3 * eg)
        gend.append(jnp.exp(gtot))
    rhs = jnp.concatenate(rhs, axis=1).astype(BF16)
    sol = _bmm(_bd2(t_hi), rhs) + _bmm(_bd2(t_lo), rhs)
    att_dir = (att, pltpu.roll(att.reshape(tm, LANE), c, axis=1).reshape(shape))
    for d in range(2):
        sl = slice(d * c, (d + 1) * c)
        u_ref[d, 0, 0] = sol[:, sl, :DN_DV].reshape(tm, DN_DV)
        w_ref[d, 0, 0] = sol[:, sl, DN_DV:].reshape(tm, DN_DK).astype(BF16)
        ke_ref[d, 0, 0] = ke[d].reshape(tm, DN_DK).astype(BF16)
        qg_ref[d, 0, 0] = qg[d].reshape(tm, DN_DK).astype(BF16)
        a_ref[d, 0, 0] = att_dir[d][:, :, :c].reshape(tm, c).astype(BF16)
        ge_ref[d, 0, 0] = gend[d]


def _dn_prep(p, ab, hp, conv_w, batch, seq, tm):
    h = DN_HEADS
    nt = seq // tm
    nb16 = tm // HALO16
    total16 = batch * seq // HALO16
    kern = functools.partial(_dn_prep_kernel, tm=tm, tiles_per_seq=nt)

    def triple(col0):
        return [pl.BlockSpec((tm, LANE), lambda i, hh: (i, col0 + hh)),
                pl.BlockSpec((HALO16, LANE), lambda i, hh: (jnp.maximum(i * nb16 - 1, 0), col0 + hh)),
                pl.BlockSpec((HALO16, LANE), lambda i, hh: (jnp.minimum((i + 1) * nb16, total16 - 1), col0 + hh))]

    cq, ck, cv = B_DQ // LANE, B_DK // LANE, B_DV // LANE
    row = lambda shape: pl.BlockSpec(shape, lambda i, hh: (0, i // nt, hh, i % nt, 0))
    seq_shape = lambda width, dt: jax.ShapeDtypeStruct((2, batch, h, seq, width), dt)
    return pl.pallas_call(
        kern,
        grid=(batch * nt, h),
        in_specs=triple(cq) + triple(ck) + triple(cv) + [
            pl.BlockSpec((3, LANE), lambda i, hh: (0, hh)),
            pl.BlockSpec((3, LANE), lambda i, hh: (0, h + hh)),
            pl.BlockSpec((3, LANE), lambda i, hh: (0, 2 * h + hh)),
            pl.BlockSpec((1, tm, 4), lambda i, hh: (hh, i, 0)),
            pl.BlockSpec((1, 1, 4), lambda i, hh: (hh, 0, 0))],
        out_specs=[row((2, 1, 1, tm, DN_DV)), row((2, 1, 1, tm, DN_DK)), row((2, 1, 1, tm, DN_DK)),
                   row((2, 1, 1, tm, DN_DK)), row((2, 1, 1, tm, DN_CHUNK)),
                   pl.BlockSpec((2, 1, 1, tm // DN_CHUNK, 1, LANE), lambda i, hh: (0, i // nt, hh, i % nt, 0, 0))],
        out_shape=[seq_shape(DN_DV, F32), seq_shape(DN_DK, BF16), seq_shape(DN_DK, BF16),
                   seq_shape(DN_DK, BF16), seq_shape(DN_CHUNK, BF16),
                   jax.ShapeDtypeStruct((2, batch, h, seq // DN_CHUNK, 1, LANE), F32)],
        scratch_shapes=[pltpu.VMEM((tm + 2 * HALO16, LANE), F32)],
        compiler_params=_params(("parallel", "parallel")),
        name="dn_prep",
    )(p, p, p, p, p, p, p, p, p, conv_w, conv_w, conv_w, ab, hp)


def _dn_scan_kernel(s0_ref, uf_ref, wf_ref, kf_ref, qf_ref, af_ref, gf_ref,
                    ub_ref, wb_ref, kb_ref, qb_ref, ab_ref, gb_ref,
                    of_ref, ob_ref, sfin_ref, s_ref, *, cs):
    n = pl.program_id(1)
    c = DN_CHUNK
    h = DN_HEADS

    @pl.when(n == 0)
    def _():
        s_ref[0:h] = s0_ref[0, 0]
        s_ref[h:] = s0_ref[1, 0]

    for ci in range(cs):
        rf = ci * c
        rb = (cs - 1 - ci) * c

        def pair(f_ref, b_ref):
            return jnp.concatenate([f_ref[0, 0, :, rf:rf + c, :], b_ref[0, 0, :, rb:rb + c, :]], axis=0)

        s = s_ref[...]
        s16 = s.astype(BF16)
        ws_qs = _bmm(jnp.concatenate([pair(wf_ref, wb_ref), pair(qf_ref, qb_ref)], axis=1), s16)
        v_new = pair(uf_ref, ub_ref) - ws_qs[:, :c]
        v16 = v_new.astype(BF16)
        o = ws_qs[:, c:] + _bmm(pair(af_ref, ab_ref), v16)
        of_ref[0, :, rf:rf + c, :] = o[:h].astype(of_ref.dtype)
        ob_ref[0, :, rb:rb + c, :] = o[h:].astype(ob_ref.dtype)
        g = jnp.concatenate([gf_ref[0, 0, :, ci], gb_ref[0, 0, :, cs - 1 - ci]], axis=0)
        s_ref[...] = s * g + lax.dot_general(pair(kf_ref, kb_ref), v16, (((1,), (1,)), ((0,), (0,))),
                                             preferred_element_type=F32)

    @pl.when(n == pl.num_programs(1) - 1)
    def _():
        sfin_ref[0, 0] = s_ref[0:h]
        sfin_ref[1, 0] = s_ref[h:]


def _dn_scan(s0, u, w, ke, qg, a, ge, cs):
    _, batch, h, seq, _ = u.shape
    ts = cs * DN_CHUNK
    ns = seq // ts
    kern = functools.partial(_dn_scan_kernel, cs=cs)

    def specs(d):
        idx = (lambda n: n) if d == 0 else (lambda n: ns - 1 - n)
        seqs = [pl.BlockSpec((1, 1, h, ts, width), lambda b, n: (d, b, 0, idx(n), 0))
                for width in (DN_DV, DN_DK, DN_DK, DN_DK, DN_CHUNK)]
        return seqs + [pl.BlockSpec((1, 1, h, cs, 1, LANE), lambda b, n: (d, b, 0, idx(n), 0, 0))]

    o_spec = lambda d: pl.BlockSpec((1, h, ts, DN_DV), lambda b, n: (b, 0, n if d == 0 else ns - 1 - n, 0))
    state = pl.BlockSpec((2, 1, h, DN_DK, DN_DV), lambda b, n: (0, b, 0, 0, 0))
    return pl.pallas_call(
        kern,
        grid=(batch, ns),
        in_specs=[state] + specs(0) + specs(1),
        out_specs=[o_spec(0), o_spec(1), state],
        out_shape=[jax.ShapeDtypeStruct((batch, h, seq, DN_DV), BF16),
                   jax.ShapeDtypeStruct((batch, h, seq, DN_DV), BF16),
                   jax.ShapeDtypeStruct((2, batch, h, DN_DK, DN_DV), F32)],
        scratch_shapes=[pltpu.VMEM((2 * h, DN_DK, DN_DV), F32)],
        compiler_params=_params(("parallel", "arbitrary")),
        name="dn_scan",
    )(s0, u, w, ke, qg, a, ge, u, w, ke, qg, a, ge)


def _merge_kernel(x_ref, mod_ref, ym_ref, of_ref, ob_ref, z_ref, nw_ref, ga_ref, gb_ref, wm_ref, wd_ref, wo_ref,
                  nf_ref, o_ref, xn_ref, *, k_gate, k_shift, k_scale):
    nw = nw_ref[...]
    heads = []
    for h in range(DN_HEADS):
        o = of_ref[0, h].astype(F32) + ob_ref[0, h].astype(F32)
        y = o * lax.rsqrt(jnp.mean(o * o, axis=-1, keepdims=True) + EPS) * nw
        heads.append((y * _silu(z_ref[:, h * DN_DV:(h + 1) * DN_DV].astype(F32))).astype(BF16))
    yd = jnp.concatenate(heads, axis=1)
    pm = jnp.dot(ym_ref[...], wm_ref[...], preferred_element_type=F32)
    pd = jnp.dot(yd, wd_ref[...], preferred_element_type=F32)
    merged = (jax.nn.sigmoid(ga_ref[...].astype(F32)) * pm
              + jax.nn.sigmoid(gb_ref[...].astype(F32)) * pd).astype(BF16)
    y = jnp.dot(merged, wo_ref[...], preferred_element_type=F32)
    x1 = x_ref[...] + mod_ref[0, k_gate:k_gate + 1, :] * y
    o_ref[...] = x1
    xn_ref[...] = _norm_mod(x1, nf_ref[...], mod_ref[0, k_shift:k_shift + 1, :],
                            mod_ref[0, k_scale:k_scale + 1, :]).astype(BF16)


def _merge(x, mod, row_of_tile, ym, o_f, o_b, p, nw, wm, wd, wo, nf, tm, k_gate, k_shift, k_scale):
    m, d = x.shape
    _, h, seq, _ = o_f.shape
    nt = seq // tm
    zw = h * DN_DV
    kern = functools.partial(_merge_kernel, k_gate=k_gate, k_shift=k_shift, k_scale=k_scale)
    const = lambda arr: pl.BlockSpec(arr.shape, lambda i: (0, 0), pipeline_mode=pl.Buffered(1))
    scan_out = pl.BlockSpec((1, h, tm, DN_DV), lambda i: (i // nt, 0, i % nt, 0))
    return pl.pallas_call(
        kern,
        grid=(m // tm,),
        in_specs=[pl.BlockSpec((tm, d), lambda i: (i, 0)),
                  pl.BlockSpec((1, 6, d), lambda i: (row_of_tile(i), 0, 0)),
                  pl.BlockSpec((tm, ym.shape[1]), lambda i: (i, 0)),
                  scan_out, scan_out,
                  pl.BlockSpec((tm, zw), lambda i: (i, B_DZ // zw)),
                  pl.BlockSpec((1, DN_DV), lambda i: (0, 0)),
                  pl.BlockSpec((tm, d), lambda i: (i, B_GA // d)),
                  pl.BlockSpec((tm, d), lambda i: (i, B_GB // d)),
                  const(wm), const(wd), const(wo),
                  pl.BlockSpec((1, d), lambda i: (0, 0))],
        out_specs=[pl.BlockSpec((tm, d), lambda i: (i, 0)), pl.BlockSpec((tm, d), lambda i: (i, 0))],
        out_shape=[jax.ShapeDtypeStruct((m, d), F32), jax.ShapeDtypeStruct((m, d), BF16)],
        compiler_params=_params(("parallel",)),
        name="merge",
    )(x, mod, ym, o_f, o_b, p, nw, p, p, wm, wd, wo, nf)


def _ffn_kernel(x_ref, xm_ref, xp_ref, xx_ref, mod_ref, wg_ref, wv_ref, cw_ref, wdn_ref, o_ref, xn_ref, ge_ref,
                *, tm, tiles_per_seq, k_gate):
    i = pl.program_id(0)
    j = pl.program_id(1)
    th = FFN_TILE

    @pl.when(j == 0)
    def _():
        first = (i % tiles_per_seq) == 0
        last = (i % tiles_per_seq) == tiles_per_seq - 1
        xn_ref[0:HALO16, :] = jnp.where(first, jnp.zeros_like(xp_ref), xp_ref[...])
        xn_ref[HALO16:HALO16 + tm, :] = xm_ref[...]
        xn_ref[HALO16 + tm:, :] = jnp.where(last, jnp.zeros_like(xx_ref), xx_ref[...])
        o_ref[...] = jnp.zeros_like(o_ref)

    ge_ref[...] = jnp.dot(xn_ref[...], wg_ref[...], preferred_element_type=F32)
    val = jnp.dot(xn_ref[HALO16:HALO16 + tm, :], wv_ref[...], preferred_element_type=F32)
    conv = (cw_ref[0:1, :] * ge_ref[HALO16 - 1:HALO16 - 1 + tm, :]
            + cw_ref[1:2, :] * ge_ref[HALO16:HALO16 + tm, :]
            + cw_ref[2:3, :] * ge_ref[HALO16 + 1:HALO16 + 1 + tm, :])
    hid = (_silu(conv) * val).astype(BF16)
    o_ref[...] += jnp.dot(hid, wdn_ref[...], preferred_element_type=F32)

    @pl.when(j == pl.num_programs(1) - 1)
    def _():
        o_ref[...] = x_ref[...] + mod_ref[0, k_gate:k_gate + 1, :] * o_ref[...]


def _ffn(x, xn, mod, row_of_tile, wg, wv, cw, wdn, seq, tm, k_gate):
    m, d = x.shape
    th = FFN_TILE
    nj = wdn.shape[0] // th
    nt = seq // tm
    nb16 = tm // HALO16
    total16 = m // HALO16
    kern = functools.partial(_ffn_kernel, tm=tm, tiles_per_seq=nt, k_gate=k_gate)
    return pl.pallas_call(
        kern,
        grid=(m // tm, nj),
        in_specs=[pl.BlockSpec((tm, d), lambda i, j: (i, 0)),
                  pl.BlockSpec((tm, d), lambda i, j: (i, 0)),
                  pl.BlockSpec((HALO16, d), lambda i, j: (jnp.maximum(i * nb16 - 1, 0), 0)),
                  pl.BlockSpec((HALO16, d), lambda i, j: (jnp.minimum((i + 1) * nb16, total16 - 1), 0)),
                  pl.BlockSpec((1, 6, d), lambda i, j: (row_of_tile(i), 0, 0)),
                  pl.BlockSpec((d, th), lambda i, j: (0, j)),
                  pl.BlockSpec((d, th), lambda i, j: (0, j)),
                  pl.BlockSpec((3, th), lambda i, j: (0, j)),
                  pl.BlockSpec((th, d), lambda i, j: (j, 0))],
        out_specs=pl.BlockSpec((tm, d), lambda i, j: (i, 0)),
        out_shape=jax.ShapeDtypeStruct((m, d), F32),
        scratch_shapes=[pltpu.VMEM((tm + 2 * HALO16, d), BF16), pltpu.VMEM((tm + 2 * HALO16, th), F32)],
        compiler_params=_params(("parallel", "arbitrary")),
        name="ffn",
    )(x, xn, xn, xn, mod, wg, wv, cw, wdn)


def _rope_pad_cols(r):
    n = MLA_ROPE // 4
    z = jnp.zeros(r.shape[:-1] + (2 * n,), r.dtype)
    return jnp.concatenate([r[..., 0:n], r[..., 2 * n:3 * n], z, r[..., n:2 * n], r[..., 3 * n:4 * n], z], axis=-1)


def _qk_pad_cols(w):
    lead = w.shape[:-1]
    w = w.reshape(lead + (MLA_HEADS, MLA_QK))
    out = jnp.concatenate([w[..., :MLA_NOPE], _rope_pad_cols(w[..., MLA_NOPE:])], axis=-1)
    return out.reshape(lead + (MLA_HEADS * MLA_QK_PAD,))


def _rope_tables(rows):
    n = MLA_ROPE // 4
    inv = ROPE_BASE ** (-jnp.arange(n, dtype=F32) / n)
    ang_r = jnp.arange(rows, dtype=F32)[:, None] * inv
    ang_c = jnp.arange(GRID_W, dtype=F32)[:, None] * inv

    def lanes(a_row, a_col, fill, count):
        return jnp.concatenate([a_row, a_col, jnp.full((count, 2 * n), fill, F32)] * 2, axis=-1)

    zr, zc = jnp.zeros((rows, n), F32), jnp.zeros((GRID_W, n), F32)
    row_cos = lanes(jnp.cos(ang_r), zr, 0.0, rows)
    row_sin = jnp.concatenate([-jnp.sin(ang_r), zr, jnp.zeros((rows, 2 * n), F32),
                               jnp.sin(ang_r), zr, jnp.zeros((rows, 2 * n), F32)], axis=-1)
    col_cos = lanes(zc, jnp.cos(ang_c), 1.0, GRID_W)
    col_sin = jnp.concatenate([zc, -jnp.sin(ang_c), jnp.zeros((GRID_W, 2 * n), F32),
                               zc, jnp.sin(ang_c), jnp.zeros((GRID_W, 2 * n), F32)], axis=-1)
    return row_cos, row_sin, col_cos, col_sin


def _pick_tile(n, pref):
    t = min(pref, n)
    while n % t:
        t //= 2
    return t


def kernel(x, c, ctx, c_ctx, w_ada, b_ada, norm_mix, w_in, q_a_norm, w_q_b, kv_a_norm, w_kv_b, q_norm, k_norm,
           w_o_mla, dn_conv, dn_a_log, dn_dt_bias, dn_o_norm, w_o_dn, w_out, norm_ffn, w_ffn_up, ffn_conv,
           w_ffn_down):
    batch, seq, d = x.shape
    lc = ctx.shape[1]
    assert w_ada.shape[0] == 1, "single-layer stack"
    assert seq % GRID_W == 0 and seq % DN_CHUNK == 0 and lc % DN_CHUNK == 0
    h = DN_HEADS
    l = 0

    wi = w_in[l]
    o_qa, o_kva, o_kr = 0, MLA_Q_RANK, MLA_Q_RANK + MLA_KV_RANK
    o_dq = o_kr + MLA_ROPE
    o_da = o_dq + 4 * h * DN_DK
    o_ga = o_da + 4 * h
    w_a = jnp.concatenate([wi[:, o_qa:o_kr], _rope_pad_cols(wi[:, o_kr:o_dq]), wi[:, o_da:o_ga],
                           jnp.zeros((d, A_WIDTH - A_DB - 2 * h), F32)], axis=1).astype(BF16)
    w_b = jnp.concatenate([wi[:, o_dq:o_da], wi[:, o_ga:]], axis=1).astype(BF16)
    wq = _qk_pad_cols(w_q_b[l]).astype(BF16)
    wkv = w_kv_b[l].reshape(MLA_KV_RANK, MLA_HEADS, MLA_NOPE + MLA_V)
    wkv = jnp.concatenate([wkv[:, :, :MLA_NOPE].reshape(MLA_KV_RANK, -1),
                           wkv[:, :, MLA_NOPE:].reshape(MLA_KV_RANK, -1)], axis=1).astype(BF16)
    qn = jnp.concatenate([q_norm[l, :MLA_NOPE], _rope_pad_cols(q_norm[l, MLA_NOPE:])])[None, :]
    kn = jnp.concatenate([k_norm[l, :MLA_NOPE], _rope_pad_cols(k_norm[l, MLA_NOPE:])])[None, :]
    hidden = w_ffn_down.shape[1]
    nj = -(-hidden // FFN_TILE)
    hpad = nj * FFN_TILE - hidden
    wg = jnp.pad(w_ffn_up[l][:, :hidden].astype(BF16), ((0, 0), (0, hpad)))
    wv = jnp.pad(w_ffn_up[l][:, hidden:].astype(BF16), ((0, 0), (0, hpad)))
    fcw = jnp.pad(ffn_conv[l], ((0, 0), (0, hpad)))
    wdn = jnp.pad(w_ffn_down[l], ((0, hpad), (0, 0))).astype(BF16)
    hp = jnp.concatenate([dn_a_log[l].T, dn_dt_bias[l].T], axis=1)[:, None, :]
    tables_l = _rope_tables(seq // GRID_W)
    tables_c = (jnp.zeros((lc // GRID_W, ROPE_PAD), F32), jnp.zeros((lc // GRID_W, ROPE_PAD), F32),
                jnp.ones((GRID_W, ROPE_PAD), F32), jnp.zeros((GRID_W, ROPE_PAD), F32))

    rows = jnp.concatenate([c, c_ctx[None, :], jnp.zeros((8 - batch - 1, d), F32)], axis=0)
    mod = _ada(rows, w_ada[l], b_ada[l][None, :]).reshape(8, 6, d)

    xf = x.reshape(batch * seq, d)
    cf = ctx.reshape(batch * lc, d)

    def stream(tokens, n, tm_a, tm_b, row_of):
        nm = norm_mix[l][None, :]
        a, xn = _norm_mod_matmul(tokens, nm, mod, row_of(tm_a), w_a, F32, tm_a, 0, 1, "in_proj_small")
        p = _matmul(xn, w_b, BF16, tm_b, 1024, "in_proj_wide")
        ab = a[:, A_DA:A_DA + 4 * h].reshape(batch * n, 4, h).transpose(2, 0, 1)
        return a, p, ab

    tm_a = _pick_tile(seq, 512)
    tm_b = _pick_tile(seq, 1024)
    lat_row = lambda tm: (lambda i: i // (seq // tm))
    a_l, p_l, ab_l = stream(xf, seq, tm_a, tm_b, lat_row)
    tc = _pick_tile(lc, 256)
    ctx_row = lambda tm: (lambda i: batch)
    a_c, p_c, ab_c = stream(cf, lc, tc, tc, ctx_row)

    qan, kvan = q_a_norm[l][None, :], kv_a_norm[l][None, :]
    tmp = _pick_tile(seq, 512)
    q_l, k_l, v_l = _mla_proj(a_l, tables_l, qan, kvan, qn, kn, wq, wkv, tmp, seq)
    _, k_c, v_c = _mla_proj(a_c, tables_c, qan, kvan, qn, kn, wq, wkv, tc, lc)
    tq = _pick_tile(seq, 512)
    tk = _pick_tile(seq, 512)

    def with_ones_row(vt):
        lead, n = vt.shape[:-2], vt.shape[-1]
        return jnp.concatenate([vt, jnp.ones(lead + (1, n), BF16), jnp.zeros(lead + (HALO16 - 1, n), BF16)],
                               axis=-2)

    vt = with_ones_row(v_l.reshape(batch, seq // tk, tk, MLA_HEADS, MLA_V).transpose(0, 3, 1, 4, 2))
    vct = with_ones_row(v_c.reshape(batch, lc, MLA_HEADS, MLA_V).transpose(0, 2, 3, 1))
    y_mla = _attention(q_l.reshape(batch, seq, -1), k_l.reshape(batch, seq, -1), vt, k_c.reshape(batch, lc, -1), vct, tq)
    y_mla = y_mla.reshape(batch * seq, MLA_HEADS * MLA_V)

    conv_w = dn_conv[l]
    prep_c = _dn_prep(p_c, ab_c, hp, conv_w, batch, lc, _pick_tile(lc, 256))
    s_zero = jnp.zeros((2, batch, h, DN_DK, DN_DV), F32)
    _, _, s_ctx = _dn_scan(s_zero, *prep_c, _pick_tile(lc, 256) // DN_CHUNK)
    prep_l = _dn_prep(p_l, ab_l, hp, conv_w, batch, seq, _pick_tile(seq, 1024))
    o_f, o_b, _ = _dn_scan(s_ctx, *prep_l, _pick_tile(seq, 256) // DN_CHUNK)

    tmm = _pick_tile(seq, 256)
    x1, xn2 = _merge(xf, mod, lat_row(tmm), y_mla, o_f, o_b, p_l, dn_o_norm[l][None, :], w_o_mla[l].astype(BF16),
                     w_o_dn[l].astype(BF16), w_out[l].astype(BF16), norm_ffn[l][None, :], tmm, 2, 3, 4)

    tmf = _pick_tile(seq, 512)
    out = _ffn(x1, xn2, mod, lat_row(tmf), wg, wv, fcw, wdn, seq, tmf, 5)
    return out.reshape(batch, seq, d)
```

```python
import functools
import math

import jax
import jax.numpy as jnp
import numpy as np
from jax import lax
from jax.experimental import pallas as pl
from jax.experimental.pallas import tpu as pltpu

F32 = jnp.float32
BF16 = jnp.bfloat16
HIGHEST = lax.Precision.HIGHEST

EPS = 1e-6
GRID_W = 64
ROPE_BASE = 10000.0

MLA_HEADS = 8
MLA_Q_RANK = 512
MLA_KV_RANK = 512
MLA_NOPE = 128
MLA_ROPE = 64
MLA_V = 128
MLA_QK = MLA_NOPE + MLA_ROPE
MLA_QK_PAD = 256

DN_HEADS = 8
DN_DK = 128
DN_DV = 128
DN_CHUNK = 64

LANE = 128
SUBLANE = 8
ROPE_PAD = 128
VMEM_LIMIT = 56 * 1024 * 1024

A_QA, A_KVA, A_KR, A_DA, A_DB, A_WIDTH = 0, 512, 1024, 1152, 1168, 1280
B_DQ, B_DK, B_DV, B_DZ, B_GA, B_GB, B_WIDTH = 0, 1024, 2048, 3072, 4096, 6144, 8192

FFN_TILE = 512
HALO16 = 16


def _params(sem, vmem=VMEM_LIMIT):
    return pltpu.CompilerParams(dimension_semantics=sem, vmem_limit_bytes=vmem)


def _nt_dot(a, b):
    return lax.dot_general(a, b, (((1,), (1,)), ((), ())), preferred_element_type=F32)


def _silu(x):
    return x * jax.nn.sigmoid(x)


def _ada_kernel(c_ref, w_ref, b_ref, o_ref):
    a = _silu(c_ref[...]).astype(BF16)
    o_ref[...] = jnp.dot(a, w_ref[...].astype(BF16), preferred_element_type=F32) + b_ref[...]


def _ada(cs, w, b):
    m, d = cs.shape
    n = w.shape[1]
    tn = 512
    return pl.pallas_call(
        _ada_kernel,
        grid=(n // tn,),
        in_specs=[pl.BlockSpec((m, d), lambda j: (0, 0)),
                  pl.BlockSpec((d, tn), lambda j: (0, j)),
                  pl.BlockSpec((1, tn), lambda j: (0, j))],
        out_specs=pl.BlockSpec((m, tn), lambda j: (0, j)),
        out_shape=jax.ShapeDtypeStruct((m, n), F32),
        compiler_params=_params(("parallel",)),
        name="ada",
    )(cs, w, b)


def _norm_mod(x, nw, shift, scale):
    ms = jnp.mean(x * x, axis=-1, keepdims=True)
    y = x * lax.rsqrt(ms + EPS) * nw
    return y * (1.0 + scale) + shift


def _nmm_kernel(x_ref, nw_ref, mod_ref, w_ref, o_ref, xn_ref, *, k_shift, k_scale):
    y = _norm_mod(x_ref[...], nw_ref[...], mod_ref[0, k_shift:k_shift + 1, :], mod_ref[0, k_scale:k_scale + 1, :])
    xn = y.astype(BF16)
    xn_ref[...] = xn
    o_ref[...] = jnp.dot(xn, w_ref[...], preferred_element_type=F32).astype(o_ref.dtype)


def _norm_mod_matmul(x, nw, mod, row_of_tile, w, out_dtype, tm, k_shift, k_scale, name):
    m, d = x.shape
    n = w.shape[1]
    kern = functools.partial(_nmm_kernel, k_shift=k_shift, k_scale=k_scale)
    return pl.pallas_call(
        kern,
        grid=(m // tm,),
        in_specs=[pl.BlockSpec((tm, d), lambda i: (i, 0)),
                  pl.BlockSpec((1, d), lambda i: (0, 0)),
                  pl.BlockSpec((1, 6, d), lambda i: (row_of_tile(i), 0, 0)),
                  pl.BlockSpec((d, n), lambda i: (0, 0))],
        out_specs=[pl.BlockSpec((tm, n), lambda i: (i, 0)), pl.BlockSpec((tm, d), lambda i: (i, 0))],
        out_shape=[jax.ShapeDtypeStruct((m, n), out_dtype), jax.ShapeDtypeStruct((m, d), BF16)],
        compiler_params=_params(("parallel",)),
        name=name,
    )(x, nw, mod, w)


def _mm_kernel(x_ref, w_ref, o_ref):
    o_ref[...] = jnp.dot(x_ref[...], w_ref[...], preferred_element_type=F32).astype(o_ref.dtype)


def _matmul(x, w, out_dtype, tm, tn, name):
    m, d = x.shape
    n = w.shape[1]
    return pl.pallas_call(
        _mm_kernel,
        grid=(m // tm, n // tn),
        in_specs=[pl.BlockSpec((tm, d), lambda i, j: (i, 0)),
                  pl.BlockSpec((d, tn), lambda i, j: (0, j))],
        out_specs=pl.BlockSpec((tm, tn), lambda i, j: (i, j)),
        out_shape=jax.ShapeDtypeStruct((m, n), out_dtype),
        compiler_params=_params(("parallel", "parallel")),
        name=name,
    )(x, w)


def _mla_proj_kernel(a_ref, cr_ref, sr_ref, cc_ref, sc_ref, qan_ref, kvan_ref, qn_ref, kn_ref, wq_ref, wkv_ref,
                     q_ref, k_ref, v_ref):
    tm = a_ref.shape[0]
    g = tm // GRID_W

    def table(row_ref, col_ref):
        r = jnp.broadcast_to(row_ref[...][:, None, :], (g, GRID_W, ROPE_PAD)).reshape(tm, ROPE_PAD)
        c = jnp.broadcast_to(col_ref[...][None], (g, GRID_W, ROPE_PAD)).reshape(tm, ROPE_PAD)
        return r + c

    cos = table(cr_ref, cc_ref)
    sin = table(sr_ref, sc_ref)

    def rope(r):
        return r * cos + pltpu.roll(r, 64, axis=1) * sin

    def rms_rows(t, w):
        return t * lax.rsqrt(jnp.mean(t * t, axis=-1, keepdims=True) + EPS) * w

    qa = rms_rows(a_ref[:, A_QA:A_QA + MLA_Q_RANK], qan_ref[...]).astype(BF16)
    q = jnp.dot(qa, wq_ref[...], preferred_element_type=F32)
    kva = rms_rows(a_ref[:, A_KVA:A_KVA + MLA_KV_RANK], kvan_ref[...]).astype(BF16)
    kv = jnp.dot(kva, wkv_ref[...], preferred_element_type=F32)
    kr = a_ref[:, A_KR:A_KR + ROPE_PAD]
    kr_ss = jnp.sum(kr * kr, axis=-1, keepdims=True)
    qn = qn_ref[...]
    kn = kn_ref[...]
    scale = MLA_QK ** -0.5 * math.log2(math.e)
    for h in range(MLA_HEADS):
        qh = q[:, h * MLA_QK_PAD:(h + 1) * MLA_QK_PAD]
        inv = lax.rsqrt(jnp.sum(qh * qh, axis=-1, keepdims=True) * (1.0 / MLA_QK) + EPS) * scale
        qh = qh * inv * qn
        q_ref[:, h * MLA_QK_PAD:h * MLA_QK_PAD + MLA_NOPE] = qh[:, :MLA_NOPE].astype(BF16)
        q_ref[:, h * MLA_QK_PAD + MLA_NOPE:(h + 1) * MLA_QK_PAD] = rope(qh[:, MLA_NOPE:]).astype(BF16)

        kh = kv[:, h * MLA_NOPE:(h + 1) * MLA_NOPE]
        inv = lax.rsqrt((jnp.sum(kh * kh, axis=-1, keepdims=True) + kr_ss) * (1.0 / MLA_QK) + EPS)
        k_ref[:, h * MLA_QK_PAD:h * MLA_QK_PAD + MLA_NOPE] = (kh * inv * kn[:, :MLA_NOPE]).astype(BF16)
        k_ref[:, h * MLA_QK_PAD + MLA_NOPE:(h + 1) * MLA_QK_PAD] = rope(
            kr * inv * kn[:, MLA_NOPE:]).astype(BF16)
    v_ref[...] = kv[:, MLA_HEADS * MLA_NOPE:].astype(BF16)


def _mla_proj(a, tables, qan, kvan, qn, kn, wq, wkv, tm, seq):
    m = a.shape[0]
    nseq = seq // tm
    row_cos, row_sin, col_cos, col_sin = tables
    full = lambda arr: pl.BlockSpec(arr.shape, lambda i: (0, 0))
    row_spec = pl.BlockSpec((tm // GRID_W, ROPE_PAD), lambda i: (i % nseq, 0))
    return pl.pallas_call(
        _mla_proj_kernel,
        grid=(m // tm,),
        in_specs=[pl.BlockSpec((tm, A_WIDTH), lambda i: (i, 0)),
                  row_spec, row_spec, full(col_cos), full(col_sin),
                  full(qan), full(kvan), full(qn), full(kn), full(wq), full(wkv)],
        out_specs=[pl.BlockSpec((tm, MLA_HEADS * MLA_QK_PAD), lambda i: (i, 0)),
                   pl.BlockSpec((tm, MLA_HEADS * MLA_QK_PAD), lambda i: (i, 0)),
                   pl.BlockSpec((tm, MLA_HEADS * MLA_V), lambda i: (i, 0))],
        out_shape=[jax.ShapeDtypeStruct((m, MLA_HEADS * MLA_QK_PAD), BF16),
                   jax.ShapeDtypeStruct((m, MLA_HEADS * MLA_QK_PAD), BF16),
                   jax.ShapeDtypeStruct((m, MLA_HEADS * MLA_V), BF16)],
        compiler_params=_params(("parallel",)),
        name="mla_proj",
    )(a, row_cos, row_sin, col_cos, col_sin, qan, kvan, qn, kn, wq, wkv)


ATTN_UNROLL = 16
ATTN_ROWS = 64


def _attn_kernel(q_ref, kc_ref, vct_ref, k_ref, vt_ref, o_ref, qt_ref, s0_ref, s1_ref, acc_ref, *, nk, unroll):
    qt_ref[...] = q_ref[0].T
    tq = qt_ref.shape[1]
    tk = s0_ref.shape[0]
    rows = ATTN_ROWS
    s_refs = (s0_ref, s1_ref)

    def scores(j):
        return jnp.dot(k_ref[0, pl.ds(pl.multiple_of(j * tk, tk), tk), :], qt_ref[...], preferred_element_type=F32)

    def probs(s, m):
        return jnp.exp2((s - m).astype(BF16))

    def store_scores(slot, s):
        s_refs[slot][...] = s
        return jnp.max(s.reshape(tk // SUBLANE, SUBLANE, tq), axis=0)

    s = jnp.dot(kc_ref[0], qt_ref[...], preferred_element_type=F32)
    m8_first = store_scores(0, scores(0))
    m = jnp.max(s, axis=0, keepdims=True)
    acc_ref[...] = jnp.dot(vct_ref[0, 0], probs(s, m), preferred_element_type=F32)

    def tile(j, slot, carry):
        m, m8 = carry
        m8_next = store_scores(1 - slot, scores(jnp.minimum(j + 1, nk - 1)))
        m_new = jnp.maximum(m, jnp.max(m8, axis=0, keepdims=True))
        p = jnp.concatenate([probs(s_refs[slot][r:r + rows, :], m_new) for r in range(0, tk, rows)], axis=0)
        acc_ref[...] = jnp.exp2(m - m_new) * acc_ref[...] + jnp.dot(vt_ref[0, 0, j], p, preferred_element_type=F32)
        return m_new, m8_next

    def body(i, carry):
        for u in range(unroll):
            carry = tile(unroll * i + u, u % 2, carry)
        return carry

    lax.fori_loop(0, nk // unroll, body, (m, m8_first))
    acc = acc_ref[...]
    o_ref[0] = (acc[:MLA_V] * (1.0 / acc[MLA_V:MLA_V + 1])).T.astype(o_ref.dtype)


def _attention(q, k, vt, kc, vct, tq):
    b, t, _ = k.shape
    lc = kc.shape[1]
    nk, vrows, tk = vt.shape[2], vt.shape[3], vt.shape[4]
    unroll = ATTN_UNROLL if nk % ATTN_UNROLL == 0 else 2
    assert nk % unroll == 0
    kern = functools.partial(_attn_kernel, nk=nk, unroll=unroll)
    return pl.pallas_call(
        kern,
        grid=(b, MLA_HEADS, t // tq),
        in_specs=[pl.BlockSpec((1, tq, MLA_QK_PAD), lambda bi, h, i: (bi, i, h)),
                  pl.BlockSpec((1, lc, MLA_QK_PAD), lambda bi, h, i: (bi, 0, h)),
                  pl.BlockSpec((1, 1, vrows, lc), lambda bi, h, i: (bi, h, 0, 0)),
                  pl.BlockSpec((1, t, MLA_QK_PAD), lambda bi, h, i: (bi, 0, h)),
                  pl.BlockSpec((1, 1, nk, vrows, tk), lambda bi, h, i: (bi, h, 0, 0, 0))],
        out_specs=pl.BlockSpec((1, tq, MLA_V), lambda bi, h, i: (bi, i, h)),
        out_shape=jax.ShapeDtypeStruct((b, t, MLA_HEADS * MLA_V), BF16),
        scratch_shapes=[pltpu.VMEM((MLA_QK_PAD, tq), BF16), pltpu.VMEM((tk, tq), F32), pltpu.VMEM((tk, tq), F32),
                        pltpu.VMEM((vrows, tq), F32)],
        compiler_params=_params(("parallel", "parallel", "arbitrary")),
        name="attention",
    )(q, kc, vct, k, vt)


def _split2(x):
    hi = x.astype(BF16)
    return hi, (x - hi.astype(F32)).astype(BF16)


def _bmm(a, b):
    return lax.dot_general(a, b, (((2,), (1,)), ((0,), (0,))), preferred_element_type=F32)


def _bmm_nt(a, b):
    return lax.dot_general(a, b, (((2,), (2,)), ((0,), (0,))), preferred_element_type=F32)


def _bmm_hp(a, b):
    ah, al = _split2(a)
    bh, bl = _split2(b)
    return _bmm(ah, bh) + _bmm(al, bh) + _bmm(ah, bl)


def _bd2(b):
    lane = lax.broadcasted_iota(jnp.int32, b.shape, 2)
    zero = jnp.zeros_like(b)
    return jnp.concatenate([jnp.where(lane < DN_CHUNK, b, zero), jnp.where(lane >= DN_CHUNK, b, zero)], axis=1)


def _tri_inverse(m, eye):
    assert m.shape[-2] == 64
    x = eye - m
    nb = (-m).astype(BF16)
    bd = _bd2(nb)
    for _ in range(3):
        nb = _bmm(nb, bd).astype(BF16)
        bd = _bd2(nb)
        x = x + _bmm(x.astype(BF16), bd)
    xb = x.astype(BF16)
    r = eye - x - _bmm(m.astype(BF16), _bd2(xb))
    x = x + _bmm(xb, _bd2(r.astype(BF16)))
    mh, ml = _split2(m)
    xh, xl = _split2(x)
    bdh = _bd2(xh)
    r = eye - x - (_bmm(mh, bdh) + _bmm(ml, bdh) + _bmm(mh, _bd2(xl)))
    return x + _bmm(xh, _bd2(r.astype(BF16)))


def _dn_prep_kernel(qm_ref, qp_ref, qx_ref, km_ref, kp_ref, kx_ref, vm_ref, vp_ref, vx_ref,
                    cwq_ref, cwk_ref, cwv_ref, ab_ref, hp_ref,
                    u_ref, w_ref, ke_ref, qg_ref, a_ref, ge_ref, ext_ref, *, tm, tiles_per_seq):
    i = pl.program_id(0)
    first = (i % tiles_per_seq) == 0
    last = (i % tiles_per_seq) == tiles_per_seq - 1
    c = DN_CHUNK

    def conv_silu(main_ref, prev_ref, next_ref, cw_ref):
        ext_ref[0:HALO16, :] = jnp.where(first, 0.0, prev_ref[...].astype(F32))
        ext_ref[HALO16:HALO16 + tm, :] = main_ref[...].astype(F32)
        ext_ref[HALO16 + tm:, :] = jnp.where(last, 0.0, next_ref[...].astype(F32))
        y = (cw_ref[0:1, :] * ext_ref[HALO16 - 1:HALO16 - 1 + tm, :]
             + cw_ref[1:2, :] * ext_ref[HALO16:HALO16 + tm, :]
             + cw_ref[2:3, :] * ext_ref[HALO16 + 1:HALO16 + 1 + tm, :])
        return _silu(y)

    def l2n(t):
        return t * lax.rsqrt(jnp.sum(t * t, axis=-1, keepdims=True) + EPS)

    q = l2n(conv_silu(qm_ref, qp_ref, qx_ref, cwq_ref)) * (DN_DK ** -0.5)
    k = l2n(conv_silu(km_ref, kp_ref, kx_ref, cwk_ref))
    v = conv_silu(vm_ref, vp_ref, vx_ref, cwv_ref)

    hp = hp_ref[0]
    ab = ab_ref[0]
    z = ab[:, 0:2] + hp[:, 2:4]
    softplus = jnp.maximum(z, 0.0) + jnp.log(1.0 + jnp.exp(-jnp.abs(z)))
    g_all = -jnp.exp(hp[:, 0:2]) * softplus
    beta_all = jax.nn.sigmoid(ab[:, 2:4])

    nc = tm // c

    def per_chunk(t):
        return t.reshape(nc, c, t.shape[-1])

    def all_lanes(t2, d):
        return per_chunk(jnp.broadcast_to(t2[:, d:d + 1], (tm, LANE)))

    shape = (nc, c, LANE)
    rows = lax.broadcasted_iota(jnp.int32, shape, 1)
    lane = lax.broadcasted_iota(jnp.int32, shape, 2)
    bwd = lane >= c
    col = jnp.where(bwd, lane - c, lane)
    eye_mask = rows == col
    ahead = jnp.where(bwd, rows - col, col - rows)
    incl = ahead <= 0
    strict = ahead < 0
    eye = jnp.where(eye_mask, 1.0, 0.0)
    tri = jnp.where(incl, 1.0, 0.0).astype(BF16)

    def halves(t2):
        return jnp.where(bwd, all_lanes(t2, 1), all_lanes(t2, 0))

    g_hi, g_lo = _split2(halves(g_all))
    gc = _bmm(tri, _bd2(g_hi)) + _bmm(tri, _bd2(g_lo))
    gc_row = jnp.sum(jnp.where(eye_mask, gc, 0.0), axis=1, keepdims=True)
    decay = jnp.where(incl, jnp.exp(jnp.where(incl, gc - gc_row, 0.0)), 0.0)

    k3 = per_chunk(k)
    q3 = per_chunk(q)
    v3 = per_chunk(v)
    k3b = k3.astype(BF16)
    kdup = jnp.concatenate([k3b, k3b], axis=1)
    kk = _bmm_nt(k3b, kdup)
    att = _bmm_nt(q3.astype(BF16), kdup) * decay
    m = jnp.where(strict, halves(beta_all) * kk * decay, 0.0)
    t_hi, t_lo = _split2(_tri_inverse(m, eye))

    swapped = pltpu.roll(gc.reshape(tm, LANE), c, axis=1).reshape(shape)
    gc_dir = (jnp.where(bwd, swapped, gc), jnp.where(bwd, gc, swapped))
    rhs, ke, qg, gend = [], [], [], []
    for d in range(2):
        gcd = gc_dir[d]
        beta = all_lanes(beta_all, d)
        eg = jnp.exp(gcd)
        gtot = gcd[:, c - 1:c, :] if d == 0 else gcd[:, 0:1, :]
        rhs.append(jnp.concatenate([v3 * beta, k3 * beta * eg], axis=2))
        ke.append(k3 * jnp.exp(gtot - gcd))
        qg.append(q3 * eg)
        gend.append(jnp.exp(gtot))
    rhs = jnp.concatenate(rhs, axis=1).astype(BF16)
    sol = _bmm(_bd2(t_hi), rhs) + _bmm(_bd2(t_lo), rhs)
    att_dir = (att, pltpu.roll(att.reshape(tm, LANE), c, axis=1).reshape(shape))
    for d in range(2):
        sl = slice(d * c, (d + 1) * c)
        u_ref[d, 0, 0] = sol[:, sl, :DN_DV].reshape(tm, DN_DV)
        w_ref[d, 0, 0] = sol[:, sl, DN_DV:].reshape(tm, DN_DK).astype(BF16)
        ke_ref[d, 0, 0] = ke[d].reshape(tm, DN_DK).astype(BF16)
        qg_ref[d, 0, 0] = qg[d].reshape(tm, DN_DK).astype(BF16)
        a_ref[d, 0, 0] = att_dir[d][:, :, :c].reshape(tm, c).astype(BF16)
        ge_ref[d, 0, 0] = gend[d]


def _dn_prep(p, ab, hp, conv_w, batch, seq, tm):
    h = DN_HEADS
    nt = seq // tm
    nb16 = tm // HALO16
    total16 = batch * seq // HALO16
    kern = functools.partial(_dn_prep_kernel, tm=tm, tiles_per_seq=nt)

    def triple(col0):
        return [pl.BlockSpec((tm, LANE), lambda i, hh: (i, col0 + hh)),
                pl.BlockSpec((HALO16, LANE), lambda i, hh: (jnp.maximum(i * nb16 - 1, 0), col0 + hh)),
                pl.BlockSpec((HALO16, LANE), lambda i, hh: (jnp.minimum((i + 1) * nb16, total16 - 1), col0 + hh))]

    cq, ck, cv = B_DQ // LANE, B_DK // LANE, B_DV // LANE
    row = lambda shape: pl.BlockSpec(shape, lambda i, hh: (0, i // nt, hh, i % nt, 0))
    seq_shape = lambda width, dt: jax.ShapeDtypeStruct((2, batch, h, seq, width), dt)
    return pl.pallas_call(
        kern,
        grid=(batch * nt, h),
        in_specs=triple(cq) + triple(ck) + triple(cv) + [
            pl.BlockSpec((3, LANE), lambda i, hh: (0, hh)),
            pl.BlockSpec((3, LANE), lambda i, hh: (0, h + hh)),
            pl.BlockSpec((3, LANE), lambda i, hh: (0, 2 * h + hh)),
            pl.BlockSpec((1, tm, 4), lambda i, hh: (hh, i, 0)),
            pl.BlockSpec((1, 1, 4), lambda i, hh: (hh, 0, 0))],
        out_specs=[row((2, 1, 1, tm, DN_DV)), row((2, 1, 1, tm, DN_DK)), row((2, 1, 1, tm, DN_DK)),
                   row((2, 1, 1, tm, DN_DK)), row((2, 1, 1, tm, DN_CHUNK)),
                   pl.BlockSpec((2, 1, 1, tm // DN_CHUNK, 1, LANE), lambda i, hh: (0, i // nt, hh, i % nt, 0, 0))],
        out_shape=[seq_shape(DN_DV, F32), seq_shape(DN_DK, BF16), seq_shape(DN_DK, BF16),
                   seq_shape(DN_DK, BF16), seq_shape(DN_CHUNK, BF16),
                   jax.ShapeDtypeStruct((2, batch, h, seq // DN_CHUNK, 1, LANE), F32)],
        scratch_shapes=[pltpu.VMEM((tm + 2 * HALO16, LANE), F32)],
        compiler_params=_params(("parallel", "parallel")),
        name="dn_prep",
    )(p, p, p, p, p, p, p, p, p, conv_w, conv_w, conv_w, ab, hp)


def _dn_scan_kernel(s0_ref, uf_ref, wf_ref, kf_ref, qf_ref, af_ref, gf_ref,
                    ub_ref, wb_ref, kb_ref, qb_ref, ab_ref, gb_ref,
                    of_ref, ob_ref, sfin_ref, s_ref, *, cs):
    n = pl.program_id(1)
    c = DN_CHUNK
    h = DN_HEADS

    @pl.when(n == 0)
    def _():
        s_ref[0:h] = s0_ref[0, 0]
        s_ref[h:] = s0_ref[1, 0]

    for ci in range(cs):
        rf = ci * c
        rb = (cs - 1 - ci) * c

        def pair(f_ref, b_ref):
            return jnp.concatenate([f_ref[0, 0, :, rf:rf + c, :], b_ref[0, 0, :, rb:rb + c, :]], axis=0)

        s = s_ref[...]
        s16 = s.astype(BF16)
        ws_qs = _bmm(jnp.concatenate([pair(wf_ref, wb_ref), pair(qf_ref, qb_ref)], axis=1), s16)
        v_new = pair(uf_ref, ub_ref) - ws_qs[:, :c]
        v16 = v_new.astype(BF16)
        o = ws_qs[:, c:] + _bmm(pair(af_ref, ab_ref), v16)
        of_ref[0, :, rf:rf + c, :] = o[:h].astype(of_ref.dtype)
        ob_ref[0, :, rb:rb + c, :] = o[h:].astype(ob_ref.dtype)
        g = jnp.concatenate([gf_ref[0, 0, :, ci], gb_ref[0, 0, :, cs - 1 - ci]], axis=0)
        s_ref[...] = s * g + lax.dot_general(pair(kf_ref, kb_ref), v16, (((1,), (1,)), ((0,), (0,))),
                                             preferred_element_type=F32)

    @pl.when(n == pl.num_programs(1) - 1)
    def _():
        sfin_ref[0, 0] = s_ref[0:h]
        sfin_ref[1, 0] = s_ref[h:]


def _dn_scan(s0, u, w, ke, qg, a, ge, cs):
    _, batch, h, seq, _ = u.shape
    ts = cs * DN_CHUNK
    ns = seq // ts
    kern = functools.partial(_dn_scan_kernel, cs=cs)

    def specs(d):
        idx = (lambda n: n) if d == 0 else (lambda n: ns - 1 - n)
        seqs = [pl.BlockSpec((1, 1, h, ts, width), lambda b, n: (d, b, 0, idx(n), 0))
                for width in (DN_DV, DN_DK, DN_DK, DN_DK, DN_CHUNK)]
        return seqs + [pl.BlockSpec((1, 1, h, cs, 1, LANE), lambda b, n: (d, b, 0, idx(n), 0, 0))]

    o_spec = lambda d: pl.BlockSpec((1, h, ts, DN_DV), lambda b, n: (b, 0, n if d == 0 else ns - 1 - n, 0))
    state = pl.BlockSpec((2, 1, h, DN_DK, DN_DV), lambda b, n: (0, b, 0, 0, 0))
    return pl.pallas_call(
        kern,
        grid=(batch, ns),
        in_specs=[state] + specs(0) + specs(1),
        out_specs=[o_spec(0), o_spec(1), state],
        out_shape=[jax.ShapeDtypeStruct((batch, h, seq, DN_DV), BF16),
                   jax.ShapeDtypeStruct((batch, h, seq, DN_DV), BF16),
                   jax.ShapeDtypeStruct((2, batch, h, DN_DK, DN_DV), F32)],
        scratch_shapes=[pltpu.VMEM((2 * h, DN_DK, DN_DV), F32)],
        compiler_params=_params(("parallel", "arbitrary")),
        name="dn_scan",
    )(s0, u, w, ke, qg, a, ge, u, w, ke, qg, a, ge)


def _merge_kernel(x_ref, mod_ref, ym_ref, of_ref, ob_ref, z_ref, nw_ref, ga_ref, gb_ref, wm_ref, wd_ref, wo_ref,
                  nf_ref, o_ref, xn_ref, *, k_gate, k_shift, k_scale):
    nw = nw_ref[...]
    heads = []
    for h in range(DN_HEADS):
        o = of_ref[0, h].astype(F32) + ob_ref[0, h].astype(F32)
        y = o * lax.rsqrt(jnp.mean(o * o, axis=-1, keepdims=True) + EPS) * nw
        heads.append((y * _silu(z_ref[:, h * DN_DV:(h + 1) * DN_DV].astype(F32))).astype(BF16))
    yd = jnp.concatenate(heads, axis=1)
    pm = jnp.dot(ym_ref[...], wm_ref[...], preferred_element_type=F32)
    pd = jnp.dot(yd, wd_ref[...], preferred_element_type=F32)
    merged = (jax.nn.sigmoid(ga_ref[...].astype(F32)) * pm
              + jax.nn.sigmoid(gb_ref[...].astype(F32)) * pd).astype(BF16)
    y = jnp.dot(merged, wo_ref[...], preferred_element_type=F32)
    x1 = x_ref[...] + mod_ref[0, k_gate:k_gate + 1, :] * y
    o_ref[...] = x1
    xn_ref[...] = _norm_mod(x1, nf_ref[...], mod_ref[0, k_shift:k_shift + 1, :],
                            mod_ref[0, k_scale:k_scale + 1, :]).astype(BF16)


def _merge(x, mod, row_of_tile, ym, o_f, o_b, p, nw, wm, wd, wo, nf, tm, k_gate, k_shift, k_scale):
    m, d = x.shape
    _, h, seq, _ = o_f.shape
    nt = seq // tm
    zw = h * DN_DV
    kern = functools.partial(_merge_kernel, k_gate=k_gate, k_shift=k_shift, k_scale=k_scale)
    const = lambda arr: pl.BlockSpec(arr.shape, lambda i: (0, 0), pipeline_mode=pl.Buffered(1))
    scan_out = pl.BlockSpec((1, h, tm, DN_DV), lambda i: (i // nt, 0, i % nt, 0))
    return pl.pallas_call(
        kern,
        grid=(m // tm,),
        in_specs=[pl.BlockSpec((tm, d), lambda i: (i, 0)),
                  pl.BlockSpec((1, 6, d), lambda i: (row_of_tile(i), 0, 0)),
                  pl.BlockSpec((tm, ym.shape[1]), lambda i: (i, 0)),
                  scan_out, scan_out,
                  pl.BlockSpec((tm, zw), lambda i: (i, B_DZ // zw)),
                  pl.BlockSpec((1, DN_DV), lambda i: (0, 0)),
                  pl.BlockSpec((tm, d), lambda i: (i, B_GA // d)),
                  pl.BlockSpec((tm, d), lambda i: (i, B_GB // d)),
                  const(wm), const(wd), const(wo),
                  pl.BlockSpec((1, d), lambda i: (0, 0))],
        out_specs=[pl.BlockSpec((tm, d), lambda i: (i, 0)), pl.BlockSpec((tm, d), lambda i: (i, 0))],
        out_shape=[jax.ShapeDtypeStruct((m, d), F32), jax.ShapeDtypeStruct((m, d), BF16)],
        compiler_params=_params(("parallel",)),
        name="merge",
    )(x, mod, ym, o_f, o_b, p, nw, p, p, wm, wd, wo, nf)


def _ffn_kernel(x_ref, xm_ref, xp_ref, xx_ref, mod_ref, wg_ref, wv_ref, cw_ref, wdn_ref, o_ref, xn_ref, ge_ref,
                *, tm, tiles_per_seq, k_gate):
    i = pl.program_id(0)
    j = pl.program_id(1)
    th = FFN_TILE

    @pl.when(j == 0)
    def _():
        first = (i % tiles_per_seq) == 0
        last = (i % tiles_per_seq) == tiles_per_seq - 1
        xn_ref[0:HALO16, :] = jnp.where(first, jnp.zeros_like(xp_ref), xp_ref[...])
        xn_ref[HALO16:HALO16 + tm, :] = xm_ref[...]
        xn_ref[HALO16 + tm:, :] = jnp.where(last, jnp.zeros_like(xx_ref), xx_ref[...])
        o_ref[...] = jnp.zeros_like(o_ref)

    ge_ref[...] = jnp.dot(xn_ref[...], wg_ref[...], preferred_element_type=F32)
    val = jnp.dot(xn_ref[HALO16:HALO16 + tm, :], wv_ref[...], preferred_element_type=F32)
    conv = (cw_ref[0:1, :] * ge_ref[HALO16 - 1:HALO16 - 1 + tm, :]
            + cw_ref[1:2, :] * ge_ref[HALO16:HALO16 + tm, :]
            + cw_ref[2:3, :] * ge_ref[HALO16 + 1:HALO16 + 1 + tm, :])
    hid = (_silu(conv) * val).astype(BF16)
    o_ref[...] += jnp.dot(hid, wdn_ref[...], preferred_element_type=F32)

    @pl.when(j == pl.num_programs(1) - 1)
    def _():
        o_ref[...] = x_ref[...] + mod_ref[0, k_gate:k_gate + 1, :] * o_ref[...]


def _ffn(x, xn, mod, row_of_tile, wg, wv, cw, wdn, seq, tm, k_gate):
    m, d = x.shape
    th = FFN_TILE
    nj = wdn.shape[0] // th
    nt = seq // tm
    nb16 = tm // HALO16
    total16 = m // HALO16
    kern = functools.partial(_ffn_kernel, tm=tm, tiles_per_seq=nt, k_gate=k_gate)
    return pl.pallas_call(
        kern,
        grid=(m // tm, nj),
        in_specs=[pl.BlockSpec((tm, d), lambda i, j: (i, 0)),
                  pl.BlockSpec((tm, d), lambda i, j: (i, 0)),
                  pl.BlockSpec((HALO16, d), lambda i, j: (jnp.maximum(i * nb16 - 1, 0), 0)),
                  pl.BlockSpec((HALO16, d), lambda i, j: (jnp.minimum((i + 1) * nb16, total16 - 1), 0)),
                  pl.BlockSpec((1, 6, d), lambda i, j: (row_of_tile(i), 0, 0)),
                  pl.BlockSpec((d, th), lambda i, j: (0, j)),
                  pl.BlockSpec((d, th), lambda i, j: (0, j)),
                  pl.BlockSpec((3, th), lambda i, j: (0, j)),
                  pl.BlockSpec((th, d), lambda i, j: (j, 0))],
        out_specs=pl.BlockSpec((tm, d), lambda i, j: (i, 0)),
        out_shape=jax.ShapeDtypeStruct((m, d), F32),
        scratch_shapes=[pltpu.VMEM((tm + 2 * HALO16, d), BF16), pltpu.VMEM((tm + 2 * HALO16, th), F32)],
        compiler_params=_params(("parallel", "arbitrary")),
        name="ffn",
    )(x, xn, xn, xn, mod, wg, wv, cw, wdn)


def _rope_pad_cols(r):
    n = MLA_ROPE // 4
    z = jnp.zeros(r.shape[:-1] + (2 * n,), r.dtype)
    return jnp.concatenate([r[..., 0:n], r[..., 2 * n:3 * n], z, r[..., n:2 * n], r[..., 3 * n:4 * n], z], axis=-1)


def _qk_pad_cols(w):
    lead = w.shape[:-1]
    w = w.reshape(lead + (MLA_HEADS, MLA_QK))
    out = jnp.concatenate([w[..., :MLA_NOPE], _rope_pad_cols(w[..., MLA_NOPE:])], axis=-1)
    return out.reshape(lead + (MLA_HEADS * MLA_QK_PAD,))


def _rope_tables(rows):
    n = MLA_ROPE // 4
    inv = ROPE_BASE ** (-jnp.arange(n, dtype=F32) / n)
    ang_r = jnp.arange(rows, dtype=F32)[:, None] * inv
    ang_c = jnp.arange(GRID_W, dtype=F32)[:, None] * inv

    def lanes(a_row, a_col, fill, count):
        return jnp.concatenate([a_row, a_col, jnp.full((count, 2 * n), fill, F32)] * 2, axis=-1)

    zr, zc = jnp.zeros((rows, n), F32), jnp.zeros((GRID_W, n), F32)
    row_cos = lanes(jnp.cos(ang_r), zr, 0.0, rows)
    row_sin = jnp.concatenate([-jnp.sin(ang_r), zr, jnp.zeros((rows, 2 * n), F32),
                               jnp.sin(ang_r), zr, jnp.zeros((rows, 2 * n), F32)], axis=-1)
    col_cos = lanes(zc, jnp.cos(ang_c), 1.0, GRID_W)
    col_sin = jnp.concatenate([zc, -jnp.sin(ang_c), jnp.zeros((GRID_W, 2 * n), F32),
                               zc, jnp.sin(ang_c), jnp.zeros((GRID_W, 2 * n), F32)], axis=-1)
    return row_cos, row_sin, col_cos, col_sin


def _pick_tile(n, pref):
    t = min(pref, n)
    while n % t:
        t //= 2
    return t


def kernel(x, c, ctx, c_ctx, w_ada, b_ada, norm_mix, w_in, q_a_norm, w_q_b, kv_a_norm, w_kv_b, q_norm, k_norm,
           w_o_mla, dn_conv, dn_a_log, dn_dt_bias, dn_o_norm, w_o_dn, w_out, norm_ffn, w_ffn_up, ffn_conv,
           w_ffn_down):
    batch, seq, d = x.shape
    lc = ctx.shape[1]
    assert w_ada.shape[0] == 1, "single-layer stack"
    assert seq % GRID_W == 0 and seq % DN_CHUNK == 0 and lc % DN_CHUNK == 0
    h = DN_HEADS
    l = 0

    wi = w_in[l]
    o_qa, o_kva, o_kr = 0, MLA_Q_RANK, MLA_Q_RANK + MLA_KV_RANK
    o_dq = o_kr + MLA_ROPE
    o_da = o_dq + 4 * h * DN_DK
    o_ga = o_da + 4 * h
    w_a = jnp.concatenate([wi[:, o_qa:o_kr], _rope_pad_cols(wi[:, o_kr:o_dq]), wi[:, o_da:o_ga],
                           jnp.zeros((d, A_WIDTH - A_DB - 2 * h), F32)], axis=1).astype(BF16)
    w_b = jnp.concatenate([wi[:, o_dq:o_da], wi[:, o_ga:]], axis=1).astype(BF16)
    wq = _qk_pad_cols(w_q_b[l]).astype(BF16)
    wkv = w_kv_b[l].reshape(MLA_KV_RANK, MLA_HEADS, MLA_NOPE + MLA_V)
    wkv = jnp.concatenate([wkv[:, :, :MLA_NOPE].reshape(MLA_KV_RANK, -1),
                           wkv[:, :, MLA_NOPE:].reshape(MLA_KV_RANK, -1)], axis=1).astype(BF16)
    qn = jnp.concatenate([q_norm[l, :MLA_NOPE], _rope_pad_cols(q_norm[l, MLA_NOPE:])])[None, :]
    kn = jnp.concatenate([k_norm[l, :MLA_NOPE], _rope_pad_cols(k_norm[l, MLA_NOPE:])])[None, :]
    hidden = w_ffn_down.shape[1]
    nj = -(-hidden // FFN_TILE)
    hpad = nj * FFN_TILE - hidden
    wg = jnp.pad(w_ffn_up[l][:, :hidden].astype(BF16), ((0, 0), (0, hpad)))
    wv = jnp.pad(w_ffn_up[l][:, hidden:].astype(BF16), ((0, 0), (0, hpad)))
    fcw = jnp.pad(ffn_conv[l], ((0, 0), (0, hpad)))
    wdn = jnp.pad(w_ffn_down[l], ((0, hpad), (0, 0))).astype(BF16)
    hp = jnp.concatenate([dn_a_log[l].T, dn_dt_bias[l].T], axis=1)[:, None, :]
    tables_l = _rope_tables(seq // GRID_W)
    tables_c = (jnp.zeros((lc // GRID_W, ROPE_PAD), F32), jnp.zeros((lc // GRID_W, ROPE_PAD), F32),
                jnp.ones((GRID_W, ROPE_PAD), F32), jnp.zeros((GRID_W, ROPE_PAD), F32))

    rows = jnp.concatenate([c, c_ctx[None, :], jnp.zeros((8 - batch - 1, d), F32)], axis=0)
    mod = _ada(rows, w_ada[l], b_ada[l][None, :]).reshape(8, 6, d)

    xf = x.reshape(batch * seq, d)
    cf = ctx.reshape(batch * lc, d)

    def stream(tokens, n, tm_a, tm_b, row_of):
        nm = norm_mix[l][None, :]
        a, xn = _norm_mod_matmul(tokens, nm, mod, row_of(tm_a), w_a, F32, tm_a, 0, 1, "in_proj_small")
        p = _matmul(xn, w_b, BF16, tm_b, 1024, "in_proj_wide")
        ab = a[:, A_DA:A_DA + 4 * h].reshape(batch * n, 4, h).transpose(2, 0, 1)
        return a, p, ab

    tm_a = _pick_tile(seq, 512)
    tm_b = _pick_tile(seq, 1024)
    lat_row = lambda tm: (lambda i: i // (seq // tm))
    a_l, p_l, ab_l = stream(xf, seq, tm_a, tm_b, lat_row)
    tc = _pick_tile(lc, 256)
    ctx_row = lambda tm: (lambda i: batch)
    a_c, p_c, ab_c = stream(cf, lc, tc, tc, ctx_row)

    qan, kvan = q_a_norm[l][None, :], kv_a_norm[l][None, :]
    tmp = _pick_tile(seq, 512)
    q_l, k_l, v_l = _mla_proj(a_l, tables_l, qan, kvan, qn, kn, wq, wkv, tmp, seq)
    _, k_c, v_c = _mla_proj(a_c, tables_c, qan, kvan, qn, kn, wq, wkv, tc, lc)
    tq = _pick_tile(seq, 512)
    tk = _pick_tile(seq, 512)

    def with_ones_row(vt):
        lead, n = vt.shape[:-2], vt.shape[-1]
        return jnp.concatenate([vt, jnp.ones(lead + (1, n), BF16), jnp.zeros(lead + (HALO16 - 1, n), BF16)],
                               axis=-2)

    vt = with_ones_row(v_l.reshape(batch, seq // tk, tk, MLA_HEADS, MLA_V).transpose(0, 3, 1, 4, 2))
    vct = with_ones_row(v_c.reshape(batch, lc, MLA_HEADS, MLA_V).transpose(0, 2, 3, 1))
    y_mla = _attention(q_l.reshape(batch, seq, -1), k_l.reshape(batch, seq, -1), vt, k_c.reshape(batch, lc, -1), vct, tq)
    y_mla = y_mla.reshape(batch * seq, MLA_HEADS * MLA_V)

    conv_w = dn_conv[l]
    prep_c = _dn_prep(p_c, ab_c, hp, conv_w, batch, lc, _pick_tile(lc, 256))
    s_zero = jnp.zeros((2, batch, h, DN_DK, DN_DV), F32)
    _, _, s_ctx = _dn_scan(s_zero, *prep_c, _pick_tile(lc, 256) // DN_CHUNK)
    prep_l = _dn_prep(p_l, ab_l, hp, conv_w, batch, seq, _pick_tile(seq, 1024))
    o_f, o_b, _ = _dn_scan(s_ctx, *prep_l, _pick_tile(seq, 256) // DN_CHUNK)

    tmm = _pick_tile(seq, 256)
    x1, xn2 = _merge(xf, mod, lat_row(tmm), y_mla, o_f, o_b, p_l, dn_o_norm[l][None, :], w_o_mla[l].astype(BF16),
                     w_o_dn[l].astype(BF16), w_out[l].astype(BF16), norm_ffn[l][None, :], tmm, 2, 3, 4)

    tmf = _pick_tile(seq, 512)
    out = _ffn(x1, xn2, mod, lat_row(tmf), wg, wv, fcw, wdn, seq, tmf, 5)
    return out.reshape(batch, seq, d)
```

```python
import functools
import math

import jax
import jax.numpy as jnp
import numpy as np
from jax import lax
from jax.experimental import pallas as pl
from jax.experimental.pallas import tpu as pltpu

F32 = jnp.float32
BF16 = jnp.bfloat16
HIGHEST = lax.Precision.HIGHEST

EPS = 1e-6
GRID_W = 64
ROPE_BASE = 10000.0

MLA_HEADS = 8
MLA_Q_RANK = 512
MLA_KV_RANK = 512
MLA_NOPE = 128
MLA_ROPE = 64
MLA_V = 128
MLA_QK = MLA_NOPE + MLA_ROPE
MLA_QK_PAD = 256

DN_HEADS = 8
DN_DK = 128
DN_DV = 128
DN_CHUNK = 64

LANE = 128
SUBLANE = 8
ROPE_PAD = 128
VMEM_LIMIT = 56 * 1024 * 1024

A_QA, A_KVA, A_KR, A_DA, A_DB, A_WIDTH = 0, 512, 1024, 1152, 1168, 1280
B_DQ, B_DK, B_DV, B_DZ, B_GA, B_GB, B_WIDTH = 0, 1024, 2048, 3072, 4096, 6144, 8192

FFN_TILE = 512
HALO16 = 16


def _params(sem, vmem=VMEM_LIMIT):
    return pltpu.CompilerParams(dimension_semantics=sem, vmem_limit_bytes=vmem)


def _nt_dot(a, b):
    return lax.dot_general(a, b, (((1,), (1,)), ((), ())), preferred_element_type=F32)


def _silu(x):
    return x * jax.nn.sigmoid(x)


def _ada_kernel(c_ref, w_ref, b_ref, o_ref):
    a = _silu(c_ref[...]).astype(BF16)
    o_ref[...] = jnp.dot(a, w_ref[...].astype(BF16), preferred_element_type=F32) + b_ref[...]


def _ada(cs, w, b):
    m, d = cs.shape
    n = w.shape[1]
    tn = 512
    return pl.pallas_call(
        _ada_kernel,
        grid=(n // tn,),
        in_specs=[pl.BlockSpec((m, d), lambda j: (0, 0)),
                  pl.BlockSpec((d, tn), lambda j: (0, j)),
                  pl.BlockSpec((1, tn), lambda j: (0, j))],
        out_specs=pl.BlockSpec((m, tn), lambda j: (0, j)),
        out_shape=jax.ShapeDtypeStruct((m, n), F32),
        compiler_params=_params(("parallel",)),
        name="ada",
    )(cs, w, b)


def _norm_mod(x, nw, shift, scale):
    ms = jnp.mean(x * x, axis=-1, keepdims=True)
    y = x * lax.rsqrt(ms + EPS) * nw
    return y * (1.0 + scale) + shift


def _nmm_kernel(x_ref, nw_ref, mod_ref, w_ref, o_ref, xn_ref, *, k_shift, k_scale):
    y = _norm_mod(x_ref[...], nw_ref[...], mod_ref[0, k_shift:k_shift + 1, :], mod_ref[0, k_scale:k_scale + 1, :])
    xn = y.astype(BF16)
    xn_ref[...] = xn
    o_ref[...] = jnp.dot(xn, w_ref[...], preferred_element_type=F32).astype(o_ref.dtype)


def _norm_mod_matmul(x, nw, mod, row_of_tile, w, out_dtype, tm, k_shift, k_scale, name):
    m, d = x.shape
    n = w.shape[1]
    kern = functools.partial(_nmm_kernel, k_shift=k_shift, k_scale=k_scale)
    return pl.pallas_call(
        kern,
        grid=(m // tm,),
        in_specs=[pl.BlockSpec((tm, d), lambda i: (i, 0)),
                  pl.BlockSpec((1, d), lambda i: (0, 0)),
                  pl.BlockSpec((1, 6, d), lambda i: (row_of_tile(i), 0, 0)),
                  pl.BlockSpec((d, n), lambda i: (0, 0))],
        out_specs=[pl.BlockSpec((tm, n), lambda i: (i, 0)), pl.BlockSpec((tm, d), lambda i: (i, 0))],
        out_shape=[jax.ShapeDtypeStruct((m, n), out_dtype), jax.ShapeDtypeStruct((m, d), BF16)],
        compiler_params=_params(("parallel",)),
        name=name,
    )(x, nw, mod, w)


def _mm_kernel(x_ref, w_ref, o_ref):
    o_ref[...] = jnp.dot(x_ref[...], w_ref[...], preferred_element_type=F32).astype(o_ref.dtype)


def _matmul(x, w, out_dtype, tm, tn, name):
    m, d = x.shape
    n = w.shape[1]
    return pl.pallas_call(
        _mm_kernel,
        grid=(m // tm, n // tn),
        in_specs=[pl.BlockSpec((tm, d), lambda i, j: (i, 0)),
                  pl.BlockSpec((d, tn), lambda i, j: (0, j))],
        out_specs=pl.BlockSpec((tm, tn), lambda i, j: (i, j)),
        out_shape=jax.ShapeDtypeStruct((m, n), out_dtype),
        compiler_params=_params(("parallel", "parallel")),
        name=name,
    )(x, w)


def _mla_proj_kernel(a_ref, cr_ref, sr_ref, cc_ref, sc_ref, qan_ref, kvan_ref, qn_ref, kn_ref, wq_ref, wkv_ref,
                     q_ref, k_ref, v_ref):
    tm = a_ref.shape[0]
    g = tm // GRID_W

    def table(row_ref, col_ref):
        r = jnp.broadcast_to(row_ref[...][:, None, :], (g, GRID_W, ROPE_PAD)).reshape(tm, ROPE_PAD)
        c = jnp.broadcast_to(col_ref[...][None], (g, GRID_W, ROPE_PAD)).reshape(tm, ROPE_PAD)
        return r + c

    cos = table(cr_ref, cc_ref)
    sin = table(sr_ref, sc_ref)

    def rope(r):
        return r * cos + pltpu.roll(r, 64, axis=1) * sin

    def rms_rows(t, w):
        return t * lax.rsqrt(jnp.mean(t * t, axis=-1, keepdims=True) + EPS) * w

    qa = rms_rows(a_ref[:, A_QA:A_QA + MLA_Q_RANK], qan_ref[...]).astype(BF16)
    q = jnp.dot(qa, wq_ref[...], preferred_element_type=F32)
    kva = rms_rows(a_ref[:, A_KVA:A_KVA + MLA_KV_RANK], kvan_ref[...]).astype(BF16)
    kv = jnp.dot(kva, wkv_ref[...], preferred_element_type=F32)
    kr = a_ref[:, A_KR:A_KR + ROPE_PAD]
    kr_ss = jnp.sum(kr * kr, axis=-1, keepdims=True)
    qn = qn_ref[...]
    kn = kn_ref[...]
    scale = MLA_QK ** -0.5 * math.log2(math.e)
    for h in range(MLA_HEADS):
        qh = q[:, h * MLA_QK_PAD:(h + 1) * MLA_QK_PAD]
        inv = lax.rsqrt(jnp.sum(qh * qh, axis=-1, keepdims=True) * (1.0 / MLA_QK) + EPS) * scale
        qh = qh * inv * qn
        q_ref[:, h * MLA_QK_PAD:h * MLA_QK_PAD + MLA_NOPE] = qh[:, :MLA_NOPE].astype(BF16)
        q_ref[:, h * MLA_QK_PAD + MLA_NOPE:(h + 1) * MLA_QK_PAD] = rope(qh[:, MLA_NOPE:]).astype(BF16)

        kh = kv[:, h * MLA_NOPE:(h + 1) * MLA_NOPE]
        inv = lax.rsqrt((jnp.sum(kh * kh, axis=-1, keepdims=True) + kr_ss) * (1.0 / MLA_QK) + EPS)
        k_ref[:, h * MLA_QK_PAD:h * MLA_QK_PAD + MLA_NOPE] = (kh * inv * kn[:, :MLA_NOPE]).astype(BF16)
        k_ref[:, h * MLA_QK_PAD + MLA_NOPE:(h + 1) * MLA_QK_PAD] = rope(
            kr * inv * kn[:, MLA_NOPE:]).astype(BF16)
    v_ref[...] = kv[:, MLA_HEADS * MLA_NOPE:].astype(BF16)


def _mla_proj(a, tables, qan, kvan, qn, kn, wq, wkv, tm, seq):
    m = a.shape[0]
    nseq = seq // tm
    row_cos, row_sin, col_cos, col_sin = tables
    full = lambda arr: pl.BlockSpec(arr.shape, lambda i: (0, 0))
    row_spec = pl.BlockSpec((tm // GRID_W, ROPE_PAD), lambda i: (i % nseq, 0))
    return pl.pallas_call(
        _mla_proj_kernel,
        grid=(m // tm,),
        in_specs=[pl.BlockSpec((tm, A_WIDTH), lambda i: (i, 0)),
                  row_spec, row_spec, full(col_cos), full(col_sin),
                  full(qan), full(kvan), full(qn), full(kn), full(wq), full(wkv)],
        out_specs=[pl.BlockSpec((tm, MLA_HEADS * MLA_QK_PAD), lambda i: (i, 0)),
                   pl.BlockSpec((tm, MLA_HEADS * MLA_QK_PAD), lambda i: (i, 0)),
                   pl.BlockSpec((tm, MLA_HEADS * MLA_V), lambda i: (i, 0))],
        out_shape=[jax.ShapeDtypeStruct((m, MLA_HEADS * MLA_QK_PAD), BF16),
                   jax.ShapeDtypeStruct((m, MLA_HEADS * MLA_QK_PAD), BF16),
                   jax.ShapeDtypeStruct((m, MLA_HEADS * MLA_V), BF16)],
        compiler_params=_params(("parallel",)),
        name="mla_proj",
    )(a, row_cos, row_sin, col_cos, col_sin, qan, kvan, qn, kn, wq, wkv)


ATTN_UNROLL = 16
ATTN_ROWS = 64


def _attn_kernel(q_ref, kc_ref, vct_ref, k_ref, vt_ref, o_ref, qt_ref, s0_ref, s1_ref, acc_ref, *, nk, unroll):
    qt_ref[...] = q_ref[0].T
    tq = qt_ref.shape[1]
    tk = s0_ref.shape[0]
    rows = ATTN_ROWS
    s_refs = (s0_ref, s1_ref)

    def scores(j):
        return jnp.dot(k_ref[0, pl.ds(pl.multiple_of(j * tk, tk), tk), :], qt_ref[...], preferred_element_type=F32)

    def probs(s, m):
        return jnp.exp2((s - m).astype(BF16))

    def store_scores(slot, s):
        s_refs[slot][...] = s
        return jnp.max(s.reshape(tk // SUBLANE, SUBLANE, tq), axis=0)

    s = jnp.dot(kc_ref[0], qt_ref[...], preferred_element_type=F32)
    m8_first = store_scores(0, scores(0))
    m = jnp.max(s, axis=0, keepdims=True)
    acc_ref[...] = jnp.dot(vct_ref[0, 0], probs(s, m), preferred_element_type=F32)

    def tile(j, slot, carry):
        m, m8 = carry
        m8_next = store_scores(1 - slot, scores(jnp.minimum(j + 1, nk - 1)))
        m_new = jnp.maximum(m, jnp.max(m8, axis=0, keepdims=True))
        p = jnp.concatenate([probs(s_refs[slot][r:r + rows, :], m_new) for r in range(0, tk, rows)], axis=0)
        acc_ref[...] = jnp.exp2(m - m_new) * acc_ref[...] + jnp.dot(vt_ref[0, 0, j], p, preferred_element_type=F32)
        return m_new, m8_next

    def body(i, carry):
        for u in range(unroll):
            carry = tile(unroll * i + u, u % 2, carry)
        return carry

    lax.fori_loop(0, nk // unroll, body, (m, m8_first))
    acc = acc_ref[...]
    o_ref[0] = (acc[:MLA_V] * (1.0 / acc[MLA_V:MLA_V + 1])).T.astype(o_ref.dtype)


def _attention(q, k, vt, kc, vct, tq):
    b, t, _ = k.shape
    lc = kc.shape[1]
    nk, vrows, tk = vt.shape[2], vt.shape[3], vt.shape[4]
    unroll = ATTN_UNROLL if nk % ATTN_UNROLL == 0 else 2
    assert nk % unroll == 0
    kern = functools.partial(_attn_kernel, nk=nk, unroll=unroll)
    return pl.pallas_call(
        kern,
        grid=(b, MLA_HEADS, t // tq),
        in_specs=[pl.BlockSpec((1, tq, MLA_QK_PAD), lambda bi, h, i: (bi, i, h)),
                  pl.BlockSpec((1, lc, MLA_QK_PAD), lambda bi, h, i: (bi, 0, h)),
                  pl.BlockSpec((1, 1, vrows, lc), lambda bi, h, i: (bi, h, 0, 0)),
                  pl.BlockSpec((1, t, MLA_QK_PAD), lambda bi, h, i: (bi, 0, h)),
                  pl.BlockSpec((1, 1, nk, vrows, tk), lambda bi, h, i: (bi, h, 0, 0, 0))],
        out_specs=pl.BlockSpec((1, tq, MLA_V), lambda bi, h, i: (bi, i, h)),
        out_shape=jax.ShapeDtypeStruct((b, t, MLA_HEADS * MLA_V), BF16),
        scratch_shapes=[pltpu.VMEM((MLA_QK_PAD, tq), BF16), pltpu.VMEM((tk, tq), F32), pltpu.VMEM((tk, tq), F32),
                        pltpu.VMEM((vrows, tq), F32)],
        compiler_params=_params(("parallel", "parallel", "arbitrary")),
        name="attention",
    )(q, kc, vct, k, vt)


def _split2(x):
    hi = x.astype(BF16)
    return hi, (x - hi.astype(F32)).astype(BF16)


def _bmm(a, b):
    return lax.dot_general(a, b, (((2,), (1,)), ((0,), (0,))), preferred_element_type=F32)


def _bmm_nt(a, b):
    return lax.dot_general(a, b, (((2,), (2,)), ((0,), (0,))), preferred_element_type=F32)


def _bmm_hp(a, b):
    ah, al = _split2(a)
    bh, bl = _split2(b)
    return _bmm(ah, bh) + _bmm(al, bh) + _bmm(ah, bl)


def _bd2(b):
    lane = lax.broadcasted_iota(jnp.int32, b.shape, 2)
    zero = jnp.zeros_like(b)
    return jnp.concatenate([jnp.where(lane < DN_CHUNK, b, zero), jnp.where(lane >= DN_CHUNK, b, zero)], axis=1)


def _tri_inverse(m, eye):
    assert m.shape[-2] == 64
    x = eye - m
    nb = (-m).astype(BF16)
    bd = _bd2(nb)
    for _ in range(3):
        nb = _bmm(nb, bd).astype(BF16)
        bd = _bd2(nb)
        x = x + _bmm(x.astype(BF16), bd)
    xb = x.astype(BF16)
    r = eye - x - _bmm(m.astype(BF16), _bd2(xb))
    x = x + _bmm(xb, _bd2(r.astype(BF16)))
    mh, ml = _split2(m)
    xh, xl = _split2(x)
    bdh = _bd2(xh)
    r = eye - x - (_bmm(mh, bdh) + _bmm(ml, bdh) + _bmm(mh, _bd2(xl)))
    return x + _bmm(xh, _bd2(r.astype(BF16)))


def _dn_prep_kernel(qm_ref, qp_ref, qx_ref, km_ref, kp_ref, kx_ref, vm_ref, vp_ref, vx_ref,
                    cwq_ref, cwk_ref, cwv_ref, ab_ref, hp_ref,
                    u_ref, w_ref, ke_ref, qg_ref, a_ref, ge_ref, ext_ref, *, tm, tiles_per_seq):
    i = pl.program_id(0)
    first = (i % tiles_per_seq) == 0
    last = (i % tiles_per_seq) == tiles_per_seq - 1
    c = DN_CHUNK

    def conv_silu(main_ref, prev_ref, next_ref, cw_ref):
        ext_ref[0:HALO16, :] = jnp.where(first, 0.0, prev_ref[...].astype(F32))
        ext_ref[HALO16:HALO16 + tm, :] = main_ref[...].astype(F32)
        ext_ref[HALO16 + tm:, :] = jnp.where(last, 0.0, next_ref[...].astype(F32))
        y = (cw_ref[0:1, :] * ext_ref[HALO16 - 1:HALO16 - 1 + tm, :]
             + cw_ref[1:2, :] * ext_ref[HALO16:HALO16 + tm, :]
             + cw_ref[2:3, :] * ext_ref[HALO16 + 1:HALO16 + 1 + tm, :])
        return _silu(y)

    def l2n(t):
        return t * lax.rsqrt(jnp.sum(t * t, axis=-1, keepdims=True) + EPS)

    q = l2n(conv_silu(qm_ref, qp_ref, qx_ref, cwq_ref)) * (DN_DK ** -0.5)
    k = l2n(conv_silu(km_ref, kp_ref, kx_ref, cwk_ref))
    v = conv_silu(vm_ref, vp_ref, vx_ref, cwv_ref)

    hp = hp_ref[0]
    ab = ab_ref[0]
    z = ab[:, 0:2] + hp[:, 2:4]
    softplus = jnp.maximum(z, 0.0) + jnp.log(1.0 + jnp.exp(-jnp.abs(z)))
    g_all = -jnp.exp(hp[:, 0:2]) * softplus
    beta_all = jax.nn.sigmoid(ab[:, 2:4])

    nc = tm // c

    def per_chunk(t):
        return t.reshape(nc, c, t.shape[-1])

    def all_lanes(t2, d):
        return per_chunk(jnp.broadcast_to(t2[:, d:d + 1], (tm, LANE)))

    shape = (nc, c, LANE)
    rows = lax.broadcasted_iota(jnp.int32, shape, 1)
    lane = lax.broadcasted_iota(jnp.int32, shape, 2)
    bwd = lane >= c
    col = jnp.where(bwd, lane - c, lane)
    eye_mask = rows == col
    ahead = jnp.where(bwd, rows - col, col - rows)
    incl = ahead <= 0
    strict = ahead < 0
    eye = jnp.where(eye_mask, 1.0, 0.0)
    tri = jnp.where(incl, 1.0, 0.0).astype(BF16)

    def halves(t2):
        return jnp.where(bwd, all_lanes(t2, 1), all_lanes(t2, 0))

    g_hi, g_lo = _split2(halves(g_all))
    gc = _bmm(tri, _bd2(g_hi)) + _bmm(tri, _bd2(g_lo))
    gc_row = jnp.sum(jnp.where(eye_mask, gc, 0.0), axis=1, keepdims=True)
    decay = jnp.where(incl, jnp.exp(jnp.where(incl, gc - gc_row, 0.0)), 0.0)

    k3 = per_chunk(k)
    q3 = per_chunk(q)
    v3 = per_chunk(v)
    k3b = k3.astype(BF16)
    kdup = jnp.concatenate([k3b, k3b], axis=1)
    kk = _bmm_nt(k3b, kdup)
    att = _bmm_nt(q3.astype(BF16), kdup) * decay
    m = jnp.where(strict, halves(beta_all) * kk * decay, 0.0)
    t_hi, t_lo = _split2(_tri_inverse(m, eye))

    swapped = pltpu.roll(gc.reshape(tm, LANE), c, axis=1).reshape(shape)
    gc_dir = (jnp.where(bwd, swapped, gc), jnp.where(bwd, gc, swapped))
    rhs, ke, qg, gend = [], [], [], []
    for d in range(2):
        gcd = gc_dir[d]
        beta = all_lanes(beta_all, d)
        eg = jnp.exp(gcd)
        gtot = gcd[:, c - 1:c, :] if d == 0 else gcd[:, 0:1, :]
        rhs.append(jnp.concatenate([v3 * beta, k3 * beta * eg], axis=2))
        ke.append(k3 * jnp.exp(gtot - gcd))
        qg.append(q3 * eg)
        gend.append(jnp.exp(gtot))
    rhs = jnp.concatenate(rhs, axis=1).astype(BF16)
    sol = _bmm(_bd2(t_hi), rhs) + _bmm(_bd2(t_lo), rhs)
    att_dir = (att, pltpu.roll(att.reshape(tm, LANE), c, axis=1).reshape(shape))
    for d in range(2):
        sl = slice(d * c, (d + 1) * c)
        u_ref[d, 0, 0] = sol[:, sl, :DN_DV].reshape(tm, DN_DV)
        w_ref[d, 0, 0] = sol[:, sl, DN_DV:].reshape(tm, DN_DK).astype(BF16)
        ke_ref[d, 0, 0] = ke[d].reshape(tm, DN_DK).astype(BF16)
        qg_ref[d, 0, 0] = qg[d].reshape(tm, DN_DK).astype(BF16)
        a_ref[d, 0, 0] = att_dir[d][:, :, :c].reshape(tm, c).astype(BF16)
        ge_ref[d, 0, 0] = gend[d]


def _dn_prep(p, ab, hp, conv_w, batch, seq, tm):
    h = DN_HEADS
    nt = seq // tm
    nb16 = tm // HALO16
    total16 = batch * seq // HALO16
    kern = functools.partial(_dn_prep_kernel, tm=tm, tiles_per_seq=nt)

    def triple(col0):
        return [pl.BlockSpec((tm, LANE), lambda i, hh: (i, col0 + hh)),
                pl.BlockSpec((HALO16, LANE), lambda i, hh: (jnp.maximum(i * nb16 - 1, 0), col0 + hh)),
                pl.BlockSpec((HALO16, LANE), lambda i, hh: (jnp.minimum((i + 1) * nb16, total16 - 1), col0 + hh))]

    cq, ck, cv = B_DQ // LANE, B_DK // LANE, B_DV // LANE
    row = lambda shape: pl.BlockSpec(shape, lambda i, hh: (0, i // nt, hh, i % nt, 0))
    seq_shape = lambda width, dt: jax.ShapeDtypeStruct((2, batch, h, seq, width), dt)
    return pl.pallas_call(
        kern,
        grid=(batch * nt, h),
        in_specs=triple(cq) + triple(ck) + triple(cv) + [
            pl.BlockSpec((3, LANE), lambda i, hh: (0, hh)),
            pl.BlockSpec((3, LANE), lambda i, hh: (0, h + hh)),
            pl.BlockSpec((3, LANE), lambda i, hh: (0, 2 * h + hh)),
            pl.BlockSpec((1, tm, 4), lambda i, hh: (hh, i, 0)),
            pl.BlockSpec((1, 1, 4), lambda i, hh: (hh, 0, 0))],
        out_specs=[row((2, 1, 1, tm, DN_DV)), row((2, 1, 1, tm, DN_DK)), row((2, 1, 1, tm, DN_DK)),
                   row((2, 1, 1, tm, DN_DK)), row((2, 1, 1, tm, DN_CHUNK)),
                   pl.BlockSpec((2, 1, 1, tm // DN_CHUNK, 1, LANE), lambda i, hh: (0, i // nt, hh, i % nt, 0, 0))],
        out_shape=[seq_shape(DN_DV, F32), seq_shape(DN_DK, BF16), seq_shape(DN_DK, BF16),
                   seq_shape(DN_DK, BF16), seq_shape(DN_CHUNK, BF16),
                   jax.ShapeDtypeStruct((2, batch, h, seq // DN_CHUNK, 1, LANE), F32)],
        scratch_shapes=[pltpu.VMEM((tm + 2 * HALO16, LANE), F32)],
        compiler_params=_params(("parallel", "parallel")),
        name="dn_prep",
    )(p, p, p, p, p, p, p, p, p, conv_w, conv_w, conv_w, ab, hp)


def _dn_scan_kernel(s0_ref, uf_ref, wf_ref, kf_ref, qf_ref, af_ref, gf_ref,
                    ub_ref, wb_ref, kb_ref, qb_ref, ab_ref, gb_ref,
                    of_ref, ob_ref, sfin_ref, s_ref, *, cs):
    n = pl.program_id(1)
    c = DN_CHUNK
    h = DN_HEADS

    @pl.when(n == 0)
    def _():
        s_ref[0:h] = s0_ref[0, 0]
        s_ref[h:] = s0_ref[1, 0]

    for ci in range(cs):
        rf = ci * c
        rb = (cs - 1 - ci) * c

        def pair(f_ref, b_ref):
            return jnp.concatenate([f_ref[0, 0, :, rf:rf + c, :], b_ref[0, 0, :, rb:rb + c, :]], axis=0)

        s = s_ref[...]
        s16 = s.astype(BF16)
        ws_qs = _bmm(jnp.concatenate([pair(wf_ref, wb_ref), pair(qf_ref, qb_ref)], axis=1), s16)
        v_new = pair(uf_ref, ub_ref) - ws_qs[:, :c]
        v16 = v_new.astype(BF16)
        o = ws_qs[:, c:] + _bmm(pair(af_ref, ab_ref), v16)
        of_ref[0, :, rf:rf + c, :] = o[:h].astype(of_ref.dtype)
        ob_ref[0, :, rb:rb + c, :] = o[h:].astype(ob_ref.dtype)
        g = jnp.concatenate([gf_ref[0, 0, :, ci], gb_ref[0, 0, :, cs - 1 - ci]], axis=0)
        s_ref[...] = s * g + lax.dot_general(pair(kf_ref, kb_ref), v16, (((1,), (1,)), ((0,), (0,))),
                                             preferred_element_type=F32)

    @pl.when(n == pl.num_programs(1) - 1)
    def _():
        sfin_ref[0, 0] = s_ref[0:h]
        sfin_ref[1, 0] = s_ref[h:]


def _dn_scan(s0, u, w, ke, qg, a, ge, cs):
    _, batch, h, seq, _ = u.shape
    ts = cs * DN_CHUNK
    ns = seq // ts
    kern = functools.partial(_dn_scan_kernel, cs=cs)

    def specs(d):
        idx = (lambda n: n) if d == 0 else (lambda n: ns - 1 - n)
        seqs = [pl.BlockSpec((1, 1, h, ts, width), lambda b, n: (d, b, 0, idx(n), 0))
                for width in (DN_DV, DN_DK, DN_DK, DN_DK, DN_CHUNK)]
        return seqs + [pl.BlockSpec((1, 1, h, cs, 1, LANE), lambda b, n: (d, b, 0, idx(n), 0, 0))]

    o_spec = lambda d: pl.BlockSpec((1, h, ts, DN_DV), lambda b, n: (b, 0, n if d == 0 else ns - 1 - n, 0))
    state = pl.BlockSpec((2, 1, h, DN_DK, DN_DV), lambda b, n: (0, b, 0, 0, 0))
    return pl.pallas_call(
        kern,
        grid=(batch, ns),
        in_specs=[state] + specs(0) + specs(1),
        out_specs=[o_spec(0), o_spec(1), state],
        out_shape=[jax.ShapeDtypeStruct((batch, h, seq, DN_DV), BF16),
                   jax.ShapeDtypeStruct((batch, h, seq, DN_DV), BF16),
                   jax.ShapeDtypeStruct((2, batch, h, DN_DK, DN_DV), F32)],
        scratch_shapes=[pltpu.VMEM((2 * h, DN_DK, DN_DV), F32)],
        compiler_params=_params(("parallel", "arbitrary")),
        name="dn_scan",
    )(s0, u, w, ke, qg, a, ge, u, w, ke, qg, a, ge)


def _merge_kernel(x_ref, mod_ref, ym_ref, of_ref, ob_ref, z_ref, nw_ref, ga_ref, gb_ref, wm_ref, wd_ref, wo_ref,
                  nf_ref, o_ref, xn_ref, *, k_gate, k_shift, k_scale):
    nw = nw_ref[...]
    heads = []
    for h in range(DN_HEADS):
        o = of_ref[0, h].astype(F32) + ob_ref[0, h].astype(F32)
        y = o * lax.rsqrt(jnp.mean(o * o, axis=-1, keepdims=True) + EPS) * nw
        heads.append((y * _silu(z_ref[:, h * DN_DV:(h + 1) * DN_DV].astype(F32))).astype(BF16))
    yd = jnp.concatenate(heads, axis=1)
    pm = jnp.dot(ym_ref[...], wm_ref[...], preferred_element_type=F32)
    pd = jnp.dot(yd, wd_ref[...], preferred_element_type=F32)
    merged = (jax.nn.sigmoid(ga_ref[...].astype(F32)) * pm
              + jax.nn.sigmoid(gb_ref[...].astype(F32)) * pd).astype(BF16)
    y = jnp.dot(merged, wo_ref[...], preferred_element_type=F32)
    x1 = x_ref[...] + mod_ref[0, k_gate:k_gate + 1, :] * y
    o_ref[...] = x1
    xn_ref[...] = _norm_mod(x1, nf_ref[...], mod_ref[0, k_shift:k_shift + 1, :],
                            mod_ref[0, k_scale:k_scale + 1, :]).astype(BF16)


def _merge(x, mod, row_of_tile, ym, o_f, o_b, p, nw, wm, wd, wo, nf, tm, k_gate, k_shift, k_scale):
    m, d = x.shape
    _, h, seq, _ = o_f.shape
    nt = seq // tm
    zw = h * DN_DV
    kern = functools.partial(_merge_kernel, k_gate=k_gate, k_shift=k_shift, k_scale=k_scale)
    const = lambda arr: pl.BlockSpec(arr.shape, lambda i: (0, 0), pipeline_mode=pl.Buffered(1))
    scan_out = pl.BlockSpec((1, h, tm, DN_DV), lambda i: (i // nt, 0, i % nt, 0))
    return pl.pallas_call(
        kern,
        grid=(m // tm,),
        in_specs=[pl.BlockSpec((tm, d), lambda i: (i, 0)),
                  pl.BlockSpec((1, 6, d), lambda i: (row_of_tile(i), 0, 0)),
                  pl.BlockSpec((tm, ym.shape[1]), lambda i: (i, 0)),
                  scan_out, scan_out,
                  pl.BlockSpec((tm, zw), lambda i: (i, B_DZ // zw)),
                  pl.BlockSpec((1, DN_DV), lambda i: (0, 0)),
                  pl.BlockSpec((tm, d), lambda i: (i, B_GA // d)),
                  pl.BlockSpec((tm, d), lambda i: (i, B_GB // d)),
                  const(wm), const(wd), const(wo),
                  pl.BlockSpec((1, d), lambda i: (0, 0))],
        out_specs=[pl.BlockSpec((tm, d), lambda i: (i, 0)), pl.BlockSpec((tm, d), lambda i: (i, 0))],
        out_shape=[jax.ShapeDtypeStruct((m, d), F32), jax.ShapeDtypeStruct((m, d), BF16)],
        compiler_params=_params(("parallel",)),
        name="merge",
    )(x, mod, ym, o_f, o_b, p, nw, p, p, wm, wd, wo, nf)


def _ffn_kernel(x_ref, xm_ref, xp_ref, xx_ref, mod_ref, wg_ref, wv_ref, cw_ref, wdn_ref, o_ref, xn_ref, ge_ref,
                *, tm, tiles_per_seq, k_gate):
    i = pl.program_id(0)
    j = pl.program_id(1)
    th = FFN_TILE

    @pl.when(j == 0)
    def _():
        first = (i % tiles_per_seq) == 0
        last = (i % tiles_per_seq) == tiles_per_seq - 1
        xn_ref[0:HALO16, :] = jnp.where(first, jnp.zeros_like(xp_ref), xp_ref[...])
        xn_ref[HALO16:HALO16 + tm, :] = xm_ref[...]
        xn_ref[HALO16 + tm:, :] = jnp.where(last, jnp.zeros_like(xx_ref), xx_ref[...])
        o_ref[...] = jnp.zeros_like(o_ref)

    ge_ref[...] = jnp.dot(xn_ref[...], wg_ref[...], preferred_element_type=F32)
    val = jnp.dot(xn_ref[HALO16:HALO16 + tm, :], wv_ref[...], preferred_element_type=F32)
    conv = (cw_ref[0:1, :] * ge_ref[HALO16 - 1:HALO16 - 1 + tm, :]
            + cw_ref[1:2, :] * ge_ref[HALO16:HALO16 + tm, :]
            + cw_ref[2:3, :] * ge_ref[HALO16 + 1:HALO16 + 1 + tm, :])
    hid = (_silu(conv) * val).astype(BF16)
    o_ref[...] += jnp.dot(hid, wdn_ref[...], preferred_element_type=F32)

    @pl.when(j == pl.num_programs(1) - 1)
    def _():
        o_ref[...] = x_ref[...] + mod_ref[0, k_gate:k_gate + 1, :] * o_ref[...]


def _ffn(x, xn, mod, row_of_tile, wg, wv, cw, wdn, seq, tm, k_gate):
    m, d = x.shape
    th = FFN_TILE
    nj = wdn.shape[0] // th
    nt = seq // tm
    nb16 = tm // HALO16
    total16 = m // HALO16
    kern = functools.partial(_ffn_kernel, tm=tm, tiles_per_seq=nt, k_gate=k_gate)
    return pl.pallas_call(
        kern,
        grid=(m // tm, nj),
        in_specs=[pl.BlockSpec((tm, d), lambda i, j: (i, 0)),
                  pl.BlockSpec((tm, d), lambda i, j: (i, 0)),
                  pl.BlockSpec((HALO16, d), lambda i, j: (jnp.maximum(i * nb16 - 1, 0), 0)),
                  pl.BlockSpec((HALO16, d), lambda i, j: (jnp.minimum((i + 1) * nb16, total16 - 1), 0)),
                  pl.BlockSpec((1, 6, d), lambda i, j: (row_of_tile(i), 0, 0)),
                  pl.BlockSpec((d, th), lambda i, j: (0, j)),
                  pl.BlockSpec((d, th), lambda i, j: (0, j)),
                  pl.BlockSpec((3, th), lambda i, j: (0, j)),
                  pl.BlockSpec((th, d), lambda i, j: (j, 0))],
        out_specs=pl.BlockSpec((tm, d), lambda i, j: (i, 0)),
        out_shape=jax.ShapeDtypeStruct((m, d), F32),
        scratch_shapes=[pltpu.VMEM((tm + 2 * HALO16, d), BF16), pltpu.VMEM((tm + 2 * HALO16, th), F32)],
        compiler_params=_params(("parallel", "arbitrary")),
        name="ffn",
    )(x, xn, xn, xn, mod, wg, wv, cw, wdn)


def _rope_pad_cols(r):
    n = MLA_ROPE // 4
    z = jnp.zeros(r.shape[:-1] + (2 * n,), r.dtype)
    return jnp.concatenate([r[..., 0:n], r[..., 2 * n:3 * n], z, r[..., n:2 * n], r[..., 3 * n:4 * n], z], axis=-1)


def _qk_pad_cols(w):
    lead = w.shape[:-1]
    w = w.reshape(lead + (MLA_HEADS, MLA_QK))
    out = jnp.concatenate([w[..., :MLA_NOPE], _rope_pad_cols(w[..., MLA_NOPE:])], axis=-1)
    return out.reshape(lead + (MLA_HEADS * MLA_QK_PAD,))


def _rope_tables(rows):
    n = MLA_ROPE // 4
    inv = ROPE_BASE ** (-jnp.arange(n, dtype=F32) / n)
    ang_r = jnp.arange(rows, dtype=F32)[:, None] * inv
    ang_c = jnp.arange(GRID_W, dtype=F32)[:, None] * inv

    def lanes(a_row, a_col, fill, count):
        return jnp.concatenate([a_row, a_col, jnp.full((count, 2 * n), fill, F32)] * 2, axis=-1)

    zr, zc = jnp.zeros((rows, n), F32), jnp.zeros((GRID_W, n), F32)
    row_cos = lanes(jnp.cos(ang_r), zr, 0.0, rows)
    row_sin = jnp.concatenate([-jnp.sin(ang_r), zr, jnp.zeros((rows, 2 * n), F32),
                               jnp.sin(ang_r), zr, jnp.zeros((rows, 2 * n), F32)], axis=-1)
    col_cos = lanes(zc, jnp.cos(ang_c), 1.0, GRID_W)
    col_sin = jnp.concatenate([zc, -jnp.sin(ang_c), jnp.zeros((GRID_W, 2 * n), F32),
                               zc, jnp.sin(ang_c), jnp.zeros((GRID_W, 2 * n), F32)], axis=-1)
    return row_cos, row_sin, col_cos, col_sin


def _pick_tile(n, pref):
    t = min(pref, n)
    while n % t:
        t //= 2
    return t


def kernel(x, c, ctx, c_ctx, w_ada, b_ada, norm_mix, w_in, q_a_norm, w_q_b, kv_a_norm, w_kv_b, q_norm, k_norm,
           w_o_mla, dn_conv, dn_a_log, dn_dt_bias, dn_o_norm, w_o_dn, w_out, norm_ffn, w_ffn_up, ffn_conv,
           w_ffn_down):
    batch, seq, d = x.shape
    lc = ctx.shape[1]
    assert w_ada.shape[0] == 1, "single-layer stack"
    assert seq % GRID_W == 0 and seq % DN_CHUNK == 0 and lc % DN_CHUNK == 0
    h = DN_HEADS
    l = 0

    wi = w_in[l]
    o_qa, o_kva, o_kr = 0, MLA_Q_RANK, MLA_Q_RANK + MLA_KV_RANK
    o_dq = o_kr + MLA_ROPE
    o_da = o_dq + 4 * h * DN_DK
    o_ga = o_da + 4 * h
    w_a = jnp.concatenate([wi[:, o_qa:o_kr], _rope_pad_cols(wi[:, o_kr:o_dq]), wi[:, o_da:o_ga],
                           jnp.zeros((d, A_WIDTH - A_DB - 2 * h), F32)], axis=1).astype(BF16)
    w_b = jnp.concatenate([wi[:, o_dq:o_da], wi[:, o_ga:]], axis=1).astype(BF16)
    wq = _qk_pad_cols(w_q_b[l]).astype(BF16)
    wkv = w_kv_b[l].reshape(MLA_KV_RANK, MLA_HEADS, MLA_NOPE + MLA_V)
    wkv = jnp.concatenate([wkv[:, :, :MLA_NOPE].reshape(MLA_KV_RANK, -1),
                           wkv[:, :, MLA_NOPE:].reshape(MLA_KV_RANK, -1)], axis=1).astype(BF16)
    qn = jnp.concatenate([q_norm[l, :MLA_NOPE], _rope_pad_cols(q_norm[l, MLA_NOPE:])])[None, :]
    kn = jnp.concatenate([k_norm[l, :MLA_NOPE], _rope_pad_cols(k_norm[l, MLA_NOPE:])])[None, :]
    hidden = w_ffn_down.shape[1]
    nj = -(-hidden // FFN_TILE)
    hpad = nj * FFN_TILE - hidden
    wg = jnp.pad(w_ffn_up[l][:, :hidden].astype(BF16), ((0, 0), (0, hpad)))
    wv = jnp.pad(w_ffn_up[l][:, hidden:].astype(BF16), ((0, 0), (0, hpad)))
    fcw = jnp.pad(ffn_conv[l], ((0, 0), (0, hpad)))
    wdn = jnp.pad(w_ffn_down[l], ((0, hpad), (0, 0))).astype(BF16)
    hp = jnp.concatenate([dn_a_log[l].T, dn_dt_bias[l].T], axis=1)[:, None, :]
    tables_l = _rope_tables(seq // GRID_W)
    tables_c = (jnp.zeros((lc // GRID_W, ROPE_PAD), F32), jnp.zeros((lc // GRID_W, ROPE_PAD), F32),
                jnp.ones((GRID_W, ROPE_PAD), F32), jnp.zeros((GRID_W, ROPE_PAD), F32))

    rows = jnp.concatenate([c, c_ctx[None, :], jnp.zeros((8 - batch - 1, d), F32)], axis=0)
    mod = _ada(rows, w_ada[l], b_ada[l][None, :]).reshape(8, 6, d)

    xf = x.reshape(batch * seq, d)
    cf = ctx.reshape(batch * lc, d)

    def stream(tokens, n, tm_a, tm_b, row_of):
        nm = norm_mix[l][None, :]
        a, xn = _norm_mod_matmul(tokens, nm, mod, row_of(tm_a), w_a, F32, tm_a, 0, 1, "in_proj_small")
        p = _matmul(xn, w_b, BF16, tm_b, 1024, "in_proj_wide")
        ab = a[:, A_DA:A_DA + 4 * h].reshape(batch * n, 4, h).transpose(2, 0, 1)
        return a, p, ab

    tm_a = _pick_tile(seq, 512)
    tm_b = _pick_tile(seq, 1024)
    lat_row = lambda tm: (lambda i: i // (seq // tm))
    a_l, p_l, ab_l = stream(xf, seq, tm_a, tm_b, lat_row)
    tc = _pick_tile(lc, 256)
    ctx_row = lambda tm: (lambda i: batch)
    a_c, p_c, ab_c = stream(cf, lc, tc, tc, ctx_row)

    qan, kvan = q_a_norm[l][None, :], kv_a_norm[l][None, :]
    tmp = _pick_tile(seq, 512)
    q_l, k_l, v_l = _mla_proj(a_l, tables_l, qan, kvan, qn, kn, wq, wkv, tmp, seq)
    _, k_c, v_c = _mla_proj(a_c, tables_c, qan, kvan, qn, kn, wq, wkv, tc, lc)
    tq = _pick_tile(seq, 512)
    tk = _pick_tile(seq, 512)

    def with_ones_row(vt):
        lead, n = vt.shape[:-2], vt.shape[-1]
        return jnp.concatenate([vt, jnp.ones(lead + (1, n), BF16), jnp.zeros(lead + (HALO16 - 1, n), BF16)],
                               axis=-2)

    vt = with_ones_row(v_l.reshape(batch, seq // tk, tk, MLA_HEADS, MLA_V).transpose(0, 3, 1, 4, 2))
    vct = with_ones_row(v_c.reshape(batch, lc, MLA_HEADS, MLA_V).transpose(0, 2, 3, 1))
    y_mla = _attention(q_l.reshape(batch, seq, -1), k_l.reshape(batch, seq, -1), vt, k_c.reshape(batch, lc, -1), vct, tq)
    y_mla = y_mla.reshape(batch * seq, MLA_HEADS * MLA_V)

    conv_w = dn_conv[l]
    prep_c = _dn_prep(p_c, ab_c, hp, conv_w, batch, lc, _pick_tile(lc, 256))
    s_zero = jnp.zeros((2, batch, h, DN_DK, DN_DV), F32)
    _, _, s_ctx = _dn_scan(s_zero, *prep_c, _pick_tile(lc, 256) // DN_CHUNK)
    prep_l = _dn_prep(p_l, ab_l, hp, conv_w, batch, seq, _pick_tile(seq, 1024))
    o_f, o_b, _ = _dn_scan(s_ctx, *prep_l, _pick_tile(seq, 512) // DN_CHUNK)

    tmm = _pick_tile(seq, 256)
    x1, xn2 = _merge(xf, mod, lat_row(tmm), y_mla, o_f, o_b, p_l, dn_o_norm[l][None, :], w_o_mla[l].astype(BF16),
                     w_o_dn[l].astype(BF16), w_out[l].astype(BF16), norm_ffn[l][None, :], tmm, 2, 3, 4)

    tmf = _pick_tile(seq, 512)
    out = _ffn(x1, xn2, mod, lat_row(tmf), wg, wv, fcw, wdn, seq, tmf, 5)
    return out.reshape(batch, seq, d)
```

```python
import functools
import math

import jax
import jax.numpy as jnp
import numpy as np
from jax import lax
from jax.experimental import pallas as pl
from jax.experimental.pallas import tpu as pltpu

F32 = jnp.float32
BF16 = jnp.bfloat16
HIGHEST = lax.Precision.HIGHEST

EPS = 1e-6
GRID_W = 64
ROPE_BASE = 10000.0

MLA_HEADS = 8
MLA_Q_RANK = 512
MLA_KV_RANK = 512
MLA_NOPE = 128
MLA_ROPE = 64
MLA_V = 128
MLA_QK = MLA_NOPE + MLA_ROPE
MLA_QK_PAD = 256

DN_HEADS = 8
DN_DK = 128
DN_DV = 128
DN_CHUNK = 64

LANE = 128
SUBLANE = 8
ROPE_PAD = 128
VMEM_LIMIT = 56 * 1024 * 1024

A_QA, A_KVA, A_KR, A_DA, A_DB, A_WIDTH = 0, 512, 1024, 1152, 1168, 1280
B_DQ, B_DK, B_DV, B_DZ, B_GA, B_GB, B_WIDTH = 0, 1024, 2048, 3072, 4096, 6144, 8192

FFN_TILE = 512
HALO16 = 16


def _params(sem, vmem=VMEM_LIMIT):
    return pltpu.CompilerParams(dimension_semantics=sem, vmem_limit_bytes=vmem)


def _nt_dot(a, b):
    return lax.dot_general(a, b, (((1,), (1,)), ((), ())), preferred_element_type=F32)


def _silu(x):
    return x * jax.nn.sigmoid(x)


def _ada_kernel(c_ref, w_ref, b_ref, o_ref):
    a = _silu(c_ref[...]).astype(BF16)
    o_ref[...] = jnp.dot(a, w_ref[...].astype(BF16), preferred_element_type=F32) + b_ref[...]


def _ada(cs, w, b):
    m, d = cs.shape
    n = w.shape[1]
    tn = 512
    return pl.pallas_call(
        _ada_kernel,
        grid=(n // tn,),
        in_specs=[pl.BlockSpec((m, d), lambda j: (0, 0)),
                  pl.BlockSpec((d, tn), lambda j: (0, j)),
                  pl.BlockSpec((1, tn), lambda j: (0, j))],
        out_specs=pl.BlockSpec((m, tn), lambda j: (0, j)),
        out_shape=jax.ShapeDtypeStruct((m, n), F32),
        compiler_params=_params(("parallel",)),
        name="ada",
    )(cs, w, b)


def _norm_mod(x, nw, shift, scale):
    ms = jnp.mean(x * x, axis=-1, keepdims=True)
    y = x * lax.rsqrt(ms + EPS) * nw
    return y * (1.0 + scale) + shift


def _nmm_kernel(x_ref, nw_ref, mod_ref, w_ref, o_ref, xn_ref, *, k_shift, k_scale):
    y = _norm_mod(x_ref[...], nw_ref[...], mod_ref[0, k_shift:k_shift + 1, :], mod_ref[0, k_scale:k_scale + 1, :])
    xn = y.astype(BF16)
    xn_ref[...] = xn
    o_ref[...] = jnp.dot(xn, w_ref[...], preferred_element_type=F32).astype(o_ref.dtype)


def _norm_mod_matmul(x, nw, mod, row_of_tile, w, out_dtype, tm, k_shift, k_scale, name):
    m, d = x.shape
    n = w.shape[1]
    kern = functools.partial(_nmm_kernel, k_shift=k_shift, k_scale=k_scale)
    return pl.pallas_call(
        kern,
        grid=(m // tm,),
        in_specs=[pl.BlockSpec((tm, d), lambda i: (i, 0)),
                  pl.BlockSpec((1, d), lambda i: (0, 0)),
                  pl.BlockSpec((1, 6, d), lambda i: (row_of_tile(i), 0, 0)),
                  pl.BlockSpec((d, n), lambda i: (0, 0))],
        out_specs=[pl.BlockSpec((tm, n), lambda i: (i, 0)), pl.BlockSpec((tm, d), lambda i: (i, 0))],
        out_shape=[jax.ShapeDtypeStruct((m, n), out_dtype), jax.ShapeDtypeStruct((m, d), BF16)],
        compiler_params=_params(("parallel",)),
        name=name,
    )(x, nw, mod, w)


def _mm_kernel(x_ref, w_ref, o_ref):
    o_ref[...] = jnp.dot(x_ref[...], w_ref[...], preferred_element_type=F32).astype(o_ref.dtype)


def _matmul(x, w, out_dtype, tm, tn, name):
    m, d = x.shape
    n = w.shape[1]
    return pl.pallas_call(
        _mm_kernel,
        grid=(m // tm, n // tn),
        in_specs=[pl.BlockSpec((tm, d), lambda i, j: (i, 0)),
                  pl.BlockSpec((d, tn), lambda i, j: (0, j))],
        out_specs=pl.BlockSpec((tm, tn), lambda i, j: (i, j)),
        out_shape=jax.ShapeDtypeStruct((m, n), out_dtype),
        compiler_params=_params(("parallel", "parallel")),
        name=name,
    )(x, w)


def _mla_proj_kernel(a_ref, cr_ref, sr_ref, cc_ref, sc_ref, qan_ref, kvan_ref, qn_ref, kn_ref, wq_ref, wkv_ref,
                     q_ref, k_ref, v_ref):
    tm = a_ref.shape[0]
    g = tm // GRID_W

    def table(row_ref, col_ref):
        r = jnp.broadcast_to(row_ref[...][:, None, :], (g, GRID_W, ROPE_PAD)).reshape(tm, ROPE_PAD)
        c = jnp.broadcast_to(col_ref[...][None], (g, GRID_W, ROPE_PAD)).reshape(tm, ROPE_PAD)
        return r + c

    cos = table(cr_ref, cc_ref)
    sin = table(sr_ref, sc_ref)

    def rope(r):
        return r * cos + pltpu.roll(r, 64, axis=1) * sin

    def rms_rows(t, w):
        return t * lax.rsqrt(jnp.mean(t * t, axis=-1, keepdims=True) + EPS) * w

    qa = rms_rows(a_ref[:, A_QA:A_QA + MLA_Q_RANK], qan_ref[...]).astype(BF16)
    q = jnp.dot(qa, wq_ref[...], preferred_element_type=F32)
    kva = rms_rows(a_ref[:, A_KVA:A_KVA + MLA_KV_RANK], kvan_ref[...]).astype(BF16)
    kv = jnp.dot(kva, wkv_ref[...], preferred_element_type=F32)
    kr = a_ref[:, A_KR:A_KR + ROPE_PAD]
    kr_ss = jnp.sum(kr * kr, axis=-1, keepdims=True)
    qn = qn_ref[...]
    kn = kn_ref[...]
    scale = MLA_QK ** -0.5 * math.log2(math.e)
    for h in range(MLA_HEADS):
        qh = q[:, h * MLA_QK_PAD:(h + 1) * MLA_QK_PAD]
        inv = lax.rsqrt(jnp.sum(qh * qh, axis=-1, keepdims=True) * (1.0 / MLA_QK) + EPS) * scale
        qh = qh * inv * qn
        q_ref[:, h * MLA_QK_PAD:h * MLA_QK_PAD + MLA_NOPE] = qh[:, :MLA_NOPE].astype(BF16)
        q_ref[:, h * MLA_QK_PAD + MLA_NOPE:(h + 1) * MLA_QK_PAD] = rope(qh[:, MLA_NOPE:]).astype(BF16)

        kh = kv[:, h * MLA_NOPE:(h + 1) * MLA_NOPE]
        inv = lax.rsqrt((jnp.sum(kh * kh, axis=-1, keepdims=True) + kr_ss) * (1.0 / MLA_QK) + EPS)
        k_ref[:, h * MLA_QK_PAD:h * MLA_QK_PAD + MLA_NOPE] = (kh * inv * kn[:, :MLA_NOPE]).astype(BF16)
        k_ref[:, h * MLA_QK_PAD + MLA_NOPE:(h + 1) * MLA_QK_PAD] = rope(
            kr * inv * kn[:, MLA_NOPE:]).astype(BF16)
    v_ref[...] = kv[:, MLA_HEADS * MLA_NOPE:].astype(BF16)


def _mla_proj(a, tables, qan, kvan, qn, kn, wq, wkv, tm, seq):
    m = a.shape[0]
    nseq = seq // tm
    row_cos, row_sin, col_cos, col_sin = tables
    full = lambda arr: pl.BlockSpec(arr.shape, lambda i: (0, 0))
    row_spec = pl.BlockSpec((tm // GRID_W, ROPE_PAD), lambda i: (i % nseq, 0))
    return pl.pallas_call(
        _mla_proj_kernel,
        grid=(m // tm,),
        in_specs=[pl.BlockSpec((tm, A_WIDTH), lambda i: (i, 0)),
                  row_spec, row_spec, full(col_cos), full(col_sin),
                  full(qan), full(kvan), full(qn), full(kn), full(wq), full(wkv)],
        out_specs=[pl.BlockSpec((tm, MLA_HEADS * MLA_QK_PAD), lambda i: (i, 0)),
                   pl.BlockSpec((tm, MLA_HEADS * MLA_QK_PAD), lambda i: (i, 0)),
                   pl.BlockSpec((tm, MLA_HEADS * MLA_V), lambda i: (i, 0))],
        out_shape=[jax.ShapeDtypeStruct((m, MLA_HEADS * MLA_QK_PAD), BF16),
                   jax.ShapeDtypeStruct((m, MLA_HEADS * MLA_QK_PAD), BF16),
                   jax.ShapeDtypeStruct((m, MLA_HEADS * MLA_V), BF16)],
        compiler_params=_params(("parallel",)),
        name="mla_proj",
    )(a, row_cos, row_sin, col_cos, col_sin, qan, kvan, qn, kn, wq, wkv)


ATTN_UNROLL = 16
ATTN_ROWS = 64


def _attn_kernel(q_ref, kc_ref, vct_ref, k_ref, vt_ref, o_ref, qt_ref, s0_ref, s1_ref, acc_ref, *, nk, unroll):
    qt_ref[...] = q_ref[0].T
    tq = qt_ref.shape[1]
    tk = s0_ref.shape[0]
    rows = ATTN_ROWS
    s_refs = (s0_ref, s1_ref)

    def scores(j):
        return jnp.dot(k_ref[0, pl.ds(pl.multiple_of(j * tk, tk), tk), :], qt_ref[...], preferred_element_type=F32)

    def probs(s, m):
        return jnp.exp2((s - m).astype(BF16))

    def store_scores(slot, s):
        s_refs[slot][...] = s
        return jnp.max(s.reshape(tk // SUBLANE, SUBLANE, tq), axis=0)

    s = jnp.dot(kc_ref[0], qt_ref[...], preferred_element_type=F32)
    m8_first = store_scores(0, scores(0))
    m = jnp.max(s, axis=0, keepdims=True)
    acc_ref[...] = jnp.dot(vct_ref[0, 0], probs(s, m), preferred_element_type=F32)

    def tile(j, slot, carry):
        m, m8 = carry
        m8_next = store_scores(1 - slot, scores(jnp.minimum(j + 1, nk - 1)))
        m_new = jnp.maximum(m, jnp.max(m8, axis=0, keepdims=True))
        p = jnp.concatenate([probs(s_refs[slot][r:r + rows, :], m_new) for r in range(0, tk, rows)], axis=0)
        acc_ref[...] = jnp.exp2(m - m_new) * acc_ref[...] + jnp.dot(vt_ref[0, 0, j], p, preferred_element_type=F32)
        return m_new, m8_next

    def body(i, carry):
        for u in range(unroll):
            carry = tile(unroll * i + u, u % 2, carry)
        return carry

    lax.fori_loop(0, nk // unroll, body, (m, m8_first))
    acc = acc_ref[...]
    o_ref[0] = (acc[:MLA_V] * (1.0 / acc[MLA_V:MLA_V + 1])).T.astype(o_ref.dtype)


def _attention(q, k, vt, kc, vct, tq):
    b, t, _ = k.shape
    lc = kc.shape[1]
    nk, vrows, tk = vt.shape[2], vt.shape[3], vt.shape[4]
    unroll = ATTN_UNROLL if nk % ATTN_UNROLL == 0 else 2
    assert nk % unroll == 0
    kern = functools.partial(_attn_kernel, nk=nk, unroll=unroll)
    return pl.pallas_call(
        kern,
        grid=(b, MLA_HEADS, t // tq),
        in_specs=[pl.BlockSpec((1, tq, MLA_QK_PAD), lambda bi, h, i: (bi, i, h)),
                  pl.BlockSpec((1, lc, MLA_QK_PAD), lambda bi, h, i: (bi, 0, h)),
                  pl.BlockSpec((1, 1, vrows, lc), lambda bi, h, i: (bi, h, 0, 0)),
                  pl.BlockSpec((1, t, MLA_QK_PAD), lambda bi, h, i: (bi, 0, h)),
                  pl.BlockSpec((1, 1, nk, vrows, tk), lambda bi, h, i: (bi, h, 0, 0, 0))],
        out_specs=pl.BlockSpec((1, tq, MLA_V), lambda bi, h, i: (bi, i, h)),
        out_shape=jax.ShapeDtypeStruct((b, t, MLA_HEADS * MLA_V), BF16),
        scratch_shapes=[pltpu.VMEM((MLA_QK_PAD, tq), BF16), pltpu.VMEM((tk, tq), F32), pltpu.VMEM((tk, tq), F32),
                        pltpu.VMEM((vrows, tq), F32)],
        compiler_params=_params(("parallel", "parallel", "arbitrary")),
        name="attention",
    )(q, kc, vct, k, vt)


def _split2(x):
    hi = x.astype(BF16)
    return hi, (x - hi.astype(F32)).astype(BF16)


def _bmm(a, b):
    return lax.dot_general(a, b, (((2,), (1,)), ((0,), (0,))), preferred_element_type=F32)


def _bmm_nt(a, b):
    return lax.dot_general(a, b, (((2,), (2,)), ((0,), (0,))), preferred_element_type=F32)


def _bmm_hp(a, b):
    ah, al = _split2(a)
    bh, bl = _split2(b)
    return _bmm(ah, bh) + _bmm(al, bh) + _bmm(ah, bl)


def _bd2(b):
    lane = lax.broadcasted_iota(jnp.int32, b.shape, 2)
    zero = jnp.zeros_like(b)
    return jnp.concatenate([jnp.where(lane < DN_CHUNK, b, zero), jnp.where(lane >= DN_CHUNK, b, zero)], axis=1)


def _tri_inverse(m, eye):
    assert m.shape[-2] == 64
    x = eye - m
    nb = (-m).astype(BF16)
    bd = _bd2(nb)
    for _ in range(3):
        nb = _bmm(nb, bd).astype(BF16)
        bd = _bd2(nb)
        x = x + _bmm(x.astype(BF16), bd)
    xb = x.astype(BF16)
    r = eye - x - _bmm(m.astype(BF16), _bd2(xb))
    x = x + _bmm(xb, _bd2(r.astype(BF16)))
    mh, ml = _split2(m)
    xh, xl = _split2(x)
    bdh = _bd2(xh)
    r = eye - x - (_bmm(mh, bdh) + _bmm(ml, bdh) + _bmm(mh, _bd2(xl)))
    return x + _bmm(xh, _bd2(r.astype(BF16)))


def _dn_prep_kernel(qm_ref, qp_ref, qx_ref, km_ref, kp_ref, kx_ref, vm_ref, vp_ref, vx_ref,
                    cwq_ref, cwk_ref, cwv_ref, ab_ref, hp_ref,
                    u_ref, w_ref, ke_ref, qg_ref, a_ref, ge_ref, ext_ref, *, tm, tiles_per_seq):
    i = pl.program_id(0)
    first = (i % tiles_per_seq) == 0
    last = (i % tiles_per_seq) == tiles_per_seq - 1
    c = DN_CHUNK

    def conv_silu(main_ref, prev_ref, next_ref, cw_ref):
        ext_ref[0:HALO16, :] = jnp.where(first, 0.0, prev_ref[...].astype(F32))
        ext_ref[HALO16:HALO16 + tm, :] = main_ref[...].astype(F32)
        ext_ref[HALO16 + tm:, :] = jnp.where(last, 0.0, next_ref[...].astype(F32))
        y = (cw_ref[0:1, :] * ext_ref[HALO16 - 1:HALO16 - 1 + tm, :]
             + cw_ref[1:2, :] * ext_ref[HALO16:HALO16 + tm, :]
             + cw_ref[2:3, :] * ext_ref[HALO16 + 1:HALO16 + 1 + tm, :])
        return _silu(y)

    def l2n(t):
        return t * lax.rsqrt(jnp.sum(t * t, axis=-1, keepdims=True) + EPS)

    q = l2n(conv_silu(qm_ref, qp_ref, qx_ref, cwq_ref)) * (DN_DK ** -0.5)
    k = l2n(conv_silu(km_ref, kp_ref, kx_ref, cwk_ref))
    v = conv_silu(vm_ref, vp_ref, vx_ref, cwv_ref)

    hp = hp_ref[0]
    ab = ab_ref[0]
    z = ab[:, 0:2] + hp[:, 2:4]
    softplus = jnp.maximum(z, 0.0) + jnp.log(1.0 + jnp.exp(-jnp.abs(z)))
    g_all = -jnp.exp(hp[:, 0:2]) * softplus
    beta_all = jax.nn.sigmoid(ab[:, 2:4])

    nc = tm // c

    def per_chunk(t):
        return t.reshape(nc, c, t.shape[-1])

    def all_lanes(t2, d):
        return per_chunk(jnp.broadcast_to(t2[:, d:d + 1], (tm, LANE)))

    shape = (nc, c, LANE)
    rows = lax.broadcasted_iota(jnp.int32, shape, 1)
    lane = lax.broadcasted_iota(jnp.int32, shape, 2)
    bwd = lane >= c
    col = jnp.where(bwd, lane - c, lane)
    eye_mask = rows == col
    ahead = jnp.where(bwd, rows - col, col - rows)
    incl = ahead <= 0
    strict = ahead < 0
    eye = jnp.where(eye_mask, 1.0, 0.0)
    tri = jnp.where(incl, 1.0, 0.0).astype(BF16)

    def halves(t2):
        return jnp.where(bwd, all_lanes(t2, 1), all_lanes(t2, 0))

    g_hi, g_lo = _split2(halves(g_all))
    gc = _bmm(tri, _bd2(g_hi)) + _bmm(tri, _bd2(g_lo))
    gc_row = jnp.sum(jnp.where(eye_mask, gc, 0.0), axis=1, keepdims=True)
    decay = jnp.where(incl, jnp.exp(jnp.where(incl, gc - gc_row, 0.0)), 0.0)

    k3 = per_chunk(k)
    q3 = per_chunk(q)
    v3 = per_chunk(v)
    k3b = k3.astype(BF16)
    kdup = jnp.concatenate([k3b, k3b], axis=1)
    kk = _bmm_nt(k3b, kdup)
    att = _bmm_nt(q3.astype(BF16), kdup) * decay
    m = jnp.where(strict, halves(beta_all) * kk * decay, 0.0)
    t_hi, t_lo = _split2(_tri_inverse(m, eye))

    swapped = pltpu.roll(gc.reshape(tm, LANE), c, axis=1).reshape(shape)
    gc_dir = (jnp.where(bwd, swapped, gc), jnp.where(bwd, gc, swapped))
    rhs, ke, qg, gend = [], [], [], []
    for d in range(2):
        gcd = gc_dir[d]
        beta = all_lanes(beta_all, d)
        eg = jnp.exp(gcd)
        gtot = gcd[:, c - 1:c, :] if d == 0 else gcd[:, 0:1, :]
        rhs.append(jnp.concatenate([v3 * beta, k3 * beta * eg], axis=2))
        ke.append(k3 * jnp.exp(gtot - gcd))
        qg.append(q3 * eg)
        gend.append(jnp.exp(gtot))
    rhs = jnp.concatenate(rhs, axis=1).astype(BF16)
    sol = _bmm(_bd2(t_hi), rhs) + _bmm(_bd2(t_lo), rhs)
    att_dir = (att, pltpu.roll(att.reshape(tm, LANE), c, axis=1).reshape(shape))
    for d in range(2):
        sl = slice(d * c, (d + 1) * c)
        u_ref[d, 0, 0] = sol[:, sl, :DN_DV].reshape(tm, DN_DV)
        w_ref[d, 0, 0] = sol[:, sl, DN_DV:].reshape(tm, DN_DK).astype(BF16)
        ke_ref[d, 0, 0] = ke[d].reshape(tm, DN_DK).astype(BF16)
        qg_ref[d, 0, 0] = qg[d].reshape(tm, DN_DK).astype(BF16)
        a_ref[d, 0, 0] = att_dir[d][:, :, :c].reshape(tm, c).astype(BF16)
        ge_ref[d, 0, 0] = gend[d]


def _dn_prep(p, ab, hp, conv_w, batch, seq, tm):
    h = DN_HEADS
    nt = seq // tm
    nb16 = tm // HALO16
    total16 = batch * seq // HALO16
    kern = functools.partial(_dn_prep_kernel, tm=tm, tiles_per_seq=nt)

    def triple(col0):
        return [pl.BlockSpec((tm, LANE), lambda i, hh: (i, col0 + hh)),
                pl.BlockSpec((HALO16, LANE), lambda i, hh: (jnp.maximum(i * nb16 - 1, 0), col0 + hh)),
                pl.BlockSpec((HALO16, LANE), lambda i, hh: (jnp.minimum((i + 1) * nb16, total16 - 1), col0 + hh))]

    cq, ck, cv = B_DQ // LANE, B_DK // LANE, B_DV // LANE
    row = lambda shape: pl.BlockSpec(shape, lambda i, hh: (0, i // nt, hh, i % nt, 0))
    seq_shape = lambda width, dt: jax.ShapeDtypeStruct((2, batch, h, seq, width), dt)
    return pl.pallas_call(
        kern,
        grid=(batch * nt, h),
        in_specs=triple(cq) + triple(ck) + triple(cv) + [
            pl.BlockSpec((3, LANE), lambda i, hh: (0, hh)),
            pl.BlockSpec((3, LANE), lambda i, hh: (0, h + hh)),
            pl.BlockSpec((3, LANE), lambda i, hh: (0, 2 * h + hh)),
            pl.BlockSpec((1, tm, 4), lambda i, hh: (hh, i, 0)),
            pl.BlockSpec((1, 1, 4), lambda i, hh: (hh, 0, 0))],
        out_specs=[row((2, 1, 1, tm, DN_DV)), row((2, 1, 1, tm, DN_DK)), row((2, 1, 1, tm, DN_DK)),
                   row((2, 1, 1, tm, DN_DK)), row((2, 1, 1, tm, DN_CHUNK)),
                   pl.BlockSpec((2, 1, 1, tm // DN_CHUNK, 1, LANE), lambda i, hh: (0, i // nt, hh, i % nt, 0, 0))],
        out_shape=[seq_shape(DN_DV, F32), seq_shape(DN_DK, BF16), seq_shape(DN_DK, BF16),
                   seq_shape(DN_DK, BF16), seq_shape(DN_CHUNK, BF16),
                   jax.ShapeDtypeStruct((2, batch, h, seq // DN_CHUNK, 1, LANE), F32)],
        scratch_shapes=[pltpu.VMEM((tm + 2 * HALO16, LANE), F32)],
        compiler_params=_params(("parallel", "parallel")),
        name="dn_prep",
    )(p, p, p, p, p, p, p, p, p, conv_w, conv_w, conv_w, ab, hp)


def _dn_scan_kernel(s0_ref, uf_ref, wf_ref, kf_ref, qf_ref, af_ref, gf_ref,
                    ub_ref, wb_ref, kb_ref, qb_ref, ab_ref, gb_ref,
                    of_ref, ob_ref, sfin_ref, s_ref, *, cs):
    n = pl.program_id(1)
    c = DN_CHUNK
    h = DN_HEADS

    @pl.when(n == 0)
    def _():
        s_ref[0:h] = s0_ref[0, 0]
        s_ref[h:] = s0_ref[1, 0]

    for ci in range(cs):
        rf = ci * c
        rb = (cs - 1 - ci) * c

        def pair(f_ref, b_ref):
            return jnp.concatenate([f_ref[0, 0, :, rf:rf + c, :], b_ref[0, 0, :, rb:rb + c, :]], axis=0)

        s = s_ref[...]
        s16 = s.astype(BF16)
        ws_qs = _bmm(jnp.concatenate([pair(wf_ref, wb_ref), pair(qf_ref, qb_ref)], axis=1), s16)
        v_new = pair(uf_ref, ub_ref) - ws_qs[:, :c]
        v16 = v_new.astype(BF16)
        o = ws_qs[:, c:] + _bmm(pair(af_ref, ab_ref), v16)
        of_ref[0, :, rf:rf + c, :] = o[:h].astype(of_ref.dtype)
        ob_ref[0, :, rb:rb + c, :] = o[h:].astype(ob_ref.dtype)
        g = jnp.concatenate([gf_ref[0, 0, :, ci], gb_ref[0, 0, :, cs - 1 - ci]], axis=0)
        s_ref[...] = s * g + lax.dot_general(pair(kf_ref, kb_ref), v16, (((1,), (1,)), ((0,), (0,))),
                                             preferred_element_type=F32)

    @pl.when(n == pl.num_programs(1) - 1)
    def _():
        sfin_ref[0, 0] = s_ref[0:h]
        sfin_ref[1, 0] = s_ref[h:]


def _dn_scan(s0, u, w, ke, qg, a, ge, cs):
    _, batch, h, seq, _ = u.shape
    ts = cs * DN_CHUNK
    ns = seq // ts
    kern = functools.partial(_dn_scan_kernel, cs=cs)

    def specs(d):
        idx = (lambda n: n) if d == 0 else (lambda n: ns - 1 - n)
        seqs = [pl.BlockSpec((1, 1, h, ts, width), lambda b, n: (d, b, 0, idx(n), 0))
                for width in (DN_DV, DN_DK, DN_DK, DN_DK, DN_CHUNK)]
        return seqs + [pl.BlockSpec((1, 1, h, cs, 1, LANE), lambda b, n: (d, b, 0, idx(n), 0, 0))]

    o_spec = lambda d: pl.BlockSpec((1, h, ts, DN_DV), lambda b, n: (b, 0, n if d == 0 else ns - 1 - n, 0))
    state = pl.BlockSpec((2, 1, h, DN_DK, DN_DV), lambda b, n: (0, b, 0, 0, 0))
    return pl.pallas_call(
        kern,
        grid=(batch, ns),
        in_specs=[state] + specs(0) + specs(1),
        out_specs=[o_spec(0), o_spec(1), state],
        out_shape=[jax.ShapeDtypeStruct((batch, h, seq, DN_DV), BF16),
                   jax.ShapeDtypeStruct((batch, h, seq, DN_DV), BF16),
                   jax.ShapeDtypeStruct((2, batch, h, DN_DK, DN_DV), F32)],
        scratch_shapes=[pltpu.VMEM((2 * h, DN_DK, DN_DV), F32)],
        compiler_params=_params(("parallel", "arbitrary")),
        name="dn_scan",
    )(s0, u, w, ke, qg, a, ge, u, w, ke, qg, a, ge)


def _merge_kernel(x_ref, mod_ref, ym_ref, of_ref, ob_ref, z_ref, nw_ref, ga_ref, gb_ref, wm_ref, wd_ref, wo_ref,
                  nf_ref, o_ref, xn_ref, *, k_gate, k_shift, k_scale):
    nw = nw_ref[...]
    heads = []
    for h in range(DN_HEADS):
        o = of_ref[0, h].astype(F32) + ob_ref[0, h].astype(F32)
        y = o * lax.rsqrt(jnp.mean(o * o, axis=-1, keepdims=True) + EPS) * nw
        heads.append((y * _silu(z_ref[:, h * DN_DV:(h + 1) * DN_DV].astype(F32))).astype(BF16))
    yd = jnp.concatenate(heads, axis=1)
    pm = jnp.dot(ym_ref[...], wm_ref[...], preferred_element_type=F32)
    pd = jnp.dot(yd, wd_ref[...], preferred_element_type=F32)
    merged = (jax.nn.sigmoid(ga_ref[...].astype(F32)) * pm
              + jax.nn.sigmoid(gb_ref[...].astype(F32)) * pd).astype(BF16)
    y = jnp.dot(merged, wo_ref[...], preferred_element_type=F32)
    x1 = x_ref[...] + mod_ref[0, k_gate:k_gate + 1, :] * y
    o_ref[...] = x1
    xn_ref[...] = _norm_mod(x1, nf_ref[...], mod_ref[0, k_shift:k_shift + 1, :],
                            mod_ref[0, k_scale:k_scale + 1, :]).astype(BF16)


def _merge(x, mod, row_of_tile, ym, o_f, o_b, p, nw, wm, wd, wo, nf, tm, k_gate, k_shift, k_scale):
    m, d = x.shape
    _, h, seq, _ = o_f.shape
    nt = seq // tm
    zw = h * DN_DV
    kern = functools.partial(_merge_kernel, k_gate=k_gate, k_shift=k_shift, k_scale=k_scale)
    const = lambda arr: pl.BlockSpec(arr.shape, lambda i: (0, 0), pipeline_mode=pl.Buffered(1))
    scan_out = pl.BlockSpec((1, h, tm, DN_DV), lambda i: (i // nt, 0, i % nt, 0))
    return pl.pallas_call(
        kern,
        grid=(m // tm,),
        in_specs=[pl.BlockSpec((tm, d), lambda i: (i, 0)),
                  pl.BlockSpec((1, 6, d), lambda i: (row_of_tile(i), 0, 0)),
                  pl.BlockSpec((tm, ym.shape[1]), lambda i: (i, 0)),
                  scan_out, scan_out,
                  pl.BlockSpec((tm, zw), lambda i: (i, B_DZ // zw)),
                  pl.BlockSpec((1, DN_DV), lambda i: (0, 0)),
                  pl.BlockSpec((tm, d), lambda i: (i, B_GA // d)),
                  pl.BlockSpec((tm, d), lambda i: (i, B_GB // d)),
                  const(wm), const(wd), const(wo),
                  pl.BlockSpec((1, d), lambda i: (0, 0))],
        out_specs=[pl.BlockSpec((tm, d), lambda i: (i, 0)), pl.BlockSpec((tm, d), lambda i: (i, 0))],
        out_shape=[jax.ShapeDtypeStruct((m, d), F32), jax.ShapeDtypeStruct((m, d), BF16)],
        compiler_params=_params(("parallel",)),
        name="merge",
    )(x, mod, ym, o_f, o_b, p, nw, p, p, wm, wd, wo, nf)


def _ffn_kernel(x_ref, xm_ref, xp_ref, xx_ref, mod_ref, wg_ref, wv_ref, cw_ref, wdn_ref, o_ref, xn_ref, ge_ref,
                *, tm, tiles_per_seq, k_gate):
    i = pl.program_id(0)
    j = pl.program_id(1)
    th = FFN_TILE

    @pl.when(j == 0)
    def _():
        first = (i % tiles_per_seq) == 0
        last = (i % tiles_per_seq) == tiles_per_seq - 1
        xn_ref[0:HALO16, :] = jnp.where(first, jnp.zeros_like(xp_ref), xp_ref[...])
        xn_ref[HALO16:HALO16 + tm, :] = xm_ref[...]
        xn_ref[HALO16 + tm:, :] = jnp.where(last, jnp.zeros_like(xx_ref), xx_ref[...])
        o_ref[...] = jnp.zeros_like(o_ref)

    ge_ref[...] = jnp.dot(xn_ref[...], wg_ref[...], preferred_element_type=F32)
    val = jnp.dot(xn_ref[HALO16:HALO16 + tm, :], wv_ref[...], preferred_element_type=F32)
    conv = (cw_ref[0:1, :] * ge_ref[HALO16 - 1:HALO16 - 1 + tm, :]
            + cw_ref[1:2, :] * ge_ref[HALO16:HALO16 + tm, :]
            + cw_ref[2:3, :] * ge_ref[HALO16 + 1:HALO16 + 1 + tm, :])
    hid = (_silu(conv) * val).astype(BF16)
    o_ref[...] += jnp.dot(hid, wdn_ref[...], preferred_element_type=F32)

    @pl.when(j == pl.num_programs(1) - 1)
    def _():
        o_ref[...] = x_ref[...] + mod_ref[0, k_gate:k_gate + 1, :] * o_ref[...]


def _ffn(x, xn, mod, row_of_tile, wg, wv, cw, wdn, seq, tm, k_gate):
    m, d = x.shape
    th = FFN_TILE
    nj = wdn.shape[0] // th
    nt = seq // tm
    nb16 = tm // HALO16
    total16 = m // HALO16
    kern = functools.partial(_ffn_kernel, tm=tm, tiles_per_seq=nt, k_gate=k_gate)
    return pl.pallas_call(
        kern,
        grid=(m // tm, nj),
        in_specs=[pl.BlockSpec((tm, d), lambda i, j: (i, 0)),
                  pl.BlockSpec((tm, d), lambda i, j: (i, 0)),
                  pl.BlockSpec((HALO16, d), lambda i, j: (jnp.maximum(i * nb16 - 1, 0), 0)),
                  pl.BlockSpec((HALO16, d), lambda i, j: (jnp.minimum((i + 1) * nb16, total16 - 1), 0)),
                  pl.BlockSpec((1, 6, d), lambda i, j: (row_of_tile(i), 0, 0)),
                  pl.BlockSpec((d, th), lambda i, j: (0, j)),
                  pl.BlockSpec((d, th), lambda i, j: (0, j)),
                  pl.BlockSpec((3, th), lambda i, j: (0, j)),
                  pl.BlockSpec((th, d), lambda i, j: (j, 0))],
        out_specs=pl.BlockSpec((tm, d), lambda i, j: (i, 0)),
        out_shape=jax.ShapeDtypeStruct((m, d), F32),
        scratch_shapes=[pltpu.VMEM((tm + 2 * HALO16, d), BF16), pltpu.VMEM((tm + 2 * HALO16, th), F32)],
        compiler_params=_params(("parallel", "arbitrary")),
        name="ffn",
    )(x, xn, xn, xn, mod, wg, wv, cw, wdn)


def _rope_pad_cols(r):
    n = MLA_ROPE // 4
    z = jnp.zeros(r.shape[:-1] + (2 * n,), r.dtype)
    return jnp.concatenate([r[..., 0:n], r[..., 2 * n:3 * n], z, r[..., n:2 * n], r[..., 3 * n:4 * n], z], axis=-1)


def _qk_pad_cols(w):
    lead = w.shape[:-1]
    w = w.reshape(lead + (MLA_HEADS, MLA_QK))
    out = jnp.concatenate([w[..., :MLA_NOPE], _rope_pad_cols(w[..., MLA_NOPE:])], axis=-1)
    return out.reshape(lead + (MLA_HEADS * MLA_QK_PAD,))


def _rope_tables(rows):
    n = MLA_ROPE // 4
    inv = ROPE_BASE ** (-jnp.arange(n, dtype=F32) / n)
    ang_r = jnp.arange(rows, dtype=F32)[:, None] * inv
    ang_c = jnp.arange(GRID_W, dtype=F32)[:, None] * inv

    def lanes(a_row, a_col, fill, count):
        return jnp.concatenate([a_row, a_col, jnp.full((count, 2 * n), fill, F32)] * 2, axis=-1)

    zr, zc = jnp.zeros((rows, n), F32), jnp.zeros((GRID_W, n), F32)
    row_cos = lanes(jnp.cos(ang_r), zr, 0.0, rows)
    row_sin = jnp.concatenate([-jnp.sin(ang_r), zr, jnp.zeros((rows, 2 * n), F32),
                               jnp.sin(ang_r), zr, jnp.zeros((rows, 2 * n), F32)], axis=-1)
    col_cos = lanes(zc, jnp.cos(ang_c), 1.0, GRID_W)
    col_sin = jnp.concatenate([zc, -jnp.sin(ang_c), jnp.zeros((GRID_W, 2 * n), F32),
                               zc, jnp.sin(ang_c), jnp.zeros((GRID_W, 2 * n), F32)], axis=-1)
    return row_cos, row_sin, col_cos, col_sin


def _pick_tile(n, pref):
    t = min(pref, n)
    while n % t:
        t //= 2
    return t


def kernel(x, c, ctx, c_ctx, w_ada, b_ada, norm_mix, w_in, q_a_norm, w_q_b, kv_a_norm, w_kv_b, q_norm, k_norm,
           w_o_mla, dn_conv, dn_a_log, dn_dt_bias, dn_o_norm, w_o_dn, w_out, norm_ffn, w_ffn_up, ffn_conv,
           w_ffn_down):
    batch, seq, d = x.shape
    lc = ctx.shape[1]
    assert w_ada.shape[0] == 1, "single-layer stack"
    assert seq % GRID_W == 0 and seq % DN_CHUNK == 0 and lc % DN_CHUNK == 0
    h = DN_HEADS
    l = 0

    wi = w_in[l]
    o_qa, o_kva, o_kr = 0, MLA_Q_RANK, MLA_Q_RANK + MLA_KV_RANK
    o_dq = o_kr + MLA_ROPE
    o_da = o_dq + 4 * h * DN_DK
    o_ga = o_da + 4 * h
    w_a = jnp.concatenate([wi[:, o_qa:o_kr], _rope_pad_cols(wi[:, o_kr:o_dq]), wi[:, o_da:o_ga],
                           jnp.zeros((d, A_WIDTH - A_DB - 2 * h), F32)], axis=1).astype(BF16)
    w_b = jnp.concatenate([wi[:, o_dq:o_da], wi[:, o_ga:]], axis=1).astype(BF16)
    wq = _qk_pad_cols(w_q_b[l]).astype(BF16)
    wkv = w_kv_b[l].reshape(MLA_KV_RANK, MLA_HEADS, MLA_NOPE + MLA_V)
    wkv = jnp.concatenate([wkv[:, :, :MLA_NOPE].reshape(MLA_KV_RANK, -1),
                           wkv[:, :, MLA_NOPE:].reshape(MLA_KV_RANK, -1)], axis=1).astype(BF16)
    qn = jnp.concatenate([q_norm[l, :MLA_NOPE], _rope_pad_cols(q_norm[l, MLA_NOPE:])])[None, :]
    kn = jnp.concatenate([k_norm[l, :MLA_NOPE], _rope_pad_cols(k_norm[l, MLA_NOPE:])])[None, :]
    hidden = w_ffn_down.shape[1]
    nj = -(-hidden // FFN_TILE)
    hpad = nj * FFN_TILE - hidden
    wg = jnp.pad(w_ffn_up[l][:, :hidden].astype(BF16), ((0, 0), (0, hpad)))
    wv = jnp.pad(w_ffn_up[l][:, hidden:].astype(BF16), ((0, 0), (0, hpad)))
    fcw = jnp.pad(ffn_conv[l], ((0, 0), (0, hpad)))
    wdn = jnp.pad(w_ffn_down[l], ((0, hpad), (0, 0))).astype(BF16)
    hp = jnp.concatenate([dn_a_log[l].T, dn_dt_bias[l].T], axis=1)[:, None, :]
    tables_l = _rope_tables(seq // GRID_W)
    tables_c = (jnp.zeros((lc // GRID_W, ROPE_PAD), F32), jnp.zeros((lc // GRID_W, ROPE_PAD), F32),
                jnp.ones((GRID_W, ROPE_PAD), F32), jnp.zeros((GRID_W, ROPE_PAD), F32))

    rows = jnp.concatenate([c, c_ctx[None, :], jnp.zeros((8 - batch - 1, d), F32)], axis=0)
    mod = _ada(rows, w_ada[l], b_ada[l][None, :]).reshape(8, 6, d)

    xf = x.reshape(batch * seq, d)
    cf = ctx.reshape(batch * lc, d)

    def stream(tokens, n, tm_a, tm_b, row_of):
        nm = norm_mix[l][None, :]
        a, xn = _norm_mod_matmul(tokens, nm, mod, row_of(tm_a), w_a, F32, tm_a, 0, 1, "in_proj_small")
        p = _matmul(xn, w_b, BF16, tm_b, 1024, "in_proj_wide")
        ab = a[:, A_DA:A_DA + 4 * h].reshape(batch * n, 4, h).transpose(2, 0, 1)
        return a, p, ab

    tm_a = _pick_tile(seq, 512)
    tm_b = _pick_tile(seq, 1024)
    lat_row = lambda tm: (lambda i: i // (seq // tm))
    a_l, p_l, ab_l = stream(xf, seq, tm_a, tm_b, lat_row)
    tc = _pick_tile(lc, 256)
    ctx_row = lambda tm: (lambda i: batch)
    a_c, p_c, ab_c = stream(cf, lc, tc, tc, ctx_row)

    qan, kvan = q_a_norm[l][None, :], kv_a_norm[l][None, :]
    tmp = _pick_tile(seq, 512)
    q_l, k_l, v_l = _mla_proj(a_l, tables_l, qan, kvan, qn, kn, wq, wkv, tmp, seq)
    _, k_c, v_c = _mla_proj(a_c, tables_c, qan, kvan, qn, kn, wq, wkv, tc, lc)
    tq = _pick_tile(seq, 512)
    tk = _pick_tile(seq, 512)

    def with_ones_row(vt):
        lead, n = vt.shape[:-2], vt.shape[-1]
        return jnp.concatenate([vt, jnp.ones(lead + (1, n), BF16), jnp.zeros(lead + (HALO16 - 1, n), BF16)],
                               axis=-2)

    vt = with_ones_row(v_l.reshape(batch, seq // tk, tk, MLA_HEADS, MLA_V).transpose(0, 3, 1, 4, 2))
    vct = with_ones_row(v_c.reshape(batch, lc, MLA_HEADS, MLA_V).transpose(0, 2, 3, 1))
    y_mla = _attention(q_l.reshape(batch, seq, -1), k_l.reshape(batch, seq, -1), vt, k_c.reshape(batch, lc, -1), vct, tq)
    y_mla = y_mla.reshape(batch * seq, MLA_HEADS * MLA_V)

    conv_w = dn_conv[l]
    prep_c = _dn_prep(p_c, ab_c, hp, conv_w, batch, lc, _pick_tile(lc, 256))
    s_zero = jnp.zeros((2, batch, h, DN_DK, DN_DV), F32)
    _, _, s_ctx = _dn_scan(s_zero, *prep_c, _pick_tile(lc, 256) // DN_CHUNK)
    prep_l = _dn_prep(p_l, ab_l, hp, conv_w, batch, seq, _pick_tile(seq, 2048))
    o_f, o_b, _ = _dn_scan(s_ctx, *prep_l, _pick_tile(seq, 512) // DN_CHUNK)

    tmm = _pick_tile(seq, 256)
    x1, xn2 = _merge(xf, mod, lat_row(tmm), y_mla, o_f, o_b, p_l, dn_o_norm[l][None, :], w_o_mla[l].astype(BF16),
                     w_o_dn[l].astype(BF16), w_out[l].astype(BF16), norm_ffn[l][None, :], tmm, 2, 3, 4)

    tmf = _pick_tile(seq, 512)
    out = _ffn(x1, xn2, mod, lat_row(tmf), wg, wv, fcw, wdn, seq, tmf, 5)
    return out.reshape(batch, seq, d)
```

```python
import functools
import math

import jax
import jax.numpy as jnp
import numpy as np
from jax import lax
from jax.experimental import pallas as pl
from jax.experimental.pallas import tpu as pltpu

F32 = jnp.float32
BF16 = jnp.bfloat16
HIGHEST = lax.Precision.HIGHEST

EPS = 1e-6
GRID_W = 64
ROPE_BASE = 10000.0

MLA_HEADS = 8
MLA_Q_RANK = 512
MLA_KV_RANK = 512
MLA_NOPE = 128
MLA_ROPE = 64
MLA_V = 128
MLA_QK = MLA_NOPE + MLA_ROPE
MLA_QK_PAD = 256

DN_HEADS = 8
DN_DK = 128
DN_DV = 128
DN_CHUNK = 64

LANE = 128
SUBLANE = 8
ROPE_PAD = 128
VMEM_LIMIT = 56 * 1024 * 1024

A_QA, A_KVA, A_KR, A_DA, A_DB, A_WIDTH = 0, 512, 1024, 1152, 1168, 1280
B_DQ, B_DK, B_DV, B_DZ, B_GA, B_GB, B_WIDTH = 0, 1024, 2048, 3072, 4096, 6144, 8192

FFN_TILE = 512
HALO16 = 16


def _params(sem, vmem=VMEM_LIMIT):
    return pltpu.CompilerParams(dimension_semantics=sem, vmem_limit_bytes=vmem)


def _nt_dot(a, b):
    return lax.dot_general(a, b, (((1,), (1,)), ((), ())), preferred_element_type=F32)


def _silu(x):
    return x * jax.nn.sigmoid(x)


def _ada_kernel(c_ref, w_ref, b_ref, o_ref):
    a = _silu(c_ref[...]).astype(BF16)
    o_ref[...] = jnp.dot(a, w_ref[...].astype(BF16), preferred_element_type=F32) + b_ref[...]


def _ada(cs, w, b):
    m, d = cs.shape
    n = w.shape[1]
    tn = 512
    return pl.pallas_call(
        _ada_kernel,
        grid=(n // tn,),
        in_specs=[pl.BlockSpec((m, d), lambda j: (0, 0)),
                  pl.BlockSpec((d, tn), lambda j: (0, j)),
                  pl.BlockSpec((1, tn), lambda j: (0, j))],
        out_specs=pl.BlockSpec((m, tn), lambda j: (0, j)),
        out_shape=jax.ShapeDtypeStruct((m, n), F32),
        compiler_params=_params(("parallel",)),
        name="ada",
    )(cs, w, b)


def _norm_mod(x, nw, shift, scale):
    ms = jnp.mean(x * x, axis=-1, keepdims=True)
    y = x * lax.rsqrt(ms + EPS) * nw
    return y * (1.0 + scale) + shift


def _nmm_kernel(x_ref, nw_ref, mod_ref, w_ref, o_ref, xn_ref, *, k_shift, k_scale):
    y = _norm_mod(x_ref[...], nw_ref[...], mod_ref[0, k_shift:k_shift + 1, :], mod_ref[0, k_scale:k_scale + 1, :])
    xn = y.astype(BF16)
    xn_ref[...] = xn
    o_ref[...] = jnp.dot(xn, w_ref[...], preferred_element_type=F32).astype(o_ref.dtype)


def _norm_mod_matmul(x, nw, mod, row_of_tile, w, out_dtype, tm, k_shift, k_scale, name):
    m, d = x.shape
    n = w.shape[1]
    kern = functools.partial(_nmm_kernel, k_shift=k_shift, k_scale=k_scale)
    return pl.pallas_call(
        kern,
        grid=(m // tm,),
        in_specs=[pl.BlockSpec((tm, d), lambda i: (i, 0)),
                  pl.BlockSpec((1, d), lambda i: (0, 0)),
                  pl.BlockSpec((1, 6, d), lambda i: (row_of_tile(i), 0, 0)),
                  pl.BlockSpec((d, n), lambda i: (0, 0))],
        out_specs=[pl.BlockSpec((tm, n), lambda i: (i, 0)), pl.BlockSpec((tm, d), lambda i: (i, 0))],
        out_shape=[jax.ShapeDtypeStruct((m, n), out_dtype), jax.ShapeDtypeStruct((m, d), BF16)],
        compiler_params=_params(("parallel",)),
        name=name,
    )(x, nw, mod, w)


def _mm_kernel(x_ref, w_ref, o_ref):
    o_ref[...] = jnp.dot(x_ref[...], w_ref[...], preferred_element_type=F32).astype(o_ref.dtype)


def _matmul(x, w, out_dtype, tm, tn, name):
    m, d = x.shape
    n = w.shape[1]
    return pl.pallas_call(
        _mm_kernel,
        grid=(m // tm, n // tn),
        in_specs=[pl.BlockSpec((tm, d), lambda i, j: (i, 0)),
                  pl.BlockSpec((d, tn), lambda i, j: (0, j))],
        out_specs=pl.BlockSpec((tm, tn), lambda i, j: (i, j)),
        out_shape=jax.ShapeDtypeStruct((m, n), out_dtype),
        compiler_params=_params(("parallel", "parallel")),
        name=name,
    )(x, w)


def _mla_proj_kernel(a_ref, cr_ref, sr_ref, cc_ref, sc_ref, qan_ref, kvan_ref, qn_ref, kn_ref, wq_ref, wkv_ref,
                     q_ref, k_ref, v_ref):
    tm = a_ref.shape[0]
    g = tm // GRID_W

    def table(row_ref, col_ref):
        r = jnp.broadcast_to(row_ref[...][:, None, :], (g, GRID_W, ROPE_PAD)).reshape(tm, ROPE_PAD)
        c = jnp.broadcast_to(col_ref[...][None], (g, GRID_W, ROPE_PAD)).reshape(tm, ROPE_PAD)
        return r + c

    cos = table(cr_ref, cc_ref)
    sin = table(sr_ref, sc_ref)

    def rope(r):
        return r * cos + pltpu.roll(r, 64, axis=1) * sin

    def rms_rows(t, w):
        return t * lax.rsqrt(jnp.mean(t * t, axis=-1, keepdims=True) + EPS) * w

    qa = rms_rows(a_ref[:, A_QA:A_QA + MLA_Q_RANK], qan_ref[...]).astype(BF16)
    q = jnp.dot(qa, wq_ref[...], preferred_element_type=F32)
    kva = rms_rows(a_ref[:, A_KVA:A_KVA + MLA_KV_RANK], kvan_ref[...]).astype(BF16)
    kv = jnp.dot(kva, wkv_ref[...], preferred_element_type=F32)
    kr = a_ref[:, A_KR:A_KR + ROPE_PAD]
    kr_ss = jnp.sum(kr * kr, axis=-1, keepdims=True)
    qn = qn_ref[...]
    kn = kn_ref[...]
    scale = MLA_QK ** -0.5 * math.log2(math.e)
    for h in range(MLA_HEADS):
        qh = q[:, h * MLA_QK_PAD:(h + 1) * MLA_QK_PAD]
        inv = lax.rsqrt(jnp.sum(qh * qh, axis=-1, keepdims=True) * (1.0 / MLA_QK) + EPS) * scale
        qh = qh * inv * qn
        q_ref[:, h * MLA_QK_PAD:h * MLA_QK_PAD + MLA_NOPE] = qh[:, :MLA_NOPE].astype(BF16)
        q_ref[:, h * MLA_QK_PAD + MLA_NOPE:(h + 1) * MLA_QK_PAD] = rope(qh[:, MLA_NOPE:]).astype(BF16)

        kh = kv[:, h * MLA_NOPE:(h + 1) * MLA_NOPE]
        inv = lax.rsqrt((jnp.sum(kh * kh, axis=-1, keepdims=True) + kr_ss) * (1.0 / MLA_QK) + EPS)
        k_ref[:, h * MLA_QK_PAD:h * MLA_QK_PAD + MLA_NOPE] = (kh * inv * kn[:, :MLA_NOPE]).astype(BF16)
        k_ref[:, h * MLA_QK_PAD + MLA_NOPE:(h + 1) * MLA_QK_PAD] = rope(
            kr * inv * kn[:, MLA_NOPE:]).astype(BF16)
    v_ref[...] = kv[:, MLA_HEADS * MLA_NOPE:].astype(BF16)


def _mla_proj(a, tables, qan, kvan, qn, kn, wq, wkv, tm, seq):
    m = a.shape[0]
    nseq = seq // tm
    row_cos, row_sin, col_cos, col_sin = tables
    full = lambda arr: pl.BlockSpec(arr.shape, lambda i: (0, 0))
    row_spec = pl.BlockSpec((tm // GRID_W, ROPE_PAD), lambda i: (i % nseq, 0))
    return pl.pallas_call(
        _mla_proj_kernel,
        grid=(m // tm,),
        in_specs=[pl.BlockSpec((tm, A_WIDTH), lambda i: (i, 0)),
                  row_spec, row_spec, full(col_cos), full(col_sin),
                  full(qan), full(kvan), full(qn), full(kn), full(wq), full(wkv)],
        out_specs=[pl.BlockSpec((tm, MLA_HEADS * MLA_QK_PAD), lambda i: (i, 0)),
                   pl.BlockSpec((tm, MLA_HEADS * MLA_QK_PAD), lambda i: (i, 0)),
                   pl.BlockSpec((tm, MLA_HEADS * MLA_V), lambda i: (i, 0))],
        out_shape=[jax.ShapeDtypeStruct((m, MLA_HEADS * MLA_QK_PAD), BF16),
                   jax.ShapeDtypeStruct((m, MLA_HEADS * MLA_QK_PAD), BF16),
                   jax.ShapeDtypeStruct((m, MLA_HEADS * MLA_V), BF16)],
        compiler_params=_params(("parallel",)),
        name="mla_proj",
    )(a, row_cos, row_sin, col_cos, col_sin, qan, kvan, qn, kn, wq, wkv)


ATTN_UNROLL = 16
ATTN_ROWS = 64


def _attn_kernel(q_ref, kc_ref, vct_ref, k_ref, vt_ref, o_ref, qt_ref, s0_ref, s1_ref, acc_ref, *, nk, unroll):
    qt_ref[...] = q_ref[0].T
    tq = qt_ref.shape[1]
    tk = s0_ref.shape[0]
    rows = ATTN_ROWS
    s_refs = (s0_ref, s1_ref)

    def scores(j):
        return jnp.dot(k_ref[0, pl.ds(pl.multiple_of(j * tk, tk), tk), :], qt_ref[...], preferred_element_type=F32)

    def probs(s, m):
        return jnp.exp2((s - m).astype(BF16))

    def store_scores(slot, s):
        s_refs[slot][...] = s
        return jnp.max(s.reshape(tk // SUBLANE, SUBLANE, tq), axis=0)

    s = jnp.dot(kc_ref[0], qt_ref[...], preferred_element_type=F32)
    m8_first = store_scores(0, scores(0))
    m = jnp.max(s, axis=0, keepdims=True)
    acc_ref[...] = jnp.dot(vct_ref[0, 0], probs(s, m), preferred_element_type=F32)

    def tile(j, slot, carry):
        m, m8 = carry
        m8_next = store_scores(1 - slot, scores(jnp.minimum(j + 1, nk - 1)))
        m_new = jnp.maximum(m, jnp.max(m8, axis=0, keepdims=True))
        p = jnp.concatenate([probs(s_refs[slot][r:r + rows, :], m_new) for r in range(0, tk, rows)], axis=0)
        acc_ref[...] = jnp.exp2(m - m_new) * acc_ref[...] + jnp.dot(vt_ref[0, 0, j], p, preferred_element_type=F32)
        return m_new, m8_next

    def body(i, carry):
        for u in range(unroll):
            carry = tile(unroll * i + u, u % 2, carry)
        return carry

    lax.fori_loop(0, nk // unroll, body, (m, m8_first))
    acc = acc_ref[...]
    o_ref[0] = (acc[:MLA_V] * (1.0 / acc[MLA_V:MLA_V + 1])).T.astype(o_ref.dtype)


def _attention(q, k, vt, kc, vct, tq):
    b, t, _ = k.shape
    lc = kc.shape[1]
    nk, vrows, tk = vt.shape[2], vt.shape[3], vt.shape[4]
    unroll = ATTN_UNROLL if nk % ATTN_UNROLL == 0 else 2
    assert nk % unroll == 0
    kern = functools.partial(_attn_kernel, nk=nk, unroll=unroll)
    return pl.pallas_call(
        kern,
        grid=(b, MLA_HEADS, t // tq),
        in_specs=[pl.BlockSpec((1, tq, MLA_QK_PAD), lambda bi, h, i: (bi, i, h)),
                  pl.BlockSpec((1, lc, MLA_QK_PAD), lambda bi, h, i: (bi, 0, h)),
                  pl.BlockSpec((1, 1, vrows, lc), lambda bi, h, i: (bi, h, 0, 0)),
                  pl.BlockSpec((1, t, MLA_QK_PAD), lambda bi, h, i: (bi, 0, h)),
                  pl.BlockSpec((1, 1, nk, vrows, tk), lambda bi, h, i: (bi, h, 0, 0, 0))],
        out_specs=pl.BlockSpec((1, tq, MLA_V), lambda bi, h, i: (bi, i, h)),
        out_shape=jax.ShapeDtypeStruct((b, t, MLA_HEADS * MLA_V), BF16),
        scratch_shapes=[pltpu.VMEM((MLA_QK_PAD, tq), BF16), pltpu.VMEM((tk, tq), F32), pltpu.VMEM((tk, tq), F32),
                        pltpu.VMEM((vrows, tq), F32)],
        compiler_params=_params(("parallel", "parallel", "arbitrary")),
        name="attention",
    )(q, kc, vct, k, vt)


def _split2(x):
    hi = x.astype(BF16)
    return hi, (x - hi.astype(F32)).astype(BF16)


def _bmm(a, b):
    return lax.dot_general(a, b, (((2,), (1,)), ((0,), (0,))), preferred_element_type=F32)


def _bmm_nt(a, b):
    return lax.dot_general(a, b, (((2,), (2,)), ((0,), (0,))), preferred_element_type=F32)


def _bmm_hp(a, b):
    ah, al = _split2(a)
    bh, bl = _split2(b)
    return _bmm(ah, bh) + _bmm(al, bh) + _bmm(ah, bl)


def _bd2(b):
    lane = lax.broadcasted_iota(jnp.int32, b.shape, 2)
    zero = jnp.zeros_like(b)
    return jnp.concatenate([jnp.where(lane < DN_CHUNK, b, zero), jnp.where(lane >= DN_CHUNK, b, zero)], axis=1)


def _tri_inverse(m, eye):
    assert m.shape[-2] == 64
    x = eye - m
    nb = (-m).astype(BF16)
    bd = _bd2(nb)
    for _ in range(3):
        nb = _bmm(nb, bd).astype(BF16)
        bd = _bd2(nb)
        x = x + _bmm(x.astype(BF16), bd)
    xb = x.astype(BF16)
    r = eye - x - _bmm(m.astype(BF16), _bd2(xb))
    x = x + _bmm(xb, _bd2(r.astype(BF16)))
    mh, ml = _split2(m)
    xh, xl = _split2(x)
    bdh = _bd2(xh)
    r = eye - x - (_bmm(mh, bdh) + _bmm(ml, bdh) + _bmm(mh, _bd2(xl)))
    return x + _bmm(xh, _bd2(r.astype(BF16)))


def _dn_prep_kernel(qm_ref, qp_ref, qx_ref, km_ref, kp_ref, kx_ref, vm_ref, vp_ref, vx_ref,
                    cwq_ref, cwk_ref, cwv_ref, ab_ref, hp_ref,
                    u_ref, w_ref, ke_ref, qg_ref, a_ref, ge_ref, ext_ref, *, tm, tiles_per_seq):
    i = pl.program_id(0)
    first = (i % tiles_per_seq) == 0
    last = (i % tiles_per_seq) == tiles_per_seq - 1
    c = DN_CHUNK

    def conv_silu(main_ref, prev_ref, next_ref, cw_ref):
        ext_ref[0:HALO16, :] = jnp.where(first, 0.0, prev_ref[...].astype(F32))
        ext_ref[HALO16:HALO16 + tm, :] = main_ref[...].astype(F32)
        ext_ref[HALO16 + tm:, :] = jnp.where(last, 0.0, next_ref[...].astype(F32))
        y = (cw_ref[0:1, :] * ext_ref[HALO16 - 1:HALO16 - 1 + tm, :]
             + cw_ref[1:2, :] * ext_ref[HALO16:HALO16 + tm, :]
             + cw_ref[2:3, :] * ext_ref[HALO16 + 1:HALO16 + 1 + tm, :])
        return _silu(y)

    def l2n(t):
        return t * lax.rsqrt(jnp.sum(t * t, axis=-1, keepdims=True) + EPS)

    q = l2n(conv_silu(qm_ref, qp_ref, qx_ref, cwq_ref)) * (DN_DK ** -0.5)
    k = l2n(conv_silu(km_ref, kp_ref, kx_ref, cwk_ref))
    v = conv_silu(vm_ref, vp_ref, vx_ref, cwv_ref)

    hp = hp_ref[0]
    ab = ab_ref[0]
    z = ab[:, 0:2] + hp[:, 2:4]
    softplus = jnp.maximum(z, 0.0) + jnp.log(1.0 + jnp.exp(-jnp.abs(z)))
    g_all = -jnp.exp(hp[:, 0:2]) * softplus
    beta_all = jax.nn.sigmoid(ab[:, 2:4])

    nc = tm // c

    def per_chunk(t):
        return t.reshape(nc, c, t.shape[-1])

    def all_lanes(t2, d):
        return per_chunk(jnp.broadcast_to(t2[:, d:d + 1], (tm, LANE)))

    shape = (nc, c, LANE)
    rows = lax.broadcasted_iota(jnp.int32, shape, 1)
    lane = lax.broadcasted_iota(jnp.int32, shape, 2)
    bwd = lane >= c
    col = jnp.where(bwd, lane - c, lane)
    eye_mask = rows == col
    ahead = jnp.where(bwd, rows - col, col - rows)
    incl = ahead <= 0
    strict = ahead < 0
    eye = jnp.where(eye_mask, 1.0, 0.0)
    tri = jnp.where(incl, 1.0, 0.0).astype(BF16)

    def halves(t2):
        return jnp.where(bwd, all_lanes(t2, 1), all_lanes(t2, 0))

    g_hi, g_lo = _split2(halves(g_all))
    gc = _bmm(tri, _bd2(g_hi)) + _bmm(tri, _bd2(g_lo))
    gc_row = jnp.sum(jnp.where(eye_mask, gc, 0.0), axis=1, keepdims=True)
    decay = jnp.where(incl, jnp.exp(jnp.where(incl, gc - gc_row, 0.0)), 0.0)

    k3 = per_chunk(k)
    q3 = per_chunk(q)
    v3 = per_chunk(v)
    k3b = k3.astype(BF16)
    kdup = jnp.concatenate([k3b, k3b], axis=1)
    kk = _bmm_nt(k3b, kdup)
    att = _bmm_nt(q3.astype(BF16), kdup) * decay
    m = jnp.where(strict, halves(beta_all) * kk * decay, 0.0)
    t_hi, t_lo = _split2(_tri_inverse(m, eye))

    swapped = pltpu.roll(gc.reshape(tm, LANE), c, axis=1).reshape(shape)
    gc_dir = (jnp.where(bwd, swapped, gc), jnp.where(bwd, gc, swapped))
    rhs, ke, qg, gend = [], [], [], []
    for d in range(2):
        gcd = gc_dir[d]
        beta = all_lanes(beta_all, d)
        eg = jnp.exp(gcd)
        gtot = gcd[:, c - 1:c, :] if d == 0 else gcd[:, 0:1, :]
        rhs.append(jnp.concatenate([v3 * beta, k3 * beta * eg], axis=2))
        ke.append(k3 * jnp.exp(gtot - gcd))
        qg.append(q3 * eg)
        gend.append(jnp.exp(gtot))
    rhs = jnp.concatenate(rhs, axis=1).astype(BF16)
    sol = _bmm(_bd2(t_hi), rhs) + _bmm(_bd2(t_lo), rhs)
    att_dir = (att, pltpu.roll(att.reshape(tm, LANE), c, axis=1).reshape(shape))
    for d in range(2):
        sl = slice(d * c, (d + 1) * c)
        u_ref[d, 0, 0] = sol[:, sl, :DN_DV].reshape(tm, DN_DV)
        w_ref[d, 0, 0] = sol[:, sl, DN_DV:].reshape(tm, DN_DK).astype(BF16)
        ke_ref[d, 0, 0] = ke[d].reshape(tm, DN_DK).astype(BF16)
        qg_ref[d, 0, 0] = qg[d].reshape(tm, DN_DK).astype(BF16)
        a_ref[d, 0, 0] = att_dir[d][:, :, :c].reshape(tm, c).astype(BF16)
        ge_ref[d, 0, 0] = gend[d]


def _dn_prep(p, ab, hp, conv_w, batch, seq, tm):
    h = DN_HEADS
    nt = seq // tm
    nb16 = tm // HALO16
    total16 = batch * seq // HALO16
    kern = functools.partial(_dn_prep_kernel, tm=tm, tiles_per_seq=nt)

    def triple(col0):
        return [pl.BlockSpec((tm, LANE), lambda i, hh: (i, col0 + hh)),
                pl.BlockSpec((HALO16, LANE), lambda i, hh: (jnp.maximum(i * nb16 - 1, 0), col0 + hh)),
                pl.BlockSpec((HALO16, LANE), lambda i, hh: (jnp.minimum((i + 1) * nb16, total16 - 1), col0 + hh))]

    cq, ck, cv = B_DQ // LANE, B_DK // LANE, B_DV // LANE
    row = lambda shape: pl.BlockSpec(shape, lambda i, hh: (0, i // nt, hh, i % nt, 0))
    seq_shape = lambda width, dt: jax.ShapeDtypeStruct((2, batch, h, seq, width), dt)
    return pl.pallas_call(
        kern,
        grid=(batch * nt, h),
        in_specs=triple(cq) + triple(ck) + triple(cv) + [
            pl.BlockSpec((3, LANE), lambda i, hh: (0, hh)),
            pl.BlockSpec((3, LANE), lambda i, hh: (0, h + hh)),
            pl.BlockSpec((3, LANE), lambda i, hh: (0, 2 * h + hh)),
            pl.BlockSpec((1, tm, 4), lambda i, hh: (hh, i, 0)),
            pl.BlockSpec((1, 1, 4), lambda i, hh: (hh, 0, 0))],
        out_specs=[row((2, 1, 1, tm, DN_DV)), row((2, 1, 1, tm, DN_DK)), row((2, 1, 1, tm, DN_DK)),
                   row((2, 1, 1, tm, DN_DK)), row((2, 1, 1, tm, DN_CHUNK)),
                   pl.BlockSpec((2, 1, 1, tm // DN_CHUNK, 1, LANE), lambda i, hh: (0, i // nt, hh, i % nt, 0, 0))],
        out_shape=[seq_shape(DN_DV, F32), seq_shape(DN_DK, BF16), seq_shape(DN_DK, BF16),
                   seq_shape(DN_DK, BF16), seq_shape(DN_CHUNK, BF16),
                   jax.ShapeDtypeStruct((2, batch, h, seq // DN_CHUNK, 1, LANE), F32)],
        scratch_shapes=[pltpu.VMEM((tm + 2 * HALO16, LANE), F32)],
        compiler_params=_params(("parallel", "parallel")),
        name="dn_prep",
    )(p, p, p, p, p, p, p, p, p, conv_w, conv_w, conv_w, ab, hp)


def _dn_scan_kernel(s0_ref, uf_ref, wf_ref, kf_ref, qf_ref, af_ref, gf_ref,
                    ub_ref, wb_ref, kb_ref, qb_ref, ab_ref, gb_ref,
                    of_ref, ob_ref, sfin_ref, s_ref, *, cs):
    n = pl.program_id(1)
    c = DN_CHUNK
    h = DN_HEADS

    @pl.when(n == 0)
    def _():
        s_ref[0:h] = s0_ref[0, 0]
        s_ref[h:] = s0_ref[1, 0]

    for ci in range(cs):
        rf = ci * c
        rb = (cs - 1 - ci) * c

        def pair(f_ref, b_ref):
            return jnp.concatenate([f_ref[0, 0, :, rf:rf + c, :], b_ref[0, 0, :, rb:rb + c, :]], axis=0)

        s = s_ref[...]
        s16 = s.astype(BF16)
        ws_qs = _bmm(jnp.concatenate([pair(wf_ref, wb_ref), pair(qf_ref, qb_ref)], axis=1), s16)
        v_new = pair(uf_ref, ub_ref) - ws_qs[:, :c]
        v16 = v_new.astype(BF16)
        o = ws_qs[:, c:] + _bmm(pair(af_ref, ab_ref), v16)
        of_ref[0, :, rf:rf + c, :] = o[:h].astype(of_ref.dtype)
        ob_ref[0, :, rb:rb + c, :] = o[h:].astype(ob_ref.dtype)
        g = jnp.concatenate([gf_ref[0, 0, :, ci], gb_ref[0, 0, :, cs - 1 - ci]], axis=0)
        s_ref[...] = s * g + lax.dot_general(pair(kf_ref, kb_ref), v16, (((1,), (1,)), ((0,), (0,))),
                                             preferred_element_type=F32)

    @pl.when(n == pl.num_programs(1) - 1)
    def _():
        sfin_ref[0, 0] = s_ref[0:h]
        sfin_ref[1, 0] = s_ref[h:]


def _dn_scan(s0, u, w, ke, qg, a, ge, cs):
    _, batch, h, seq, _ = u.shape
    ts = cs * DN_CHUNK
    ns = seq // ts
    kern = functools.partial(_dn_scan_kernel, cs=cs)

    def specs(d):
        idx = (lambda n: n) if d == 0 else (lambda n: ns - 1 - n)
        seqs = [pl.BlockSpec((1, 1, h, ts, width), lambda b, n: (d, b, 0, idx(n), 0))
                for width in (DN_DV, DN_DK, DN_DK, DN_DK, DN_CHUNK)]
        return seqs + [pl.BlockSpec((1, 1, h, cs, 1, LANE), lambda b, n: (d, b, 0, idx(n), 0, 0))]

    o_spec = lambda d: pl.BlockSpec((1, h, ts, DN_DV), lambda b, n: (b, 0, n if d == 0 else ns - 1 - n, 0))
    state = pl.BlockSpec((2, 1, h, DN_DK, DN_DV), lambda b, n: (0, b, 0, 0, 0))
    return pl.pallas_call(
        kern,
        grid=(batch, ns),
        in_specs=[state] + specs(0) + specs(1),
        out_specs=[o_spec(0), o_spec(1), state],
        out_shape=[jax.ShapeDtypeStruct((batch, h, seq, DN_DV), BF16),
                   jax.ShapeDtypeStruct((batch, h, seq, DN_DV), BF16),
                   jax.ShapeDtypeStruct((2, batch, h, DN_DK, DN_DV), F32)],
        scratch_shapes=[pltpu.VMEM((2 * h, DN_DK, DN_DV), F32)],
        compiler_params=_params(("parallel", "arbitrary")),
        name="dn_scan",
    )(s0, u, w, ke, qg, a, ge, u, w, ke, qg, a, ge)


def _merge_kernel(x_ref, mod_ref, ym_ref, of_ref, ob_ref, z_ref, nw_ref, ga_ref, gb_ref, wm_ref, wd_ref, wo_ref,
                  nf_ref, o_ref, xn_ref, *, k_gate, k_shift, k_scale):
    nw = nw_ref[...]
    heads = []
    for h in range(DN_HEADS):
        o = of_ref[0, h].astype(F32) + ob_ref[0, h].astype(F32)
        y = o * lax.rsqrt(jnp.mean(o * o, axis=-1, keepdims=True) + EPS) * nw
        heads.append((y * _silu(z_ref[:, h * DN_DV:(h + 1) * DN_DV].astype(F32))).astype(BF16))
    yd = jnp.concatenate(heads, axis=1)
    pm = jnp.dot(ym_ref[...], wm_ref[...], preferred_element_type=F32)
    pd = jnp.dot(yd, wd_ref[...], preferred_element_type=F32)
    merged = (jax.nn.sigmoid(ga_ref[...].astype(F32)) * pm
              + jax.nn.sigmoid(gb_ref[...].astype(F32)) * pd).astype(BF16)
    y = jnp.dot(merged, wo_ref[...], preferred_element_type=F32)
    x1 = x_ref[...] + mod_ref[0, k_gate:k_gate + 1, :] * y
    o_ref[...] = x1
    xn_ref[...] = _norm_mod(x1, nf_ref[...], mod_ref[0, k_shift:k_shift + 1, :],
                            mod_ref[0, k_scale:k_scale + 1, :]).astype(BF16)


def _merge(x, mod, row_of_tile, ym, o_f, o_b, p, nw, wm, wd, wo, nf, tm, k_gate, k_shift, k_scale):
    m, d = x.shape
    _, h, seq, _ = o_f.shape
    nt = seq // tm
    zw = h * DN_DV
    kern = functools.partial(_merge_kernel, k_gate=k_gate, k_shift=k_shift, k_scale=k_scale)
    const = lambda arr: pl.BlockSpec(arr.shape, lambda i: (0, 0), pipeline_mode=pl.Buffered(1))
    scan_out = pl.BlockSpec((1, h, tm, DN_DV), lambda i: (i // nt, 0, i % nt, 0))
    return pl.pallas_call(
        kern,
        grid=(m // tm,),
        in_specs=[pl.BlockSpec((tm, d), lambda i: (i, 0)),
                  pl.BlockSpec((1, 6, d), lambda i: (row_of_tile(i), 0, 0)),
                  pl.BlockSpec((tm, ym.shape[1]), lambda i: (i, 0)),
                  scan_out, scan_out,
                  pl.BlockSpec((tm, zw), lambda i: (i, B_DZ // zw)),
                  pl.BlockSpec((1, DN_DV), lambda i: (0, 0)),
                  pl.BlockSpec((tm, d), lambda i: (i, B_GA // d)),
                  pl.BlockSpec((tm, d), lambda i: (i, B_GB // d)),
                  const(wm), const(wd), const(wo),
                  pl.BlockSpec((1, d), lambda i: (0, 0))],
        out_specs=[pl.BlockSpec((tm, d), lambda i: (i, 0)), pl.BlockSpec((tm, d), lambda i: (i, 0))],
        out_shape=[jax.ShapeDtypeStruct((m, d), F32), jax.ShapeDtypeStruct((m, d), BF16)],
        compiler_params=_params(("parallel",)),
        name="merge",
    )(x, mod, ym, o_f, o_b, p, nw, p, p, wm, wd, wo, nf)


def _ffn_kernel(x_ref, xm_ref, xp_ref, xx_ref, mod_ref, wg_ref, wv_ref, cw_ref, wdn_ref, o_ref, xn_ref, ge_ref,
                hid_ref, *, tm, tiles_per_seq, k_gate):
    i = pl.program_id(0)
    j = pl.program_id(1)
    th = FFN_TILE

    @pl.when(j == 0)
    def _():
        first = (i % tiles_per_seq) == 0
        last = (i % tiles_per_seq) == tiles_per_seq - 1
        xn_ref[0:HALO16, :] = jnp.where(first, jnp.zeros_like(xp_ref), xp_ref[...])
        xn_ref[HALO16:HALO16 + tm, :] = xm_ref[...]
        xn_ref[HALO16 + tm:, :] = jnp.where(last, jnp.zeros_like(xx_ref), xx_ref[...])

    ge_ref[...] = jnp.dot(xn_ref[...], wg_ref[...], preferred_element_type=F32)
    val = jnp.dot(xn_ref[HALO16:HALO16 + tm, :], wv_ref[...], preferred_element_type=F32)
    conv = (cw_ref[0:1, :] * ge_ref[HALO16 - 1:HALO16 - 1 + tm, :]
            + cw_ref[1:2, :] * ge_ref[HALO16:HALO16 + tm, :]
            + cw_ref[2:3, :] * ge_ref[HALO16 + 1:HALO16 + 1 + tm, :])
    hid_ref[...] = (_silu(conv) * val).astype(BF16)
    nj = pl.num_programs(1)

    def down():
        return jnp.dot(hid_ref[...], wdn_ref[...], preferred_element_type=F32)

    @pl.when(j == 0)
    def _():
        o_ref[...] = down()

    @pl.when(jnp.logical_and(j > 0, j < nj - 1))
    def _():
        o_ref[...] += down()

    @pl.when(j == nj - 1)
    def _():
        o_ref[...] = x_ref[...] + mod_ref[0, k_gate:k_gate + 1, :] * (o_ref[...] + down())


def _ffn(x, xn, mod, row_of_tile, wg, wv, cw, wdn, seq, tm, k_gate):
    m, d = x.shape
    th = FFN_TILE
    nj = wdn.shape[0] // th
    nt = seq // tm
    nb16 = tm // HALO16
    total16 = m // HALO16
    kern = functools.partial(_ffn_kernel, tm=tm, tiles_per_seq=nt, k_gate=k_gate)
    return pl.pallas_call(
        kern,
        grid=(m // tm, nj),
        in_specs=[pl.BlockSpec((tm, d), lambda i, j: (i, 0)),
                  pl.BlockSpec((tm, d), lambda i, j: (i, 0)),
                  pl.BlockSpec((HALO16, d), lambda i, j: (jnp.maximum(i * nb16 - 1, 0), 0)),
                  pl.BlockSpec((HALO16, d), lambda i, j: (jnp.minimum((i + 1) * nb16, total16 - 1), 0)),
                  pl.BlockSpec((1, 6, d), lambda i, j: (row_of_tile(i), 0, 0)),
                  pl.BlockSpec((d, th), lambda i, j: (0, j)),
                  pl.BlockSpec((d, th), lambda i, j: (0, j)),
                  pl.BlockSpec((3, th), lambda i, j: (0, j)),
                  pl.BlockSpec((th, d), lambda i, j: (j, 0))],
        out_specs=pl.BlockSpec((tm, d), lambda i, j: (i, 0)),
        out_shape=jax.ShapeDtypeStruct((m, d), F32),
        scratch_shapes=[pltpu.VMEM((tm + 2 * HALO16, d), BF16), pltpu.VMEM((tm + 2 * HALO16, th), F32),
                        pltpu.VMEM((tm, th), BF16)],
        compiler_params=_params(("parallel", "arbitrary")),
        name="ffn",
    )(x, xn, xn, xn, mod, wg, wv, cw, wdn)


def _rope_pad_cols(r):
    n = MLA_ROPE // 4
    z = jnp.zeros(r.shape[:-1] + (2 * n,), r.dtype)
    return jnp.concatenate([r[..., 0:n], r[..., 2 * n:3 * n], z, r[..., n:2 * n], r[..., 3 * n:4 * n], z], axis=-1)


def _qk_pad_cols(w):
    lead = w.shape[:-1]
    w = w.reshape(lead + (MLA_HEADS, MLA_QK))
    out = jnp.concatenate([w[..., :MLA_NOPE], _rope_pad_cols(w[..., MLA_NOPE:])], axis=-1)
    return out.reshape(lead + (MLA_HEADS * MLA_QK_PAD,))


def _rope_tables(rows):
    n = MLA_ROPE // 4
    inv = ROPE_BASE ** (-jnp.arange(n, dtype=F32) / n)
    ang_r = jnp.arange(rows, dtype=F32)[:, None] * inv
    ang_c = jnp.arange(GRID_W, dtype=F32)[:, None] * inv

    def lanes(a_row, a_col, fill, count):
        return jnp.concatenate([a_row, a_col, jnp.full((count, 2 * n), fill, F32)] * 2, axis=-1)

    zr, zc = jnp.zeros((rows, n), F32), jnp.zeros((GRID_W, n), F32)
    row_cos = lanes(jnp.cos(ang_r), zr, 0.0, rows)
    row_sin = jnp.concatenate([-jnp.sin(ang_r), zr, jnp.zeros((rows, 2 * n), F32),
                               jnp.sin(ang_r), zr, jnp.zeros((rows, 2 * n), F32)], axis=-1)
    col_cos = lanes(zc, jnp.cos(ang_c), 1.0, GRID_W)
    col_sin = jnp.concatenate([zc, -jnp.sin(ang_c), jnp.zeros((GRID_W, 2 * n), F32),
                               zc, jnp.sin(ang_c), jnp.zeros((GRID_W, 2 * n), F32)], axis=-1)
    return row_cos, row_sin, col_cos, col_sin


def _pick_tile(n, pref):
    t = min(pref, n)
    while n % t:
        t //= 2
    return t


def kernel(x, c, ctx, c_ctx, w_ada, b_ada, norm_mix, w_in, q_a_norm, w_q_b, kv_a_norm, w_kv_b, q_norm, k_norm,
           w_o_mla, dn_conv, dn_a_log, dn_dt_bias, dn_o_norm, w_o_dn, w_out, norm_ffn, w_ffn_up, ffn_conv,
           w_ffn_down):
    batch, seq, d = x.shape
    lc = ctx.shape[1]
    assert w_ada.shape[0] == 1, "single-layer stack"
    assert seq % GRID_W == 0 and seq % DN_CHUNK == 0 and lc % DN_CHUNK == 0
    h = DN_HEADS
    l = 0

    wi = w_in[l]
    o_qa, o_kva, o_kr = 0, MLA_Q_RANK, MLA_Q_RANK + MLA_KV_RANK
    o_dq = o_kr + MLA_ROPE
    o_da = o_dq + 4 * h * DN_DK
    o_ga = o_da + 4 * h
    w_a = jnp.concatenate([wi[:, o_qa:o_kr], _rope_pad_cols(wi[:, o_kr:o_dq]), wi[:, o_da:o_ga],
                           jnp.zeros((d, A_WIDTH - A_DB - 2 * h), F32)], axis=1).astype(BF16)
    w_b = jnp.concatenate([wi[:, o_dq:o_da], wi[:, o_ga:]], axis=1).astype(BF16)
    wq = _qk_pad_cols(w_q_b[l]).astype(BF16)
    wkv = w_kv_b[l].reshape(MLA_KV_RANK, MLA_HEADS, MLA_NOPE + MLA_V)
    wkv = jnp.concatenate([wkv[:, :, :MLA_NOPE].reshape(MLA_KV_RANK, -1),
                           wkv[:, :, MLA_NOPE:].reshape(MLA_KV_RANK, -1)], axis=1).astype(BF16)
    qn = jnp.concatenate([q_norm[l, :MLA_NOPE], _rope_pad_cols(q_norm[l, MLA_NOPE:])])[None, :]
    kn = jnp.concatenate([k_norm[l, :MLA_NOPE], _rope_pad_cols(k_norm[l, MLA_NOPE:])])[None, :]
    hidden = w_ffn_down.shape[1]
    nj = -(-hidden // FFN_TILE)
    hpad = nj * FFN_TILE - hidden
    wg = jnp.pad(w_ffn_up[l][:, :hidden].astype(BF16), ((0, 0), (0, hpad)))
    wv = jnp.pad(w_ffn_up[l][:, hidden:].astype(BF16), ((0, 0), (0, hpad)))
    fcw = jnp.pad(ffn_conv[l], ((0, 0), (0, hpad)))
    wdn = jnp.pad(w_ffn_down[l], ((0, hpad), (0, 0))).astype(BF16)
    hp = jnp.concatenate([dn_a_log[l].T, dn_dt_bias[l].T], axis=1)[:, None, :]
    tables_l = _rope_tables(seq // GRID_W)
    tables_c = (jnp.zeros((lc // GRID_W, ROPE_PAD), F32), jnp.zeros((lc // GRID_W, ROPE_PAD), F32),
                jnp.ones((GRID_W, ROPE_PAD), F32), jnp.zeros((GRID_W, ROPE_PAD), F32))

    rows = jnp.concatenate([c, c_ctx[None, :], jnp.zeros((8 - batch - 1, d), F32)], axis=0)
    mod = _ada(rows, w_ada[l], b_ada[l][None, :]).reshape(8, 6, d)

    xf = x.reshape(batch * seq, d)
    cf = ctx.reshape(batch * lc, d)

    def stream(tokens, n, tm_a, tm_b, row_of):
        nm = norm_mix[l][None, :]
        a, xn = _norm_mod_matmul(tokens, nm, mod, row_of(tm_a), w_a, F32, tm_a, 0, 1, "in_proj_small")
        p = _matmul(xn, w_b, BF16, tm_b, 1024, "in_proj_wide")
        ab = a[:, A_DA:A_DA + 4 * h].reshape(batch * n, 4, h).transpose(2, 0, 1)
        return a, p, ab

    tm_a = _pick_tile(seq, 512)
    tm_b = _pick_tile(seq, 1024)
    lat_row = lambda tm: (lambda i: i // (seq // tm))
    a_l, p_l, ab_l = stream(xf, seq, tm_a, tm_b, lat_row)
    tc = _pick_tile(lc, 256)
    ctx_row = lambda tm: (lambda i: batch)
    a_c, p_c, ab_c = stream(cf, lc, tc, tc, ctx_row)

    qan, kvan = q_a_norm[l][None, :], kv_a_norm[l][None, :]
    tmp = _pick_tile(seq, 512)
    q_l, k_l, v_l = _mla_proj(a_l, tables_l, qan, kvan, qn, kn, wq, wkv, tmp, seq)
    _, k_c, v_c = _mla_proj(a_c, tables_c, qan, kvan, qn, kn, wq, wkv, tc, lc)
    tq = _pick_tile(seq, 512)
    tk = _pick_tile(seq, 512)

    def with_ones_row(vt):
        lead, n = vt.shape[:-2], vt.shape[-1]
        return jnp.concatenate([vt, jnp.ones(lead + (1, n), BF16), jnp.zeros(lead + (HALO16 - 1, n), BF16)],
                               axis=-2)

    vt = with_ones_row(v_l.reshape(batch, seq // tk, tk, MLA_HEADS, MLA_V).transpose(0, 3, 1, 4, 2))
    vct = with_ones_row(v_c.reshape(batch, lc, MLA_HEADS, MLA_V).transpose(0, 2, 3, 1))
    y_mla = _attention(q_l.reshape(batch, seq, -1), k_l.reshape(batch, seq, -1), vt, k_c.reshape(batch, lc, -1), vct, tq)
    y_mla = y_mla.reshape(batch * seq, MLA_HEADS * MLA_V)

    conv_w = dn_conv[l]
    prep_c = _dn_prep(p_c, ab_c, hp, conv_w, batch, lc, _pick_tile(lc, 256))
    s_zero = jnp.zeros((2, batch, h, DN_DK, DN_DV), F32)
    _, _, s_ctx = _dn_scan(s_zero, *prep_c, _pick_tile(lc, 256) // DN_CHUNK)
    prep_l = _dn_prep(p_l, ab_l, hp, conv_w, batch, seq, _pick_tile(seq, 2048))
    o_f, o_b, _ = _dn_scan(s_ctx, *prep_l, _pick_tile(seq, 512) // DN_CHUNK)

    tmm = _pick_tile(seq, 256)
    x1, xn2 = _merge(xf, mod, lat_row(tmm), y_mla, o_f, o_b, p_l, dn_o_norm[l][None, :], w_o_mla[l].astype(BF16),
                     w_o_dn[l].astype(BF16), w_out[l].astype(BF16), norm_ffn[l][None, :], tmm, 2, 3, 4)

    tmf = _pick_tile(seq, 512)
    out = _ffn(x1, xn2, mod, lat_row(tmf), wg, wv, fcw, wdn, seq, tmf, 5)
    return out.reshape(batch, seq, d)
```
